```python
import jax, jax.numpy as jnp
from jax import lax
import numpy as np

D_MODEL = 2048
BATCH = 4
SEQ = 4096
DEPTH = 2

CHUNK = 64
N_A_LAYERS = DEPTH // 2
N_B_LAYERS = DEPTH - N_A_LAYERS
MIX_WIDTH = D_MODEL
MEM_TOKENS = 256
MEM_HEADS = 4
MEM_WIDTH = MIX_WIDTH // 4
MEM_HEAD_DIM = MEM_WIDTH // MEM_HEADS
SEQ_WIDTH = MIX_WIDTH - MEM_WIDTH
HEAD_DIM = 64
SEQ_HEADS = SEQ_WIDTH // HEAD_DIM
DECAY_LORA = max(32, int(round(1.8 * D_MODEL ** 0.5 / 32)) * 32)
ICLR_LORA = max(32, int(round(1.8 * D_MODEL ** 0.5 / 32)) * 32)
GATE_LORA = max(32, int(round(0.6 * D_MODEL ** 0.8 / 32)) * 32)
GN_EPS = 64e-5
RMS_EPS = 1e-6
LEFT_CHUNKS = 8
BAND = (LEFT_CHUNKS + 1) * CHUNK
REL_MAX = 128
REL_SIZE = (CHUNK - 1) + REL_MAX + 1
PEER_KEYS = 128
PEER_EXPERTS = PEER_KEYS * PEER_KEYS
PEER_HEADS = 8
PEER_DKEY = 256
PEER_DHALF = PEER_DKEY // 2
PEER_TOPK = 16
PEER_BLOCK = 128

kernel_name = 'yoco_rwkv7_chunkbias_peer_hybrid'


def rms_norm(x, g):
    xf = x.astype(jnp.float32)
    y = xf * lax.rsqrt(jnp.mean(xf * xf, axis=-1, keepdims=True) + RMS_EPS)
    return (y * g.astype(jnp.float32)).astype(x.dtype)


def wkv7_scan(r, decay, k, v, a, b):
    bsz, _, nh, n = r.shape
    tm = lambda z: jnp.moveaxis(z.astype(jnp.float32), 1, 0)

    def step(S, inp):
        rt, wt, kt, vt, at, bt = inp
        sa = jnp.einsum('bhvk,bhk->bhv', S, at)
        S = S * wt[:, :, None, :] + sa[..., None] * bt[:, :, None, :] + vt[..., None] * kt[:, :, None, :]
        return S, jnp.einsum('bhvk,bhk->bhv', S, rt)

    S0 = jnp.zeros((bsz, nh, n, n), jnp.float32)
    _, ys = lax.scan(step, S0, (tm(r), tm(decay), tm(k), tm(v), tm(a), tm(b)))
    return jnp.moveaxis(ys, 0, 1)


def rwkv7_mix(h, mix, w_in, w0, w1, w2, a0, a1, a2, g1, g2, k_k, k_a, r_k, ln_g, ln_b):
    bsz, t, _ = h.shape
    xx = jnp.pad(h[:, :-1], ((0, 0), (1, 0), (0, 0))) - h
    xr, xw, xk, xv, xa, xg = [h + xx * mix[j] for j in range(6)]
    r = xr @ w_in[:, :SEQ_WIDTH]
    k = xk @ w_in[:, SEQ_WIDTH:2 * SEQ_WIDTH]
    v = xv @ w_in[:, 2 * SEQ_WIDTH:3 * SEQ_WIDTH]
    q_mem = h @ w_in[:, 3 * SEQ_WIDTH:]
    wlog = -jax.nn.softplus(-(w0 + jnp.tanh(xw @ w1) @ w2)) - 0.5
    decay = jnp.exp(-jnp.exp(wlog.astype(jnp.float32)))
    a = jax.nn.sigmoid(a0 + (xa @ a1) @ a2).astype(jnp.float32)
    g = jax.nn.sigmoid(xg @ g1) @ g2
    heads = lambda z: z.reshape(bsz, t, SEQ_HEADS, HEAD_DIM)
    kk = heads((k * k_k).astype(jnp.float32))
    kk = kk / jnp.maximum(jnp.sqrt(jnp.sum(kk * kk, axis=-1, keepdims=True)), 1e-12)
    k = k.astype(jnp.float32) * (1.0 + (a - 1.0) * k_a.astype(jnp.float32))
    r_h, k_h, v_h, a_h = heads(r.astype(jnp.float32)), heads(k), heads(v.astype(jnp.float32)), heads(a)
    y = wkv7_scan(r_h, heads(decay), k_h, v_h, -kk, kk * a_h)
    mu = jnp.mean(y, axis=-1, keepdims=True)
    var = jnp.mean(jnp.square(y - mu), axis=-1, keepdims=True)
    y = ((y - mu) * lax.rsqrt(var + GN_EPS)).reshape(bsz, t, SEQ_WIDTH)
    y = y * ln_g.astype(jnp.float32) + ln_b.astype(jnp.float32)
    bonus = jnp.sum(r_h * k_h * r_k.astype(jnp.float32), axis=-1, keepdims=True) * v_h
    y = y + bonus.reshape(bsz, t, SEQ_WIDTH)
    return (y * g.astype(jnp.float32)).astype(h.dtype), q_mem


def memory_attn(q_mem, mem, norm_g, w_kv):
    bsz, t, _ = q_mem.shape
    m = rms_norm(mem, norm_g) @ w_kv
    mk = m[..., :MEM_WIDTH].reshape(bsz, -1, MEM_HEADS, MEM_HEAD_DIM)
    mv = m[..., MEM_WIDTH:].reshape(bsz, -1, MEM_HEADS, MEM_HEAD_DIM)
    q = q_mem.reshape(bsz, t, MEM_HEADS, MEM_HEAD_DIM)
    s = jnp.einsum('bthd,bmhd->bhtm', q, mk).astype(jnp.float32) * (MEM_HEAD_DIM ** -0.5)
    p = jax.nn.softmax(s, axis=-1).astype(mv.dtype)
    return jnp.einsum('bhtm,bmhd->bthd', p, mv).reshape(bsz, t, MEM_WIDTH)


def chunk_band_attn(q, k_shared, v_shared, rel_bias):
    bsz, t, nh, hd = q.shape
    n_chunks = t // CHUNK
    pad = LEFT_CHUNKS * CHUNK
    qc = jnp.moveaxis(q.reshape(bsz, n_chunks, CHUNK, nh, hd), 1, 0)
    kp = jnp.pad(k_shared, ((0, 0), (pad, 0), (0, 0), (0, 0)))
    vp = jnp.pad(v_shared, ((0, 0), (pad, 0), (0, 0), (0, 0)))
    dist = pad + np.arange(CHUNK)[:, None] - np.arange(BAND)[None, :]
    idx = np.clip(dist, -(CHUNK - 1), REL_MAX) + (CHUNK - 1)
    bias = rel_bias[:, idx].astype(jnp.float32)
    scale = hd ** -0.5

    def one_chunk(args):
        qb, c = args
        start = c * CHUNK
        kb = lax.dynamic_slice_in_dim(kp, start, BAND, axis=1)
        vb = lax.dynamic_slice_in_dim(vp, start, BAND, axis=1)
        s = jnp.einsum('bqhd,bkhd->bhqk', qb, kb).astype(jnp.float32) * scale + bias
        valid = (start - pad + jnp.arange(BAND)) >= 0
        s = jnp.where(valid, s, -jnp.inf)
        p = jax.nn.softmax(s, axis=-1).astype(vb.dtype)
        return jnp.einsum('bhqk,bkhd->bqhd', p, vb)

    out = lax.map(one_chunk, (qc, jnp.arange(n_chunks)))
    return jnp.moveaxis(out, 0, 1).reshape(bsz, t, nh * hd)


def peer_ffn(h, wq, sub_keys, u, v):
    bsz, t, d = h.shape
    n_tok = bsz * t
    hf = h.reshape(n_tok, d)
    q = (hf @ wq).reshape(n_tok, PEER_HEADS, 2, PEER_DHALF)
    s = jnp.einsum('nhpd,hpkd->nhpk', q, sub_keys).astype(jnp.float32)
    s1, i1 = lax.top_k(s[:, :, 0], PEER_TOPK)
    s2, i2 = lax.top_k(s[:, :, 1], PEER_TOPK)
    cand = (s1[..., :, None] + s2[..., None, :]).reshape(n_tok, PEER_HEADS, PEER_TOPK * PEER_TOPK)
    cidx = (i1[..., :, None] * PEER_KEYS + i2[..., None, :]).reshape(n_tok, PEER_HEADS, PEER_TOPK * PEER_TOPK)
    top, pos = lax.top_k(cand, PEER_TOPK)
    n_blk = n_tok // PEER_BLOCK
    eidx = jnp.take_along_axis(cidx, pos, axis=-1).reshape(n_blk, PEER_BLOCK, PEER_HEADS * PEER_TOPK)
    gate = jax.nn.softmax(top, axis=-1).astype(h.dtype).reshape(n_blk, PEER_BLOCK, PEER_HEADS * PEER_TOPK)
    xb = hf.reshape(n_blk, PEER_BLOCK, d)

    def block(args):
        xt, e, g = args
        act = jax.nn.gelu(jnp.einsum('td,ted->te', xt, u[e]), approximate=False)
        return jnp.einsum('te,ted->td', g * act, v[e])

    return lax.map(block, (xb, eidx, gate)).reshape(bsz, t, d)


def setup_inputs(seed: int = 0) -> dict:
    key = jax.random.key(seed)
    ks = iter(jax.random.split(key, 32))
    nrm = lambda shape, scale: scale * jax.random.normal(next(ks), shape, jnp.float32)
    gain = lambda shape: 1.0 + nrm(shape, 0.02)
    D = D_MODEL
    return {
        'x': nrm((BATCH, SEQ, D), 1.0),
        'mem': nrm((BATCH, MEM_TOKENS, D), 1.0),
        'norm_mix': gain((DEPTH, D)),
        'norm_ffn': gain((DEPTH, D)),
        'norm_mem': gain((DEPTH, D)),
        'w_mem_kv': nrm((DEPTH, D, 2 * MEM_WIDTH), D ** -0.5),
        'w_out': nrm((DEPTH, MIX_WIDTH, D), MIX_WIDTH ** -0.5),
        'peer_wq': nrm((DEPTH, D, PEER_HEADS * PEER_DKEY), D ** -0.5),
        'peer_keys': nrm((DEPTH, PEER_HEADS, 2, PEER_KEYS, PEER_DHALF), PEER_DHALF ** -0.5),
        'peer_u': nrm((DEPTH, PEER_EXPERTS, D), D ** -0.5),
        'peer_v': nrm((DEPTH, PEER_EXPERTS, D), PEER_HEADS ** -0.5),
        'a_mix': jax.random.uniform(next(ks), (N_A_LAYERS, 6, D), jnp.float32),
        'a_w_in': nrm((N_A_LAYERS, D, 3 * SEQ_WIDTH + MEM_WIDTH), D ** -0.5),
        'a_w0': jax.random.uniform(next(ks), (N_A_LAYERS, SEQ_WIDTH), jnp.float32, -6.0, -1.0),
        'a_w1': nrm((N_A_LAYERS, D, DECAY_LORA), D ** -0.5),
        'a_w2': nrm((N_A_LAYERS, DECAY_LORA, SEQ_WIDTH), 0.5 * DECAY_LORA ** -0.5),
        'a_a0': nrm((N_A_LAYERS, SEQ_WIDTH), 0.1),
        'a_a1': nrm((N_A_LAYERS, D, ICLR_LORA), D ** -0.5),
        'a_a2': nrm((N_A_LAYERS, ICLR_LORA, SEQ_WIDTH), ICLR_LORA ** -0.5),
        'a_g1': nrm((N_A_LAYERS, D, GATE_LORA), D ** -0.5),
        'a_g2': nrm((N_A_LAYERS, GATE_LORA, SEQ_WIDTH), GATE_LORA ** -0.5),
        'a_k_k': 0.85 + nrm((N_A_LAYERS, SEQ_WIDTH), 0.05),
        'a_k_a': 1.0 + nrm((N_A_LAYERS, SEQ_WIDTH), 0.05),
        'a_r_k': nrm((N_A_LAYERS, SEQ_HEADS, HEAD_DIM), 0.1),
        'a_ln_g': gain((N_A_LAYERS, SEQ_WIDTH)),
        'a_ln_b': nrm((N_A_LAYERS, SEQ_WIDTH), 0.02),
        'kv_norm': gain((D,)),
        'w_kv_shared': nrm((D, 2 * SEQ_WIDTH), D ** -0.5),
        'b_w_in': nrm((N_B_LAYERS, D, SEQ_WIDTH + MEM_WIDTH), D ** -0.5),
        'b_rel_bias': nrm((N_B_LAYERS, SEQ_HEADS, REL_SIZE), 0.5),
        'final_norm': gain((D,)),
    }


def reference(x, mem, norm_mix, norm_ffn, norm_mem, w_mem_kv, w_out, peer_wq, peer_keys, peer_u, peer_v,
              a_mix, a_w_in, a_w0, a_w1, a_w2, a_a0, a_a1, a_a2, a_g1, a_g2, a_k_k, a_k_a, a_r_k,
              a_ln_g, a_ln_b, kv_norm, w_kv_shared, b_w_in, b_rel_bias, final_norm):
    bsz, t, _ = x.shape
    shared_k = None
    shared_v = None
    for i in range(DEPTH):
        h = rms_norm(x, norm_mix[i])
        if i < N_A_LAYERS:
            j = i
            seq_out, q_mem = rwkv7_mix(h, a_mix[j], a_w_in[j], a_w0[j], a_w1[j], a_w2[j], a_a0[j], a_a1[j],
                                       a_a2[j], a_g1[j], a_g2[j], a_k_k[j], a_k_a[j], a_r_k[j],
                                       a_ln_g[j], a_ln_b[j])
        else:
            j = i - N_A_LAYERS
            proj = h @ b_w_in[j]
            q = proj[..., :SEQ_WIDTH].reshape(bsz, t, SEQ_HEADS, HEAD_DIM)
            q_mem = proj[..., SEQ_WIDTH:]
            seq_out = chunk_band_attn(q, shared_k, shared_v, b_rel_bias[j])
        mem_out = memory_attn(q_mem, mem, norm_mem[i], w_mem_kv[i])
        x = x + jnp.concatenate([seq_out, mem_out], axis=-1) @ w_out[i]
        x = x + peer_ffn(rms_norm(x, norm_ffn[i]), peer_wq[i], peer_keys[i], peer_u[i], peer_v[i])
        if i == N_A_LAYERS - 1:
            kv = rms_norm(x, kv_norm) @ w_kv_shared
            shared_k = kv[..., :SEQ_WIDTH].reshape(bsz, t, SEQ_HEADS, HEAD_DIM)
            shared_v = kv[..., SEQ_WIDTH:].reshape(bsz, t, SEQ_HEADS, HEAD_DIM)
    return rms_norm(x, final_norm)
```

```python
import functools

import numpy as np
import jax
import jax.numpy as jnp
from jax import lax
from jax.experimental import pallas as pl
from jax.experimental.pallas import tpu as pltpu

F32 = jnp.float32
BF16 = jnp.bfloat16
HI = lax.Precision.HIGHEST

HEAD_DIM = 64
PAIR = 2 * HEAD_DIM
CHUNK = 64
LEFT_CHUNKS = 8
BAND = (LEFT_CHUNKS + 1) * CHUNK
REL_MAX = 128
MEM_HEADS = 4
MEM_HEAD_DIM = 128
PEER_KEYS = 128
PEER_HEADS = 8
PEER_TOPK = 16
PEER_PICKS = PEER_HEADS * PEER_TOPK
GN_EPS = 64e-5
RMS_EPS = 1e-6
LORA_PAD = 128
VMEM_LIMIT = 48 * 1024 * 1024


def _params(*sem):
    return pltpu.CompilerParams(dimension_semantics=sem, vmem_limit_bytes=VMEM_LIMIT)


def _dot(a, b):
    return jnp.dot(a, b, preferred_element_type=F32)


def _dot_nt(a, b, precision=None):
    return lax.dot_general(a, b, (((1,), (1,)), ((), ())), precision=precision,
                           preferred_element_type=F32)


def _dot_tn(a, b, precision=None):
    return lax.dot_general(a, b, (((0,), (0,)), ((), ())), precision=precision,
                           preferred_element_type=F32)


def _rmsnorm_body(x_ref, g_ref, o_ref):
    x = x_ref[...]
    ms = jnp.mean(x * x, axis=-1, keepdims=True)
    o_ref[...] = (x * lax.rsqrt(ms + RMS_EPS) * g_ref[...]).astype(o_ref.dtype)


def rmsnorm(x, g, tm=512):
    n, d = x.shape
    tm = min(tm, n)
    return pl.pallas_call(
        _rmsnorm_body,
        grid=(n // tm,),
        in_specs=[pl.BlockSpec((tm, d), lambda i: (i, 0)), pl.BlockSpec((1, d), lambda i: (0, 0))],
        out_specs=pl.BlockSpec((tm, d), lambda i: (i, 0)),
        out_shape=jax.ShapeDtypeStruct((n, d), F32),
        compiler_params=_params("parallel"),
        name="rmsnorm",
    )(x, g.reshape(1, d))


def _normmm_body(x_ref, g_ref, w_ref, o_ref, *rest, emit_hn):
    lhs_ref = rest[-1]

    @pl.when(pl.program_id(1) == 0)
    def _():
        x = x_ref[...]
        ms = jnp.mean(x * x, axis=-1, keepdims=True)
        hn = x * lax.rsqrt(ms + RMS_EPS) * g_ref[...]
        lhs_ref[...] = hn.astype(BF16)
        if emit_hn:
            rest[0][...] = hn

    o_ref[...] = _dot(lhs_ref[...], w_ref[...])


def norm_matmul(x, g, w_bf16, emit_hn=False, tm=512, tn=512):
    n, d = x.shape
    nc = w_bf16.shape[1]
    tm, tn = min(tm, n), min(tn, nc)
    out_shape = [jax.ShapeDtypeStruct((n, nc), F32)]
    out_specs = [pl.BlockSpec((tm, tn), lambda i, j: (i, j))]
    if emit_hn:
        out_shape.append(jax.ShapeDtypeStruct((n, d), F32))
        out_specs.append(pl.BlockSpec((tm, d), lambda i, j: (i, 0)))
    res = pl.pallas_call(
        functools.partial(_normmm_body, emit_hn=emit_hn),
        grid=(n // tm, nc // tn),
        in_specs=[pl.BlockSpec((tm, d), lambda i, j: (i, 0)),
                  pl.BlockSpec((1, d), lambda i, j: (0, 0)),
                  pl.BlockSpec((d, tn), lambda i, j: (0, j))],
        out_specs=out_specs,
        out_shape=out_shape,
        scratch_shapes=[pltpu.VMEM((tm, d), BF16)],
        compiler_params=_params("parallel", "arbitrary"),
        name="norm_matmul",
    )(x, g.reshape(1, d), w_bf16)
    return res if emit_hn else res[0]


def _mixmm_body(gid_ref, h_ref, hp_ref, mix_ref, w_ref, o_ref, lhs_ref, *, tm, seq):
    i = pl.program_id(0)
    j = pl.program_id(1)
    new_group = jnp.logical_or(j == 0, gid_ref[j] != gid_ref[jnp.maximum(j - 1, 0)])

    @pl.when(new_group)
    def _():
        h = h_ref[...]
        prev = jnp.where((i * tm) % seq == 0, 0.0, hp_ref[7:8, :])
        row = lax.broadcasted_iota(jnp.int32, h.shape, 0)
        shifted = jnp.where(row == 0, prev, pltpu.roll(h, 1, 0))
        lhs_ref[...] = (h + (shifted - h) * mix_ref[0]).astype(BF16)

    o_ref[...] = _dot(lhs_ref[...], w_ref[...])


def mix_matmul(h, mix_tab, gid, w_bf16, seq, tn, tm=512):
    n, d = h.shape
    nc = w_bf16.shape[1]
    tm = min(tm, seq)
    sub = tm // 8
    grid_spec = pltpu.PrefetchScalarGridSpec(
        num_scalar_prefetch=1,
        grid=(n // tm, nc // tn),
        in_specs=[pl.BlockSpec((tm, d), lambda i, j, g: (i, 0)),
                  pl.BlockSpec((8, d), lambda i, j, g: (jnp.maximum(i * sub - 1, 0), 0)),
                  pl.BlockSpec((1, 1, d), lambda i, j, g: (g[j], 0, 0)),
                  pl.BlockSpec((d, tn), lambda i, j, g: (0, j))],
        out_specs=pl.BlockSpec((tm, tn), lambda i, j, g: (i, j)),
        scratch_shapes=[pltpu.VMEM((tm, d), BF16)],
    )
    return pl.pallas_call(
        functools.partial(_mixmm_body, tm=tm, seq=seq),
        grid_spec=grid_spec,
        out_shape=jax.ShapeDtypeStruct((n, nc), F32),
        compiler_params=_params("parallel", "arbitrary"),
        name="mix_matmul",
    )(gid, h, h, mix_tab.reshape(mix_tab.shape[0], 1, d), w_bf16)


def _sigmoid(x):
    return 1.0 / (1.0 + jnp.exp(-x))


def _lora2_body(t_ref, w2_ref, a2_ref, g2_ref, w0_ref, a0_ref, lw_ref, a_ref, g_ref):
    t = t_ref[...]
    tw = jnp.tanh(t[:, :LORA_PAD]).astype(BF16)
    ta = t[:, LORA_PAD:2 * LORA_PAD].astype(BF16)
    tg = _sigmoid(t[:, 2 * LORA_PAD:]).astype(BF16)
    u = w0_ref[...] + _dot(tw, w2_ref[...])
    softplus_neg_u = jnp.maximum(-u, 0.0) + jnp.log(1.0 + jnp.exp(-jnp.abs(u)))
    lw_ref[...] = -jnp.exp(-softplus_neg_u - 0.5)
    a_ref[...] = _sigmoid(a0_ref[...] + _dot(ta, a2_ref[...]))
    g_ref[...] = _dot(tg, g2_ref[...])


def lora_stage2(t1, w2p, a2p, g2, w0, a0, tm=256):
    n = t1.shape[0]
    width = w2p.shape[1]
    tm = min(tm, n)
    full = lambda a: pl.BlockSpec(a.shape, lambda i: (0, 0))
    row = pl.BlockSpec((tm, width), lambda i: (i, 0))
    w0 = w0.reshape(1, width)
    a0 = a0.reshape(1, width)
    return pl.pallas_call(
        _lora2_body,
        grid=(n // tm,),
        in_specs=[pl.BlockSpec((tm, t1.shape[1]), lambda i: (i, 0)), full(w2p), full(a2p), full(g2),
                  full(w0), full(a0)],
        out_specs=[row, row, row],
        out_shape=[jax.ShapeDtypeStruct((n, width), F32)] * 3,
        compiler_params=_params("parallel"),
        name="lora_stage2",
    )(t1, w2p, a2p, g2, w0, a0)


def _head_group_sum(x):
    r = lax.broadcasted_iota(jnp.int32, (PAIR, PAIR), 0) // HEAD_DIM
    c = lax.broadcasted_iota(jnp.int32, (PAIR, PAIR), 1) // HEAD_DIM
    ones = jnp.where(r == c, 1.0, 0.0).astype(F32)
    return jnp.dot(x, ones, precision=HI, preferred_element_type=F32)


def _wkv_prepare_body(r_ref, k_ref, v_ref, lw_ref, a_ref, kk_ref, ka_ref,
                      g_ref, s0c_ref, rp_ref, y0_ref, *, tb):
    r = r_ref[...]
    k = k_ref[...]
    v = v_ref[...]
    lw = lw_ref[...]
    a = a_ref[...]
    kk = k * kk_ref[...]
    norm = jnp.sqrt(_head_group_sum(kk * kk))
    kk = kk / jnp.maximum(norm, 1e-12)
    kmod = k * (1.0 + (a - 1.0) * ka_ref[...])
    avec = -kk
    bvec = kk * a

    tr = lax.broadcasted_iota(jnp.int32, (tb, tb), 0)
    tc = lax.broadcasted_iota(jnp.int32, (tb, tb), 1)
    tri = jnp.where((tr >= tc) & (tr // CHUNK == tc // CHUNK), 1.0, 0.0).astype(F32)
    cum = jnp.dot(tri, lw, precision=HI, preferred_element_type=F32)

    lane = lax.broadcasted_iota(jnp.int32, (CHUNK, PAIR), 1)
    head0 = lane < HEAD_DIM
    row = lax.broadcasted_iota(jnp.int32, (PAIR, PAIR), 0)
    col = lax.broadcasted_iota(jnp.int32, (PAIR, PAIR), 1)
    strict = row > col
    lower = row >= col
    eye = jnp.where(row == col, 1.0, 0.0).astype(F32)

    def stack(x):
        return jnp.concatenate([jnp.where(head0, x, 0.0), jnp.where(head0, 0.0, x)], axis=0)

    for c in range(tb // CHUNK):
        sl = slice(c * CHUNK, (c + 1) * CHUNK)
        cm = cum[sl]
        cend = cm[CHUNK - 1:CHUNK]
        e_in = jnp.exp(cm)
        e_out = jnp.exp(-cm)
        e_tail = jnp.exp(cend - cm)
        a_s = stack(avec[sl] * jnp.exp(cm - lw[sl]))
        r_s = stack(r[sl] * e_in)
        b_s = stack(bvec[sl] * e_out)
        k_s = stack(kmod[sl] * e_out)
        bt_s = stack(bvec[sl] * e_tail)
        kt_s = stack(kmod[sl] * e_tail)
        v_s = stack(v[sl])

        p = _dot_nt(jnp.concatenate([a_s, r_s], axis=0).astype(BF16),
                    jnp.concatenate([b_s, k_s], axis=0).astype(BF16))
        l_ab = jnp.where(strict, p[:PAIR, :PAIR], 0.0)
        l_ak = jnp.where(strict, p[:PAIR, PAIR:], 0.0)
        a_rb = jnp.where(lower, p[PAIR:, :PAIR], 0.0)
        a_rk = jnp.where(lower, p[PAIR:, PAIR:], 0.0)

        m = l_ab
        t = eye + m
        for _ in range(5):
            m = jnp.dot(m, m, precision=HI, preferred_element_type=F32)
            t = t + jnp.dot(t, m, precision=HI, preferred_element_type=F32)

        v_b = v_s.astype(BF16)
        lv = _dot(l_ak.astype(BF16), v_b)
        au = jnp.dot(t, jnp.concatenate([a_s, lv], axis=1), precision=HI,
                     preferred_element_type=F32)
        x = _dot(a_rb.astype(BF16), au.astype(BF16))
        rp_ref[0, c] = r_s + x[:, :PAIR]
        y0_ref[0, c] = x[:, PAIR:] + _dot(a_rk.astype(BF16), v_b)
        ap = au[:, :PAIR]
        u0 = au[:, PAIR:]
        g_ref[0, c] = eye * jnp.exp(cend) + _dot_tn(ap.astype(BF16), bt_s.astype(BF16))
        s0c_ref[0, c] = _dot_tn(jnp.concatenate([u0, v_s], axis=0).astype(BF16),
                                jnp.concatenate([bt_s, kt_s], axis=0).astype(BF16))


def wkv_chunk_prepare(proj, lw, a, k_k, k_a, n_pairs, tb=256):
    n = proj.shape[0]
    tb = min(tb, n)
    cpb = tb // CHUNK
    col = lambda off: pl.BlockSpec((tb, PAIR), lambda i, p: (i, off + p))
    par = pl.BlockSpec((1, PAIR), lambda i, p: (0, p))
    blk = pl.BlockSpec((1, cpb, PAIR, PAIR), lambda i, p: (p, i, 0, 0))
    shp = jax.ShapeDtypeStruct((n_pairs, n // CHUNK, PAIR, PAIR), F32)
    return pl.pallas_call(
        functools.partial(_wkv_prepare_body, tb=tb),
        grid=(n // tb, n_pairs),
        in_specs=[col(0), col(n_pairs), col(2 * n_pairs), col(0), col(0), par, par],
        out_specs=[blk] * 4,
        out_shape=[shp] * 4,
        compiler_params=_params("parallel", "parallel"),
        name="wkv_chunk_prepare",
    )(proj, proj, proj, lw, a, k_k.reshape(1, -1), k_a.reshape(1, -1))


def _wkv_scan_body(g_ref, s0c_ref, rp_ref, y0_ref, r_ref, k_ref, v_ref, a_ref, gate_ref,
                   ka_ref, rk_ref, lng_ref, lnb_ref, o_ref, s_ref, y_ref, *, cpb):
    @pl.when(pl.program_id(2) == 0)
    def _():
        s_ref[...] = jnp.zeros_like(s_ref)

    s = s_ref[...]
    for c in range(cpb):
        s_b = s.astype(BF16)
        y_st = _dot_nt(rp_ref[0, c].astype(BF16), s_b) + y0_ref[0, c]
        y_ref[c * CHUNK:(c + 1) * CHUNK, :] = y_st[:CHUNK] + y_st[CHUNK:]
        s = _dot(s_b, g_ref[0, c].astype(BF16)) + s0c_ref[0, c]
    s_ref[...] = s

    y = y_ref[...]
    inv = 1.0 / HEAD_DIM
    mu = _head_group_sum(y) * inv
    yc = y - mu
    var = _head_group_sum(yc * yc) * inv
    yn = yc * lax.rsqrt(var + GN_EPS) * lng_ref[...] + lnb_ref[...]
    r = r_ref[...]
    kmod = k_ref[...] * (1.0 + (a_ref[...] - 1.0) * ka_ref[...])
    bonus = _head_group_sum(r * kmod * rk_ref[...]) * v_ref[...]
    o_ref[...] = ((yn + bonus) * gate_ref[...]).astype(o_ref.dtype)


def wkv_chunk_scan(prep, proj, a, gate, k_a, r_k, ln_g, ln_b, n_pairs, seq, tb=512):
    g_all, s0c_all, rp_all, y0_all = prep
    n = proj.shape[0]
    tb = min(tb, seq)
    cpb = tb // CHUNK
    nblk = seq // tb
    blk = pl.BlockSpec((1, cpb, PAIR, PAIR), lambda b, p, t: (p, b * nblk + t, 0, 0))
    col = lambda off: pl.BlockSpec((tb, PAIR), lambda b, p, t: (b * nblk + t, off + p))
    par = pl.BlockSpec((1, PAIR), lambda b, p, t: (0, p))
    vec = lambda z: z.reshape(1, -1)
    return pl.pallas_call(
        functools.partial(_wkv_scan_body, cpb=cpb),
        grid=(n // seq, n_pairs, nblk),
        in_specs=[blk] * 4 + [col(0), col(n_pairs), col(2 * n_pairs), col(0), col(0)] + [par] * 4,
        out_specs=col(0),
        out_shape=jax.ShapeDtypeStruct((n, n_pairs * PAIR), BF16),
        scratch_shapes=[pltpu.VMEM((PAIR, PAIR), F32), pltpu.VMEM((tb, PAIR), F32)],
        compiler_params=_params("parallel", "parallel", "arbitrary"),
        name="wkv_chunk_scan",
    )(g_all, s0c_all, rp_all, y0_all, proj, proj, proj, a, gate,
      vec(k_a), vec(r_k), vec(ln_g), vec(ln_b))


def _memattn_body(q_ref, m_ref, o_ref):
    width = MEM_HEADS * MEM_HEAD_DIM
    scale = MEM_HEAD_DIM ** -0.5
    for h in range(MEM_HEADS):
        sl = slice(h * MEM_HEAD_DIM, (h + 1) * MEM_HEAD_DIM)
        q = q_ref[:, sl].astype(BF16)
        mk = m_ref[:, sl].astype(BF16)
        mv = m_ref[:, width + h * MEM_HEAD_DIM:width + (h + 1) * MEM_HEAD_DIM].astype(BF16)
        s = _dot_nt(q, mk) * scale
        e = jnp.exp(s - jnp.max(s, axis=-1, keepdims=True))
        p = e / jnp.sum(e, axis=-1, keepdims=True)
        o_ref[:, sl] = _dot(p.astype(BF16), mv).astype(o_ref.dtype)


def memory_attention(proj, q_col_block, mkv, seq, n_mem, tm=512):
    n = proj.shape[0]
    width = MEM_HEADS * MEM_HEAD_DIM
    tm = min(tm, seq)
    nblk = seq // tm
    return pl.pallas_call(
        _memattn_body,
        grid=(n // seq, nblk),
        in_specs=[pl.BlockSpec((tm, width), lambda b, t: (b * nblk + t, q_col_block)),
                  pl.BlockSpec((n_mem, 2 * width), lambda b, t: (b, 0))],
        out_specs=pl.BlockSpec((tm, width), lambda b, t: (b * nblk + t, 0)),
        out_shape=jax.ShapeDtypeStruct((n, width), BF16),
        compiler_params=_params("parallel", "parallel"),
        name="memory_attention",
    )(proj, mkv)


def _outproj_body(s_ref, m_ref, w1_ref, w2_ref, x_ref, o_ref):
    o_ref[...] = x_ref[...] + _dot(s_ref[...], w1_ref[...]) + _dot(m_ref[...], w2_ref[...])


def out_projection(seq_out, mem_out, w_seq, w_mem, x, tm=512, tn=512):
    n, d = x.shape
    tm, tn = min(tm, n), min(tn, d)
    ws, wm = seq_out.shape[1], mem_out.shape[1]
    return pl.pallas_call(
        _outproj_body,
        grid=(n // tm, d // tn),
        in_specs=[pl.BlockSpec((tm, ws), lambda i, j: (i, 0)),
                  pl.BlockSpec((tm, wm), lambda i, j: (i, 0)),
                  pl.BlockSpec((ws, tn), lambda i, j: (0, j)),
                  pl.BlockSpec((wm, tn), lambda i, j: (0, j)),
                  pl.BlockSpec((tm, tn), lambda i, j: (i, j))],
        out_specs=pl.BlockSpec((tm, tn), lambda i, j: (i, j)),
        out_shape=jax.ShapeDtypeStruct((n, d), F32),
        compiler_params=_params("parallel", "parallel"),
        name="out_projection",
    )(seq_out, mem_out, w_seq, w_mem, x)


def _bandattn_body(q_ref, kp_ref, kc_ref, vp_ref, vc_ref, bias_ref, o_ref, *, tq):
    qi = pl.program_id(2)
    scale = HEAD_DIM ** -0.5
    k_all = jnp.concatenate([kp_ref[...], kc_ref[...]], axis=0).astype(BF16)
    v_all = jnp.concatenate([vp_ref[...], vc_ref[...]], axis=0).astype(BF16)
    lane = lax.broadcasted_iota(jnp.int32, (CHUNK, PAIR), 1)
    head0 = lane < HEAD_DIM
    kcol = lax.broadcasted_iota(jnp.int32, (CHUNK, BAND), 1)
    pad = LEFT_CHUNKS * CHUNK
    for j in range(tq // CHUNK):
        q = q_ref[j * CHUNK:(j + 1) * CHUNK, :]
        start = tq - pad + j * CHUNK
        kb = k_all[start:start + BAND]
        vb = v_all[start:start + BAND]
        valid = jnp.logical_or(qi > 0, kcol + start >= tq)
        outs = []
        for hh in range(2):
            qm = jnp.where(head0 if hh == 0 else ~head0, q, 0.0).astype(BF16)
            s = _dot_nt(qm, kb) * scale + bias_ref[hh]
            s = jnp.where(valid, s, -jnp.inf)
            e = jnp.exp(s - jnp.max(s, axis=-1, keepdims=True))
            p = e / jnp.sum(e, axis=-1, keepdims=True)
            outs.append(_dot(p.astype(BF16), vb))
        o_ref[j * CHUNK:(j + 1) * CHUNK, :] = jnp.where(head0, outs[0], outs[1]).astype(o_ref.dtype)


def band_attention(proj, kv, bias, n_pairs, seq, tq=512):
    n = proj.shape[0]
    tq = min(tq, seq)
    assert tq >= LEFT_CHUNKS * CHUNK
    nblk = seq // tq
    cur = lambda off: pl.BlockSpec((tq, PAIR), lambda b, p, t: (b * nblk + t, off + p))
    prev = lambda off: pl.BlockSpec((tq, PAIR), lambda b, p, t: (b * nblk + jnp.maximum(t - 1, 0), off + p))
    return pl.pallas_call(
        functools.partial(_bandattn_body, tq=tq),
        grid=(n // seq, n_pairs, nblk),
        in_specs=[cur(0), prev(0), cur(0), prev(n_pairs), cur(n_pairs),
                  pl.BlockSpec((2, CHUNK, BAND), lambda b, p, t: (p, 0, 0))],
        out_specs=cur(0),
        out_shape=jax.ShapeDtypeStruct((n, n_pairs * PAIR), BF16),
        compiler_params=_params("parallel", "parallel", "parallel"),
        name="band_attention",
    )(proj, kv, kv, kv, kv, bias)


def _top16(s, payload=None):
    nrows, nl = s.shape
    rid = lax.broadcasted_iota(jnp.int32, (nrows, nl), 0).astype(F32)
    slot = lax.broadcasted_iota(jnp.int32, (PEER_TOPK, nl), 0)

    def body(i, carry):
        s, vals, picks = carry
        m = jnp.max(s, axis=0, keepdims=True)
        am = jnp.min(jnp.where(s == m, rid, float(nrows)), axis=0, keepdims=True)
        hit = rid == am
        if payload is None:
            pick = am
        else:
            pick = jnp.sum(jnp.where(hit, payload, 0.0), axis=0, keepdims=True)
        s = jnp.where(hit, -jnp.inf, s)
        sel = slot == i
        return s, jnp.where(sel, m, vals), jnp.where(sel, pick, picks)

    zero = jnp.zeros((PEER_TOPK, nl), F32)
    _, vals, picks = lax.fori_loop(0, PEER_TOPK, body, (s, zero, zero))
    return vals, picks


def _peer_topk_body(q_ref, keys_ref, eidx_ref, gate_ref):
    q = q_ref[...].astype(BF16)
    s1 = _dot_nt(keys_ref[0, 0].astype(BF16), q[:, :PEER_KEYS])
    s2 = _dot_nt(keys_ref[0, 1].astype(BF16), q[:, PEER_KEYS:])
    a, i1 = _top16(s1)
    b, i2 = _top16(s2)
    cand = jnp.concatenate([a[i:i + 1] + b for i in range(PEER_TOPK)], axis=0)
    cidx = jnp.concatenate([i1[i:i + 1] * PEER_KEYS + i2 for i in range(PEER_TOPK)], axis=0)
    top, eidx = _top16(cand, cidx)
    e = jnp.exp(top - top[0:1])
    gate_ref[...] = e / jnp.sum(e, axis=0, keepdims=True)
    eidx_ref[...] = eidx.astype(jnp.int32)


def peer_topk(q, keys, tl=128):
    n = q.shape[0]
    tl = min(tl, n)
    blk = pl.BlockSpec((PEER_TOPK, tl), lambda i, h: (h, i))
    return pl.pallas_call(
        _peer_topk_body,
        grid=(n // tl, PEER_HEADS),
        in_specs=[pl.BlockSpec((tl, 2 * PEER_KEYS), lambda i, h: (i, h)),
                  pl.BlockSpec((1, 2, PEER_KEYS, PEER_KEYS), lambda i, h: (h, 0, 0, 0))],
        out_specs=[blk, blk],
        out_shape=[jax.ShapeDtypeStruct((PEER_PICKS, n), jnp.int32),
                   jax.ShapeDtypeStruct((PEER_PICKS, n), F32)],
        compiler_params=_params("parallel", "parallel"),
        name="peer_topk",
    )(q, keys)


PEER_SLOTS = 4


def _peer_ffn_body(eidx_ref, gate_ref, hn_ref, x_ref, tab_ref, o_ref, rows_ref, sem_ref, *, tb, d):
    lane_chunks = d // 128

    def row_copy(row, slot, e):
        return pltpu.make_async_copy(tab_ref.at[pl.ds(row, 1)], rows_ref.at[slot, pl.ds(e, 1)],
                                     sem_ref.at[slot])

    def issue(t, slot):
        for e in range(PEER_PICKS):
            row_copy(eidx_ref[e, t], slot, e).start()

    def wait(slot):
        for e in range(PEER_PICKS):
            row_copy(0, slot, e).wait()

    tok_lane = lax.broadcasted_iota(jnp.int32, (PEER_PICKS, tb), 1)

    def compute(t, slot):
        xt = hn_ref[pl.ds(t, 1), :]
        acc = jnp.zeros((PEER_PICKS, 128), F32)
        for c in range(lane_chunks):
            sl = slice(c * 128, (c + 1) * 128)
            acc = acc + rows_ref[slot, :, sl] * xt[:, sl]
        act = jnp.sum(acc, axis=1, keepdims=True)
        act = 0.5 * act * (1.0 + lax.erf(act * (2.0 ** -0.5)))
        gate = jnp.sum(jnp.where(tok_lane == t, gate_ref[...], 0.0), axis=1, keepdims=True)
        w = jnp.broadcast_to(gate * act, (PEER_PICKS, 128))
        outs = [jnp.sum(rows_ref[slot, :, d + c * 128:d + (c + 1) * 128] * w, axis=0, keepdims=True)
                for c in range(lane_chunks)]
        o_ref[pl.ds(t, 1), :] = x_ref[pl.ds(t, 1), :] + jnp.concatenate(outs, axis=1)

    ahead = PEER_SLOTS - 1
    for t in range(ahead):
        issue(t, t)

    def steady(t, carry):
        issue(t + ahead, (t + ahead) % PEER_SLOTS)
        wait(t % PEER_SLOTS)
        compute(t, t % PEER_SLOTS)
        return carry

    lax.fori_loop(0, tb - ahead, steady, 0)

    def drain(t, carry):
        wait(t % PEER_SLOTS)
        compute(t, t % PEER_SLOTS)
        return carry

    lax.fori_loop(tb - ahead, tb, drain, 0)


def peer_expert_ffn(eidx, gate, hn, x, table, tb=128):
    n, d = x.shape
    tb = min(tb, n)
    return pl.pallas_call(
        functools.partial(_peer_ffn_body, tb=tb, d=d),
        grid=(n // tb,),
        in_specs=[pl.BlockSpec((PEER_PICKS, tb), lambda i: (0, i), memory_space=pltpu.SMEM),
                  pl.BlockSpec((PEER_PICKS, tb), lambda i: (0, i)),
                  pl.BlockSpec((tb, d), lambda i: (i, 0)),
                  pl.BlockSpec((tb, d), lambda i: (i, 0)),
                  pl.BlockSpec(memory_space=pl.ANY)],
        out_specs=pl.BlockSpec((tb, d), lambda i: (i, 0)),
        out_shape=jax.ShapeDtypeStruct((n, d), F32),
        scratch_shapes=[pltpu.VMEM((PEER_SLOTS, PEER_PICKS, 2 * d), F32),
                        pltpu.SemaphoreType.DMA((PEER_SLOTS,))],
        compiler_params=_params("arbitrary"),
        name="peer_expert_ffn",
    )(eidx, gate, hn, x, table)


def peer_layer(x, norm_g, wq, keys, u, v):
    q, hn = norm_matmul(x, norm_g, wq.astype(BF16), emit_hn=True)
    eidx, gate = peer_topk(q, keys)
    table = jnp.concatenate([u, v], axis=1)
    return peer_expert_ffn(eidx, gate, hn, x, table)


def _band_bias(rel_bias):
    pad = LEFT_CHUNKS * CHUNK
    dist = pad + np.arange(CHUNK)[:, None] - np.arange(BAND)[None, :]
    idx = np.clip(dist, -(CHUNK - 1), REL_MAX) + (CHUNK - 1)
    return rel_bias[:, idx].astype(F32)


def kernel(x, mem, norm_mix, norm_ffn, norm_mem, w_mem_kv, w_out, peer_wq, peer_keys, peer_u, peer_v, a_mix, a_w_in, a_w0, a_w1, a_w2, a_a0, a_a1, a_a2, a_g1, a_g2, a_k_k, a_k_a, a_r_k, a_ln_g, a_ln_b, kv_norm, w_kv_shared, b_w_in, b_rel_bias, final_norm):
    bsz, seq, d = x.shape
    n = bsz * seq
    n_mem = mem.shape[1]
    seq_width = a_w0.shape[1]
    n_pairs = seq_width // PAIR
    mem_width = MEM_HEADS * MEM_HEAD_DIM
    x = x.reshape(n, d)
    mem2 = mem.reshape(bsz * n_mem, d)

    def mixer_tail(x, seq_out, proj, q_col_block, layer):
        mkv = norm_matmul(mem2, norm_mem[layer], w_mem_kv[layer].astype(BF16), tm=256)
        mem_out = memory_attention(proj, q_col_block, mkv, seq, n_mem)
        wo = w_out[layer].astype(BF16)
        x = out_projection(seq_out, mem_out, wo[:seq_width], wo[seq_width:], x)
        return peer_layer(x, norm_ffn[layer], peer_wq[layer], peer_keys[layer], peer_u[layer], peer_v[layer])

    h = rmsnorm(x, norm_mix[0])
    mix_tab = jnp.concatenate([a_mix[0], jnp.zeros((1, d), F32)], axis=0)
    tiles = seq_width // 512
    gid_main = jnp.asarray([0] * tiles + [2] * tiles + [3] * tiles + [6] * (mem_width // 512), jnp.int32)
    proj = mix_matmul(h, mix_tab, gid_main, a_w_in[0].astype(BF16), seq, tn=512)
    rank = a_w1.shape[2]
    padc = lambda w: jnp.pad(w, ((0, 0), (0, LORA_PAD - rank)))
    padr = lambda w: jnp.pad(w, ((0, LORA_PAD - rank), (0, 0)))
    w_l1 = jnp.concatenate([padc(a_w1[0]), padc(a_a1[0]), a_g1[0]], axis=1).astype(BF16)
    gid_l1 = jnp.asarray([1, 4] + [5] * (a_g1.shape[2] // LORA_PAD), jnp.int32)
    t1 = mix_matmul(h, mix_tab, gid_l1, w_l1, seq, tn=LORA_PAD)
    lw, a_iclr, gate = lora_stage2(t1, padr(a_w2[0]).astype(BF16), padr(a_a2[0]).astype(BF16),
                                   a_g2[0].astype(BF16), a_w0[0], a_a0[0])
    prep = wkv_chunk_prepare(proj, lw, a_iclr, a_k_k[0], a_k_a[0], n_pairs)
    seq_out = wkv_chunk_scan(prep, proj, a_iclr, gate, a_k_a[0], a_r_k[0].reshape(-1),
                             a_ln_g[0], a_ln_b[0], n_pairs, seq)
    x = mixer_tail(x, seq_out, proj, (3 * seq_width) // mem_width, 0)

    kv = norm_matmul(x, kv_norm, w_kv_shared.astype(BF16))

    proj = norm_matmul(x, norm_mix[1], b_w_in[0].astype(BF16))
    seq_out = band_attention(proj, kv, _band_bias(b_rel_bias[0]), n_pairs, seq)
    x = mixer_tail(x, seq_out, proj, seq_width // mem_width, 1)

    return rmsnorm(x, final_norm).reshape(bsz, seq, d)
```

```python
import functools

import numpy as np
import jax
import jax.numpy as jnp
from jax import lax
from jax.experimental import pallas as pl
from jax.experimental.pallas import tpu as pltpu

F32 = jnp.float32
BF16 = jnp.bfloat16
HI = lax.Precision.HIGHEST

HEAD_DIM = 64
PAIR = 2 * HEAD_DIM
CHUNK = 64
LEFT_CHUNKS = 8
BAND = (LEFT_CHUNKS + 1) * CHUNK
REL_MAX = 128
MEM_HEADS = 4
MEM_HEAD_DIM = 128
PEER_KEYS = 128
PEER_HEADS = 8
PEER_TOPK = 16
PEER_PICKS = PEER_HEADS * PEER_TOPK
GN_EPS = 64e-5
RMS_EPS = 1e-6
LORA_PAD = 128
VMEM_LIMIT = 48 * 1024 * 1024


def _params(*sem):
    return pltpu.CompilerParams(dimension_semantics=sem, vmem_limit_bytes=VMEM_LIMIT)


def _dot(a, b):
    return jnp.dot(a, b, preferred_element_type=F32)


def _dot_nt(a, b, precision=None):
    return lax.dot_general(a, b, (((1,), (1,)), ((), ())), precision=precision,
                           preferred_element_type=F32)


def _dot_tn(a, b, precision=None):
    return lax.dot_general(a, b, (((0,), (0,)), ((), ())), precision=precision,
                           preferred_element_type=F32)


def _rmsnorm_body(x_ref, g_ref, o_ref):
    x = x_ref[...]
    ms = jnp.mean(x * x, axis=-1, keepdims=True)
    o_ref[...] = (x * lax.rsqrt(ms + RMS_EPS) * g_ref[...]).astype(o_ref.dtype)


def rmsnorm(x, g, tm=512):
    n, d = x.shape
    tm = min(tm, n)
    return pl.pallas_call(
        _rmsnorm_body,
        grid=(n // tm,),
        in_specs=[pl.BlockSpec((tm, d), lambda i: (i, 0)), pl.BlockSpec((1, d), lambda i: (0, 0))],
        out_specs=pl.BlockSpec((tm, d), lambda i: (i, 0)),
        out_shape=jax.ShapeDtypeStruct((n, d), F32),
        compiler_params=_params("parallel"),
        name="rmsnorm",
    )(x, g.reshape(1, d))


def _normmm_body(x_ref, g_ref, w_ref, o_ref, *rest, emit_hn):
    lhs_ref = rest[-1]

    @pl.when(pl.program_id(1) == 0)
    def _():
        x = x_ref[...]
        ms = jnp.mean(x * x, axis=-1, keepdims=True)
        hn = x * lax.rsqrt(ms + RMS_EPS) * g_ref[...]
        lhs_ref[...] = hn.astype(BF16)
        if emit_hn:
            rest[0][...] = hn

    o_ref[...] = _dot(lhs_ref[...], w_ref[...])


def norm_matmul(x, g, w_bf16, emit_hn=False, tm=512, tn=512):
    n, d = x.shape
    nc = w_bf16.shape[1]
    tm, tn = min(tm, n), min(tn, nc)
    out_shape = [jax.ShapeDtypeStruct((n, nc), F32)]
    out_specs = [pl.BlockSpec((tm, tn), lambda i, j: (i, j))]
    if emit_hn:
        out_shape.append(jax.ShapeDtypeStruct((n, d), F32))
        out_specs.append(pl.BlockSpec((tm, d), lambda i, j: (i, 0)))
    res = pl.pallas_call(
        functools.partial(_normmm_body, emit_hn=emit_hn),
        grid=(n // tm, nc // tn),
        in_specs=[pl.BlockSpec((tm, d), lambda i, j: (i, 0)),
                  pl.BlockSpec((1, d), lambda i, j: (0, 0)),
                  pl.BlockSpec((d, tn), lambda i, j: (0, j))],
        out_specs=out_specs,
        out_shape=out_shape,
        scratch_shapes=[pltpu.VMEM((tm, d), BF16)],
        compiler_params=_params("parallel", "arbitrary"),
        name="norm_matmul",
    )(x, g.reshape(1, d), w_bf16)
    return res if emit_hn else res[0]


def _mixmm_body(gid_ref, h_ref, hp_ref, mix_ref, w_ref, o_ref, lhs_ref, *, tm, seq):
    i = pl.program_id(0)
    j = pl.program_id(1)
    new_group = jnp.logical_or(j == 0, gid_ref[j] != gid_ref[jnp.maximum(j - 1, 0)])

    @pl.when(new_group)
    def _():
        h = h_ref[...]
        prev = jnp.where((i * tm) % seq == 0, 0.0, hp_ref[7:8, :])
        row = lax.broadcasted_iota(jnp.int32, h.shape, 0)
        shifted = jnp.where(row == 0, prev, pltpu.roll(h, 1, 0))
        lhs_ref[...] = (h + (shifted - h) * mix_ref[0]).astype(BF16)

    o_ref[...] = _dot(lhs_ref[...], w_ref[...])


def mix_matmul(h, mix_tab, gid, w_bf16, seq, tn, tm=512):
    n, d = h.shape
    nc = w_bf16.shape[1]
    tm = min(tm, seq)
    sub = tm // 8
    grid_spec = pltpu.PrefetchScalarGridSpec(
        num_scalar_prefetch=1,
        grid=(n // tm, nc // tn),
        in_specs=[pl.BlockSpec((tm, d), lambda i, j, g: (i, 0)),
                  pl.BlockSpec((8, d), lambda i, j, g: (jnp.maximum(i * sub - 1, 0), 0)),
                  pl.BlockSpec((1, 1, d), lambda i, j, g: (g[j], 0, 0)),
                  pl.BlockSpec((d, tn), lambda i, j, g: (0, j))],
        out_specs=pl.BlockSpec((tm, tn), lambda i, j, g: (i, j)),
        scratch_shapes=[pltpu.VMEM((tm, d), BF16)],
    )
    return pl.pallas_call(
        functools.partial(_mixmm_body, tm=tm, seq=seq),
        grid_spec=grid_spec,
        out_shape=jax.ShapeDtypeStruct((n, nc), F32),
        compiler_params=_params("parallel", "arbitrary"),
        name="mix_matmul",
    )(gid, h, h, mix_tab.reshape(mix_tab.shape[0], 1, d), w_bf16)


def _sigmoid(x):
    return 1.0 / (1.0 + jnp.exp(-x))


def _lora2_body(t_ref, w2_ref, a2_ref, g2_ref, w0_ref, a0_ref, lw_ref, a_ref, g_ref):
    t = t_ref[...]
    tw = jnp.tanh(t[:, :LORA_PAD]).astype(BF16)
    ta = t[:, LORA_PAD:2 * LORA_PAD].astype(BF16)
    tg = _sigmoid(t[:, 2 * LORA_PAD:]).astype(BF16)
    u = w0_ref[...] + _dot(tw, w2_ref[...])
    softplus_neg_u = jnp.maximum(-u, 0.0) + jnp.log(1.0 + jnp.exp(-jnp.abs(u)))
    lw_ref[...] = -jnp.exp(-softplus_neg_u - 0.5)
    a_ref[...] = _sigmoid(a0_ref[...] + _dot(ta, a2_ref[...]))
    g_ref[...] = _dot(tg, g2_ref[...])


def lora_stage2(t1, w2p, a2p, g2, w0, a0, tm=256):
    n = t1.shape[0]
    width = w2p.shape[1]
    tm = min(tm, n)
    full = lambda a: pl.BlockSpec(a.shape, lambda i: (0, 0))
    row = pl.BlockSpec((tm, width), lambda i: (i, 0))
    w0 = w0.reshape(1, width)
    a0 = a0.reshape(1, width)
    return pl.pallas_call(
        _lora2_body,
        grid=(n // tm,),
        in_specs=[pl.BlockSpec((tm, t1.shape[1]), lambda i: (i, 0)), full(w2p), full(a2p), full(g2),
                  full(w0), full(a0)],
        out_specs=[row, row, row],
        out_shape=[jax.ShapeDtypeStruct((n, width), F32)] * 3,
        compiler_params=_params("parallel"),
        name="lora_stage2",
    )(t1, w2p, a2p, g2, w0, a0)


def _head_group_sum(x):
    r = lax.broadcasted_iota(jnp.int32, (PAIR, PAIR), 0) // HEAD_DIM
    c = lax.broadcasted_iota(jnp.int32, (PAIR, PAIR), 1) // HEAD_DIM
    ones = jnp.where(r == c, 1.0, 0.0).astype(F32)
    return jnp.dot(x, ones, precision=HI, preferred_element_type=F32)


def _wkv_prepare_body(r_ref, k_ref, v_ref, lw_ref, a_ref, kk_ref, ka_ref,
                      g_ref, s0c_ref, rp_ref, y0_ref, *, tb):
    r = r_ref[...]
    k = k_ref[...]
    v = v_ref[...]
    lw = lw_ref[...]
    a = a_ref[...]
    kk = k * kk_ref[...]
    norm = jnp.sqrt(_head_group_sum(kk * kk))
    kk = kk / jnp.maximum(norm, 1e-12)
    kmod = k * (1.0 + (a - 1.0) * ka_ref[...])
    avec = -kk
    bvec = kk * a

    tr = lax.broadcasted_iota(jnp.int32, (tb, tb), 0)
    tc = lax.broadcasted_iota(jnp.int32, (tb, tb), 1)
    tri = jnp.where((tr >= tc) & (tr // CHUNK == tc // CHUNK), 1.0, 0.0).astype(F32)
    cum = jnp.dot(tri, lw, precision=HI, preferred_element_type=F32)

    lane = lax.broadcasted_iota(jnp.int32, (CHUNK, PAIR), 1)
    head0 = lane < HEAD_DIM
    row = lax.broadcasted_iota(jnp.int32, (PAIR, PAIR), 0)
    col = lax.broadcasted_iota(jnp.int32, (PAIR, PAIR), 1)
    strict = row > col
    lower = row >= col
    eye = jnp.where(row == col, 1.0, 0.0).astype(F32)

    def stack(x):
        return jnp.concatenate([jnp.where(head0, x, 0.0), jnp.where(head0, 0.0, x)], axis=0)

    for c in range(tb // CHUNK):
        sl = slice(c * CHUNK, (c + 1) * CHUNK)
        cm = cum[sl]
        cend = cm[CHUNK - 1:CHUNK]
        e_in = jnp.exp(cm)
        e_out = jnp.exp(-cm)
        e_tail = jnp.exp(cend - cm)
        a_s = stack(avec[sl] * jnp.exp(cm - lw[sl]))
        r_s = stack(r[sl] * e_in)
        b_s = stack(bvec[sl] * e_out)
        k_s = stack(kmod[sl] * e_out)
        bt_s = stack(bvec[sl] * e_tail)
        kt_s = stack(kmod[sl] * e_tail)
        v_s = stack(v[sl])

        p = _dot_nt(jnp.concatenate([a_s, r_s], axis=0).astype(BF16),
                    jnp.concatenate([b_s, k_s], axis=0).astype(BF16))
        l_ab = jnp.where(strict, p[:PAIR, :PAIR], 0.0)
        l_ak = jnp.where(strict, p[:PAIR, PAIR:], 0.0)
        a_rb = jnp.where(lower, p[PAIR:, :PAIR], 0.0)
        a_rk = jnp.where(lower, p[PAIR:, PAIR:], 0.0)

        m = l_ab
        t = eye + m
        for _ in range(5):
            m_b = m.astype(BF16)
            m = _dot(m_b, m_b)
            t = t + _dot(t.astype(BF16), m.astype(BF16))

        v_b = v_s.astype(BF16)
        lv = _dot(l_ak.astype(BF16), v_b)
        au = _dot(t.astype(BF16), jnp.concatenate([a_s, lv], axis=1).astype(BF16))
        x = _dot(a_rb.astype(BF16), au.astype(BF16))
        rp_ref[0, c] = r_s + x[:, :PAIR]
        y0_ref[0, c] = x[:, PAIR:] + _dot(a_rk.astype(BF16), v_b)
        ap = au[:, :PAIR]
        u0 = au[:, PAIR:]
        g_ref[0, c] = eye * jnp.exp(cend) + _dot_tn(ap.astype(BF16), bt_s.astype(BF16))
        s0c_ref[0, c] = _dot_tn(jnp.concatenate([u0, v_s], axis=0).astype(BF16),
                                jnp.concatenate([bt_s, kt_s], axis=0).astype(BF16))


def wkv_chunk_prepare(proj, lw, a, k_k, k_a, n_pairs, tb=256):
    n = proj.shape[0]
    tb = min(tb, n)
    cpb = tb // CHUNK
    col = lambda off: pl.BlockSpec((tb, PAIR), lambda i, p: (i, off + p))
    par = pl.BlockSpec((1, PAIR), lambda i, p: (0, p))
    blk = pl.BlockSpec((1, cpb, PAIR, PAIR), lambda i, p: (p, i, 0, 0))
    shp = jax.ShapeDtypeStruct((n_pairs, n // CHUNK, PAIR, PAIR), F32)
    return pl.pallas_call(
        functools.partial(_wkv_prepare_body, tb=tb),
        grid=(n // tb, n_pairs),
        in_specs=[col(0), col(n_pairs), col(2 * n_pairs), col(0), col(0), par, par],
        out_specs=[blk] * 4,
        out_shape=[shp] * 4,
        compiler_params=_params("parallel", "parallel"),
        name="wkv_chunk_prepare",
    )(proj, proj, proj, lw, a, k_k.reshape(1, -1), k_a.reshape(1, -1))


def _wkv_scan_body(g_ref, s0c_ref, rp_ref, y0_ref, r_ref, k_ref, v_ref, a_ref, gate_ref,
                   ka_ref, rk_ref, lng_ref, lnb_ref, o_ref, s_ref, y_ref, *, cpb):
    @pl.when(pl.program_id(2) == 0)
    def _():
        s_ref[...] = jnp.zeros_like(s_ref)

    s = s_ref[...]
    for c in range(cpb):
        s_b = s.astype(BF16)
        y_st = _dot_nt(rp_ref[0, c].astype(BF16), s_b) + y0_ref[0, c]
        y_ref[c * CHUNK:(c + 1) * CHUNK, :] = y_st[:CHUNK] + y_st[CHUNK:]
        s = _dot(s_b, g_ref[0, c].astype(BF16)) + s0c_ref[0, c]
    s_ref[...] = s

    y = y_ref[...]
    inv = 1.0 / HEAD_DIM
    mu = _head_group_sum(y) * inv
    yc = y - mu
    var = _head_group_sum(yc * yc) * inv
    yn = yc * lax.rsqrt(var + GN_EPS) * lng_ref[...] + lnb_ref[...]
    r = r_ref[...]
    kmod = k_ref[...] * (1.0 + (a_ref[...] - 1.0) * ka_ref[...])
    bonus = _head_group_sum(r * kmod * rk_ref[...]) * v_ref[...]
    o_ref[...] = ((yn + bonus) * gate_ref[...]).astype(o_ref.dtype)


def wkv_chunk_scan(prep, proj, a, gate, k_a, r_k, ln_g, ln_b, n_pairs, seq, tb=512):
    g_all, s0c_all, rp_all, y0_all = prep
    n = proj.shape[0]
    tb = min(tb, seq)
    cpb = tb // CHUNK
    nblk = seq // tb
    blk = pl.BlockSpec((1, cpb, PAIR, PAIR), lambda b, p, t: (p, b * nblk + t, 0, 0))
    col = lambda off: pl.BlockSpec((tb, PAIR), lambda b, p, t: (b * nblk + t, off + p))
    par = pl.BlockSpec((1, PAIR), lambda b, p, t: (0, p))
    vec = lambda z: z.reshape(1, -1)
    return pl.pallas_call(
        functools.partial(_wkv_scan_body, cpb=cpb),
        grid=(n // seq, n_pairs, nblk),
        in_specs=[blk] * 4 + [col(0), col(n_pairs), col(2 * n_pairs), col(0), col(0)] + [par] * 4,
        out_specs=col(0),
        out_shape=jax.ShapeDtypeStruct((n, n_pairs * PAIR), BF16),
        scratch_shapes=[pltpu.VMEM((PAIR, PAIR), F32), pltpu.VMEM((tb, PAIR), F32)],
        compiler_params=_params("parallel", "parallel", "arbitrary"),
        name="wkv_chunk_scan",
    )(g_all, s0c_all, rp_all, y0_all, proj, proj, proj, a, gate,
      vec(k_a), vec(r_k), vec(ln_g), vec(ln_b))


def _memattn_body(q_ref, m_ref, o_ref):
    width = MEM_HEADS * MEM_HEAD_DIM
    scale = MEM_HEAD_DIM ** -0.5
    for h in range(MEM_HEADS):
        sl = slice(h * MEM_HEAD_DIM, (h + 1) * MEM_HEAD_DIM)
        q = q_ref[:, sl].astype(BF16)
        mk = m_ref[:, sl].astype(BF16)
        mv = m_ref[:, width + h * MEM_HEAD_DIM:width + (h + 1) * MEM_HEAD_DIM].astype(BF16)
        s = _dot_nt(q, mk) * scale
        e = jnp.exp(s - jnp.max(s, axis=-1, keepdims=True))
        p = e / jnp.sum(e, axis=-1, keepdims=True)
        o_ref[:, sl] = _dot(p.astype(BF16), mv).astype(o_ref.dtype)


def memory_attention(proj, q_col_block, mkv, seq, n_mem, tm=512):
    n = proj.shape[0]
    width = MEM_HEADS * MEM_HEAD_DIM
    tm = min(tm, seq)
    nblk = seq // tm
    return pl.pallas_call(
        _memattn_body,
        grid=(n // seq, nblk),
        in_specs=[pl.BlockSpec((tm, width), lambda b, t: (b * nblk + t, q_col_block)),
                  pl.BlockSpec((n_mem, 2 * width), lambda b, t: (b, 0))],
        out_specs=pl.BlockSpec((tm, width), lambda b, t: (b * nblk + t, 0)),
        out_shape=jax.ShapeDtypeStruct((n, width), BF16),
        compiler_params=_params("parallel", "parallel"),
        name="memory_attention",
    )(proj, mkv)


def _outproj_body(s_ref, m_ref, w1_ref, w2_ref, x_ref, o_ref):
    o_ref[...] = x_ref[...] + _dot(s_ref[...], w1_ref[...]) + _dot(m_ref[...], w2_ref[...])


def out_projection(seq_out, mem_out, w_seq, w_mem, x, tm=512, tn=512):
    n, d = x.shape
    tm, tn = min(tm, n), min(tn, d)
    ws, wm = seq_out.shape[1], mem_out.shape[1]
    return pl.pallas_call(
        _outproj_body,
        grid=(n // tm, d // tn),
        in_specs=[pl.BlockSpec((tm, ws), lambda i, j: (i, 0)),
                  pl.BlockSpec((tm, wm), lambda i, j: (i, 0)),
                  pl.BlockSpec((ws, tn), lambda i, j: (0, j)),
                  pl.BlockSpec((wm, tn), lambda i, j: (0, j)),
                  pl.BlockSpec((tm, tn), lambda i, j: (i, j))],
        out_specs=pl.BlockSpec((tm, tn), lambda i, j: (i, j)),
        out_shape=jax.ShapeDtypeStruct((n, d), F32),
        compiler_params=_params("parallel", "parallel"),
        name="out_projection",
    )(seq_out, mem_out, w_seq, w_mem, x)


def _bandattn_body(q_ref, kp_ref, kc_ref, vp_ref, vc_ref, bias_ref, o_ref, *, tq):
    qi = pl.program_id(2)
    scale = HEAD_DIM ** -0.5
    k_all = jnp.concatenate([kp_ref[...], kc_ref[...]], axis=0).astype(BF16)
    v_all = jnp.concatenate([vp_ref[...], vc_ref[...]], axis=0).astype(BF16)
    lane = lax.broadcasted_iota(jnp.int32, (CHUNK, PAIR), 1)
    head0 = lane < HEAD_DIM
    kcol = lax.broadcasted_iota(jnp.int32, (CHUNK, BAND), 1)
    pad = LEFT_CHUNKS * CHUNK
    for j in range(tq // CHUNK):
        q = q_ref[j * CHUNK:(j + 1) * CHUNK, :]
        start = tq - pad + j * CHUNK
        kb = k_all[start:start + BAND]
        vb = v_all[start:start + BAND]
        valid = jnp.logical_or(qi > 0, kcol + start >= tq)
        outs = []
        for hh in range(2):
            qm = jnp.where(head0 if hh == 0 else ~head0, q, 0.0).astype(BF16)
            s = _dot_nt(qm, kb) * scale + bias_ref[hh]
            s = jnp.where(valid, s, -jnp.inf)
            e = jnp.exp(s - jnp.max(s, axis=-1, keepdims=True))
            p = e / jnp.sum(e, axis=-1, keepdims=True)
            outs.append(_dot(p.astype(BF16), vb))
        o_ref[j * CHUNK:(j + 1) * CHUNK, :] = jnp.where(head0, outs[0], outs[1]).astype(o_ref.dtype)


def band_attention(proj, kv, bias, n_pairs, seq, tq=512):
    n = proj.shape[0]
    tq = min(tq, seq)
    assert tq >= LEFT_CHUNKS * CHUNK
    nblk = seq // tq
    cur = lambda off: pl.BlockSpec((tq, PAIR), lambda b, p, t: (b * nblk + t, off + p))
    prev = lambda off: pl.BlockSpec((tq, PAIR), lambda b, p, t: (b * nblk + jnp.maximum(t - 1, 0), off + p))
    return pl.pallas_call(
        functools.partial(_bandattn_body, tq=tq),
        grid=(n // seq, n_pairs, nblk),
        in_specs=[cur(0), prev(0), cur(0), prev(n_pairs), cur(n_pairs),
                  pl.BlockSpec((2, CHUNK, BAND), lambda b, p, t: (p, 0, 0))],
        out_specs=cur(0),
        out_shape=jax.ShapeDtypeStruct((n, n_pairs * PAIR), BF16),
        compiler_params=_params("parallel", "parallel", "parallel"),
        name="band_attention",
    )(proj, kv, kv, kv, kv, bias)


def _top16(scores, payloads):
    nl = scores[0].shape[1]
    rids = [lax.broadcasted_iota(jnp.int32, s.shape, 0).astype(F32) for s in scores]
    slot = lax.broadcasted_iota(jnp.int32, (PEER_TOPK, nl), 0)

    def body(i, carry):
        sel = slot == i
        out = []
        for (s, vals, picks), rid, payload in zip(carry, rids, payloads):
            m = jnp.max(s, axis=0, keepdims=True)
            am = jnp.min(jnp.where(s == m, rid, float(s.shape[0])), axis=0, keepdims=True)
            hit = rid == am
            if payload is None:
                pick = am
            else:
                pick = jnp.sum(jnp.where(hit, payload, 0.0), axis=0, keepdims=True)
            out.append((jnp.where(hit, -jnp.inf, s), jnp.where(sel, m, vals), jnp.where(sel, pick, picks)))
        return tuple(out)

    zero = jnp.zeros((PEER_TOPK, nl), F32)
    res = lax.fori_loop(0, PEER_TOPK, body, tuple((s, zero, zero) for s in scores))
    return [(vals, picks) for _, vals, picks in res]


def _pair_candidates(a, b, combine):
    half = PEER_TOPK // 2
    rows = [combine(a[0:1], b)]
    rows += [combine(a[i:i + 1], b[:half]) for i in range(1, half)]
    rows.append(combine(a[half:], b[0:1]))
    return jnp.concatenate(rows, axis=0)


def _peer_topk_body(q_ref, keys_ref, eidx_ref, gate_ref):
    q = q_ref[...].astype(BF16)
    s1 = _dot_nt(keys_ref[0, 0].astype(BF16), q[:, :PEER_KEYS])
    s2 = _dot_nt(keys_ref[0, 1].astype(BF16), q[:, PEER_KEYS:])
    (a, i1), (b, i2) = _top16([s1, s2], [None, None])
    cand = _pair_candidates(a, b, lambda x, y: x + y)
    cidx = _pair_candidates(i1, i2, lambda x, y: x * PEER_KEYS + y)
    ((top, eidx),) = _top16([cand], [cidx])
    e = jnp.exp(top - top[0:1])
    gate_ref[...] = e / jnp.sum(e, axis=0, keepdims=True)
    eidx_ref[...] = eidx.astype(jnp.int32)


def peer_topk(q, keys, tl=128):
    n = q.shape[0]
    tl = min(tl, n)
    blk = pl.BlockSpec((PEER_TOPK, tl), lambda i, h: (h, i))
    return pl.pallas_call(
        _peer_topk_body,
        grid=(n // tl, PEER_HEADS),
        in_specs=[pl.BlockSpec((tl, 2 * PEER_KEYS), lambda i, h: (i, h)),
                  pl.BlockSpec((1, 2, PEER_KEYS, PEER_KEYS), lambda i, h: (h, 0, 0, 0))],
        out_specs=[blk, blk],
        out_shape=[jax.ShapeDtypeStruct((PEER_PICKS, n), jnp.int32),
                   jax.ShapeDtypeStruct((PEER_PICKS, n), F32)],
        compiler_params=_params("parallel", "parallel"),
        name="peer_topk",
    )(q, keys)


PEER_SLOTS = 8
LANES = 128
PITCH_PAD = 4


def _pack_expert_table(u, v):
    n_exp, d = u.shape
    hi = lax.bitcast_convert_type(u.astype(BF16), jnp.uint16).astype(jnp.uint32)
    lo = lax.bitcast_convert_type(v.astype(BF16), jnp.uint16).astype(jnp.uint32)
    return ((hi << 16) | lo).reshape(n_exp * (d // LANES), LANES)


def _peer_ffn_body(eidx_ref, gate_ref, hn_ref, x_ref, tab_ref, o_ref, *scratch, tb, d):
    rows_refs, sem_ref = scratch[:PEER_SLOTS], scratch[PEER_SLOTS]
    chunks = d // LANES
    pitch = chunks + PITCH_PAD

    def issue(t, slot):
        for e in range(PEER_PICKS):
            src = tab_ref.at[pl.ds(pl.multiple_of(eidx_ref[t, e] * chunks, chunks), chunks), :]
            dst = rows_refs[slot].at[pl.ds(e * pitch, chunks), :]
            pltpu.make_async_copy(src, dst, sem_ref.at[slot]).start()

    def wait(slot):
        total = PEER_PICKS * chunks
        pltpu.make_async_copy(tab_ref.at[pl.ds(0, total), :],
                              rows_refs[slot].at[pl.ds(0, total), :], sem_ref.at[slot]).wait()

    tok_lane = lax.broadcasted_iota(jnp.int32, (PEER_PICKS, tb), 1)
    hi_mask = jnp.uint32(0xFFFF0000)

    def words(slot, c):
        return rows_refs[slot][pl.ds(c, PEER_PICKS, stride=pitch), :]

    def pick_weights(t, slot):
        xt = hn_ref[pl.ds(t, 1), :]
        acc = jnp.zeros((PEER_PICKS, LANES), F32)
        for c in range(chunks):
            u = lax.bitcast_convert_type(words(slot, c) & hi_mask, F32)
            acc = acc + u * xt[:, c * LANES:(c + 1) * LANES]
        act = jnp.sum(acc, axis=1, keepdims=True)
        act = 0.5 * act * (1.0 + lax.erf(act * (2.0 ** -0.5)))
        gate = jnp.sum(jnp.where(tok_lane == t, gate_ref[...], 0.0), axis=1, keepdims=True)
        return jnp.broadcast_to(gate * act, (PEER_PICKS, LANES))

    def combine(t, slot, w):
        outs = []
        for c in range(chunks):
            v = lax.bitcast_convert_type(words(slot, c) << 16, F32)
            outs.append(jnp.sum(v * w, axis=0, keepdims=True))
        o_ref[pl.ds(t, 1), :] = x_ref[pl.ds(t, 1), :] + jnp.concatenate(outs, axis=1)

    ahead = PEER_SLOTS - 1
    groups = tb // PEER_SLOTS
    for t in range(ahead):
        issue(t, t)
    wait(0)
    w0 = pick_weights(0, 0)

    def group(g, w):
        for s in range(PEER_SLOTS):
            t = g * PEER_SLOTS + s
            wait((s + 1) % PEER_SLOTS)
            issue(t + ahead, (s + ahead) % PEER_SLOTS)
            w_next = pick_weights(t + 1, (s + 1) % PEER_SLOTS)
            combine(t, s, w)
            w = w_next
        return w

    w = lax.fori_loop(0, groups - 1, group, w0)

    for s in range(PEER_SLOTS):
        t = (groups - 1) * PEER_SLOTS + s
        if t + ahead < tb:
            issue(t + ahead, (s + ahead) % PEER_SLOTS)
        if t + 1 < tb:
            wait((s + 1) % PEER_SLOTS)
            w_next = pick_weights(t + 1, (s + 1) % PEER_SLOTS)
        combine(t, s, w)
        w = w_next


def peer_expert_ffn(eidx, gate, hn, x, table, tb=128):
    n, d = x.shape
    tb = min(tb, n)
    assert tb % PEER_SLOTS == 0 and table.shape[1] == LANES
    assert table.shape[0] >= PEER_PICKS * (d // LANES)
    slot_rows = PEER_PICKS * (d // LANES + PITCH_PAD)
    return pl.pallas_call(
        functools.partial(_peer_ffn_body, tb=tb, d=d),
        grid=(n // tb,),
        in_specs=[pl.BlockSpec((tb, PEER_PICKS), lambda i: (i, 0), memory_space=pltpu.SMEM),
                  pl.BlockSpec((PEER_PICKS, tb), lambda i: (0, i)),
                  pl.BlockSpec((tb, d), lambda i: (i, 0)),
                  pl.BlockSpec((tb, d), lambda i: (i, 0)),
                  pl.BlockSpec(memory_space=pl.ANY)],
        out_specs=pl.BlockSpec((tb, d), lambda i: (i, 0)),
        out_shape=jax.ShapeDtypeStruct((n, d), F32),
        scratch_shapes=[pltpu.VMEM((slot_rows, LANES), jnp.uint32)] * PEER_SLOTS
                       + [pltpu.SemaphoreType.DMA((PEER_SLOTS,))],
        compiler_params=_params("arbitrary"),
        name="peer_expert_ffn",
    )(eidx, gate, hn, x, table)


def peer_layer(x, norm_g, wq, keys, u, v, tb=128):
    q, hn = norm_matmul(x, norm_g, wq.astype(BF16), emit_hn=True)
    eidx, gate = peer_topk(q, keys)
    return peer_expert_ffn(eidx.T, gate, hn, x, _pack_expert_table(u, v), tb=tb)


def _band_bias(rel_bias):
    pad = LEFT_CHUNKS * CHUNK
    dist = pad + np.arange(CHUNK)[:, None] - np.arange(BAND)[None, :]
    idx = np.clip(dist, -(CHUNK - 1), REL_MAX) + (CHUNK - 1)
    return rel_bias[:, idx].astype(F32)


def kernel(x, mem, norm_mix, norm_ffn, norm_mem, w_mem_kv, w_out, peer_wq, peer_keys, peer_u, peer_v, a_mix, a_w_in, a_w0, a_w1, a_w2, a_a0, a_a1, a_a2, a_g1, a_g2, a_k_k, a_k_a, a_r_k, a_ln_g, a_ln_b, kv_norm, w_kv_shared, b_w_in, b_rel_bias, final_norm):
    bsz, seq, d = x.shape
    n = bsz * seq
    n_mem = mem.shape[1]
    seq_width = a_w0.shape[1]
    n_pairs = seq_width // PAIR
    mem_width = MEM_HEADS * MEM_HEAD_DIM
    x = x.reshape(n, d)
    mem2 = mem.reshape(bsz * n_mem, d)

    def mixer_tail(x, seq_out, proj, q_col_block, layer):
        mkv = norm_matmul(mem2, norm_mem[layer], w_mem_kv[layer].astype(BF16), tm=256)
        mem_out = memory_attention(proj, q_col_block, mkv, seq, n_mem)
        wo = w_out[layer].astype(BF16)
        x = out_projection(seq_out, mem_out, wo[:seq_width], wo[seq_width:], x)
        return peer_layer(x, norm_ffn[layer], peer_wq[layer], peer_keys[layer], peer_u[layer], peer_v[layer])

    h = rmsnorm(x, norm_mix[0])
    mix_tab = jnp.concatenate([a_mix[0], jnp.zeros((1, d), F32)], axis=0)
    tiles = seq_width // 512
    gid_main = jnp.asarray([0] * tiles + [2] * tiles + [3] * tiles + [6] * (mem_width // 512), jnp.int32)
    proj = mix_matmul(h, mix_tab, gid_main, a_w_in[0].astype(BF16), seq, tn=512)
    rank = a_w1.shape[2]
    padc = lambda w: jnp.pad(w, ((0, 0), (0, LORA_PAD - rank)))
    padr = lambda w: jnp.pad(w, ((0, LORA_PAD - rank), (0, 0)))
    w_l1 = jnp.concatenate([padc(a_w1[0]), padc(a_a1[0]), a_g1[0]], axis=1).astype(BF16)
    gid_l1 = jnp.asarray([1, 4] + [5] * (a_g1.shape[2] // LORA_PAD), jnp.int32)
    t1 = mix_matmul(h, mix_tab, gid_l1, w_l1, seq, tn=LORA_PAD)
    lw, a_iclr, gate = lora_stage2(t1, padr(a_w2[0]).astype(BF16), padr(a_a2[0]).astype(BF16),
                                   a_g2[0].astype(BF16), a_w0[0], a_a0[0])
    prep = wkv_chunk_prepare(proj, lw, a_iclr, a_k_k[0], a_k_a[0], n_pairs)
    seq_out = wkv_chunk_scan(prep, proj, a_iclr, gate, a_k_a[0], a_r_k[0].reshape(-1),
                             a_ln_g[0], a_ln_b[0], n_pairs, seq)
    x = mixer_tail(x, seq_out, proj, (3 * seq_width) // mem_width, 0)

    kv = norm_matmul(x, kv_norm, w_kv_shared.astype(BF16))

    proj = norm_matmul(x, norm_mix[1], b_w_in[0].astype(BF16))
    seq_out = band_attention(proj, kv, _band_bias(b_rel_bias[0]), n_pairs, seq)
    x = mixer_tail(x, seq_out, proj, seq_width // mem_width, 1)

    return rmsnorm(x, final_norm).reshape(bsz, seq, d)
```

```python
import functools

import numpy as np
import jax
import jax.numpy as jnp
from jax import lax
from jax.experimental import pallas as pl
from jax.experimental.pallas import tpu as pltpu

F32 = jnp.float32
BF16 = jnp.bfloat16
HI = lax.Precision.HIGHEST

HEAD_DIM = 64
PAIR = 2 * HEAD_DIM
CHUNK = 64
LEFT_CHUNKS = 8
BAND = (LEFT_CHUNKS + 1) * CHUNK
REL_MAX = 128
MEM_HEADS = 4
MEM_HEAD_DIM = 128
PEER_KEYS = 128
PEER_HEADS = 8
PEER_TOPK = 16
PEER_PICKS = PEER_HEADS * PEER_TOPK
GN_EPS = 64e-5
RMS_EPS = 1e-6
LORA_PAD = 128
VMEM_LIMIT = 48 * 1024 * 1024


def _params(*sem):
    return pltpu.CompilerParams(dimension_semantics=sem, vmem_limit_bytes=VMEM_LIMIT)


def _dot(a, b):
    return jnp.dot(a, b, preferred_element_type=F32)


def _dot_nt(a, b, precision=None):
    return lax.dot_general(a, b, (((1,), (1,)), ((), ())), precision=precision,
                           preferred_element_type=F32)


def _dot_tn(a, b, precision=None):
    return lax.dot_general(a, b, (((0,), (0,)), ((), ())), precision=precision,
                           preferred_element_type=F32)


def _rmsnorm_body(x_ref, g_ref, o_ref):
    x = x_ref[...]
    ms = jnp.mean(x * x, axis=-1, keepdims=True)
    o_ref[...] = (x * lax.rsqrt(ms + RMS_EPS) * g_ref[...]).astype(o_ref.dtype)


def rmsnorm(x, g, tm=512):
    n, d = x.shape
    tm = min(tm, n)
    return pl.pallas_call(
        _rmsnorm_body,
        grid=(n // tm,),
        in_specs=[pl.BlockSpec((tm, d), lambda i: (i, 0)), pl.BlockSpec((1, d), lambda i: (0, 0))],
        out_specs=pl.BlockSpec((tm, d), lambda i: (i, 0)),
        out_shape=jax.ShapeDtypeStruct((n, d), F32),
        compiler_params=_params("parallel"),
        name="rmsnorm",
    )(x, g.reshape(1, d))


def _normmm_body(x_ref, g_ref, w_ref, o_ref, *rest, emit_hn):
    lhs_ref = rest[-1]

    @pl.when(pl.program_id(1) == 0)
    def _():
        x = x_ref[...]
        ms = jnp.mean(x * x, axis=-1, keepdims=True)
        hn = x * lax.rsqrt(ms + RMS_EPS) * g_ref[...]
        lhs_ref[...] = hn.astype(BF16)
        if emit_hn:
            rest[0][...] = hn

    o_ref[...] = _dot(lhs_ref[...], w_ref[...])


def norm_matmul(x, g, w_bf16, emit_hn=False, tm=512, tn=512):
    n, d = x.shape
    nc = w_bf16.shape[1]
    tm, tn = min(tm, n), min(tn, nc)
    out_shape = [jax.ShapeDtypeStruct((n, nc), F32)]
    out_specs = [pl.BlockSpec((tm, tn), lambda i, j: (i, j))]
    if emit_hn:
        out_shape.append(jax.ShapeDtypeStruct((n, d), F32))
        out_specs.append(pl.BlockSpec((tm, d), lambda i, j: (i, 0)))
    res = pl.pallas_call(
        functools.partial(_normmm_body, emit_hn=emit_hn),
        grid=(n // tm, nc // tn),
        in_specs=[pl.BlockSpec((tm, d), lambda i, j: (i, 0)),
                  pl.BlockSpec((1, d), lambda i, j: (0, 0)),
                  pl.BlockSpec((d, tn), lambda i, j: (0, j))],
        out_specs=out_specs,
        out_shape=out_shape,
        scratch_shapes=[pltpu.VMEM((tm, d), BF16)],
        compiler_params=_params("parallel", "arbitrary"),
        name="norm_matmul",
    )(x, g.reshape(1, d), w_bf16)
    return res if emit_hn else res[0]


def _mixmm_body(gid_ref, h_ref, hp_ref, mix_ref, w_ref, o_ref, lhs_ref, *, tm, seq):
    i = pl.program_id(0)
    j = pl.program_id(1)
    new_group = jnp.logical_or(j == 0, gid_ref[j] != gid_ref[jnp.maximum(j - 1, 0)])

    @pl.when(new_group)
    def _():
        h = h_ref[...]
        prev = jnp.where((i * tm) % seq == 0, 0.0, hp_ref[7:8, :])
        row = lax.broadcasted_iota(jnp.int32, h.shape, 0)
        shifted = jnp.where(row == 0, prev, pltpu.roll(h, 1, 0))
        lhs_ref[...] = (h + (shifted - h) * mix_ref[0]).astype(BF16)

    o_ref[...] = _dot(lhs_ref[...], w_ref[...])


def mix_matmul(h, mix_tab, gid, w_bf16, seq, tn, tm=512):
    n, d = h.shape
    nc = w_bf16.shape[1]
    tm = min(tm, seq)
    sub = tm // 8
    grid_spec = pltpu.PrefetchScalarGridSpec(
        num_scalar_prefetch=1,
        grid=(n // tm, nc // tn),
        in_specs=[pl.BlockSpec((tm, d), lambda i, j, g: (i, 0)),
                  pl.BlockSpec((8, d), lambda i, j, g: (jnp.maximum(i * sub - 1, 0), 0)),
                  pl.BlockSpec((1, 1, d), lambda i, j, g: (g[j], 0, 0)),
                  pl.BlockSpec((d, tn), lambda i, j, g: (0, j))],
        out_specs=pl.BlockSpec((tm, tn), lambda i, j, g: (i, j)),
        scratch_shapes=[pltpu.VMEM((tm, d), BF16)],
    )
    return pl.pallas_call(
        functools.partial(_mixmm_body, tm=tm, seq=seq),
        grid_spec=grid_spec,
        out_shape=jax.ShapeDtypeStruct((n, nc), F32),
        compiler_params=_params("parallel", "arbitrary"),
        name="mix_matmul",
    )(gid, h, h, mix_tab.reshape(mix_tab.shape[0], 1, d), w_bf16)


def _sigmoid(x):
    return 1.0 / (1.0 + jnp.exp(-x))


def _lora2_body(t_ref, w2_ref, a2_ref, g2_ref, w0_ref, a0_ref, lw_ref, a_ref, g_ref):
    t = t_ref[...]
    tw = jnp.tanh(t[:, :LORA_PAD]).astype(BF16)
    ta = t[:, LORA_PAD:2 * LORA_PAD].astype(BF16)
    tg = _sigmoid(t[:, 2 * LORA_PAD:]).astype(BF16)
    u = w0_ref[...] + _dot(tw, w2_ref[...])
    softplus_neg_u = jnp.maximum(-u, 0.0) + jnp.log(1.0 + jnp.exp(-jnp.abs(u)))
    lw_ref[...] = -jnp.exp(-softplus_neg_u - 0.5)
    a_ref[...] = _sigmoid(a0_ref[...] + _dot(ta, a2_ref[...]))
    g_ref[...] = _dot(tg, g2_ref[...])


def lora_stage2(t1, w2p, a2p, g2, w0, a0, tm=256):
    n = t1.shape[0]
    width = w2p.shape[1]
    tm = min(tm, n)
    full = lambda a: pl.BlockSpec(a.shape, lambda i: (0, 0))
    row = pl.BlockSpec((tm, width), lambda i: (i, 0))
    w0 = w0.reshape(1, width)
    a0 = a0.reshape(1, width)
    return pl.pallas_call(
        _lora2_body,
        grid=(n // tm,),
        in_specs=[pl.BlockSpec((tm, t1.shape[1]), lambda i: (i, 0)), full(w2p), full(a2p), full(g2),
                  full(w0), full(a0)],
        out_specs=[row, row, row],
        out_shape=[jax.ShapeDtypeStruct((n, width), F32)] * 3,
        compiler_params=_params("parallel"),
        name="lora_stage2",
    )(t1, w2p, a2p, g2, w0, a0)


def _head_group_sum(x):
    r = lax.broadcasted_iota(jnp.int32, (PAIR, PAIR), 0) // HEAD_DIM
    c = lax.broadcasted_iota(jnp.int32, (PAIR, PAIR), 1) // HEAD_DIM
    ones = jnp.where(r == c, 1.0, 0.0).astype(F32)
    return jnp.dot(x, ones, precision=HI, preferred_element_type=F32)


def _wkv_prepare_body(r_ref, k_ref, v_ref, lw_ref, a_ref, kk_ref, ka_ref,
                      g_ref, s0c_ref, rp_ref, y0_ref, *, tb):
    r = r_ref[...]
    k = k_ref[...]
    v = v_ref[...]
    lw = lw_ref[...]
    a = a_ref[...]
    kk = k * kk_ref[...]
    norm = jnp.sqrt(_head_group_sum(kk * kk))
    kk = kk / jnp.maximum(norm, 1e-12)
    kmod = k * (1.0 + (a - 1.0) * ka_ref[...])
    avec = -kk
    bvec = kk * a

    tr = lax.broadcasted_iota(jnp.int32, (tb, tb), 0)
    tc = lax.broadcasted_iota(jnp.int32, (tb, tb), 1)
    tri = jnp.where((tr >= tc) & (tr // CHUNK == tc // CHUNK), 1.0, 0.0).astype(F32)
    cum = jnp.dot(tri, lw, precision=HI, preferred_element_type=F32)

    lane = lax.broadcasted_iota(jnp.int32, (CHUNK, PAIR), 1)
    head0 = lane < HEAD_DIM
    row = lax.broadcasted_iota(jnp.int32, (PAIR, PAIR), 0)
    col = lax.broadcasted_iota(jnp.int32, (PAIR, PAIR), 1)
    strict = row > col
    lower = row >= col
    eye = jnp.where(row == col, 1.0, 0.0).astype(F32)

    def stack(x):
        return jnp.concatenate([jnp.where(head0, x, 0.0), jnp.where(head0, 0.0, x)], axis=0)

    for c in range(tb // CHUNK):
        sl = slice(c * CHUNK, (c + 1) * CHUNK)
        cm = cum[sl]
        cend = cm[CHUNK - 1:CHUNK]
        e_in = jnp.exp(cm)
        e_out = jnp.exp(-cm)
        e_tail = jnp.exp(cend - cm)
        a_s = stack(avec[sl] * jnp.exp(cm - lw[sl]))
        r_s = stack(r[sl] * e_in)
        b_s = stack(bvec[sl] * e_out)
        k_s = stack(kmod[sl] * e_out)
        bt_s = stack(bvec[sl] * e_tail)
        kt_s = stack(kmod[sl] * e_tail)
        v_s = stack(v[sl])

        p = _dot_nt(jnp.concatenate([a_s, r_s], axis=0).astype(BF16),
                    jnp.concatenate([b_s, k_s], axis=0).astype(BF16))
        l_ab = jnp.where(strict, p[:PAIR, :PAIR], 0.0)
        l_ak = jnp.where(strict, p[:PAIR, PAIR:], 0.0)
        a_rb = jnp.where(lower, p[PAIR:, :PAIR], 0.0)
        a_rk = jnp.where(lower, p[PAIR:, PAIR:], 0.0)

        m = l_ab
        t = eye + m
        for _ in range(5):
            m_b = m.astype(BF16)
            m = _dot(m_b, m_b)
            t = t + _dot(t.astype(BF16), m.astype(BF16))

        v_b = v_s.astype(BF16)
        lv = _dot(l_ak.astype(BF16), v_b)
        au = _dot(t.astype(BF16), jnp.concatenate([a_s, lv], axis=1).astype(BF16))
        x = _dot(a_rb.astype(BF16), au.astype(BF16))
        rp_ref[0, c] = r_s + x[:, :PAIR]
        y0_ref[0, c] = x[:, PAIR:] + _dot(a_rk.astype(BF16), v_b)
        ap = au[:, :PAIR]
        u0 = au[:, PAIR:]
        g_ref[0, c] = eye * jnp.exp(cend) + _dot_tn(ap.astype(BF16), bt_s.astype(BF16))
        s0c_ref[0, c] = _dot_tn(jnp.concatenate([u0, v_s], axis=0).astype(BF16),
                                jnp.concatenate([bt_s, kt_s], axis=0).astype(BF16))


def wkv_chunk_prepare(proj, lw, a, k_k, k_a, n_pairs, tb=256):
    n = proj.shape[0]
    tb = min(tb, n)
    cpb = tb // CHUNK
    col = lambda off: pl.BlockSpec((tb, PAIR), lambda i, p: (i, off + p))
    par = pl.BlockSpec((1, PAIR), lambda i, p: (0, p))
    blk = pl.BlockSpec((1, cpb, PAIR, PAIR), lambda i, p: (p, i, 0, 0))
    shp = jax.ShapeDtypeStruct((n_pairs, n // CHUNK, PAIR, PAIR), F32)
    return pl.pallas_call(
        functools.partial(_wkv_prepare_body, tb=tb),
        grid=(n // tb, n_pairs),
        in_specs=[col(0), col(n_pairs), col(2 * n_pairs), col(0), col(0), par, par],
        out_specs=[blk] * 4,
        out_shape=[shp] * 4,
        compiler_params=_params("parallel", "parallel"),
        name="wkv_chunk_prepare",
    )(proj, proj, proj, lw, a, k_k.reshape(1, -1), k_a.reshape(1, -1))


def _wkv_scan_body(g_ref, s0c_ref, rp_ref, y0_ref, r_ref, k_ref, v_ref, a_ref, gate_ref,
                   ka_ref, rk_ref, lng_ref, lnb_ref, o_ref, s_ref, y_ref, *, cpb):
    @pl.when(pl.program_id(2) == 0)
    def _():
        s_ref[...] = jnp.zeros_like(s_ref)

    s = s_ref[...]
    for c in range(cpb):
        s_b = s.astype(BF16)
        y_st = _dot_nt(rp_ref[0, c].astype(BF16), s_b) + y0_ref[0, c]
        y_ref[c * CHUNK:(c + 1) * CHUNK, :] = y_st[:CHUNK] + y_st[CHUNK:]
        s = _dot(s_b, g_ref[0, c].astype(BF16)) + s0c_ref[0, c]
    s_ref[...] = s

    y = y_ref[...]
    inv = 1.0 / HEAD_DIM
    mu = _head_group_sum(y) * inv
    yc = y - mu
    var = _head_group_sum(yc * yc) * inv
    yn = yc * lax.rsqrt(var + GN_EPS) * lng_ref[...] + lnb_ref[...]
    r = r_ref[...]
    kmod = k_ref[...] * (1.0 + (a_ref[...] - 1.0) * ka_ref[...])
    bonus = _head_group_sum(r * kmod * rk_ref[...]) * v_ref[...]
    o_ref[...] = ((yn + bonus) * gate_ref[...]).astype(o_ref.dtype)


def wkv_chunk_scan(prep, proj, a, gate, k_a, r_k, ln_g, ln_b, n_pairs, seq, tb=512):
    g_all, s0c_all, rp_all, y0_all = prep
    n = proj.shape[0]
    tb = min(tb, seq)
    cpb = tb // CHUNK
    nblk = seq // tb
    blk = pl.BlockSpec((1, cpb, PAIR, PAIR), lambda b, p, t: (p, b * nblk + t, 0, 0))
    col = lambda off: pl.BlockSpec((tb, PAIR), lambda b, p, t: (b * nblk + t, off + p))
    par = pl.BlockSpec((1, PAIR), lambda b, p, t: (0, p))
    vec = lambda z: z.reshape(1, -1)
    return pl.pallas_call(
        functools.partial(_wkv_scan_body, cpb=cpb),
        grid=(n // seq, n_pairs, nblk),
        in_specs=[blk] * 4 + [col(0), col(n_pairs), col(2 * n_pairs), col(0), col(0)] + [par] * 4,
        out_specs=col(0),
        out_shape=jax.ShapeDtypeStruct((n, n_pairs * PAIR), BF16),
        scratch_shapes=[pltpu.VMEM((PAIR, PAIR), F32), pltpu.VMEM((tb, PAIR), F32)],
        compiler_params=_params("parallel", "parallel", "arbitrary"),
        name="wkv_chunk_scan",
    )(g_all, s0c_all, rp_all, y0_all, proj, proj, proj, a, gate,
      vec(k_a), vec(r_k), vec(ln_g), vec(ln_b))


def _memattn_body(q_ref, m_ref, o_ref):
    width = MEM_HEADS * MEM_HEAD_DIM
    scale = MEM_HEAD_DIM ** -0.5
    for h in range(MEM_HEADS):
        sl = slice(h * MEM_HEAD_DIM, (h + 1) * MEM_HEAD_DIM)
        q = q_ref[:, sl].astype(BF16)
        mk = m_ref[:, sl].astype(BF16)
        mv = m_ref[:, width + h * MEM_HEAD_DIM:width + (h + 1) * MEM_HEAD_DIM].astype(BF16)
        s = _dot_nt(q, mk) * scale
        e = jnp.exp(s - jnp.max(s, axis=-1, keepdims=True))
        p = e / jnp.sum(e, axis=-1, keepdims=True)
        o_ref[:, sl] = _dot(p.astype(BF16), mv).astype(o_ref.dtype)


def memory_attention(proj, q_col_block, mkv, seq, n_mem, tm=512):
    n = proj.shape[0]
    width = MEM_HEADS * MEM_HEAD_DIM
    tm = min(tm, seq)
    nblk = seq // tm
    return pl.pallas_call(
        _memattn_body,
        grid=(n // seq, nblk),
        in_specs=[pl.BlockSpec((tm, width), lambda b, t: (b * nblk + t, q_col_block)),
                  pl.BlockSpec((n_mem, 2 * width), lambda b, t: (b, 0))],
        out_specs=pl.BlockSpec((tm, width), lambda b, t: (b * nblk + t, 0)),
        out_shape=jax.ShapeDtypeStruct((n, width), BF16),
        compiler_params=_params("parallel", "parallel"),
        name="memory_attention",
    )(proj, mkv)


def _outproj_body(s_ref, m_ref, w1_ref, w2_ref, x_ref, o_ref):
    o_ref[...] = x_ref[...] + _dot(s_ref[...], w1_ref[...]) + _dot(m_ref[...], w2_ref[...])


def out_projection(seq_out, mem_out, w_seq, w_mem, x, tm=512, tn=512):
    n, d = x.shape
    tm, tn = min(tm, n), min(tn, d)
    ws, wm = seq_out.shape[1], mem_out.shape[1]
    return pl.pallas_call(
        _outproj_body,
        grid=(n // tm, d // tn),
        in_specs=[pl.BlockSpec((tm, ws), lambda i, j: (i, 0)),
                  pl.BlockSpec((tm, wm), lambda i, j: (i, 0)),
                  pl.BlockSpec((ws, tn), lambda i, j: (0, j)),
                  pl.BlockSpec((wm, tn), lambda i, j: (0, j)),
                  pl.BlockSpec((tm, tn), lambda i, j: (i, j))],
        out_specs=pl.BlockSpec((tm, tn), lambda i, j: (i, j)),
        out_shape=jax.ShapeDtypeStruct((n, d), F32),
        compiler_params=_params("parallel", "parallel"),
        name="out_projection",
    )(seq_out, mem_out, w_seq, w_mem, x)


def _bandattn_body(q_ref, kp_ref, kc_ref, vp_ref, vc_ref, bias_ref, o_ref, *, tq):
    qi = pl.program_id(2)
    scale = HEAD_DIM ** -0.5
    k_all = jnp.concatenate([kp_ref[...], kc_ref[...]], axis=0).astype(BF16)
    v_all = jnp.concatenate([vp_ref[...], vc_ref[...]], axis=0).astype(BF16)
    lane = lax.broadcasted_iota(jnp.int32, (CHUNK, PAIR), 1)
    head0 = lane < HEAD_DIM
    kcol = lax.broadcasted_iota(jnp.int32, (2 * CHUNK, BAND), 1)
    pad = LEFT_CHUNKS * CHUNK
    bias = bias_ref[...]
    for j in range(tq // CHUNK):
        q = q_ref[j * CHUNK:(j + 1) * CHUNK, :]
        start = tq - pad + j * CHUNK
        kb = k_all[start:start + BAND]
        vb = v_all[start:start + BAND]
        valid = jnp.logical_or(qi > 0, kcol + start >= tq)
        q2 = jnp.concatenate([jnp.where(head0, q, 0.0), jnp.where(head0, 0.0, q)], axis=0).astype(BF16)
        s = jnp.where(valid, _dot_nt(q2, kb) * scale + bias, -jnp.inf)
        e = jnp.exp(s - jnp.max(s, axis=-1, keepdims=True))
        o = _dot(e.astype(BF16), vb) / jnp.sum(e, axis=-1, keepdims=True)
        o_ref[j * CHUNK:(j + 1) * CHUNK, :] = jnp.where(head0, o[:CHUNK], o[CHUNK:]).astype(o_ref.dtype)


def band_attention(proj, kv, bias, n_pairs, seq, tq=512):
    n = proj.shape[0]
    tq = min(tq, seq)
    assert tq >= LEFT_CHUNKS * CHUNK
    nblk = seq // tq
    cur = lambda off: pl.BlockSpec((tq, PAIR), lambda b, p, t: (b * nblk + t, off + p))
    prev = lambda off: pl.BlockSpec((tq, PAIR), lambda b, p, t: (b * nblk + jnp.maximum(t - 1, 0), off + p))
    return pl.pallas_call(
        functools.partial(_bandattn_body, tq=tq),
        grid=(n // seq, n_pairs, nblk),
        in_specs=[cur(0), prev(0), cur(0), prev(n_pairs), cur(n_pairs),
                  pl.BlockSpec((2 * CHUNK, BAND), lambda b, p, t: (p, 0))],
        out_specs=cur(0),
        out_shape=jax.ShapeDtypeStruct((n, n_pairs * PAIR), BF16),
        compiler_params=_params("parallel", "parallel", "parallel"),
        name="band_attention",
    )(proj, kv, kv, kv, kv, bias.reshape(-1, BAND))


def _top16(scores, payloads):
    nl = scores[0].shape[1]
    rids = [lax.broadcasted_iota(jnp.int32, s.shape, 0).astype(F32) for s in scores]
    slot = lax.broadcasted_iota(jnp.int32, (PEER_TOPK, nl), 0)

    def body(i, carry):
        sel = slot == i
        out = []
        for (s, vals, picks), rid, payload in zip(carry, rids, payloads):
            m = jnp.max(s, axis=0, keepdims=True)
            am = jnp.min(jnp.where(s == m, rid, float(s.shape[0])), axis=0, keepdims=True)
            hit = rid == am
            if payload is None:
                pick = am
            else:
                pick = jnp.sum(jnp.where(hit, payload, 0.0), axis=0, keepdims=True)
            out.append((jnp.where(hit, -jnp.inf, s), jnp.where(sel, m, vals), jnp.where(sel, pick, picks)))
        return tuple(out)

    zero = jnp.zeros((PEER_TOPK, nl), F32)
    res = lax.fori_loop(0, PEER_TOPK, body, tuple((s, zero, zero) for s in scores))
    return [(vals, picks) for _, vals, picks in res]


def _pair_candidates(a, b, combine):
    half = PEER_TOPK // 2
    rows = [combine(a[0:1], b)]
    rows += [combine(a[i:i + 1], b[:half]) for i in range(1, half)]
    rows.append(combine(a[half:], b[0:1]))
    return jnp.concatenate(rows, axis=0)


def _peer_topk_body(q_ref, keys_ref, eidx_ref, gate_ref):
    q = q_ref[...].astype(BF16)
    s1 = _dot_nt(keys_ref[0, 0].astype(BF16), q[:, :PEER_KEYS])
    s2 = _dot_nt(keys_ref[0, 1].astype(BF16), q[:, PEER_KEYS:])
    (a, i1), (b, i2) = _top16([s1, s2], [None, None])
    cand = _pair_candidates(a, b, lambda x, y: x + y)
    cidx = _pair_candidates(i1, i2, lambda x, y: x * PEER_KEYS + y)
    ((top, eidx),) = _top16([cand], [cidx])
    e = jnp.exp(top - top[0:1])
    gate_ref[...] = e / jnp.sum(e, axis=0, keepdims=True)
    eidx_ref[...] = eidx.astype(jnp.int32)


def peer_topk(q, keys, tl=128):
    n = q.shape[0]
    tl = min(tl, n)
    blk = pl.BlockSpec((PEER_TOPK, tl), lambda i, h: (h, i))
    return pl.pallas_call(
        _peer_topk_body,
        grid=(n // tl, PEER_HEADS),
        in_specs=[pl.BlockSpec((tl, 2 * PEER_KEYS), lambda i, h: (i, h)),
                  pl.BlockSpec((1, 2, PEER_KEYS, PEER_KEYS), lambda i, h: (h, 0, 0, 0))],
        out_specs=[blk, blk],
        out_shape=[jax.ShapeDtypeStruct((PEER_PICKS, n), jnp.int32),
                   jax.ShapeDtypeStruct((PEER_PICKS, n), F32)],
        compiler_params=_params("parallel", "parallel"),
        name="peer_topk",
    )(q, keys)


PEER_SLOTS = 8
LANES = 128
PITCH_PAD = 4


def _pack_expert_table(u, v):
    n_exp, d = u.shape
    hi = lax.bitcast_convert_type(u.astype(BF16), jnp.uint16).astype(jnp.uint32)
    lo = lax.bitcast_convert_type(v.astype(BF16), jnp.uint16).astype(jnp.uint32)
    return ((hi << 16) | lo).reshape(n_exp * (d // LANES), LANES)


def _peer_ffn_body(eidx_ref, gate_ref, hn_ref, x_ref, tab_ref, o_ref, *scratch, tb, d):
    rows_refs, sem_ref = scratch[:PEER_SLOTS], scratch[PEER_SLOTS]
    chunks = d // LANES
    pitch = chunks + PITCH_PAD

    def issue(t, slot, part=0, parts=1):
        per = PEER_PICKS // parts
        for e in range(part * per, (part + 1) * per):
            src = tab_ref.at[pl.ds(pl.multiple_of(eidx_ref[t, e] * chunks, chunks), chunks), :]
            dst = rows_refs[slot].at[pl.ds(e * pitch, chunks), :]
            pltpu.make_async_copy(src, dst, sem_ref.at[slot]).start()

    def wait(slot):
        total = PEER_PICKS * chunks
        pltpu.make_async_copy(tab_ref.at[pl.ds(0, total), :],
                              rows_refs[slot].at[pl.ds(0, total), :], sem_ref.at[slot]).wait()

    tok_lane = lax.broadcasted_iota(jnp.int32, (PEER_PICKS, tb), 1)
    hi_mask = jnp.uint32(0xFFFF0000)

    def words(slot, c):
        return rows_refs[slot][pl.ds(c, PEER_PICKS, stride=pitch), :]

    def pick_weights(t, slot):
        xt = hn_ref[pl.ds(t, 1), :]
        acc = jnp.zeros((PEER_PICKS, LANES), F32)
        for c in range(chunks):
            u = lax.bitcast_convert_type(words(slot, c) & hi_mask, F32)
            acc = acc + u * xt[:, c * LANES:(c + 1) * LANES]
        act = jnp.sum(acc, axis=1, keepdims=True)
        act = 0.5 * act * (1.0 + lax.erf(act * (2.0 ** -0.5)))
        gate = jnp.sum(jnp.where(tok_lane == t, gate_ref[...], 0.0), axis=1, keepdims=True)
        return jnp.broadcast_to(gate * act, (PEER_PICKS, LANES))

    group_row = lax.broadcasted_iota(jnp.int32, (PEER_SLOTS, d), 0)

    def combine(slot, w, acc):
        outs = []
        for c in range(chunks):
            v = lax.bitcast_convert_type(words(slot, c) << 16, F32)
            outs.append(jnp.sum(v * w, axis=0, keepdims=True))
        return jnp.where(group_row == slot, jnp.concatenate(outs, axis=1), acc)

    def store_group(g, acc):
        rows = pl.ds(pl.multiple_of(g * PEER_SLOTS, PEER_SLOTS), PEER_SLOTS)
        o_ref[rows, :] = x_ref[rows, :] + acc

    ahead = PEER_SLOTS - 1
    groups = tb // PEER_SLOTS
    for t in range(ahead):
        issue(t, t)
    wait(0)
    w0 = pick_weights(0, 0)

    def group(g, w):
        acc = jnp.zeros((PEER_SLOTS, d), F32)
        for s in range(PEER_SLOTS):
            t = g * PEER_SLOTS + s
            wait((s + 1) % PEER_SLOTS)
            issue(t + ahead, (s + ahead) % PEER_SLOTS)
            w_next = pick_weights(t + 1, (s + 1) % PEER_SLOTS)
            acc = combine(s, w, acc)
            w = w_next
        store_group(g, acc)
        return w

    w = lax.fori_loop(0, groups - 1, group, w0)

    acc = jnp.zeros((PEER_SLOTS, d), F32)
    for s in range(PEER_SLOTS):
        t = (groups - 1) * PEER_SLOTS + s
        if t + ahead < tb:
            issue(t + ahead, (s + ahead) % PEER_SLOTS)
        if t + 1 < tb:
            wait((s + 1) % PEER_SLOTS)
            w_next = pick_weights(t + 1, (s + 1) % PEER_SLOTS)
        acc = combine(s, w, acc)
        w = w_next
    store_group(groups - 1, acc)


def peer_expert_ffn(eidx, gate, hn, x, table, tb=128):
    n, d = x.shape
    tb = min(tb, n)
    assert tb % PEER_SLOTS == 0 and table.shape[1] == LANES
    assert table.shape[0] >= PEER_PICKS * (d // LANES)
    slot_rows = PEER_PICKS * (d // LANES + PITCH_PAD)
    return pl.pallas_call(
        functools.partial(_peer_ffn_body, tb=tb, d=d),
        grid=(n // tb,),
        in_specs=[pl.BlockSpec((tb, PEER_PICKS), lambda i: (i, 0), memory_space=pltpu.SMEM),
                  pl.BlockSpec((PEER_PICKS, tb), lambda i: (0, i)),
                  pl.BlockSpec((tb, d), lambda i: (i, 0)),
                  pl.BlockSpec((tb, d), lambda i: (i, 0)),
                  pl.BlockSpec(memory_space=pl.ANY)],
        out_specs=pl.BlockSpec((tb, d), lambda i: (i, 0)),
        out_shape=jax.ShapeDtypeStruct((n, d), F32),
        scratch_shapes=[pltpu.VMEM((slot_rows, LANES), jnp.uint32)] * PEER_SLOTS
                       + [pltpu.SemaphoreType.DMA((PEER_SLOTS,))],
        compiler_params=_params("arbitrary"),
        name="peer_expert_ffn",
    )(eidx, gate, hn, x, table)


def peer_layer(x, norm_g, wq, keys, u, v, tb=256):
    q, hn = norm_matmul(x, norm_g, wq.astype(BF16), emit_hn=True)
    eidx, gate = peer_topk(q, keys)
    return peer_expert_ffn(eidx.T, gate, hn, x, _pack_expert_table(u, v), tb=tb)


def _band_bias(rel_bias):
    pad = LEFT_CHUNKS * CHUNK
    dist = pad + np.arange(CHUNK)[:, None] - np.arange(BAND)[None, :]
    idx = np.clip(dist, -(CHUNK - 1), REL_MAX) + (CHUNK - 1)
    return rel_bias[:, idx].astype(F32)


def kernel(x, mem, norm_mix, norm_ffn, norm_mem, w_mem_kv, w_out, peer_wq, peer_keys, peer_u, peer_v, a_mix, a_w_in, a_w0, a_w1, a_w2, a_a0, a_a1, a_a2, a_g1, a_g2, a_k_k, a_k_a, a_r_k, a_ln_g, a_ln_b, kv_norm, w_kv_shared, b_w_in, b_rel_bias, final_norm):
    bsz, seq, d = x.shape
    n = bsz * seq
    n_mem = mem.shape[1]
    seq_width = a_w0.shape[1]
    n_pairs = seq_width // PAIR
    mem_width = MEM_HEADS * MEM_HEAD_DIM
    x = x.reshape(n, d)
    mem2 = mem.reshape(bsz * n_mem, d)

    def mixer_tail(x, seq_out, proj, q_col_block, layer):
        mkv = norm_matmul(mem2, norm_mem[layer], w_mem_kv[layer].astype(BF16), tm=256)
        mem_out = memory_attention(proj, q_col_block, mkv, seq, n_mem)
        wo = w_out[layer].astype(BF16)
        x = out_projection(seq_out, mem_out, wo[:seq_width], wo[seq_width:], x)
        return peer_layer(x, norm_ffn[layer], peer_wq[layer], peer_keys[layer], peer_u[layer], peer_v[layer])

    h = rmsnorm(x, norm_mix[0])
    mix_tab = jnp.concatenate([a_mix[0], jnp.zeros((1, d), F32)], axis=0)
    tiles = seq_width // 512
    gid_main = jnp.asarray([0] * tiles + [2] * tiles + [3] * tiles + [6] * (mem_width // 512), jnp.int32)
    proj = mix_matmul(h, mix_tab, gid_main, a_w_in[0].astype(BF16), seq, tn=512)
    rank = a_w1.shape[2]
    padc = lambda w: jnp.pad(w, ((0, 0), (0, LORA_PAD - rank)))
    padr = lambda w: jnp.pad(w, ((0, LORA_PAD - rank), (0, 0)))
    w_l1 = jnp.concatenate([padc(a_w1[0]), padc(a_a1[0]), a_g1[0]], axis=1).astype(BF16)
    gid_l1 = jnp.asarray([1, 4] + [5] * (a_g1.shape[2] // LORA_PAD), jnp.int32)
    t1 = mix_matmul(h, mix_tab, gid_l1, w_l1, seq, tn=LORA_PAD)
    lw, a_iclr, gate = lora_stage2(t1, padr(a_w2[0]).astype(BF16), padr(a_a2[0]).astype(BF16),
                                   a_g2[0].astype(BF16), a_w0[0], a_a0[0])
    prep = wkv_chunk_prepare(proj, lw, a_iclr, a_k_k[0], a_k_a[0], n_pairs)
    seq_out = wkv_chunk_scan(prep, proj, a_iclr, gate, a_k_a[0], a_r_k[0].reshape(-1),
                             a_ln_g[0], a_ln_b[0], n_pairs, seq)
    x = mixer_tail(x, seq_out, proj, (3 * seq_width) // mem_width, 0)

    kv = norm_matmul(x, kv_norm, w_kv_shared.astype(BF16))

    proj = norm_matmul(x, norm_mix[1], b_w_in[0].astype(BF16))
    seq_out = band_attention(proj, kv, _band_bias(b_rel_bias[0]), n_pairs, seq)
    x = mixer_tail(x, seq_out, proj, seq_width // mem_width, 1)

    return rmsnorm(x, final_norm).reshape(bsz, seq, d)
```

```python
import functools

import numpy as np
import jax
import jax.numpy as jnp
from jax import lax
from jax.experimental import pallas as pl
from jax.experimental.pallas import tpu as pltpu

F32 = jnp.float32
BF16 = jnp.bfloat16
HI = lax.Precision.HIGHEST

HEAD_DIM = 64
PAIR = 2 * HEAD_DIM
CHUNK = 64
SCAN_PAIRS = 2
LEFT_CHUNKS = 8
BAND = (LEFT_CHUNKS + 1) * CHUNK
REL_MAX = 128
MEM_HEADS = 4
MEM_HEAD_DIM = 128
PEER_KEYS = 128
PEER_HEADS = 8
PEER_TOPK = 16
PEER_PICKS = PEER_HEADS * PEER_TOPK
GN_EPS = 64e-5
RMS_EPS = 1e-6
LORA_PAD = 128
VMEM_LIMIT = 48 * 1024 * 1024


def _params(*sem):
    return pltpu.CompilerParams(dimension_semantics=sem, vmem_limit_bytes=VMEM_LIMIT)


def _dot(a, b):
    return jnp.dot(a, b, preferred_element_type=F32)


def _dot_nt(a, b, precision=None):
    return lax.dot_general(a, b, (((1,), (1,)), ((), ())), precision=precision,
                           preferred_element_type=F32)


def _dot_tn(a, b, precision=None):
    return lax.dot_general(a, b, (((0,), (0,)), ((), ())), precision=precision,
                           preferred_element_type=F32)


def _rmsnorm_body(x_ref, g_ref, o_ref):
    x = x_ref[...]
    ms = jnp.mean(x * x, axis=-1, keepdims=True)
    o_ref[...] = (x * lax.rsqrt(ms + RMS_EPS) * g_ref[...]).astype(o_ref.dtype)


def rmsnorm(x, g, tm=512):
    n, d = x.shape
    tm = min(tm, n)
    return pl.pallas_call(
        _rmsnorm_body,
        grid=(n // tm,),
        in_specs=[pl.BlockSpec((tm, d), lambda i: (i, 0)), pl.BlockSpec((1, d), lambda i: (0, 0))],
        out_specs=pl.BlockSpec((tm, d), lambda i: (i, 0)),
        out_shape=jax.ShapeDtypeStruct((n, d), F32),
        compiler_params=_params("parallel"),
        name="rmsnorm",
    )(x, g.reshape(1, d))


def _normmm_body(x_ref, g_ref, w_ref, o_ref, *rest, emit_hn):
    lhs_ref = rest[-1]

    @pl.when(pl.program_id(1) == 0)
    def _():
        x = x_ref[...]
        ms = jnp.mean(x * x, axis=-1, keepdims=True)
        hn = x * lax.rsqrt(ms + RMS_EPS) * g_ref[...]
        lhs_ref[...] = hn.astype(BF16)
        if emit_hn:
            rest[0][...] = hn

    o_ref[...] = _dot(lhs_ref[...], w_ref[...])


def norm_matmul(x, g, w_bf16, emit_hn=False, tm=512, tn=512):
    n, d = x.shape
    nc = w_bf16.shape[1]
    tm, tn = min(tm, n), min(tn, nc)
    out_shape = [jax.ShapeDtypeStruct((n, nc), F32)]
    out_specs = [pl.BlockSpec((tm, tn), lambda i, j: (i, j))]
    if emit_hn:
        out_shape.append(jax.ShapeDtypeStruct((n, d), F32))
        out_specs.append(pl.BlockSpec((tm, d), lambda i, j: (i, 0)))
    res = pl.pallas_call(
        functools.partial(_normmm_body, emit_hn=emit_hn),
        grid=(n // tm, nc // tn),
        in_specs=[pl.BlockSpec((tm, d), lambda i, j: (i, 0)),
                  pl.BlockSpec((1, d), lambda i, j: (0, 0)),
                  pl.BlockSpec((d, tn), lambda i, j: (0, j))],
        out_specs=out_specs,
        out_shape=out_shape,
        scratch_shapes=[pltpu.VMEM((tm, d), BF16)],
        compiler_params=_params("parallel", "arbitrary"),
        name="norm_matmul",
    )(x, g.reshape(1, d), w_bf16)
    return res if emit_hn else res[0]


def _mixmm_body(gid_ref, h_ref, hp_ref, mix_ref, w_ref, o_ref, lhs_ref, *, tm, seq):
    i = pl.program_id(0)
    j = pl.program_id(1)
    new_group = jnp.logical_or(j == 0, gid_ref[j] != gid_ref[jnp.maximum(j - 1, 0)])

    @pl.when(new_group)
    def _():
        h = h_ref[...]
        prev = jnp.where((i * tm) % seq == 0, 0.0, hp_ref[7:8, :])
        row = lax.broadcasted_iota(jnp.int32, h.shape, 0)
        shifted = jnp.where(row == 0, prev, pltpu.roll(h, 1, 0))
        lhs_ref[...] = (h + (shifted - h) * mix_ref[0]).astype(BF16)

    o_ref[...] = _dot(lhs_ref[...], w_ref[...])


def mix_matmul(h, mix_tab, gid, w_bf16, seq, tn, tm=512):
    n, d = h.shape
    nc = w_bf16.shape[1]
    tm = min(tm, seq)
    sub = tm // 8
    grid_spec = pltpu.PrefetchScalarGridSpec(
        num_scalar_prefetch=1,
        grid=(n // tm, nc // tn),
        in_specs=[pl.BlockSpec((tm, d), lambda i, j, g: (i, 0)),
                  pl.BlockSpec((8, d), lambda i, j, g: (jnp.maximum(i * sub - 1, 0), 0)),
                  pl.BlockSpec((1, 1, d), lambda i, j, g: (g[j], 0, 0)),
                  pl.BlockSpec((d, tn), lambda i, j, g: (0, j))],
        out_specs=pl.BlockSpec((tm, tn), lambda i, j, g: (i, j)),
        scratch_shapes=[pltpu.VMEM((tm, d), BF16)],
    )
    return pl.pallas_call(
        functools.partial(_mixmm_body, tm=tm, seq=seq),
        grid_spec=grid_spec,
        out_shape=jax.ShapeDtypeStruct((n, nc), F32),
        compiler_params=_params("parallel", "arbitrary"),
        name="mix_matmul",
    )(gid, h, h, mix_tab.reshape(mix_tab.shape[0], 1, d), w_bf16)


def _sigmoid(x):
    return 1.0 / (1.0 + jnp.exp(-x))


def _lora2_body(t_ref, w2_ref, a2_ref, g2_ref, w0_ref, a0_ref, lw_ref, a_ref, g_ref):
    t = t_ref[...]
    tw = jnp.tanh(t[:, :LORA_PAD]).astype(BF16)
    ta = t[:, LORA_PAD:2 * LORA_PAD].astype(BF16)
    tg = _sigmoid(t[:, 2 * LORA_PAD:]).astype(BF16)
    u = w0_ref[...] + _dot(tw, w2_ref[...])
    softplus_neg_u = jnp.maximum(-u, 0.0) + jnp.log(1.0 + jnp.exp(-jnp.abs(u)))
    lw_ref[...] = -jnp.exp(-softplus_neg_u - 0.5)
    a_ref[...] = _sigmoid(a0_ref[...] + _dot(ta, a2_ref[...]))
    g_ref[...] = _dot(tg, g2_ref[...])


def lora_stage2(t1, w2p, a2p, g2, w0, a0, tm=256):
    n = t1.shape[0]
    width = w2p.shape[1]
    tm = min(tm, n)
    full = lambda a: pl.BlockSpec(a.shape, lambda i: (0, 0))
    row = pl.BlockSpec((tm, width), lambda i: (i, 0))
    w0 = w0.reshape(1, width)
    a0 = a0.reshape(1, width)
    return pl.pallas_call(
        _lora2_body,
        grid=(n // tm,),
        in_specs=[pl.BlockSpec((tm, t1.shape[1]), lambda i: (i, 0)), full(w2p), full(a2p), full(g2),
                  full(w0), full(a0)],
        out_specs=[row, row, row],
        out_shape=[jax.ShapeDtypeStruct((n, width), F32)] * 3,
        compiler_params=_params("parallel"),
        name="lora_stage2",
    )(t1, w2p, a2p, g2, w0, a0)


def _head_group_sum(x):
    width = x.shape[-1]
    r = lax.broadcasted_iota(jnp.int32, (width, width), 0) // HEAD_DIM
    c = lax.broadcasted_iota(jnp.int32, (width, width), 1) // HEAD_DIM
    ones = jnp.where(r == c, 1.0, 0.0).astype(F32)
    return jnp.dot(x, ones, precision=HI, preferred_element_type=F32)


def _wkv_prepare_body(r_ref, k_ref, v_ref, lw_ref, a_ref, kk_ref, ka_ref,
                      g_ref, s0c_ref, rp_ref, y0_ref, *, tb):
    r = r_ref[...]
    k = k_ref[...]
    v = v_ref[...]
    lw = lw_ref[...]
    a = a_ref[...]
    kk = k * kk_ref[...]
    norm = jnp.sqrt(_head_group_sum(kk * kk))
    kk = kk / jnp.maximum(norm, 1e-12)
    kmod = k * (1.0 + (a - 1.0) * ka_ref[...])
    avec = -kk
    bvec = kk * a

    tr = lax.broadcasted_iota(jnp.int32, (tb, tb), 0)
    tc = lax.broadcasted_iota(jnp.int32, (tb, tb), 1)
    tri = jnp.where((tr >= tc) & (tr // CHUNK == tc // CHUNK), 1.0, 0.0).astype(F32)
    cum = jnp.dot(tri, lw, precision=HI, preferred_element_type=F32)

    lane = lax.broadcasted_iota(jnp.int32, (CHUNK, PAIR), 1)
    head0 = lane < HEAD_DIM
    row = lax.broadcasted_iota(jnp.int32, (PAIR, PAIR), 0)
    col = lax.broadcasted_iota(jnp.int32, (PAIR, PAIR), 1)
    strict = row > col
    lower = row >= col
    eye = jnp.where(row == col, 1.0, 0.0).astype(F32)

    def stack(x):
        return jnp.concatenate([jnp.where(head0, x, 0.0), jnp.where(head0, 0.0, x)], axis=0)

    chunks = range(tb // CHUNK)
    pre = []
    for c in chunks:
        sl = slice(c * CHUNK, (c + 1) * CHUNK)
        cm = cum[sl]
        cend = cm[CHUNK - 1:CHUNK]
        e_in = jnp.exp(cm)
        e_out = jnp.exp(-cm)
        e_tail = jnp.exp(cend - cm)
        a_s = stack(avec[sl] * jnp.exp(cm - lw[sl]))
        r_s = stack(r[sl] * e_in)
        b_s = stack(bvec[sl] * e_out)
        k_s = stack(kmod[sl] * e_out)
        bt_s = stack(bvec[sl] * e_tail)
        kt_s = stack(kmod[sl] * e_tail)
        v_s = stack(v[sl])
        p = _dot_nt(jnp.concatenate([a_s, r_s], axis=0).astype(BF16),
                    jnp.concatenate([b_s, k_s], axis=0).astype(BF16))
        pre.append(dict(cend=cend, a_s=a_s, r_s=r_s, bt_s=bt_s, kt_s=kt_s, v_s=v_s,
                        l_ab=jnp.where(strict, p[:PAIR, :PAIR], 0.0),
                        l_ak=jnp.where(strict, p[:PAIR, PAIR:], 0.0),
                        a_rb=jnp.where(lower, p[PAIR:, :PAIR], 0.0),
                        a_rk=jnp.where(lower, p[PAIR:, PAIR:], 0.0)))

    ms = [q["l_ab"] for q in pre]
    ts = [eye + m for m in ms]
    for _ in range(5):
        ms = [_dot(m.astype(BF16), m.astype(BF16)) for m in ms]
        ts = [t + _dot(t.astype(BF16), m.astype(BF16)) for t, m in zip(ts, ms)]

    lvs = [_dot(q["l_ak"].astype(BF16), q["v_s"].astype(BF16)) for q in pre]
    aus = [_dot(t.astype(BF16), jnp.concatenate([q["a_s"], lv], axis=1).astype(BF16))
           for t, q, lv in zip(ts, pre, lvs)]
    xs = [_dot(q["a_rb"].astype(BF16), au.astype(BF16)) for q, au in zip(pre, aus)]
    for c, q, au, x in zip(chunks, pre, aus, xs):
        rp_ref[0, c] = q["r_s"] + x[:, :PAIR]
        y0_ref[0, c] = x[:, PAIR:] + _dot(q["a_rk"].astype(BF16), q["v_s"].astype(BF16))
        ap = au[:, :PAIR]
        u0 = au[:, PAIR:]
        g_ref[0, c] = eye * jnp.exp(q["cend"]) + _dot_tn(ap.astype(BF16), q["bt_s"].astype(BF16))
        s0c_ref[0, c] = _dot_tn(jnp.concatenate([u0, q["v_s"]], axis=0).astype(BF16),
                                jnp.concatenate([q["bt_s"], q["kt_s"]], axis=0).astype(BF16))


def wkv_chunk_prepare(proj, lw, a, k_k, k_a, n_pairs, tb=512):
    n = proj.shape[0]
    tb = min(tb, n)
    cpb = tb // CHUNK
    col = lambda off: pl.BlockSpec((tb, PAIR), lambda i, p: (i, off + p))
    par = pl.BlockSpec((1, PAIR), lambda i, p: (0, p))
    blk = pl.BlockSpec((1, cpb, PAIR, PAIR), lambda i, p: (p, i, 0, 0))
    shp = jax.ShapeDtypeStruct((n_pairs, n // CHUNK, PAIR, PAIR), F32)
    return pl.pallas_call(
        functools.partial(_wkv_prepare_body, tb=tb),
        grid=(n // tb, n_pairs),
        in_specs=[col(0), col(n_pairs), col(2 * n_pairs), col(0), col(0), par, par],
        out_specs=[blk] * 4,
        out_shape=[shp] * 4,
        compiler_params=_params("parallel", "parallel"),
        name="wkv_chunk_prepare",
    )(proj, proj, proj, lw, a, k_k.reshape(1, -1), k_a.reshape(1, -1))


def _wkv_scan_body(g_ref, s0c_ref, rp_ref, y0_ref, r_ref, k_ref, v_ref, a_ref, gate_ref,
                   ka_ref, rk_ref, lng_ref, lnb_ref, o_ref, s_ref, y_ref, *, cpb):
    @pl.when(pl.program_id(2) == 0)
    def _():
        s_ref[...] = jnp.zeros_like(s_ref)

    pairs = range(SCAN_PAIRS)
    states = [[s_ref[w].astype(BF16)] for w in pairs]
    for c in range(cpb):
        for w in pairs:
            s = _dot(states[w][c], g_ref[w, c].astype(BF16)) + s0c_ref[w, c]
            if c + 1 < cpb:
                states[w].append(s.astype(BF16))
            else:
                s_ref[w] = s
    for c in range(cpb):
        for w in pairs:
            y_st = _dot_nt(rp_ref[w, c].astype(BF16), states[w][c]) + y0_ref[w, c]
            y_ref[c * CHUNK:(c + 1) * CHUNK, w * PAIR:(w + 1) * PAIR] = y_st[:CHUNK] + y_st[CHUNK:]

    y = y_ref[...]
    inv = 1.0 / HEAD_DIM
    mu = _head_group_sum(y) * inv
    yc = y - mu
    var = _head_group_sum(yc * yc) * inv
    yn = yc * lax.rsqrt(var + GN_EPS) * lng_ref[...] + lnb_ref[...]
    r = r_ref[...]
    kmod = k_ref[...] * (1.0 + (a_ref[...] - 1.0) * ka_ref[...])
    bonus = _head_group_sum(r * kmod * rk_ref[...]) * v_ref[...]
    o_ref[...] = ((yn + bonus) * gate_ref[...]).astype(o_ref.dtype)


def wkv_chunk_scan(prep, proj, a, gate, k_a, r_k, ln_g, ln_b, n_pairs, seq, tb=512):
    g_all, s0c_all, rp_all, y0_all = prep
    n = proj.shape[0]
    tb = min(tb, seq)
    cpb = tb // CHUNK
    nblk = seq // tb
    assert n_pairs % SCAN_PAIRS == 0
    width = SCAN_PAIRS * PAIR
    groups = n_pairs // SCAN_PAIRS
    blk = pl.BlockSpec((SCAN_PAIRS, cpb, PAIR, PAIR), lambda b, p, t: (p, b * nblk + t, 0, 0))
    col = lambda off: pl.BlockSpec((tb, width), lambda b, p, t: (b * nblk + t, off + p))
    par = pl.BlockSpec((1, width), lambda b, p, t: (0, p))
    vec = lambda z: z.reshape(1, -1)
    return pl.pallas_call(
        functools.partial(_wkv_scan_body, cpb=cpb),
        grid=(n // seq, groups, nblk),
        in_specs=[blk] * 4 + [col(0), col(groups), col(2 * groups), col(0), col(0)] + [par] * 4,
        out_specs=col(0),
        out_shape=jax.ShapeDtypeStruct((n, n_pairs * PAIR), BF16),
        scratch_shapes=[pltpu.VMEM((SCAN_PAIRS, PAIR, PAIR), F32), pltpu.VMEM((tb, width), F32)],
        compiler_params=_params("parallel", "parallel", "arbitrary"),
        name="wkv_chunk_scan",
    )(g_all, s0c_all, rp_all, y0_all, proj, proj, proj, a, gate,
      vec(k_a), vec(r_k), vec(ln_g), vec(ln_b))


def _memattn_body(q_ref, m_ref, o_ref):
    width = MEM_HEADS * MEM_HEAD_DIM
    scale = MEM_HEAD_DIM ** -0.5
    for h in range(MEM_HEADS):
        sl = slice(h * MEM_HEAD_DIM, (h + 1) * MEM_HEAD_DIM)
        q = q_ref[:, sl].astype(BF16)
        mk = m_ref[:, sl].astype(BF16)
        mv = m_ref[:, width + h * MEM_HEAD_DIM:width + (h + 1) * MEM_HEAD_DIM].astype(BF16)
        s = _dot_nt(q, mk) * scale
        e = jnp.exp(s - jnp.max(s, axis=-1, keepdims=True))
        p = e / jnp.sum(e, axis=-1, keepdims=True)
        o_ref[:, sl] = _dot(p.astype(BF16), mv).astype(o_ref.dtype)


def memory_attention(proj, q_col_block, mkv, seq, n_mem, tm=512):
    n = proj.shape[0]
    width = MEM_HEADS * MEM_HEAD_DIM
    tm = min(tm, seq)
    nblk = seq // tm
    return pl.pallas_call(
        _memattn_body,
        grid=(n // seq, nblk),
        in_specs=[pl.BlockSpec((tm, width), lambda b, t: (b * nblk + t, q_col_block)),
                  pl.BlockSpec((n_mem, 2 * width), lambda b, t: (b, 0))],
        out_specs=pl.BlockSpec((tm, width), lambda b, t: (b * nblk + t, 0)),
        out_shape=jax.ShapeDtypeStruct((n, width), BF16),
        compiler_params=_params("parallel", "parallel"),
        name="memory_attention",
    )(proj, mkv)


def _outproj_body(s_ref, m_ref, w1_ref, w2_ref, x_ref, o_ref):
    o_ref[...] = x_ref[...] + _dot(s_ref[...], w1_ref[...]) + _dot(m_ref[...], w2_ref[...])


def out_projection(seq_out, mem_out, w_seq, w_mem, x, tm=512, tn=512):
    n, d = x.shape
    tm, tn = min(tm, n), min(tn, d)
    ws, wm = seq_out.shape[1], mem_out.shape[1]
    return pl.pallas_call(
        _outproj_body,
        grid=(n // tm, d // tn),
        in_specs=[pl.BlockSpec((tm, ws), lambda i, j: (i, 0)),
                  pl.BlockSpec((tm, wm), lambda i, j: (i, 0)),
                  pl.BlockSpec((ws, tn), lambda i, j: (0, j)),
                  pl.BlockSpec((wm, tn), lambda i, j: (0, j)),
                  pl.BlockSpec((tm, tn), lambda i, j: (i, j))],
        out_specs=pl.BlockSpec((tm, tn), lambda i, j: (i, j)),
        out_shape=jax.ShapeDtypeStruct((n, d), F32),
        compiler_params=_params("parallel", "parallel"),
        name="out_projection",
    )(seq_out, mem_out, w_seq, w_mem, x)


def _bandattn_body(q_ref, kp_ref, kc_ref, vp_ref, vc_ref, bias_ref, o_ref, *, tq):
    qi = pl.program_id(2)
    scale = HEAD_DIM ** -0.5
    k_all = jnp.concatenate([kp_ref[...], kc_ref[...]], axis=0).astype(BF16)
    v_all = jnp.concatenate([vp_ref[...], vc_ref[...]], axis=0).astype(BF16)
    lane = lax.broadcasted_iota(jnp.int32, (CHUNK, PAIR), 1)
    head0 = lane < HEAD_DIM
    kcol = lax.broadcasted_iota(jnp.int32, (2 * CHUNK, BAND), 1)
    pad = LEFT_CHUNKS * CHUNK
    bias = bias_ref[...]
    chunks = range(tq // CHUNK)
    starts = [tq - pad + j * CHUNK for j in chunks]
    scores = []
    for j, start in zip(chunks, starts):
        q = q_ref[j * CHUNK:(j + 1) * CHUNK, :]
        q2 = jnp.concatenate([jnp.where(head0, q, 0.0), jnp.where(head0, 0.0, q)], axis=0).astype(BF16)
        scores.append(_dot_nt(q2, k_all[start:start + BAND]))
    exps = []
    for start, s in zip(starts, scores):
        valid = jnp.logical_or(qi > 0, kcol + start >= tq)
        s = jnp.where(valid, s * scale + bias, -jnp.inf)
        exps.append(jnp.exp(s - jnp.max(s, axis=-1, keepdims=True)))
    outs = [_dot(e.astype(BF16), v_all[start:start + BAND]) for start, e in zip(starts, exps)]
    for j, e, o in zip(chunks, exps, outs):
        o = o / jnp.sum(e, axis=-1, keepdims=True)
        o_ref[j * CHUNK:(j + 1) * CHUNK, :] = jnp.where(head0, o[:CHUNK], o[CHUNK:]).astype(o_ref.dtype)


def band_attention(proj, kv, bias, n_pairs, seq, tq=512):
    n = proj.shape[0]
    tq = min(tq, seq)
    assert tq >= LEFT_CHUNKS * CHUNK
    nblk = seq // tq
    cur = lambda off: pl.BlockSpec((tq, PAIR), lambda b, p, t: (b * nblk + t, off + p))
    prev = lambda off: pl.BlockSpec((tq, PAIR), lambda b, p, t: (b * nblk + jnp.maximum(t - 1, 0), off + p))
    return pl.pallas_call(
        functools.partial(_bandattn_body, tq=tq),
        grid=(n // seq, n_pairs, nblk),
        in_specs=[cur(0), prev(0), cur(0), prev(n_pairs), cur(n_pairs),
                  pl.BlockSpec((2 * CHUNK, BAND), lambda b, p, t: (p, 0))],
        out_specs=cur(0),
        out_shape=jax.ShapeDtypeStruct((n, n_pairs * PAIR), BF16),
        compiler_params=_params("parallel", "parallel", "parallel"),
        name="band_attention",
    )(proj, kv, kv, kv, kv, bias.reshape(-1, BAND))


def _top16(scores, payloads):
    nl = scores[0].shape[1]
    rids = [lax.broadcasted_iota(jnp.int32, s.shape, 0).astype(F32) for s in scores]
    slot = lax.broadcasted_iota(jnp.int32, (PEER_TOPK, nl), 0)

    def body(i, carry):
        sel = slot == i
        out = []
        for (s, vals, picks), rid, payload in zip(carry, rids, payloads):
            m = jnp.max(s, axis=0, keepdims=True)
            am = jnp.min(jnp.where(s == m, rid, float(s.shape[0])), axis=0, keepdims=True)
            hit = rid == am
            if payload is None:
                pick = am
            else:
                pick = jnp.sum(jnp.where(hit, payload, 0.0), axis=0, keepdims=True)
            out.append((jnp.where(hit, -jnp.inf, s), jnp.where(sel, m, vals), jnp.where(sel, pick, picks)))
        return tuple(out)

    zero = jnp.zeros((PEER_TOPK, nl), F32)
    res = lax.fori_loop(0, PEER_TOPK, body, tuple((s, zero, zero) for s in scores))
    return [(vals, picks) for _, vals, picks in res]


def _pair_candidates(a, b, combine):
    half = PEER_TOPK // 2
    rows = [combine(a[0:1], b)]
    rows += [combine(a[i:i + 1], b[:half]) for i in range(1, half)]
    rows.append(combine(a[half:], b[0:1]))
    return jnp.concatenate(rows, axis=0)


def _peer_topk_body(q_ref, keys_ref, eidx_ref, gate_ref):
    q = q_ref[...].astype(BF16)
    s1 = _dot_nt(keys_ref[0, 0].astype(BF16), q[:, :PEER_KEYS])
    s2 = _dot_nt(keys_ref[0, 1].astype(BF16), q[:, PEER_KEYS:])
    (a, i1), (b, i2) = _top16([s1, s2], [None, None])
    cand = _pair_candidates(a, b, lambda x, y: x + y)
    cidx = _pair_candidates(i1, i2, lambda x, y: x * PEER_KEYS + y)
    ((top, eidx),) = _top16([cand], [cidx])
    e = jnp.exp(top - top[0:1])
    gate_ref[...] = e / jnp.sum(e, axis=0, keepdims=True)
    eidx_ref[...] = eidx.astype(jnp.int32)


def peer_topk(q, keys, tl=128):
    n = q.shape[0]
    tl = min(tl, n)
    blk = pl.BlockSpec((PEER_TOPK, tl), lambda i, h: (h, i))
    return pl.pallas_call(
        _peer_topk_body,
        grid=(n // tl, PEER_HEADS),
        in_specs=[pl.BlockSpec((tl, 2 * PEER_KEYS), lambda i, h: (i, h)),
                  pl.BlockSpec((1, 2, PEER_KEYS, PEER_KEYS), lambda i, h: (h, 0, 0, 0))],
        out_specs=[blk, blk],
        out_shape=[jax.ShapeDtypeStruct((PEER_PICKS, n), jnp.int32),
                   jax.ShapeDtypeStruct((PEER_PICKS, n), F32)],
        compiler_params=_params("parallel", "parallel"),
        name="peer_topk",
    )(q, keys)


PEER_SLOTS = 8
LANES = 128
PITCH_PAD = 4


def _pack_expert_table(u, v):
    n_exp, d = u.shape
    hi = lax.bitcast_convert_type(u.astype(BF16), jnp.uint16).astype(jnp.uint32)
    lo = lax.bitcast_convert_type(v.astype(BF16), jnp.uint16).astype(jnp.uint32)
    return ((hi << 16) | lo).reshape(n_exp * (d // LANES), LANES)


def _peer_ffn_body(eidx_ref, gate_ref, hn_ref, x_ref, tab_ref, o_ref, *scratch, tb, d):
    rows_refs, sem_ref = scratch[:PEER_SLOTS], scratch[PEER_SLOTS]
    chunks = d // LANES
    pitch = chunks + PITCH_PAD

    def issue(t, slot, part=0, parts=1):
        per = PEER_PICKS // parts
        for e in range(part * per, (part + 1) * per):
            src = tab_ref.at[pl.ds(pl.multiple_of(eidx_ref[t, e] * chunks, chunks), chunks), :]
            dst = rows_refs[slot].at[pl.ds(e * pitch, chunks), :]
            pltpu.make_async_copy(src, dst, sem_ref.at[slot]).start(priority=e % 2)

    def wait(slot):
        total = PEER_PICKS * chunks
        pltpu.make_async_copy(tab_ref.at[pl.ds(0, total), :],
                              rows_refs[slot].at[pl.ds(0, total), :], sem_ref.at[slot]).wait()

    tok_lane = lax.broadcasted_iota(jnp.int32, (PEER_PICKS, tb), 1)
    hi_mask = jnp.uint32(0xFFFF0000)

    def words(slot, c):
        return rows_refs[slot][pl.ds(c, PEER_PICKS, stride=pitch), :]

    def pick_weights(t, slot):
        xt = hn_ref[pl.ds(t, 1), :]
        acc = jnp.zeros((PEER_PICKS, LANES), F32)
        for c in range(chunks):
            u = lax.bitcast_convert_type(words(slot, c) & hi_mask, F32)
            acc = acc + u * xt[:, c * LANES:(c + 1) * LANES]
        act = jnp.sum(acc, axis=1, keepdims=True)
        act = 0.5 * act * (1.0 + lax.erf(act * (2.0 ** -0.5)))
        gate = jnp.sum(jnp.where(tok_lane == t, gate_ref[...], 0.0), axis=1, keepdims=True)
        return jnp.broadcast_to(gate * act, (PEER_PICKS, LANES))

    group_row = lax.broadcasted_iota(jnp.int32, (PEER_SLOTS, d), 0)

    def combine(slot, w, acc):
        outs = []
        for c in range(chunks):
            v = lax.bitcast_convert_type(words(slot, c) << 16, F32)
            outs.append(jnp.sum(v * w, axis=0, keepdims=True))
        return jnp.where(group_row == slot, jnp.concatenate(outs, axis=1), acc)

    def store_group(g, acc):
        rows = pl.ds(pl.multiple_of(g * PEER_SLOTS, PEER_SLOTS), PEER_SLOTS)
        o_ref[rows, :] = x_ref[rows, :] + acc

    ahead = PEER_SLOTS - 1
    groups = tb // PEER_SLOTS
    for t in range(ahead):
        issue(t, t)
    wait(0)
    w0 = pick_weights(0, 0)

    def group(g, w):
        acc = jnp.zeros((PEER_SLOTS, d), F32)
        for s in range(PEER_SLOTS):
            t = g * PEER_SLOTS + s
            wait((s + 1) % PEER_SLOTS)
            issue(t + ahead, (s + ahead) % PEER_SLOTS)
            w_next = pick_weights(t + 1, (s + 1) % PEER_SLOTS)
            acc = combine(s, w, acc)
            w = w_next
        store_group(g, acc)
        return w

    w = lax.fori_loop(0, groups - 1, group, w0)

    acc = jnp.zeros((PEER_SLOTS, d), F32)
    for s in range(PEER_SLOTS):
        t = (groups - 1) * PEER_SLOTS + s
        if t + ahead < tb:
            issue(t + ahead, (s + ahead) % PEER_SLOTS)
        if t + 1 < tb:
            wait((s + 1) % PEER_SLOTS)
            w_next = pick_weights(t + 1, (s + 1) % PEER_SLOTS)
        acc = combine(s, w, acc)
        w = w_next
    store_group(groups - 1, acc)


def peer_expert_ffn(eidx, gate, hn, x, table, tb=128):
    n, d = x.shape
    tb = min(tb, n)
    assert tb % PEER_SLOTS == 0 and table.shape[1] == LANES
    assert table.shape[0] >= PEER_PICKS * (d // LANES)
    slot_rows = PEER_PICKS * (d // LANES + PITCH_PAD)
    return pl.pallas_call(
        functools.partial(_peer_ffn_body, tb=tb, d=d),
        grid=(n // tb,),
        in_specs=[pl.BlockSpec((tb, PEER_PICKS), lambda i: (i, 0), memory_space=pltpu.SMEM),
                  pl.BlockSpec((PEER_PICKS, tb), lambda i: (0, i)),
                  pl.BlockSpec((tb, d), lambda i: (i, 0)),
                  pl.BlockSpec((tb, d), lambda i: (i, 0)),
                  pl.BlockSpec(memory_space=pl.ANY)],
        out_specs=pl.BlockSpec((tb, d), lambda i: (i, 0)),
        out_shape=jax.ShapeDtypeStruct((n, d), F32),
        scratch_shapes=[pltpu.VMEM((slot_rows, LANES), jnp.uint32)] * PEER_SLOTS
                       + [pltpu.SemaphoreType.DMA((PEER_SLOTS,))],
        compiler_params=_params("arbitrary"),
        name="peer_expert_ffn",
    )(eidx, gate, hn, x, table)


def peer_layer(x, norm_g, wq, keys, u, v, tb=256):
    q, hn = norm_matmul(x, norm_g, wq.astype(BF16), emit_hn=True)
    eidx, gate = peer_topk(q, keys)
    return peer_expert_ffn(eidx.T, gate, hn, x, _pack_expert_table(u, v), tb=tb)


def _band_bias(rel_bias):
    pad = LEFT_CHUNKS * CHUNK
    dist = pad + np.arange(CHUNK)[:, None] - np.arange(BAND)[None, :]
    idx = np.clip(dist, -(CHUNK - 1), REL_MAX) + (CHUNK - 1)
    return rel_bias[:, idx].astype(F32)


def kernel(x, mem, norm_mix, norm_ffn, norm_mem, w_mem_kv, w_out, peer_wq, peer_keys, peer_u, peer_v, a_mix, a_w_in, a_w0, a_w1, a_w2, a_a0, a_a1, a_a2, a_g1, a_g2, a_k_k, a_k_a, a_r_k, a_ln_g, a_ln_b, kv_norm, w_kv_shared, b_w_in, b_rel_bias, final_norm):
    bsz, seq, d = x.shape
    n = bsz * seq
    n_mem = mem.shape[1]
    seq_width = a_w0.shape[1]
    n_pairs = seq_width // PAIR
    mem_width = MEM_HEADS * MEM_HEAD_DIM
    x = x.reshape(n, d)
    mem2 = mem.reshape(bsz * n_mem, d)

    def mixer_tail(x, seq_out, proj, q_col_block, layer):
        mkv = norm_matmul(mem2, norm_mem[layer], w_mem_kv[layer].astype(BF16), tm=256)
        mem_out = memory_attention(proj, q_col_block, mkv, seq, n_mem)
        wo = w_out[layer].astype(BF16)
        x = out_projection(seq_out, mem_out, wo[:seq_width], wo[seq_width:], x)
        return peer_layer(x, norm_ffn[layer], peer_wq[layer], peer_keys[layer], peer_u[layer], peer_v[layer])

    h = rmsnorm(x, norm_mix[0])
    mix_tab = jnp.concatenate([a_mix[0], jnp.zeros((1, d), F32)], axis=0)
    tiles = seq_width // 512
    gid_main = jnp.asarray([0] * tiles + [2] * tiles + [3] * tiles + [6] * (mem_width // 512), jnp.int32)
    proj = mix_matmul(h, mix_tab, gid_main, a_w_in[0].astype(BF16), seq, tn=512)
    rank = a_w1.shape[2]
    padc = lambda w: jnp.pad(w, ((0, 0), (0, LORA_PAD - rank)))
    padr = lambda w: jnp.pad(w, ((0, LORA_PAD - rank), (0, 0)))
    w_l1 = jnp.concatenate([padc(a_w1[0]), padc(a_a1[0]), a_g1[0]], axis=1).astype(BF16)
    gid_l1 = jnp.asarray([1, 4] + [5] * (a_g1.shape[2] // LORA_PAD), jnp.int32)
    t1 = mix_matmul(h, mix_tab, gid_l1, w_l1, seq, tn=LORA_PAD)
    lw, a_iclr, gate = lora_stage2(t1, padr(a_w2[0]).astype(BF16), padr(a_a2[0]).astype(BF16),
                                   a_g2[0].astype(BF16), a_w0[0], a_a0[0])
    prep = wkv_chunk_prepare(proj, lw, a_iclr, a_k_k[0], a_k_a[0], n_pairs)
    seq_out = wkv_chunk_scan(prep, proj, a_iclr, gate, a_k_a[0], a_r_k[0].reshape(-1),
                             a_ln_g[0], a_ln_b[0], n_pairs, seq)
    x = mixer_tail(x, seq_out, proj, (3 * seq_width) // mem_width, 0)

    kv = norm_matmul(x, kv_norm, w_kv_shared.astype(BF16))

    proj = norm_matmul(x, norm_mix[1], b_w_in[0].astype(BF16))
    seq_out = band_attention(proj, kv, _band_bias(b_rel_bias[0]), n_pairs, seq)
    x = mixer_tail(x, seq_out, proj, seq_width // mem_width, 1)

    return rmsnorm(x, final_norm).reshape(bsz, seq, d)
```

```python
import functools

import numpy as np
import jax
import jax.numpy as jnp
from jax import lax
from jax.experimental import pallas as pl
from jax.experimental.pallas import tpu as pltpu

F32 = jnp.float32
BF16 = jnp.bfloat16
HI = lax.Precision.HIGHEST

HEAD_DIM = 64
PAIR = 2 * HEAD_DIM
CHUNK = 64
SCAN_PAIRS = 2
LEFT_CHUNKS = 8
BAND = (LEFT_CHUNKS + 1) * CHUNK
REL_MAX = 128
MEM_HEADS = 4
MEM_HEAD_DIM = 128
PEER_KEYS = 128
PEER_HEADS = 8
PEER_TOPK = 16
PEER_PICKS = PEER_HEADS * PEER_TOPK
GN_EPS = 64e-5
RMS_EPS = 1e-6
LORA_PAD = 128
VMEM_LIMIT = 48 * 1024 * 1024


def _params(*sem):
    return pltpu.CompilerParams(dimension_semantics=sem, vmem_limit_bytes=VMEM_LIMIT)


def _dot(a, b):
    return jnp.dot(a, b, preferred_element_type=F32)


def _dot_nt(a, b, precision=None):
    return lax.dot_general(a, b, (((1,), (1,)), ((), ())), precision=precision,
                           preferred_element_type=F32)


def _dot_tn(a, b, precision=None):
    return lax.dot_general(a, b, (((0,), (0,)), ((), ())), precision=precision,
                           preferred_element_type=F32)


def _rmsnorm_body(x_ref, g_ref, o_ref):
    x = x_ref[...]
    ms = jnp.mean(x * x, axis=-1, keepdims=True)
    o_ref[...] = (x * lax.rsqrt(ms + RMS_EPS) * g_ref[...]).astype(o_ref.dtype)


def rmsnorm(x, g, tm=512):
    n, d = x.shape
    tm = min(tm, n)
    return pl.pallas_call(
        _rmsnorm_body,
        grid=(n // tm,),
        in_specs=[pl.BlockSpec((tm, d), lambda i: (i, 0)), pl.BlockSpec((1, d), lambda i: (0, 0))],
        out_specs=pl.BlockSpec((tm, d), lambda i: (i, 0)),
        out_shape=jax.ShapeDtypeStruct((n, d), F32),
        compiler_params=_params("parallel"),
        name="rmsnorm",
    )(x, g.reshape(1, d))


def _normmm_body(x_ref, g_ref, w_ref, o_ref, *rest, emit_hn):
    lhs_ref = rest[-1]

    @pl.when(pl.program_id(1) == 0)
    def _():
        x = x_ref[...]
        ms = jnp.mean(x * x, axis=-1, keepdims=True)
        hn = x * lax.rsqrt(ms + RMS_EPS) * g_ref[...]
        lhs_ref[...] = hn.astype(BF16)
        if emit_hn:
            rest[0][...] = hn

    o_ref[...] = _dot(lhs_ref[...], w_ref[...]).astype(o_ref.dtype)


def norm_matmul(x, g, w_bf16, emit_hn=False, tm=512, tn=1024, out_dtype=BF16):
    n, d = x.shape
    nc = w_bf16.shape[1]
    tm, tn = min(tm, n), min(tn, nc)
    out_shape = [jax.ShapeDtypeStruct((n, nc), out_dtype)]
    out_specs = [pl.BlockSpec((tm, tn), lambda i, j: (i, j))]
    if emit_hn:
        out_shape.append(jax.ShapeDtypeStruct((n, d), F32))
        out_specs.append(pl.BlockSpec((tm, d), lambda i, j: (i, 0)))
    res = pl.pallas_call(
        functools.partial(_normmm_body, emit_hn=emit_hn),
        grid=(n // tm, nc // tn),
        in_specs=[pl.BlockSpec((tm, d), lambda i, j: (i, 0)),
                  pl.BlockSpec((1, d), lambda i, j: (0, 0)),
                  pl.BlockSpec((d, tn), lambda i, j: (0, j))],
        out_specs=out_specs,
        out_shape=out_shape,
        scratch_shapes=[pltpu.VMEM((tm, d), BF16)],
        compiler_params=_params("parallel", "arbitrary"),
        name="norm_matmul",
    )(x, g.reshape(1, d), w_bf16)
    return res if emit_hn else res[0]


def _mixmm_body(gid_ref, h_ref, hp_ref, mix_ref, w_ref, o_ref, lhs_ref, *, tm, seq):
    i = pl.program_id(0)
    j = pl.program_id(1)
    new_group = jnp.logical_or(j == 0, gid_ref[j] != gid_ref[jnp.maximum(j - 1, 0)])

    @pl.when(new_group)
    def _():
        h = h_ref[...]
        prev = jnp.where((i * tm) % seq == 0, 0.0, hp_ref[7:8, :])
        row = lax.broadcasted_iota(jnp.int32, h.shape, 0)
        shifted = jnp.where(row == 0, prev, pltpu.roll(h, 1, 0))
        lhs_ref[...] = (h + (shifted - h) * mix_ref[0]).astype(BF16)

    o_ref[...] = _dot(lhs_ref[...], w_ref[...])


def mix_matmul(h, mix_tab, gid, w_bf16, seq, tn, tm=512):
    n, d = h.shape
    nc = w_bf16.shape[1]
    tm = min(tm, seq)
    sub = tm // 8
    grid_spec = pltpu.PrefetchScalarGridSpec(
        num_scalar_prefetch=1,
        grid=(n // tm, nc // tn),
        in_specs=[pl.BlockSpec((tm, d), lambda i, j, g: (i, 0)),
                  pl.BlockSpec((8, d), lambda i, j, g: (jnp.maximum(i * sub - 1, 0), 0)),
                  pl.BlockSpec((1, 1, d), lambda i, j, g: (g[j], 0, 0)),
                  pl.BlockSpec((d, tn), lambda i, j, g: (0, j))],
        out_specs=pl.BlockSpec((tm, tn), lambda i, j, g: (i, j)),
        scratch_shapes=[pltpu.VMEM((tm, d), BF16)],
    )
    return pl.pallas_call(
        functools.partial(_mixmm_body, tm=tm, seq=seq),
        grid_spec=grid_spec,
        out_shape=jax.ShapeDtypeStruct((n, nc), F32),
        compiler_params=_params("parallel", "arbitrary"),
        name="mix_matmul",
    )(gid, h, h, mix_tab.reshape(mix_tab.shape[0], 1, d), w_bf16)


def _sigmoid(x):
    return 1.0 / (1.0 + jnp.exp(-x))


def _lora2_body(t_ref, w2_ref, a2_ref, g2_ref, w0_ref, a0_ref, lw_ref, a_ref, g_ref):
    t = t_ref[...]
    tw = jnp.tanh(t[:, :LORA_PAD]).astype(BF16)
    ta = t[:, LORA_PAD:2 * LORA_PAD].astype(BF16)
    tg = _sigmoid(t[:, 2 * LORA_PAD:]).astype(BF16)
    u = w0_ref[...] + _dot(tw, w2_ref[...])
    softplus_neg_u = jnp.maximum(-u, 0.0) + jnp.log(1.0 + jnp.exp(-jnp.abs(u)))
    lw_ref[...] = -jnp.exp(-softplus_neg_u - 0.5)
    a_ref[...] = _sigmoid(a0_ref[...] + _dot(ta, a2_ref[...]))
    g_ref[...] = _dot(tg, g2_ref[...])


def lora_stage2(t1, w2p, a2p, g2, w0, a0, tm=256):
    n = t1.shape[0]
    width = w2p.shape[1]
    tm = min(tm, n)
    full = lambda a: pl.BlockSpec(a.shape, lambda i: (0, 0))
    row = pl.BlockSpec((tm, width), lambda i: (i, 0))
    w0 = w0.reshape(1, width)
    a0 = a0.reshape(1, width)
    return pl.pallas_call(
        _lora2_body,
        grid=(n // tm,),
        in_specs=[pl.BlockSpec((tm, t1.shape[1]), lambda i: (i, 0)), full(w2p), full(a2p), full(g2),
                  full(w0), full(a0)],
        out_specs=[row, row, row],
        out_shape=[jax.ShapeDtypeStruct((n, width), F32)] * 3,
        compiler_params=_params("parallel"),
        name="lora_stage2",
    )(t1, w2p, a2p, g2, w0, a0)


def _head_group_sum(x):
    width = x.shape[-1]
    r = lax.broadcasted_iota(jnp.int32, (width, width), 0) // HEAD_DIM
    c = lax.broadcasted_iota(jnp.int32, (width, width), 1) // HEAD_DIM
    ones = jnp.where(r == c, 1.0, 0.0).astype(F32)
    return jnp.dot(x, ones, precision=HI, preferred_element_type=F32)


def _wkv_prepare_body(r_ref, k_ref, v_ref, lw_ref, a_ref, kk_ref, ka_ref,
                      g_ref, s0c_ref, rp_ref, y0_ref, *, tb):
    r = r_ref[...]
    k = k_ref[...]
    v = v_ref[...]
    lw = lw_ref[...]
    a = a_ref[...]
    kk = k * kk_ref[...]
    norm = jnp.sqrt(_head_group_sum(kk * kk))
    kk = kk / jnp.maximum(norm, 1e-12)
    kmod = k * (1.0 + (a - 1.0) * ka_ref[...])
    avec = -kk
    bvec = kk * a

    tr = lax.broadcasted_iota(jnp.int32, (tb, tb), 0)
    tc = lax.broadcasted_iota(jnp.int32, (tb, tb), 1)
    tri = jnp.where((tr >= tc) & (tr // CHUNK == tc // CHUNK), 1.0, 0.0).astype(F32)
    cum = jnp.dot(tri, lw, precision=HI, preferred_element_type=F32)

    lane = lax.broadcasted_iota(jnp.int32, (CHUNK, PAIR), 1)
    head0 = lane < HEAD_DIM
    row = lax.broadcasted_iota(jnp.int32, (PAIR, PAIR), 0)
    col = lax.broadcasted_iota(jnp.int32, (PAIR, PAIR), 1)
    strict = row > col
    lower = row >= col
    eye = jnp.where(row == col, 1.0, 0.0).astype(F32)

    def stack(x):
        return jnp.concatenate([jnp.where(head0, x, 0.0), jnp.where(head0, 0.0, x)], axis=0)

    chunks = range(tb // CHUNK)
    pre = []
    for c in chunks:
        sl = slice(c * CHUNK, (c + 1) * CHUNK)
        cm = cum[sl]
        cend = cm[CHUNK - 1:CHUNK]
        e_in = jnp.exp(cm)
        e_out = jnp.exp(-cm)
        e_tail = jnp.exp(cend - cm)
        a_s = stack(avec[sl] * jnp.exp(cm - lw[sl]))
        r_s = stack(r[sl] * e_in)
        b_s = stack(bvec[sl] * e_out)
        k_s = stack(kmod[sl] * e_out)
        bt_s = stack(bvec[sl] * e_tail)
        kt_s = stack(kmod[sl] * e_tail)
        v_s = stack(v[sl])
        p = _dot_nt(jnp.concatenate([a_s, r_s], axis=0).astype(BF16),
                    jnp.concatenate([b_s, k_s], axis=0).astype(BF16))
        pre.append(dict(cend=cend, a_s=a_s, r_s=r_s, bt_s=bt_s, kt_s=kt_s, v_s=v_s,
                        l_ab=jnp.where(strict, p[:PAIR, :PAIR], 0.0),
                        l_ak=jnp.where(strict, p[:PAIR, PAIR:], 0.0),
                        a_rb=jnp.where(lower, p[PAIR:, :PAIR], 0.0),
                        a_rk=jnp.where(lower, p[PAIR:, PAIR:], 0.0)))

    ms = [q["l_ab"] for q in pre]
    ts = [eye + m for m in ms]
    for _ in range(5):
        ms = [_dot(m.astype(BF16), m.astype(BF16)) for m in ms]
        ts = [t + _dot(t.astype(BF16), m.astype(BF16)) for t, m in zip(ts, ms)]

    lvs = [_dot(q["l_ak"].astype(BF16), q["v_s"].astype(BF16)) for q in pre]
    aus = [_dot(t.astype(BF16), jnp.concatenate([q["a_s"], lv], axis=1).astype(BF16))
           for t, q, lv in zip(ts, pre, lvs)]
    xs = [_dot(q["a_rb"].astype(BF16), au.astype(BF16)) for q, au in zip(pre, aus)]
    for c, q, au, x in zip(chunks, pre, aus, xs):
        rp_ref[0, c] = q["r_s"] + x[:, :PAIR]
        y0_ref[0, c] = x[:, PAIR:] + _dot(q["a_rk"].astype(BF16), q["v_s"].astype(BF16))
        ap = au[:, :PAIR]
        u0 = au[:, PAIR:]
        g_ref[0, c] = eye * jnp.exp(q["cend"]) + _dot_tn(ap.astype(BF16), q["bt_s"].astype(BF16))
        s0c_ref[0, c] = _dot_tn(jnp.concatenate([u0, q["v_s"]], axis=0).astype(BF16),
                                jnp.concatenate([q["bt_s"], q["kt_s"]], axis=0).astype(BF16))


def wkv_chunk_prepare(proj, lw, a, k_k, k_a, n_pairs, tb=512):
    n = proj.shape[0]
    tb = min(tb, n)
    cpb = tb // CHUNK
    col = lambda off: pl.BlockSpec((tb, PAIR), lambda i, p: (i, off + p))
    par = pl.BlockSpec((1, PAIR), lambda i, p: (0, p))
    blk = pl.BlockSpec((1, cpb, PAIR, PAIR), lambda i, p: (p, i, 0, 0))
    shp = jax.ShapeDtypeStruct((n_pairs, n // CHUNK, PAIR, PAIR), F32)
    return pl.pallas_call(
        functools.partial(_wkv_prepare_body, tb=tb),
        grid=(n // tb, n_pairs),
        in_specs=[col(0), col(n_pairs), col(2 * n_pairs), col(0), col(0), par, par],
        out_specs=[blk] * 4,
        out_shape=[shp] * 4,
        compiler_params=_params("parallel", "parallel"),
        name="wkv_chunk_prepare",
    )(proj, proj, proj, lw, a, k_k.reshape(1, -1), k_a.reshape(1, -1))


def _wkv_scan_body(g_ref, s0c_ref, rp_ref, y0_ref, r_ref, k_ref, v_ref, a_ref, gate_ref,
                   ka_ref, rk_ref, lng_ref, lnb_ref, o_ref, s_ref, y_ref, *, cpb):
    @pl.when(pl.program_id(2) == 0)
    def _():
        s_ref[...] = jnp.zeros_like(s_ref)

    pairs = range(SCAN_PAIRS)
    states = [[s_ref[w].astype(BF16)] for w in pairs]
    for c in range(cpb):
        for w in pairs:
            s = _dot(states[w][c], g_ref[w, c].astype(BF16)) + s0c_ref[w, c]
            if c + 1 < cpb:
                states[w].append(s.astype(BF16))
            else:
                s_ref[w] = s
    for c in range(cpb):
        for w in pairs:
            y_st = _dot_nt(rp_ref[w, c].astype(BF16), states[w][c]) + y0_ref[w, c]
            y_ref[c * CHUNK:(c + 1) * CHUNK, w * PAIR:(w + 1) * PAIR] = y_st[:CHUNK] + y_st[CHUNK:]

    y = y_ref[...]
    inv = 1.0 / HEAD_DIM
    mu = _head_group_sum(y) * inv
    yc = y - mu
    var = _head_group_sum(yc * yc) * inv
    yn = yc * lax.rsqrt(var + GN_EPS) * lng_ref[...] + lnb_ref[...]
    r = r_ref[...]
    kmod = k_ref[...] * (1.0 + (a_ref[...] - 1.0) * ka_ref[...])
    bonus = _head_group_sum(r * kmod * rk_ref[...]) * v_ref[...]
    o_ref[...] = ((yn + bonus) * gate_ref[...]).astype(o_ref.dtype)


def wkv_chunk_scan(prep, proj, a, gate, k_a, r_k, ln_g, ln_b, n_pairs, seq, tb=512):
    g_all, s0c_all, rp_all, y0_all = prep
    n = proj.shape[0]
    tb = min(tb, seq)
    cpb = tb // CHUNK
    nblk = seq // tb
    assert n_pairs % SCAN_PAIRS == 0
    width = SCAN_PAIRS * PAIR
    groups = n_pairs // SCAN_PAIRS
    blk = pl.BlockSpec((SCAN_PAIRS, cpb, PAIR, PAIR), lambda b, p, t: (p, b * nblk + t, 0, 0))
    col = lambda off: pl.BlockSpec((tb, width), lambda b, p, t: (b * nblk + t, off + p))
    par = pl.BlockSpec((1, width), lambda b, p, t: (0, p))
    vec = lambda z: z.reshape(1, -1)
    return pl.pallas_call(
        functools.partial(_wkv_scan_body, cpb=cpb),
        grid=(n // seq, groups, nblk),
        in_specs=[blk] * 4 + [col(0), col(groups), col(2 * groups), col(0), col(0)] + [par] * 4,
        out_specs=col(0),
        out_shape=jax.ShapeDtypeStruct((n, n_pairs * PAIR), BF16),
        scratch_shapes=[pltpu.VMEM((SCAN_PAIRS, PAIR, PAIR), F32), pltpu.VMEM((tb, width), F32)],
        compiler_params=_params("parallel", "parallel", "arbitrary"),
        name="wkv_chunk_scan",
    )(g_all, s0c_all, rp_all, y0_all, proj, proj, proj, a, gate,
      vec(k_a), vec(r_k), vec(ln_g), vec(ln_b))


def _memattn_body(q_ref, m_ref, o_ref):
    width = MEM_HEADS * MEM_HEAD_DIM
    scale = MEM_HEAD_DIM ** -0.5
    for h in range(MEM_HEADS):
        sl = slice(h * MEM_HEAD_DIM, (h + 1) * MEM_HEAD_DIM)
        q = q_ref[:, sl].astype(BF16)
        mk = m_ref[:, sl].astype(BF16)
        mv = m_ref[:, width + h * MEM_HEAD_DIM:width + (h + 1) * MEM_HEAD_DIM].astype(BF16)
        s = _dot_nt(q, mk) * scale
        e = jnp.exp(s - jnp.max(s, axis=-1, keepdims=True))
        p = e / jnp.sum(e, axis=-1, keepdims=True)
        o_ref[:, sl] = _dot(p.astype(BF16), mv).astype(o_ref.dtype)


def memory_attention(proj, q_col_block, mkv, seq, n_mem, tm=512):
    n = proj.shape[0]
    width = MEM_HEADS * MEM_HEAD_DIM
    tm = min(tm, seq)
    nblk = seq // tm
    return pl.pallas_call(
        _memattn_body,
        grid=(n // seq, nblk),
        in_specs=[pl.BlockSpec((tm, width), lambda b, t: (b * nblk + t, q_col_block)),
                  pl.BlockSpec((n_mem, 2 * width), lambda b, t: (b, 0))],
        out_specs=pl.BlockSpec((tm, width), lambda b, t: (b * nblk + t, 0)),
        out_shape=jax.ShapeDtypeStruct((n, width), BF16),
        compiler_params=_params("parallel", "parallel"),
        name="memory_attention",
    )(proj, mkv)


def _outproj_body(s_ref, m_ref, w1_ref, w2_ref, x_ref, o_ref):
    o_ref[...] = x_ref[...] + _dot(s_ref[...], w1_ref[...]) + _dot(m_ref[...], w2_ref[...])


def out_projection(seq_out, mem_out, w_seq, w_mem, x, tm=512, tn=512):
    n, d = x.shape
    tm, tn = min(tm, n), min(tn, d)
    ws, wm = seq_out.shape[1], mem_out.shape[1]
    return pl.pallas_call(
        _outproj_body,
        grid=(n // tm, d // tn),
        in_specs=[pl.BlockSpec((tm, ws), lambda i, j: (i, 0)),
                  pl.BlockSpec((tm, wm), lambda i, j: (i, 0)),
                  pl.BlockSpec((ws, tn), lambda i, j: (0, j)),
                  pl.BlockSpec((wm, tn), lambda i, j: (0, j)),
                  pl.BlockSpec((tm, tn), lambda i, j: (i, j))],
        out_specs=pl.BlockSpec((tm, tn), lambda i, j: (i, j)),
        out_shape=jax.ShapeDtypeStruct((n, d), F32),
        compiler_params=_params("parallel", "parallel"),
        name="out_projection",
    )(seq_out, mem_out, w_seq, w_mem, x)


def _bandattn_body(q_ref, kp_ref, kc_ref, vp_ref, vc_ref, bias_ref, o_ref, *, tq):
    qi = pl.program_id(2)
    scale = HEAD_DIM ** -0.5
    k_all = jnp.concatenate([kp_ref[...], kc_ref[...]], axis=0).astype(BF16)
    v_all = jnp.concatenate([vp_ref[...], vc_ref[...]], axis=0).astype(BF16)
    lane = lax.broadcasted_iota(jnp.int32, (CHUNK, PAIR), 1)
    head0 = lane < HEAD_DIM
    kcol = lax.broadcasted_iota(jnp.int32, (2 * CHUNK, BAND), 1)
    pad = LEFT_CHUNKS * CHUNK
    bias = bias_ref[...]
    chunks = range(tq // CHUNK)
    starts = [tq - pad + j * CHUNK for j in chunks]
    scores = []
    for j, start in zip(chunks, starts):
        q = q_ref[j * CHUNK:(j + 1) * CHUNK, :]
        q2 = jnp.concatenate([jnp.where(head0, q, 0.0), jnp.where(head0, 0.0, q)], axis=0).astype(BF16)
        scores.append(_dot_nt(q2, k_all[start:start + BAND]))
    exps = []
    for start, s in zip(starts, scores):
        valid = jnp.logical_or(qi > 0, kcol + start >= tq)
        s = jnp.where(valid, s * scale + bias, -jnp.inf)
        exps.append(jnp.exp(s - jnp.max(s, axis=-1, keepdims=True)))
    outs = [_dot(e.astype(BF16), v_all[start:start + BAND]) for start, e in zip(starts, exps)]
    for j, e, o in zip(chunks, exps, outs):
        o = o / jnp.sum(e, axis=-1, keepdims=True)
        o_ref[j * CHUNK:(j + 1) * CHUNK, :] = jnp.where(head0, o[:CHUNK], o[CHUNK:]).astype(o_ref.dtype)


def band_attention(proj, kv, bias, n_pairs, seq, tq=512):
    n = proj.shape[0]
    tq = min(tq, seq)
    assert tq >= LEFT_CHUNKS * CHUNK
    nblk = seq // tq
    cur = lambda off: pl.BlockSpec((tq, PAIR), lambda b, p, t: (b * nblk + t, off + p))
    prev = lambda off: pl.BlockSpec((tq, PAIR), lambda b, p, t: (b * nblk + jnp.maximum(t - 1, 0), off + p))
    return pl.pallas_call(
        functools.partial(_bandattn_body, tq=tq),
        grid=(n // seq, n_pairs, nblk),
        in_specs=[cur(0), prev(0), cur(0), prev(n_pairs), cur(n_pairs),
                  pl.BlockSpec((2 * CHUNK, BAND), lambda b, p, t: (p, 0))],
        out_specs=cur(0),
        out_shape=jax.ShapeDtypeStruct((n, n_pairs * PAIR), BF16),
        compiler_params=_params("parallel", "parallel", "parallel"),
        name="band_attention",
    )(proj, kv, kv, kv, kv, bias.reshape(-1, BAND))


def _top16(scores, payloads):
    nl = scores[0].shape[1]
    rids = [lax.broadcasted_iota(jnp.int32, s.shape, 0).astype(F32) for s in scores]
    slot = lax.broadcasted_iota(jnp.int32, (PEER_TOPK, nl), 0)

    def body(i, carry):
        sel = slot == i
        out = []
        for (s, vals, picks), rid, payload in zip(carry, rids, payloads):
            m = jnp.max(s, axis=0, keepdims=True)
            am = jnp.min(jnp.where(s == m, rid, float(s.shape[0])), axis=0, keepdims=True)
            hit = rid == am
            if payload is None:
                pick = am
            else:
                pick = jnp.sum(jnp.where(hit, payload, 0.0), axis=0, keepdims=True)
            out.append((jnp.where(hit, -jnp.inf, s), jnp.where(sel, m, vals), jnp.where(sel, pick, picks)))
        return tuple(out)

    zero = jnp.zeros((PEER_TOPK, nl), F32)
    res = lax.fori_loop(0, PEER_TOPK, body, tuple((s, zero, zero) for s in scores))
    return [(vals, picks) for _, vals, picks in res]


def _pair_candidates(a, b, combine):
    half = PEER_TOPK // 2
    rows = [combine(a[0:1], b)]
    rows += [combine(a[i:i + 1], b[:half]) for i in range(1, half)]
    rows.append(combine(a[half:], b[0:1]))
    return jnp.concatenate(rows, axis=0)


TOPK_HEADS = 2


def _peer_topk_body(q_ref, keys_ref, eidx_ref, gate_ref):
    q = q_ref[...].astype(BF16)
    scores = []
    for h in range(TOPK_HEADS):
        for half in range(2):
            col = (2 * h + half) * PEER_KEYS
            scores.append(_dot_nt(keys_ref[h, half].astype(BF16), q[:, col:col + PEER_KEYS]))
    cands, cidxs = [], []
    for h in range(TOPK_HEADS):
        (a, i1), (b, i2) = _top16(scores[2 * h:2 * h + 2], [None, None])
        cands.append(_pair_candidates(a, b, lambda x, y: x + y))
        cidxs.append(_pair_candidates(i1, i2, lambda x, y: x * PEER_KEYS + y))
    for h, (top, eidx) in enumerate(_top16(cands, cidxs)):
        rows = slice(h * PEER_TOPK, (h + 1) * PEER_TOPK)
        e = jnp.exp(top - top[0:1])
        gate_ref[rows, :] = e / jnp.sum(e, axis=0, keepdims=True)
        eidx_ref[rows, :] = eidx.astype(jnp.int32)


def peer_topk(q, keys, tl=128):
    n = q.shape[0]
    tl = min(tl, n)
    blk = pl.BlockSpec((TOPK_HEADS * PEER_TOPK, tl), lambda i, h: (h, i))
    return pl.pallas_call(
        _peer_topk_body,
        grid=(n // tl, PEER_HEADS // TOPK_HEADS),
        in_specs=[pl.BlockSpec((tl, TOPK_HEADS * 2 * PEER_KEYS), lambda i, h: (i, h)),
                  pl.BlockSpec((TOPK_HEADS, 2, PEER_KEYS, PEER_KEYS), lambda i, h: (h, 0, 0, 0))],
        out_specs=[blk, blk],
        out_shape=[jax.ShapeDtypeStruct((PEER_PICKS, n), jnp.int32),
                   jax.ShapeDtypeStruct((PEER_PICKS, n), F32)],
        compiler_params=_params("parallel", "parallel"),
        name="peer_topk",
    )(q, keys)


PEER_SLOTS = 8
LANES = 128
PITCH_PAD = 4


def _pack_body(u_ref, v_ref, o_ref):
    hi = lax.bitcast_convert_type(u_ref[...].astype(BF16).astype(F32), jnp.uint32)
    lo = lax.bitcast_convert_type(v_ref[...].astype(BF16).astype(F32), jnp.uint32)
    words = hi | (lo >> 16)
    for c in range(o_ref.shape[1]):
        o_ref[:, c, :] = words[:, c * LANES:(c + 1) * LANES]


def _pack_expert_table(u, v, te=256):
    n_exp, d = u.shape
    chunks = d // LANES
    te = min(te, n_exp)
    packed = pl.pallas_call(
        _pack_body,
        grid=(n_exp // te,),
        in_specs=[pl.BlockSpec((te, d), lambda i: (i, 0))] * 2,
        out_specs=pl.BlockSpec((te, chunks, LANES), lambda i: (i, 0, 0)),
        out_shape=jax.ShapeDtypeStruct((n_exp, chunks, LANES), jnp.uint32),
        compiler_params=_params("parallel"),
        name="pack_expert_table",
    )(u, v)
    return packed.reshape(n_exp * chunks, LANES)


def _peer_ffn_body(eidx_ref, gate_ref, hn_ref, x_ref, tab_ref, o_ref, *scratch, tb, d):
    rows_refs, sem_ref = scratch[:PEER_SLOTS], scratch[PEER_SLOTS]
    chunks = d // LANES
    pitch = chunks + PITCH_PAD

    def issue(t, slot, part=0, parts=1):
        per = PEER_PICKS // parts
        for e in range(part * per, (part + 1) * per):
            src = tab_ref.at[pl.ds(pl.multiple_of(eidx_ref[t, e] * chunks, chunks), chunks), :]
            dst = rows_refs[slot].at[pl.ds(e * pitch, chunks), :]
            pltpu.make_async_copy(src, dst, sem_ref.at[slot]).start(priority=e % 2)

    def wait(slot):
        total = PEER_PICKS * chunks
        pltpu.make_async_copy(tab_ref.at[pl.ds(0, total), :],
                              rows_refs[slot].at[pl.ds(0, total), :], sem_ref.at[slot]).wait()

    tok_lane = lax.broadcasted_iota(jnp.int32, (PEER_PICKS, tb), 1)
    hi_mask = jnp.uint32(0xFFFF0000)

    def words(slot, c):
        return rows_refs[slot][pl.ds(c, PEER_PICKS, stride=pitch), :]

    def pick_weights(t, slot):
        xt = hn_ref[pl.ds(t, 1), :]
        acc = jnp.zeros((PEER_PICKS, LANES), F32)
        for c in range(chunks):
            u = lax.bitcast_convert_type(words(slot, c) & hi_mask, F32)
            acc = acc + u * xt[:, c * LANES:(c + 1) * LANES]
        act = jnp.sum(acc, axis=1, keepdims=True)
        act = 0.5 * act * (1.0 + lax.erf(act * (2.0 ** -0.5)))
        gate = jnp.sum(jnp.where(tok_lane == t, gate_ref[...], 0.0), axis=1, keepdims=True)
        return jnp.broadcast_to(gate * act, (PEER_PICKS, LANES))

    group_row = lax.broadcasted_iota(jnp.int32, (PEER_SLOTS, d), 0)

    def combine(slot, w, acc):
        outs = []
        for c in range(chunks):
            v = lax.bitcast_convert_type(words(slot, c) << 16, F32)
            outs.append(jnp.sum(v * w, axis=0, keepdims=True))
        return jnp.where(group_row == slot, jnp.concatenate(outs, axis=1), acc)

    def store_group(g, acc):
        rows = pl.ds(pl.multiple_of(g * PEER_SLOTS, PEER_SLOTS), PEER_SLOTS)
        o_ref[rows, :] = x_ref[rows, :] + acc

    ahead = PEER_SLOTS - 1
    groups = tb // PEER_SLOTS
    for t in range(ahead):
        issue(t, t)
    wait(0)
    w0 = pick_weights(0, 0)

    def group(g, w):
        acc = jnp.zeros((PEER_SLOTS, d), F32)
        for s in range(PEER_SLOTS):
            t = g * PEER_SLOTS + s
            wait((s + 1) % PEER_SLOTS)
            issue(t + ahead, (s + ahead) % PEER_SLOTS)
            w_next = pick_weights(t + 1, (s + 1) % PEER_SLOTS)
            acc = combine(s, w, acc)
            w = w_next
        store_group(g, acc)
        return w

    w = lax.fori_loop(0, groups - 1, group, w0)

    acc = jnp.zeros((PEER_SLOTS, d), F32)
    for s in range(PEER_SLOTS):
        t = (groups - 1) * PEER_SLOTS + s
        if t + ahead < tb:
            issue(t + ahead, (s + ahead) % PEER_SLOTS)
        if t + 1 < tb:
            wait((s + 1) % PEER_SLOTS)
            w_next = pick_weights(t + 1, (s + 1) % PEER_SLOTS)
        acc = combine(s, w, acc)
        w = w_next
    store_group(groups - 1, acc)


def peer_expert_ffn(eidx, gate, hn, x, table, tb=128):
    n, d = x.shape
    tb = min(tb, n)
    assert tb % PEER_SLOTS == 0 and table.shape[1] == LANES
    assert table.shape[0] >= PEER_PICKS * (d // LANES)
    slot_rows = PEER_PICKS * (d // LANES + PITCH_PAD)
    return pl.pallas_call(
        functools.partial(_peer_ffn_body, tb=tb, d=d),
        grid=(n // tb,),
        in_specs=[pl.BlockSpec((tb, PEER_PICKS), lambda i: (i, 0), memory_space=pltpu.SMEM),
                  pl.BlockSpec((PEER_PICKS, tb), lambda i: (0, i)),
                  pl.BlockSpec((tb, d), lambda i: (i, 0)),
                  pl.BlockSpec((tb, d), lambda i: (i, 0)),
                  pl.BlockSpec(memory_space=pl.ANY)],
        out_specs=pl.BlockSpec((tb, d), lambda i: (i, 0)),
        out_shape=jax.ShapeDtypeStruct((n, d), F32),
        scratch_shapes=[pltpu.VMEM((slot_rows, LANES), jnp.uint32)] * PEER_SLOTS
                       + [pltpu.SemaphoreType.DMA((PEER_SLOTS,))],
        compiler_params=_params("arbitrary"),
        name="peer_expert_ffn",
    )(eidx, gate, hn, x, table)


def peer_layer(x, norm_g, wq, keys, u, v, tb=256):
    q, hn = norm_matmul(x, norm_g, wq.astype(BF16), emit_hn=True)
    eidx, gate = peer_topk(q, keys)
    return peer_expert_ffn(eidx.T, gate, hn, x, _pack_expert_table(u, v), tb=tb)


def _band_bias(rel_bias):
    pad = LEFT_CHUNKS * CHUNK
    dist = pad + np.arange(CHUNK)[:, None] - np.arange(BAND)[None, :]
    idx = np.clip(dist, -(CHUNK - 1), REL_MAX) + (CHUNK - 1)
    return rel_bias[:, idx].astype(F32)


def kernel(x, mem, norm_mix, norm_ffn, norm_mem, w_mem_kv, w_out, peer_wq, peer_keys, peer_u, peer_v, a_mix, a_w_in, a_w0, a_w1, a_w2, a_a0, a_a1, a_a2, a_g1, a_g2, a_k_k, a_k_a, a_r_k, a_ln_g, a_ln_b, kv_norm, w_kv_shared, b_w_in, b_rel_bias, final_norm):
    bsz, seq, d = x.shape
    n = bsz * seq
    n_mem = mem.shape[1]
    seq_width = a_w0.shape[1]
    n_pairs = seq_width // PAIR
    mem_width = MEM_HEADS * MEM_HEAD_DIM
    x = x.reshape(n, d)
    mem2 = mem.reshape(bsz * n_mem, d)

    def mixer_tail(x, seq_out, proj, q_col_block, layer):
        mkv = norm_matmul(mem2, norm_mem[layer], w_mem_kv[layer].astype(BF16), tm=256)
        mem_out = memory_attention(proj, q_col_block, mkv, seq, n_mem)
        wo = w_out[layer].astype(BF16)
        x = out_projection(seq_out, mem_out, wo[:seq_width], wo[seq_width:], x)
        return peer_layer(x, norm_ffn[layer], peer_wq[layer], peer_keys[layer], peer_u[layer], peer_v[layer])

    h = rmsnorm(x, norm_mix[0])
    mix_tab = jnp.concatenate([a_mix[0], jnp.zeros((1, d), F32)], axis=0)
    tiles = seq_width // 512
    gid_main = jnp.asarray([0] * tiles + [2] * tiles + [3] * tiles + [6] * (mem_width // 512), jnp.int32)
    proj = mix_matmul(h, mix_tab, gid_main, a_w_in[0].astype(BF16), seq, tn=512)
    rank = a_w1.shape[2]
    padc = lambda w: jnp.pad(w, ((0, 0), (0, LORA_PAD - rank)))
    padr = lambda w: jnp.pad(w, ((0, LORA_PAD - rank), (0, 0)))
    w_l1 = jnp.concatenate([padc(a_w1[0]), padc(a_a1[0]), a_g1[0]], axis=1).astype(BF16)
    gid_l1 = jnp.asarray([1, 4] + [5] * (a_g1.shape[2] // LORA_PAD), jnp.int32)
    t1 = mix_matmul(h, mix_tab, gid_l1, w_l1, seq, tn=LORA_PAD)
    lw, a_iclr, gate = lora_stage2(t1, padr(a_w2[0]).astype(BF16), padr(a_a2[0]).astype(BF16),
                                   a_g2[0].astype(BF16), a_w0[0], a_a0[0])
    prep = wkv_chunk_prepare(proj, lw, a_iclr, a_k_k[0], a_k_a[0], n_pairs)
    seq_out = wkv_chunk_scan(prep, proj, a_iclr, gate, a_k_a[0], a_r_k[0].reshape(-1),
                             a_ln_g[0], a_ln_b[0], n_pairs, seq)
    x = mixer_tail(x, seq_out, proj, (3 * seq_width) // mem_width, 0)

    kv = norm_matmul(x, kv_norm, w_kv_shared.astype(BF16))

    proj = norm_matmul(x, norm_mix[1], b_w_in[0].astype(BF16))
    seq_out = band_attention(proj, kv, _band_bias(b_rel_bias[0]), n_pairs, seq)
    x = mixer_tail(x, seq_out, proj, seq_width // mem_width, 1)

    return rmsnorm(x, final_norm).reshape(bsz, seq, d)
```

```python
import functools

import numpy as np
import jax
import jax.numpy as jnp
from jax import lax
from jax.experimental import pallas as pl
from jax.experimental.pallas import tpu as pltpu

F32 = jnp.float32
BF16 = jnp.bfloat16
HI = lax.Precision.HIGHEST

HEAD_DIM = 64
PAIR = 2 * HEAD_DIM
CHUNK = 64
SCAN_PAIRS = 2
LEFT_CHUNKS = 8
BAND = (LEFT_CHUNKS + 1) * CHUNK
REL_MAX = 128
MEM_HEADS = 4
MEM_HEAD_DIM = 128
PEER_KEYS = 128
PEER_HEADS = 8
PEER_TOPK = 16
PEER_PICKS = PEER_HEADS * PEER_TOPK
GN_EPS = 64e-5
RMS_EPS = 1e-6
LORA_PAD = 128
VMEM_LIMIT = 48 * 1024 * 1024


def _params(*sem):
    return pltpu.CompilerParams(dimension_semantics=sem, vmem_limit_bytes=VMEM_LIMIT)


def _dot(a, b):
    return jnp.dot(a, b, preferred_element_type=F32)


def _dot_nt(a, b, precision=None):
    return lax.dot_general(a, b, (((1,), (1,)), ((), ())), precision=precision,
                           preferred_element_type=F32)


def _dot_tn(a, b, precision=None):
    return lax.dot_general(a, b, (((0,), (0,)), ((), ())), precision=precision,
                           preferred_element_type=F32)


def _rmsnorm_body(x_ref, g_ref, o_ref):
    x = x_ref[...]
    ms = jnp.mean(x * x, axis=-1, keepdims=True)
    o_ref[...] = (x * lax.rsqrt(ms + RMS_EPS) * g_ref[...]).astype(o_ref.dtype)


def rmsnorm(x, g, tm=512):
    n, d = x.shape
    tm = min(tm, n)
    return pl.pallas_call(
        _rmsnorm_body,
        grid=(n // tm,),
        in_specs=[pl.BlockSpec((tm, d), lambda i: (i, 0)), pl.BlockSpec((1, d), lambda i: (0, 0))],
        out_specs=pl.BlockSpec((tm, d), lambda i: (i, 0)),
        out_shape=jax.ShapeDtypeStruct((n, d), F32),
        compiler_params=_params("parallel"),
        name="rmsnorm",
    )(x, g.reshape(1, d))


def _normmm_body(x_ref, g_ref, w_ref, o_ref, *rest, emit_hn):
    lhs_ref = rest[-1]

    @pl.when(pl.program_id(1) == 0)
    def _():
        x = x_ref[...]
        ms = jnp.mean(x * x, axis=-1, keepdims=True)
        hn = x * lax.rsqrt(ms + RMS_EPS) * g_ref[...]
        lhs_ref[...] = hn.astype(BF16)
        if emit_hn:
            rest[0][...] = hn

    o_ref[...] = _dot(lhs_ref[...], w_ref[...]).astype(o_ref.dtype)


def norm_matmul(x, g, w_bf16, emit_hn=False, tm=512, tn=1024, out_dtype=BF16):
    n, d = x.shape
    nc = w_bf16.shape[1]
    tm, tn = min(tm, n), min(tn, nc)
    out_shape = [jax.ShapeDtypeStruct((n, nc), out_dtype)]
    out_specs = [pl.BlockSpec((tm, tn), lambda i, j: (i, j))]
    if emit_hn:
        out_shape.append(jax.ShapeDtypeStruct((n, d), F32))
        out_specs.append(pl.BlockSpec((tm, d), lambda i, j: (i, 0)))
    res = pl.pallas_call(
        functools.partial(_normmm_body, emit_hn=emit_hn),
        grid=(n // tm, nc // tn),
        in_specs=[pl.BlockSpec((tm, d), lambda i, j: (i, 0)),
                  pl.BlockSpec((1, d), lambda i, j: (0, 0)),
                  pl.BlockSpec((d, tn), lambda i, j: (0, j))],
        out_specs=out_specs,
        out_shape=out_shape,
        scratch_shapes=[pltpu.VMEM((tm, d), BF16)],
        compiler_params=_params("parallel", "arbitrary"),
        name="norm_matmul",
    )(x, g.reshape(1, d), w_bf16)
    return res if emit_hn else res[0]


def _mixmm_body(gid_ref, h_ref, hp_ref, mix_ref, w_ref, o_ref, lhs_ref, *, tm, seq):
    i = pl.program_id(0)
    j = pl.program_id(1)
    new_group = jnp.logical_or(j == 0, gid_ref[j] != gid_ref[jnp.maximum(j - 1, 0)])

    @pl.when(new_group)
    def _():
        h = h_ref[...]
        prev = jnp.where((i * tm) % seq == 0, 0.0, hp_ref[7:8, :])
        row = lax.broadcasted_iota(jnp.int32, h.shape, 0)
        shifted = jnp.where(row == 0, prev, pltpu.roll(h, 1, 0))
        lhs_ref[...] = (h + (shifted - h) * mix_ref[0]).astype(BF16)

    o_ref[...] = _dot(lhs_ref[...], w_ref[...])


def mix_matmul(h, mix_tab, gid, w_bf16, seq, tn, tm=1024):
    n, d = h.shape
    nc = w_bf16.shape[1]
    tm = min(tm, seq)
    sub = tm // 8
    grid_spec = pltpu.PrefetchScalarGridSpec(
        num_scalar_prefetch=1,
        grid=(n // tm, nc // tn),
        in_specs=[pl.BlockSpec((tm, d), lambda i, j, g: (i, 0)),
                  pl.BlockSpec((8, d), lambda i, j, g: (jnp.maximum(i * sub - 1, 0), 0)),
                  pl.BlockSpec((1, 1, d), lambda i, j, g: (g[j], 0, 0)),
                  pl.BlockSpec((d, tn), lambda i, j, g: (0, j))],
        out_specs=pl.BlockSpec((tm, tn), lambda i, j, g: (i, j)),
        scratch_shapes=[pltpu.VMEM((tm, d), BF16)],
    )
    return pl.pallas_call(
        functools.partial(_mixmm_body, tm=tm, seq=seq),
        grid_spec=grid_spec,
        out_shape=jax.ShapeDtypeStruct((n, nc), F32),
        compiler_params=_params("parallel", "arbitrary"),
        name="mix_matmul",
    )(gid, h, h, mix_tab.reshape(mix_tab.shape[0], 1, d), w_bf16)


def _sigmoid(x):
    return 1.0 / (1.0 + jnp.exp(-x))


def _lora2_body(t_ref, w2_ref, a2_ref, g2_ref, w0_ref, a0_ref, lw_ref, a_ref, g_ref):
    t = t_ref[...]
    tw = jnp.tanh(t[:, :LORA_PAD]).astype(BF16)
    ta = t[:, LORA_PAD:2 * LORA_PAD].astype(BF16)
    tg = _sigmoid(t[:, 2 * LORA_PAD:]).astype(BF16)
    u = w0_ref[...] + _dot(tw, w2_ref[...])
    softplus_neg_u = jnp.maximum(-u, 0.0) + jnp.log(1.0 + jnp.exp(-jnp.abs(u)))
    lw_ref[...] = -jnp.exp(-softplus_neg_u - 0.5)
    a_ref[...] = _sigmoid(a0_ref[...] + _dot(ta, a2_ref[...]))
    g_ref[...] = _dot(tg, g2_ref[...])


def lora_stage2(t1, w2p, a2p, g2, w0, a0, tm=256):
    n = t1.shape[0]
    width = w2p.shape[1]
    tm = min(tm, n)
    full = lambda a: pl.BlockSpec(a.shape, lambda i: (0, 0))
    row = pl.BlockSpec((tm, width), lambda i: (i, 0))
    w0 = w0.reshape(1, width)
    a0 = a0.reshape(1, width)
    return pl.pallas_call(
        _lora2_body,
        grid=(n // tm,),
        in_specs=[pl.BlockSpec((tm, t1.shape[1]), lambda i: (i, 0)), full(w2p), full(a2p), full(g2),
                  full(w0), full(a0)],
        out_specs=[row, row, row],
        out_shape=[jax.ShapeDtypeStruct((n, width), F32)] * 3,
        compiler_params=_params("parallel"),
        name="lora_stage2",
    )(t1, w2p, a2p, g2, w0, a0)


def _head_group_sum(x):
    width = x.shape[-1]
    r = lax.broadcasted_iota(jnp.int32, (width, width), 0) // HEAD_DIM
    c = lax.broadcasted_iota(jnp.int32, (width, width), 1) // HEAD_DIM
    ones = jnp.where(r == c, 1.0, 0.0).astype(F32)
    return jnp.dot(x, ones, precision=HI, preferred_element_type=F32)


def _wkv_prepare_body(r_ref, k_ref, v_ref, lw_ref, a_ref, kk_ref, ka_ref,
                      g_ref, s0c_ref, rp_ref, y0_ref, *, tb):
    r = r_ref[...]
    k = k_ref[...]
    v = v_ref[...]
    lw = lw_ref[...]
    a = a_ref[...]
    kk = k * kk_ref[...]
    norm = jnp.sqrt(_head_group_sum(kk * kk))
    kk = kk / jnp.maximum(norm, 1e-12)
    kmod = k * (1.0 + (a - 1.0) * ka_ref[...])
    avec = -kk
    bvec = kk * a

    tr = lax.broadcasted_iota(jnp.int32, (tb, tb), 0)
    tc = lax.broadcasted_iota(jnp.int32, (tb, tb), 1)
    tri = jnp.where((tr >= tc) & (tr // CHUNK == tc // CHUNK), 1.0, 0.0).astype(F32)
    cum = jnp.dot(tri, lw, precision=HI, preferred_element_type=F32)

    lane = lax.broadcasted_iota(jnp.int32, (CHUNK, PAIR), 1)
    head0 = lane < HEAD_DIM
    row = lax.broadcasted_iota(jnp.int32, (PAIR, PAIR), 0)
    col = lax.broadcasted_iota(jnp.int32, (PAIR, PAIR), 1)
    strict = row > col
    lower = row >= col
    eye = jnp.where(row == col, 1.0, 0.0).astype(F32)

    def stack(x):
        return jnp.concatenate([jnp.where(head0, x, 0.0), jnp.where(head0, 0.0, x)], axis=0)

    chunks = range(tb // CHUNK)
    pre = []
    for c in chunks:
        sl = slice(c * CHUNK, (c + 1) * CHUNK)
        cm = cum[sl]
        cend = cm[CHUNK - 1:CHUNK]
        e_in = jnp.exp(cm)
        e_out = jnp.exp(-cm)
        e_tail = jnp.exp(cend - cm)
        a_s = stack(avec[sl] * jnp.exp(cm - lw[sl]))
        r_s = stack(r[sl] * e_in)
        b_s = stack(bvec[sl] * e_out)
        k_s = stack(kmod[sl] * e_out)
        bt_s = stack(bvec[sl] * e_tail)
        kt_s = stack(kmod[sl] * e_tail)
        v_s = stack(v[sl])
        p = _dot_nt(jnp.concatenate([a_s, r_s], axis=0).astype(BF16),
                    jnp.concatenate([b_s, k_s], axis=0).astype(BF16))
        pre.append(dict(cend=cend, a_s=a_s, r_s=r_s, bt_s=bt_s, kt_s=kt_s, v_s=v_s,
                        l_ab=jnp.where(strict, p[:PAIR, :PAIR], 0.0),
                        l_ak=jnp.where(strict, p[:PAIR, PAIR:], 0.0),
                        a_rb=jnp.where(lower, p[PAIR:, :PAIR], 0.0),
                        a_rk=jnp.where(lower, p[PAIR:, PAIR:], 0.0)))

    ms = [q["l_ab"] for q in pre]
    ts = [eye + m for m in ms]
    for _ in range(5):
        ms = [_dot(m.astype(BF16), m.astype(BF16)) for m in ms]
        ts = [t + _dot(t.astype(BF16), m.astype(BF16)) for t, m in zip(ts, ms)]

    lvs = [_dot(q["l_ak"].astype(BF16), q["v_s"].astype(BF16)) for q in pre]
    aus = [_dot(t.astype(BF16), jnp.concatenate([q["a_s"], lv], axis=1).astype(BF16))
           for t, q, lv in zip(ts, pre, lvs)]
    xs = [_dot(q["a_rb"].astype(BF16), au.astype(BF16)) for q, au in zip(pre, aus)]
    for c, q, au, x in zip(chunks, pre, aus, xs):
        rp_ref[0, c] = q["r_s"] + x[:, :PAIR]
        y0_ref[0, c] = x[:, PAIR:] + _dot(q["a_rk"].astype(BF16), q["v_s"].astype(BF16))
        ap = au[:, :PAIR]
        u0 = au[:, PAIR:]
        g_ref[0, c] = eye * jnp.exp(q["cend"]) + _dot_tn(ap.astype(BF16), q["bt_s"].astype(BF16))
        s0c_ref[0, c] = _dot_tn(jnp.concatenate([u0, q["v_s"]], axis=0).astype(BF16),
                                jnp.concatenate([q["bt_s"], q["kt_s"]], axis=0).astype(BF16))


def wkv_chunk_prepare(proj, lw, a, k_k, k_a, n_pairs, tb=512):
    n = proj.shape[0]
    tb = min(tb, n)
    cpb = tb // CHUNK
    col = lambda off: pl.BlockSpec((tb, PAIR), lambda i, p: (i, off + p))
    par = pl.BlockSpec((1, PAIR), lambda i, p: (0, p))
    blk = pl.BlockSpec((1, cpb, PAIR, PAIR), lambda i, p: (p, i, 0, 0))
    shp = jax.ShapeDtypeStruct((n_pairs, n // CHUNK, PAIR, PAIR), F32)
    return pl.pallas_call(
        functools.partial(_wkv_prepare_body, tb=tb),
        grid=(n // tb, n_pairs),
        in_specs=[col(0), col(n_pairs), col(2 * n_pairs), col(0), col(0), par, par],
        out_specs=[blk] * 4,
        out_shape=[shp] * 4,
        compiler_params=_params("parallel", "parallel"),
        name="wkv_chunk_prepare",
    )(proj, proj, proj, lw, a, k_k.reshape(1, -1), k_a.reshape(1, -1))


def _wkv_scan_body(g_ref, s0c_ref, rp_ref, y0_ref, r_ref, k_ref, v_ref, a_ref, gate_ref,
                   ka_ref, rk_ref, lng_ref, lnb_ref, o_ref, s_ref, y_ref, *, cpb):
    @pl.when(pl.program_id(2) == 0)
    def _():
        s_ref[...] = jnp.zeros_like(s_ref)

    pairs = range(SCAN_PAIRS)
    states = [[s_ref[w].astype(BF16)] for w in pairs]
    for c in range(cpb):
        for w in pairs:
            s = _dot(states[w][c], g_ref[w, c].astype(BF16)) + s0c_ref[w, c]
            if c + 1 < cpb:
                states[w].append(s.astype(BF16))
            else:
                s_ref[w] = s
    for c in range(cpb):
        for w in pairs:
            y_st = _dot_nt(rp_ref[w, c].astype(BF16), states[w][c]) + y0_ref[w, c]
            y_ref[c * CHUNK:(c + 1) * CHUNK, w * PAIR:(w + 1) * PAIR] = y_st[:CHUNK] + y_st[CHUNK:]

    y = y_ref[...]
    inv = 1.0 / HEAD_DIM
    mu = _head_group_sum(y) * inv
    yc = y - mu
    var = _head_group_sum(yc * yc) * inv
    yn = yc * lax.rsqrt(var + GN_EPS) * lng_ref[...] + lnb_ref[...]
    r = r_ref[...]
    kmod = k_ref[...] * (1.0 + (a_ref[...] - 1.0) * ka_ref[...])
    bonus = _head_group_sum(r * kmod * rk_ref[...]) * v_ref[...]
    o_ref[...] = ((yn + bonus) * gate_ref[...]).astype(o_ref.dtype)


def wkv_chunk_scan(prep, proj, a, gate, k_a, r_k, ln_g, ln_b, n_pairs, seq, tb=512):
    g_all, s0c_all, rp_all, y0_all = prep
    n = proj.shape[0]
    tb = min(tb, seq)
    cpb = tb // CHUNK
    nblk = seq // tb
    assert n_pairs % SCAN_PAIRS == 0
    width = SCAN_PAIRS * PAIR
    groups = n_pairs // SCAN_PAIRS
    blk = pl.BlockSpec((SCAN_PAIRS, cpb, PAIR, PAIR), lambda b, p, t: (p, b * nblk + t, 0, 0))
    col = lambda off: pl.BlockSpec((tb, width), lambda b, p, t: (b * nblk + t, off + p))
    par = pl.BlockSpec((1, width), lambda b, p, t: (0, p))
    vec = lambda z: z.reshape(1, -1)
    return pl.pallas_call(
        functools.partial(_wkv_scan_body, cpb=cpb),
        grid=(n // seq, groups, nblk),
        in_specs=[blk] * 4 + [col(0), col(groups), col(2 * groups), col(0), col(0)] + [par] * 4,
        out_specs=col(0),
        out_shape=jax.ShapeDtypeStruct((n, n_pairs * PAIR), BF16),
        scratch_shapes=[pltpu.VMEM((SCAN_PAIRS, PAIR, PAIR), F32), pltpu.VMEM((tb, width), F32)],
        compiler_params=_params("parallel", "parallel", "arbitrary"),
        name="wkv_chunk_scan",
    )(g_all, s0c_all, rp_all, y0_all, proj, proj, proj, a, gate,
      vec(k_a), vec(r_k), vec(ln_g), vec(ln_b))


def _memattn_body(q_ref, m_ref, o_ref):
    width = MEM_HEADS * MEM_HEAD_DIM
    scale = MEM_HEAD_DIM ** -0.5
    for h in range(MEM_HEADS):
        sl = slice(h * MEM_HEAD_DIM, (h + 1) * MEM_HEAD_DIM)
        q = q_ref[:, sl].astype(BF16)
        mk = m_ref[:, sl].astype(BF16)
        mv = m_ref[:, width + h * MEM_HEAD_DIM:width + (h + 1) * MEM_HEAD_DIM].astype(BF16)
        s = _dot_nt(q, mk) * scale
        e = jnp.exp(s - jnp.max(s, axis=-1, keepdims=True))
        p = e / jnp.sum(e, axis=-1, keepdims=True)
        o_ref[:, sl] = _dot(p.astype(BF16), mv).astype(o_ref.dtype)


def memory_attention(proj, q_col_block, mkv, seq, n_mem, tm=512):
    n = proj.shape[0]
    width = MEM_HEADS * MEM_HEAD_DIM
    tm = min(tm, seq)
    nblk = seq // tm
    return pl.pallas_call(
        _memattn_body,
        grid=(n // seq, nblk),
        in_specs=[pl.BlockSpec((tm, width), lambda b, t: (b * nblk + t, q_col_block)),
                  pl.BlockSpec((n_mem, 2 * width), lambda b, t: (b, 0))],
        out_specs=pl.BlockSpec((tm, width), lambda b, t: (b * nblk + t, 0)),
        out_shape=jax.ShapeDtypeStruct((n, width), BF16),
        compiler_params=_params("parallel", "parallel"),
        name="memory_attention",
    )(proj, mkv)


def _outproj_body(s_ref, m_ref, w1_ref, w2_ref, x_ref, o_ref):
    o_ref[...] = x_ref[...] + _dot(s_ref[...], w1_ref[...]) + _dot(m_ref[...], w2_ref[...])


def out_projection(seq_out, mem_out, w_seq, w_mem, x, tm=512, tn=512):
    n, d = x.shape
    tm, tn = min(tm, n), min(tn, d)
    ws, wm = seq_out.shape[1], mem_out.shape[1]
    return pl.pallas_call(
        _outproj_body,
        grid=(n // tm, d // tn),
        in_specs=[pl.BlockSpec((tm, ws), lambda i, j: (i, 0)),
                  pl.BlockSpec((tm, wm), lambda i, j: (i, 0)),
                  pl.BlockSpec((ws, tn), lambda i, j: (0, j)),
                  pl.BlockSpec((wm, tn), lambda i, j: (0, j)),
                  pl.BlockSpec((tm, tn), lambda i, j: (i, j))],
        out_specs=pl.BlockSpec((tm, tn), lambda i, j: (i, j)),
        out_shape=jax.ShapeDtypeStruct((n, d), F32),
        compiler_params=_params("parallel", "parallel"),
        name="out_projection",
    )(seq_out, mem_out, w_seq, w_mem, x)


def _bandattn_body(q_ref, kp_ref, kc_ref, vp_ref, vc_ref, bias_ref, o_ref, *, tq):
    qi = pl.program_id(2)
    scale = HEAD_DIM ** -0.5
    k_all = jnp.concatenate([kp_ref[...], kc_ref[...]], axis=0).astype(BF16)
    v_all = jnp.concatenate([vp_ref[...], vc_ref[...]], axis=0).astype(BF16)
    lane = lax.broadcasted_iota(jnp.int32, (CHUNK, PAIR), 1)
    head0 = lane < HEAD_DIM
    kcol = lax.broadcasted_iota(jnp.int32, (2 * CHUNK, BAND), 1)
    pad = LEFT_CHUNKS * CHUNK
    bias = bias_ref[...]
    chunks = range(tq // CHUNK)
    starts = [tq - pad + j * CHUNK for j in chunks]
    scores = []
    for j, start in zip(chunks, starts):
        q = q_ref[j * CHUNK:(j + 1) * CHUNK, :]
        q2 = jnp.concatenate([jnp.where(head0, q, 0.0), jnp.where(head0, 0.0, q)], axis=0).astype(BF16)
        scores.append(_dot_nt(q2, k_all[start:start + BAND]))
    exps = []
    for start, s in zip(starts, scores):
        valid = jnp.logical_or(qi > 0, kcol + start >= tq)
        s = jnp.where(valid, s * scale + bias, -jnp.inf)
        exps.append(jnp.exp(s - jnp.max(s, axis=-1, keepdims=True)))
    outs = [_dot(e.astype(BF16), v_all[start:start + BAND]) for start, e in zip(starts, exps)]
    for j, e, o in zip(chunks, exps, outs):
        o = o / jnp.sum(e, axis=-1, keepdims=True)
        o_ref[j * CHUNK:(j + 1) * CHUNK, :] = jnp.where(head0, o[:CHUNK], o[CHUNK:]).astype(o_ref.dtype)


def band_attention(proj, kv, bias, n_pairs, seq, tq=512):
    n = proj.shape[0]
    tq = min(tq, seq)
    assert tq >= LEFT_CHUNKS * CHUNK
    nblk = seq // tq
    cur = lambda off: pl.BlockSpec((tq, PAIR), lambda b, p, t: (b * nblk + t, off + p))
    prev = lambda off: pl.BlockSpec((tq, PAIR), lambda b, p, t: (b * nblk + jnp.maximum(t - 1, 0), off + p))
    return pl.pallas_call(
        functools.partial(_bandattn_body, tq=tq),
        grid=(n // seq, n_pairs, nblk),
        in_specs=[cur(0), prev(0), cur(0), prev(n_pairs), cur(n_pairs),
                  pl.BlockSpec((2 * CHUNK, BAND), lambda b, p, t: (p, 0))],
        out_specs=cur(0),
        out_shape=jax.ShapeDtypeStruct((n, n_pairs * PAIR), BF16),
        compiler_params=_params("parallel", "parallel", "parallel"),
        name="band_attention",
    )(proj, kv, kv, kv, kv, bias.reshape(-1, BAND))


def _top16(scores, payloads):
    nl = scores[0].shape[1]
    rids = [lax.broadcasted_iota(jnp.int32, s.shape, 0).astype(F32) for s in scores]
    slot = lax.broadcasted_iota(jnp.int32, (PEER_TOPK, nl), 0)

    def body(i, carry):
        sel = slot == i
        out = []
        for (s, vals, picks), rid, payload in zip(carry, rids, payloads):
            m = jnp.max(s, axis=0, keepdims=True)
            am = jnp.min(jnp.where(s == m, rid, float(s.shape[0])), axis=0, keepdims=True)
            hit = rid == am
            if payload is None:
                pick = am
            else:
                pick = jnp.sum(jnp.where(hit, payload, 0.0), axis=0, keepdims=True)
            out.append((jnp.where(hit, -jnp.inf, s), jnp.where(sel, m, vals), jnp.where(sel, pick, picks)))
        return tuple(out)

    zero = jnp.zeros((PEER_TOPK, nl), F32)
    res = lax.fori_loop(0, PEER_TOPK, body, tuple((s, zero, zero) for s in scores))
    return [(vals, picks) for _, vals, picks in res]


def _pair_candidates(a, b, combine):
    half = PEER_TOPK // 2
    rows = [combine(a[0:1], b)]
    rows += [combine(a[i:i + 1], b[:half]) for i in range(1, half)]
    rows.append(combine(a[half:], b[0:1]))
    return jnp.concatenate(rows, axis=0)


TOPK_HEADS = 2


def _peer_topk_body(q_ref, keys_ref, eidx_ref, gate_ref):
    q = q_ref[...].astype(BF16)
    scores = []
    for h in range(TOPK_HEADS):
        for half in range(2):
            col = (2 * h + half) * PEER_KEYS
            scores.append(_dot_nt(keys_ref[h, half].astype(BF16), q[:, col:col + PEER_KEYS]))
    cands, cidxs = [], []
    for h in range(TOPK_HEADS):
        (a, i1), (b, i2) = _top16(scores[2 * h:2 * h + 2], [None, None])
        cands.append(_pair_candidates(a, b, lambda x, y: x + y))
        cidxs.append(_pair_candidates(i1, i2, lambda x, y: x * PEER_KEYS + y))
    for h, (top, eidx) in enumerate(_top16(cands, cidxs)):
        rows = slice(h * PEER_TOPK, (h + 1) * PEER_TOPK)
        e = jnp.exp(top - top[0:1])
        gate_ref[rows, :] = e / jnp.sum(e, axis=0, keepdims=True)
        eidx_ref[rows, :] = eidx.astype(jnp.int32)


def peer_topk(q, keys, tl=128):
    n = q.shape[0]
    tl = min(tl, n)
    blk = pl.BlockSpec((TOPK_HEADS * PEER_TOPK, tl), lambda i, h: (h, i))
    return pl.pallas_call(
        _peer_topk_body,
        grid=(n // tl, PEER_HEADS // TOPK_HEADS),
        in_specs=[pl.BlockSpec((tl, TOPK_HEADS * 2 * PEER_KEYS), lambda i, h: (i, h)),
                  pl.BlockSpec((TOPK_HEADS, 2, PEER_KEYS, PEER_KEYS), lambda i, h: (h, 0, 0, 0))],
        out_specs=[blk, blk],
        out_shape=[jax.ShapeDtypeStruct((PEER_PICKS, n), jnp.int32),
                   jax.ShapeDtypeStruct((PEER_PICKS, n), F32)],
        compiler_params=_params("parallel", "parallel"),
        name="peer_topk",
    )(q, keys)


PEER_SLOTS = 8
LANES = 128
PITCH_PAD = 4


def _pack_body(u_ref, v_ref, o_ref):
    hi = lax.bitcast_convert_type(u_ref[0].astype(BF16).astype(F32), jnp.uint32)
    lo = lax.bitcast_convert_type(v_ref[0].astype(BF16).astype(F32), jnp.uint32)
    words = hi | (lo >> 16)
    for c in range(o_ref.shape[1]):
        o_ref[:, c, :] = words[:, c * LANES:(c + 1) * LANES]


def _pack_expert_table(u, v, layer, te=256):
    _, n_exp, d = u.shape
    chunks = d // LANES
    te = min(te, n_exp)
    packed = pl.pallas_call(
        _pack_body,
        grid=(n_exp // te,),
        in_specs=[pl.BlockSpec((1, te, d), lambda i: (layer, i, 0))] * 2,
        out_specs=pl.BlockSpec((te, chunks, LANES), lambda i: (i, 0, 0)),
        out_shape=jax.ShapeDtypeStruct((n_exp, chunks, LANES), jnp.uint32),
        compiler_params=_params("parallel"),
        name="pack_expert_table",
    )(u, v)
    return packed.reshape(n_exp * chunks, LANES)


def _peer_ffn_body(eidx_ref, gate_ref, hn_ref, x_ref, tab_ref, o_ref, *scratch, tb, d):
    rows_refs, (sem_ref, w_ref, stage_ref) = scratch[:PEER_SLOTS], scratch[PEER_SLOTS:]
    chunks = d // LANES
    pitch = chunks + PITCH_PAD

    def issue(t, slot, part=0, parts=1):
        per = PEER_PICKS // parts
        for e in range(part * per, (part + 1) * per):
            src = tab_ref.at[pl.ds(pl.multiple_of(eidx_ref[t, e] * chunks, chunks), chunks), :]
            dst = rows_refs[slot].at[pl.ds(e * pitch, chunks), :]
            pltpu.make_async_copy(src, dst, sem_ref.at[slot]).start(priority=e % 2)

    def wait(slot):
        total = PEER_PICKS * chunks
        pltpu.make_async_copy(tab_ref.at[pl.ds(0, total), :],
                              rows_refs[slot].at[pl.ds(0, total), :], sem_ref.at[slot]).wait()

    tok_lane = lax.broadcasted_iota(jnp.int32, (PEER_PICKS, tb), 1)
    hi_mask = jnp.uint32(0xFFFF0000)

    def words(slot, c):
        return rows_refs[slot][pl.ds(c, PEER_PICKS, stride=pitch), :]

    def pick_weights(t, slot):
        xt = hn_ref[pl.ds(t, 1), :]
        acc = jnp.zeros((PEER_PICKS, LANES), F32)
        for c in range(chunks):
            u = lax.bitcast_convert_type(words(slot, c) & hi_mask, F32)
            acc = acc + u * xt[:, c * LANES:(c + 1) * LANES]
        act = jnp.sum(acc, axis=1, keepdims=True)
        act = 0.5 * act * (1.0 + lax.erf(act * (2.0 ** -0.5)))
        gate = jnp.sum(jnp.where(tok_lane == t, gate_ref[...], 0.0), axis=1, keepdims=True)
        w_ref[slot] = jnp.broadcast_to(gate * act, (PEER_PICKS, LANES))

    def combine(slot):
        for c in range(chunks):
            v = lax.bitcast_convert_type(words(slot, c) << 16, F32)
            stage_ref[slot:slot + 1, c * LANES:(c + 1) * LANES] = jnp.sum(v * w_ref[slot], axis=0, keepdims=True)

    def store_group(g):
        rows = pl.ds(pl.multiple_of(g * PEER_SLOTS, PEER_SLOTS), PEER_SLOTS)
        o_ref[rows, :] = x_ref[rows, :] + stage_ref[...]

    ahead = PEER_SLOTS - 1
    groups = tb // PEER_SLOTS
    for t in range(ahead):
        issue(t, t)
    wait(0)
    pick_weights(0, 0)

    def group(g, carry):
        for s in range(PEER_SLOTS):
            t = g * PEER_SLOTS + s
            wait((s + 1) % PEER_SLOTS)
            pick_weights(t + 1, (s + 1) % PEER_SLOTS)
            combine(s)
            issue(t + ahead, (s + ahead) % PEER_SLOTS)
        store_group(g)
        return carry

    lax.fori_loop(0, groups - 1, group, 0)

    for s in range(PEER_SLOTS):
        t = (groups - 1) * PEER_SLOTS + s
        if t + ahead < tb:
            issue(t + ahead, (s + ahead) % PEER_SLOTS)
        if t + 1 < tb:
            wait((s + 1) % PEER_SLOTS)
            pick_weights(t + 1, (s + 1) % PEER_SLOTS)
        combine(s)
    store_group(groups - 1)


def peer_expert_ffn(eidx, gate, hn, x, table, tb=128):
    n, d = x.shape
    tb = min(tb, n)
    assert tb % PEER_SLOTS == 0 and table.shape[1] == LANES
    assert table.shape[0] >= PEER_PICKS * (d // LANES)
    slot_rows = PEER_PICKS * (d // LANES + PITCH_PAD)
    return pl.pallas_call(
        functools.partial(_peer_ffn_body, tb=tb, d=d),
        grid=(n // tb,),
        in_specs=[pl.BlockSpec((tb, PEER_PICKS), lambda i: (i, 0), memory_space=pltpu.SMEM),
                  pl.BlockSpec((PEER_PICKS, tb), lambda i: (0, i)),
                  pl.BlockSpec((tb, d), lambda i: (i, 0)),
                  pl.BlockSpec((tb, d), lambda i: (i, 0)),
                  pl.BlockSpec(memory_space=pl.ANY)],
        out_specs=pl.BlockSpec((tb, d), lambda i: (i, 0)),
        out_shape=jax.ShapeDtypeStruct((n, d), F32),
        scratch_shapes=[pltpu.VMEM((slot_rows, LANES), jnp.uint32)] * PEER_SLOTS
                       + [pltpu.SemaphoreType.DMA((PEER_SLOTS,)),
                          pltpu.VMEM((PEER_SLOTS, PEER_PICKS, LANES), F32),
                          pltpu.VMEM((PEER_SLOTS, d), F32)],
        compiler_params=_params("arbitrary"),
        name="peer_expert_ffn",
    )(eidx, gate, hn, x, table)


def peer_layer(x, norm_g, wq, keys, u_all, v_all, layer, tb=256):
    q, hn = norm_matmul(x, norm_g, wq.astype(BF16), emit_hn=True)
    eidx, gate = peer_topk(q, keys)
    return peer_expert_ffn(eidx.T, gate, hn, x, _pack_expert_table(u_all, v_all, layer), tb=tb)


def _band_bias(rel_bias):
    pad = LEFT_CHUNKS * CHUNK
    dist = pad + np.arange(CHUNK)[:, None] - np.arange(BAND)[None, :]
    idx = np.clip(dist, -(CHUNK - 1), REL_MAX) + (CHUNK - 1)
    return rel_bias[:, idx].astype(F32)


def kernel(x, mem, norm_mix, norm_ffn, norm_mem, w_mem_kv, w_out, peer_wq, peer_keys, peer_u, peer_v, a_mix, a_w_in, a_w0, a_w1, a_w2, a_a0, a_a1, a_a2, a_g1, a_g2, a_k_k, a_k_a, a_r_k, a_ln_g, a_ln_b, kv_norm, w_kv_shared, b_w_in, b_rel_bias, final_norm):
    bsz, seq, d = x.shape
    n = bsz * seq
    n_mem = mem.shape[1]
    seq_width = a_w0.shape[1]
    n_pairs = seq_width // PAIR
    mem_width = MEM_HEADS * MEM_HEAD_DIM
    x = x.reshape(n, d)
    mem2 = mem.reshape(bsz * n_mem, d)

    def mixer_tail(x, seq_out, proj, q_col_block, layer):
        mkv = norm_matmul(mem2, norm_mem[layer], w_mem_kv[layer].astype(BF16), tm=256)
        mem_out = memory_attention(proj, q_col_block, mkv, seq, n_mem)
        wo = w_out[layer].astype(BF16)
        x = out_projection(seq_out, mem_out, wo[:seq_width], wo[seq_width:], x)
        return peer_layer(x, norm_ffn[layer], peer_wq[layer], peer_keys[layer], peer_u, peer_v, layer)

    h = rmsnorm(x, norm_mix[0])
    mix_tab = jnp.concatenate([a_mix[0], jnp.zeros((1, d), F32)], axis=0)
    tiles = seq_width // 512
    gid_main = jnp.asarray([0] * tiles + [2] * tiles + [3] * tiles + [6] * (mem_width // 512), jnp.int32)
    proj = mix_matmul(h, mix_tab, gid_main, a_w_in[0].astype(BF16), seq, tn=512)
    rank = a_w1.shape[2]
    padc = lambda w: jnp.pad(w, ((0, 0), (0, LORA_PAD - rank)))
    padr = lambda w: jnp.pad(w, ((0, LORA_PAD - rank), (0, 0)))
    w_l1 = jnp.concatenate([padc(a_w1[0]), padc(a_a1[0]), a_g1[0]], axis=1).astype(BF16)
    gid_l1 = jnp.asarray([1, 4] + [5] * (a_g1.shape[2] // LORA_PAD), jnp.int32)
    t1 = mix_matmul(h, mix_tab, gid_l1, w_l1, seq, tn=LORA_PAD)
    lw, a_iclr, gate = lora_stage2(t1, padr(a_w2[0]).astype(BF16), padr(a_a2[0]).astype(BF16),
                                   a_g2[0].astype(BF16), a_w0[0], a_a0[0])
    prep = wkv_chunk_prepare(proj, lw, a_iclr, a_k_k[0], a_k_a[0], n_pairs)
    seq_out = wkv_chunk_scan(prep, proj, a_iclr, gate, a_k_a[0], a_r_k[0].reshape(-1),
                             a_ln_g[0], a_ln_b[0], n_pairs, seq)
    x = mixer_tail(x, seq_out, proj, (3 * seq_width) // mem_width, 0)

    kv = norm_matmul(x, kv_norm, w_kv_shared.astype(BF16))

    proj = norm_matmul(x, norm_mix[1], b_w_in[0].astype(BF16))
    seq_out = band_attention(proj, kv, _band_bias(b_rel_bias[0]), n_pairs, seq)
    x = mixer_tail(x, seq_out, proj, seq_width // mem_width, 1)

    return rmsnorm(x, final_norm).reshape(bsz, seq, d)
```

```python
import functools

import numpy as np
import jax
import jax.numpy as jnp
from jax import lax
from jax.experimental import pallas as pl
from jax.experimental.pallas import tpu as pltpu

F32 = jnp.float32
BF16 = jnp.bfloat16
HI = lax.Precision.HIGHEST

HEAD_DIM = 64
PAIR = 2 * HEAD_DIM
CHUNK = 64
SCAN_PAIRS = 2
LEFT_CHUNKS = 8
BAND = (LEFT_CHUNKS + 1) * CHUNK
REL_MAX = 128
MEM_HEADS = 4
MEM_HEAD_DIM = 128
PEER_KEYS = 128
PEER_HEADS = 8
PEER_TOPK = 16
PEER_PICKS = PEER_HEADS * PEER_TOPK
GN_EPS = 64e-5
RMS_EPS = 1e-6
LORA_PAD = 128
VMEM_LIMIT = 48 * 1024 * 1024


def _params(*sem):
    return pltpu.CompilerParams(dimension_semantics=sem, vmem_limit_bytes=VMEM_LIMIT)


def _dot(a, b):
    return jnp.dot(a, b, preferred_element_type=F32)


def _dot_nt(a, b, precision=None):
    return lax.dot_general(a, b, (((1,), (1,)), ((), ())), precision=precision,
                           preferred_element_type=F32)


def _dot_tn(a, b, precision=None):
    return lax.dot_general(a, b, (((0,), (0,)), ((), ())), precision=precision,
                           preferred_element_type=F32)


def _rmsnorm_body(x_ref, g_ref, o_ref):
    x = x_ref[...]
    ms = jnp.mean(x * x, axis=-1, keepdims=True)
    o_ref[...] = (x * lax.rsqrt(ms + RMS_EPS) * g_ref[...]).astype(o_ref.dtype)


def rmsnorm(x, g, tm=512):
    n, d = x.shape
    tm = min(tm, n)
    return pl.pallas_call(
        _rmsnorm_body,
        grid=(n // tm,),
        in_specs=[pl.BlockSpec((tm, d), lambda i: (i, 0)), pl.BlockSpec((1, d), lambda i: (0, 0))],
        out_specs=pl.BlockSpec((tm, d), lambda i: (i, 0)),
        out_shape=jax.ShapeDtypeStruct((n, d), F32),
        compiler_params=_params("parallel"),
        name="rmsnorm",
    )(x, g.reshape(1, d))


def _normmm_body(x_ref, g_ref, w_ref, o_ref, *rest, emit_hn):
    lhs_ref = rest[-1]

    @pl.when(pl.program_id(1) == 0)
    def _():
        x = x_ref[...]
        ms = jnp.mean(x * x, axis=-1, keepdims=True)
        hn = x * lax.rsqrt(ms + RMS_EPS) * g_ref[...]
        lhs_ref[...] = hn.astype(BF16)
        if emit_hn:
            rest[0][...] = hn

    o_ref[...] = _dot(lhs_ref[...], w_ref[...]).astype(o_ref.dtype)


def norm_matmul(x, g, w_bf16, emit_hn=False, tm=512, tn=1024, out_dtype=BF16):
    n, d = x.shape
    nc = w_bf16.shape[1]
    tm, tn = min(tm, n), min(tn, nc)
    out_shape = [jax.ShapeDtypeStruct((n, nc), out_dtype)]
    out_specs = [pl.BlockSpec((tm, tn), lambda i, j: (i, j))]
    if emit_hn:
        out_shape.append(jax.ShapeDtypeStruct((n, d), F32))
        out_specs.append(pl.BlockSpec((tm, d), lambda i, j: (i, 0)))
    res = pl.pallas_call(
        functools.partial(_normmm_body, emit_hn=emit_hn),
        grid=(n // tm, nc // tn),
        in_specs=[pl.BlockSpec((tm, d), lambda i, j: (i, 0)),
                  pl.BlockSpec((1, d), lambda i, j: (0, 0)),
                  pl.BlockSpec((d, tn), lambda i, j: (0, j))],
        out_specs=out_specs,
        out_shape=out_shape,
        scratch_shapes=[pltpu.VMEM((tm, d), BF16)],
        compiler_params=_params("parallel", "arbitrary"),
        name="norm_matmul",
    )(x, g.reshape(1, d), w_bf16)
    return res if emit_hn else res[0]


def _mixmm_body(gid_ref, h_ref, hp_ref, mix_ref, w_ref, o_ref, lhs_ref, *, tm, seq):
    i = pl.program_id(0)
    j = pl.program_id(1)
    new_group = jnp.logical_or(j == 0, gid_ref[j] != gid_ref[jnp.maximum(j - 1, 0)])

    @pl.when(new_group)
    def _():
        h = h_ref[...]
        prev = jnp.where((i * tm) % seq == 0, 0.0, hp_ref[7:8, :])
        row = lax.broadcasted_iota(jnp.int32, h.shape, 0)
        shifted = jnp.where(row == 0, prev, pltpu.roll(h, 1, 0))
        lhs_ref[...] = (h + (shifted - h) * mix_ref[0]).astype(BF16)

    o_ref[...] = _dot(lhs_ref[...], w_ref[...])


def mix_matmul(h, mix_tab, gid, w_bf16, seq, tn, tm=1024):
    n, d = h.shape
    nc = w_bf16.shape[1]
    tm = min(tm, seq)
    sub = tm // 8
    grid_spec = pltpu.PrefetchScalarGridSpec(
        num_scalar_prefetch=1,
        grid=(n // tm, nc // tn),
        in_specs=[pl.BlockSpec((tm, d), lambda i, j, g: (i, 0)),
                  pl.BlockSpec((8, d), lambda i, j, g: (jnp.maximum(i * sub - 1, 0), 0)),
                  pl.BlockSpec((1, 1, d), lambda i, j, g: (g[j], 0, 0)),
                  pl.BlockSpec((d, tn), lambda i, j, g: (0, j))],
        out_specs=pl.BlockSpec((tm, tn), lambda i, j, g: (i, j)),
        scratch_shapes=[pltpu.VMEM((tm, d), BF16)],
    )
    return pl.pallas_call(
        functools.partial(_mixmm_body, tm=tm, seq=seq),
        grid_spec=grid_spec,
        out_shape=jax.ShapeDtypeStruct((n, nc), F32),
        compiler_params=_params("parallel", "arbitrary"),
        name="mix_matmul",
    )(gid, h, h, mix_tab.reshape(mix_tab.shape[0], 1, d), w_bf16)


def _sigmoid(x):
    return 1.0 / (1.0 + jnp.exp(-x))


def _lora2_body(t_ref, w2_ref, a2_ref, g2_ref, w0_ref, a0_ref, lw_ref, a_ref, g_ref):
    t = t_ref[...]
    tw = jnp.tanh(t[:, :LORA_PAD]).astype(BF16)
    ta = t[:, LORA_PAD:2 * LORA_PAD].astype(BF16)
    tg = _sigmoid(t[:, 2 * LORA_PAD:]).astype(BF16)
    u = w0_ref[...] + _dot(tw, w2_ref[...])
    softplus_neg_u = jnp.maximum(-u, 0.0) + jnp.log(1.0 + jnp.exp(-jnp.abs(u)))
    lw_ref[...] = -jnp.exp(-softplus_neg_u - 0.5)
    a_ref[...] = _sigmoid(a0_ref[...] + _dot(ta, a2_ref[...]))
    g_ref[...] = _dot(tg, g2_ref[...])


def lora_stage2(t1, w2p, a2p, g2, w0, a0, tm=256):
    n = t1.shape[0]
    width = w2p.shape[1]
    tm = min(tm, n)
    full = lambda a: pl.BlockSpec(a.shape, lambda i: (0, 0))
    row = pl.BlockSpec((tm, width), lambda i: (i, 0))
    w0 = w0.reshape(1, width)
    a0 = a0.reshape(1, width)
    return pl.pallas_call(
        _lora2_body,
        grid=(n // tm,),
        in_specs=[pl.BlockSpec((tm, t1.shape[1]), lambda i: (i, 0)), full(w2p), full(a2p), full(g2),
                  full(w0), full(a0)],
        out_specs=[row, row, row],
        out_shape=[jax.ShapeDtypeStruct((n, width), F32)] * 3,
        compiler_params=_params("parallel"),
        name="lora_stage2",
    )(t1, w2p, a2p, g2, w0, a0)


def _head_group_sum(x):
    width = x.shape[-1]
    r = lax.broadcasted_iota(jnp.int32, (width, width), 0) // HEAD_DIM
    c = lax.broadcasted_iota(jnp.int32, (width, width), 1) // HEAD_DIM
    ones = jnp.where(r == c, 1.0, 0.0).astype(F32)
    return jnp.dot(x, ones, precision=HI, preferred_element_type=F32)


def _wkv_prepare_body(r_ref, k_ref, v_ref, lw_ref, a_ref, kk_ref, ka_ref,
                      g_ref, s0c_ref, rp_ref, y0_ref, *, tb):
    r = r_ref[...]
    k = k_ref[...]
    v = v_ref[...]
    lw = lw_ref[...]
    a = a_ref[...]
    kk = k * kk_ref[...]
    norm = jnp.sqrt(_head_group_sum(kk * kk))
    kk = kk / jnp.maximum(norm, 1e-12)
    kmod = k * (1.0 + (a - 1.0) * ka_ref[...])
    avec = -kk
    bvec = kk * a

    tr = lax.broadcasted_iota(jnp.int32, (CHUNK, CHUNK), 0)
    tc = lax.broadcasted_iota(jnp.int32, (CHUNK, CHUNK), 1)
    tri = jnp.where(tr >= tc, 1.0, 0.0).astype(F32)

    lane = lax.broadcasted_iota(jnp.int32, (CHUNK, PAIR), 1)
    head0 = lane < HEAD_DIM
    row = lax.broadcasted_iota(jnp.int32, (PAIR, PAIR), 0)
    col = lax.broadcasted_iota(jnp.int32, (PAIR, PAIR), 1)
    strict = row > col
    lower = row >= col
    eye = jnp.where(row == col, 1.0, 0.0).astype(F32)

    def stack(x):
        return jnp.concatenate([jnp.where(head0, x, 0.0), jnp.where(head0, 0.0, x)], axis=0)

    chunks = range(tb // CHUNK)
    pre = []
    for c in chunks:
        sl = slice(c * CHUNK, (c + 1) * CHUNK)
        cm = jnp.dot(tri, lw[sl], precision=HI, preferred_element_type=F32)
        cend = cm[CHUNK - 1:CHUNK]
        e_in = jnp.exp(cm)
        e_out = jnp.exp(-cm)
        e_tail = jnp.exp(cend - cm)
        a_s = stack(avec[sl] * jnp.exp(cm - lw[sl]))
        r_s = stack(r[sl] * e_in)
        b_s = stack(bvec[sl] * e_out)
        k_s = stack(kmod[sl] * e_out)
        bt_s = stack(bvec[sl] * e_tail)
        kt_s = stack(kmod[sl] * e_tail)
        v_s = stack(v[sl])
        p = _dot_nt(jnp.concatenate([a_s, r_s], axis=0).astype(BF16),
                    jnp.concatenate([b_s, k_s], axis=0).astype(BF16))
        pre.append(dict(cend=cend, a_s=a_s, r_s=r_s, bt_s=bt_s, kt_s=kt_s, v_s=v_s,
                        l_ab=jnp.where(strict, p[:PAIR, :PAIR], 0.0),
                        l_ak=jnp.where(strict, p[:PAIR, PAIR:], 0.0),
                        a_rb=jnp.where(lower, p[PAIR:, :PAIR], 0.0),
                        a_rk=jnp.where(lower, p[PAIR:, PAIR:], 0.0)))

    ms = [q["l_ab"] for q in pre]
    ts = [eye + m for m in ms]
    for _ in range(5):
        ms = [_dot(m.astype(BF16), m.astype(BF16)) for m in ms]
        ts = [t + _dot(t.astype(BF16), m.astype(BF16)) for t, m in zip(ts, ms)]

    lvs = [_dot(q["l_ak"].astype(BF16), q["v_s"].astype(BF16)) for q in pre]
    aus = [_dot(t.astype(BF16), jnp.concatenate([q["a_s"], lv], axis=1).astype(BF16))
           for t, q, lv in zip(ts, pre, lvs)]
    xs = [_dot(q["a_rb"].astype(BF16), au.astype(BF16)) for q, au in zip(pre, aus)]
    for c, q, au, x in zip(chunks, pre, aus, xs):
        rp_ref[0, c] = q["r_s"] + x[:, :PAIR]
        y0_ref[0, c] = x[:, PAIR:] + _dot(q["a_rk"].astype(BF16), q["v_s"].astype(BF16))
        ap = au[:, :PAIR]
        u0 = au[:, PAIR:]
        g_ref[0, c] = eye * jnp.exp(q["cend"]) + _dot_tn(ap.astype(BF16), q["bt_s"].astype(BF16))
        s0c_ref[0, c] = _dot_tn(jnp.concatenate([u0, q["v_s"]], axis=0).astype(BF16),
                                jnp.concatenate([q["bt_s"], q["kt_s"]], axis=0).astype(BF16))


def wkv_chunk_prepare(proj, lw, a, k_k, k_a, n_pairs, tb=512):
    n = proj.shape[0]
    tb = min(tb, n)
    cpb = tb // CHUNK
    col = lambda off: pl.BlockSpec((tb, PAIR), lambda i, p: (i, off + p))
    par = pl.BlockSpec((1, PAIR), lambda i, p: (0, p))
    blk = pl.BlockSpec((1, cpb, PAIR, PAIR), lambda i, p: (p, i, 0, 0))
    shp = jax.ShapeDtypeStruct((n_pairs, n // CHUNK, PAIR, PAIR), F32)
    return pl.pallas_call(
        functools.partial(_wkv_prepare_body, tb=tb),
        grid=(n // tb, n_pairs),
        in_specs=[col(0), col(n_pairs), col(2 * n_pairs), col(0), col(0), par, par],
        out_specs=[blk] * 4,
        out_shape=[shp] * 4,
        compiler_params=_params("parallel", "parallel"),
        name="wkv_chunk_prepare",
    )(proj, proj, proj, lw, a, k_k.reshape(1, -1), k_a.reshape(1, -1))


def _wkv_scan_body(g_ref, s0c_ref, rp_ref, y0_ref, r_ref, k_ref, v_ref, a_ref, gate_ref,
                   ka_ref, rk_ref, lng_ref, lnb_ref, o_ref, s_ref, y_ref, *, cpb):
    @pl.when(pl.program_id(2) == 0)
    def _():
        s_ref[...] = jnp.zeros_like(s_ref)

    pairs = range(SCAN_PAIRS)
    states = [[s_ref[w].astype(BF16)] for w in pairs]
    for c in range(cpb):
        for w in pairs:
            s = _dot(states[w][c], g_ref[w, c].astype(BF16)) + s0c_ref[w, c]
            if c + 1 < cpb:
                states[w].append(s.astype(BF16))
            else:
                s_ref[w] = s
    for c in range(cpb):
        for w in pairs:
            y_st = _dot_nt(rp_ref[w, c].astype(BF16), states[w][c]) + y0_ref[w, c]
            y_ref[c * CHUNK:(c + 1) * CHUNK, w * PAIR:(w + 1) * PAIR] = y_st[:CHUNK] + y_st[CHUNK:]

    y = y_ref[...]
    inv = 1.0 / HEAD_DIM
    mu = _head_group_sum(y) * inv
    yc = y - mu
    var = _head_group_sum(yc * yc) * inv
    yn = yc * lax.rsqrt(var + GN_EPS) * lng_ref[...] + lnb_ref[...]
    r = r_ref[...]
    kmod = k_ref[...] * (1.0 + (a_ref[...] - 1.0) * ka_ref[...])
    bonus = _head_group_sum(r * kmod * rk_ref[...]) * v_ref[...]
    o_ref[...] = ((yn + bonus) * gate_ref[...]).astype(o_ref.dtype)


def wkv_chunk_scan(prep, proj, a, gate, k_a, r_k, ln_g, ln_b, n_pairs, seq, tb=512):
    g_all, s0c_all, rp_all, y0_all = prep
    n = proj.shape[0]
    tb = min(tb, seq)
    cpb = tb // CHUNK
    nblk = seq // tb
    assert n_pairs % SCAN_PAIRS == 0
    width = SCAN_PAIRS * PAIR
    groups = n_pairs // SCAN_PAIRS
    blk = pl.BlockSpec((SCAN_PAIRS, cpb, PAIR, PAIR), lambda b, p, t: (p, b * nblk + t, 0, 0))
    col = lambda off: pl.BlockSpec((tb, width), lambda b, p, t: (b * nblk + t, off + p))
    par = pl.BlockSpec((1, width), lambda b, p, t: (0, p))
    vec = lambda z: z.reshape(1, -1)
    return pl.pallas_call(
        functools.partial(_wkv_scan_body, cpb=cpb),
        grid=(n // seq, groups, nblk),
        in_specs=[blk] * 4 + [col(0), col(groups), col(2 * groups), col(0), col(0)] + [par] * 4,
        out_specs=col(0),
        out_shape=jax.ShapeDtypeStruct((n, n_pairs * PAIR), BF16),
        scratch_shapes=[pltpu.VMEM((SCAN_PAIRS, PAIR, PAIR), F32), pltpu.VMEM((tb, width), F32)],
        compiler_params=_params("parallel", "parallel", "arbitrary"),
        name="wkv_chunk_scan",
    )(g_all, s0c_all, rp_all, y0_all, proj, proj, proj, a, gate,
      vec(k_a), vec(r_k), vec(ln_g), vec(ln_b))


def _memattn_body(q_ref, m_ref, o_ref):
    width = MEM_HEADS * MEM_HEAD_DIM
    scale = MEM_HEAD_DIM ** -0.5
    for h in range(MEM_HEADS):
        sl = slice(h * MEM_HEAD_DIM, (h + 1) * MEM_HEAD_DIM)
        q = q_ref[:, sl].astype(BF16)
        mk = m_ref[:, sl].astype(BF16)
        mv = m_ref[:, width + h * MEM_HEAD_DIM:width + (h + 1) * MEM_HEAD_DIM].astype(BF16)
        s = _dot_nt(q, mk) * scale
        e = jnp.exp(s - jnp.max(s, axis=-1, keepdims=True))
        p = e / jnp.sum(e, axis=-1, keepdims=True)
        o_ref[:, sl] = _dot(p.astype(BF16), mv).astype(o_ref.dtype)


def memory_attention(proj, q_col_block, mkv, seq, n_mem, tm=512):
    n = proj.shape[0]
    width = MEM_HEADS * MEM_HEAD_DIM
    tm = min(tm, seq)
    nblk = seq // tm
    return pl.pallas_call(
        _memattn_body,
        grid=(n // seq, nblk),
        in_specs=[pl.BlockSpec((tm, width), lambda b, t: (b * nblk + t, q_col_block)),
                  pl.BlockSpec((n_mem, 2 * width), lambda b, t: (b, 0))],
        out_specs=pl.BlockSpec((tm, width), lambda b, t: (b * nblk + t, 0)),
        out_shape=jax.ShapeDtypeStruct((n, width), BF16),
        compiler_params=_params("parallel", "parallel"),
        name="memory_attention",
    )(proj, mkv)


def _outproj_body(s_ref, m_ref, w1_ref, w2_ref, x_ref, o_ref):
    o_ref[...] = x_ref[...] + _dot(s_ref[...], w1_ref[...]) + _dot(m_ref[...], w2_ref[...])


def out_projection(seq_out, mem_out, w_seq, w_mem, x, tm=512, tn=512):
    n, d = x.shape
    tm, tn = min(tm, n), min(tn, d)
    ws, wm = seq_out.shape[1], mem_out.shape[1]
    return pl.pallas_call(
        _outproj_body,
        grid=(n // tm, d // tn),
        in_specs=[pl.BlockSpec((tm, ws), lambda i, j: (i, 0)),
                  pl.BlockSpec((tm, wm), lambda i, j: (i, 0)),
                  pl.BlockSpec((ws, tn), lambda i, j: (0, j)),
                  pl.BlockSpec((wm, tn), lambda i, j: (0, j)),
                  pl.BlockSpec((tm, tn), lambda i, j: (i, j))],
        out_specs=pl.BlockSpec((tm, tn), lambda i, j: (i, j)),
        out_shape=jax.ShapeDtypeStruct((n, d), F32),
        compiler_params=_params("parallel", "parallel"),
        name="out_projection",
    )(seq_out, mem_out, w_seq, w_mem, x)


def _bandattn_body(q_ref, kp_ref, kc_ref, vp_ref, vc_ref, bias_ref, o_ref, *, tq):
    qi = pl.program_id(2)
    scale = HEAD_DIM ** -0.5
    k_all = jnp.concatenate([kp_ref[...], kc_ref[...]], axis=0).astype(BF16)
    v_all = jnp.concatenate([vp_ref[...], vc_ref[...]], axis=0).astype(BF16)
    lane = lax.broadcasted_iota(jnp.int32, (CHUNK, PAIR), 1)
    head0 = lane < HEAD_DIM
    kcol = lax.broadcasted_iota(jnp.int32, (2 * CHUNK, BAND), 1)
    pad = LEFT_CHUNKS * CHUNK
    bias = bias_ref[...]
    chunks = range(tq // CHUNK)
    starts = [tq - pad + j * CHUNK for j in chunks]
    scores = []
    for j, start in zip(chunks, starts):
        q = q_ref[j * CHUNK:(j + 1) * CHUNK, :]
        q2 = jnp.concatenate([jnp.where(head0, q, 0.0), jnp.where(head0, 0.0, q)], axis=0).astype(BF16)
        scores.append(_dot_nt(q2, k_all[start:start + BAND]))
    exps = []
    for start, s in zip(starts, scores):
        valid = jnp.logical_or(qi > 0, kcol + start >= tq)
        s = jnp.where(valid, s * scale + bias, -jnp.inf)
        exps.append(jnp.exp(s - jnp.max(s, axis=-1, keepdims=True)))
    outs = [_dot(e.astype(BF16), v_all[start:start + BAND]) for start, e in zip(starts, exps)]
    for j, e, o in zip(chunks, exps, outs):
        o = o / jnp.sum(e, axis=-1, keepdims=True)
        o_ref[j * CHUNK:(j + 1) * CHUNK, :] = jnp.where(head0, o[:CHUNK], o[CHUNK:]).astype(o_ref.dtype)


def band_attention(proj, kv, bias, n_pairs, seq, tq=512):
    n = proj.shape[0]
    tq = min(tq, seq)
    assert tq >= LEFT_CHUNKS * CHUNK
    nblk = seq // tq
    cur = lambda off: pl.BlockSpec((tq, PAIR), lambda b, p, t: (b * nblk + t, off + p))
    prev = lambda off: pl.BlockSpec((tq, PAIR), lambda b, p, t: (b * nblk + jnp.maximum(t - 1, 0), off + p))
    return pl.pallas_call(
        functools.partial(_bandattn_body, tq=tq),
        grid=(n // seq, n_pairs, nblk),
        in_specs=[cur(0), prev(0), cur(0), prev(n_pairs), cur(n_pairs),
                  pl.BlockSpec((2 * CHUNK, BAND), lambda b, p, t: (p, 0))],
        out_specs=cur(0),
        out_shape=jax.ShapeDtypeStruct((n, n_pairs * PAIR), BF16),
        compiler_params=_params("parallel", "parallel", "parallel"),
        name="band_attention",
    )(proj, kv, kv, kv, kv, bias.reshape(-1, BAND))


def _top16(scores, payloads):
    nl = scores[0].shape[1]
    rids = [lax.broadcasted_iota(jnp.int32, s.shape, 0).astype(F32) for s in scores]
    slot = lax.broadcasted_iota(jnp.int32, (PEER_TOPK, nl), 0)

    def body(i, carry):
        sel = slot == i
        out = []
        for (s, vals, picks), rid, payload in zip(carry, rids, payloads):
            m = jnp.max(s, axis=0, keepdims=True)
            am = jnp.min(jnp.where(s == m, rid, float(s.shape[0])), axis=0, keepdims=True)
            hit = rid == am
            if payload is None:
                pick = am
            else:
                pick = jnp.sum(jnp.where(hit, payload, 0.0), axis=0, keepdims=True)
            out.append((jnp.where(hit, -jnp.inf, s), jnp.where(sel, m, vals), jnp.where(sel, pick, picks)))
        return tuple(out)

    zero = jnp.zeros((PEER_TOPK, nl), F32)
    res = lax.fori_loop(0, PEER_TOPK, body, tuple((s, zero, zero) for s in scores))
    return [(vals, picks) for _, vals, picks in res]


def _pair_candidates(a, b, combine):
    half = PEER_TOPK // 2
    rows = [combine(a[0:1], b)]
    rows += [combine(a[i:i + 1], b[:half]) for i in range(1, half)]
    rows.append(combine(a[half:], b[0:1]))
    return jnp.concatenate(rows, axis=0)


TOPK_HEADS = 2


def _peer_topk_body(q_ref, keys_ref, eidx_ref, gate_ref):
    q = q_ref[...].astype(BF16)
    scores = []
    for h in range(TOPK_HEADS):
        for half in range(2):
            col = (2 * h + half) * PEER_KEYS
            scores.append(_dot_nt(keys_ref[h, half].astype(BF16), q[:, col:col + PEER_KEYS]))
    cands, cidxs = [], []
    for h in range(TOPK_HEADS):
        (a, i1), (b, i2) = _top16(scores[2 * h:2 * h + 2], [None, None])
        cands.append(_pair_candidates(a, b, lambda x, y: x + y))
        cidxs.append(_pair_candidates(i1, i2, lambda x, y: x * PEER_KEYS + y))
    for h, (top, eidx) in enumerate(_top16(cands, cidxs)):
        rows = slice(h * PEER_TOPK, (h + 1) * PEER_TOPK)
        e = jnp.exp(top - top[0:1])
        gate_ref[rows, :] = e / jnp.sum(e, axis=0, keepdims=True)
        eidx_ref[rows, :] = eidx.astype(jnp.int32)


def peer_topk(q, keys, tl=128):
    n = q.shape[0]
    tl = min(tl, n)
    blk = pl.BlockSpec((TOPK_HEADS * PEER_TOPK, tl), lambda i, h: (h, i))
    return pl.pallas_call(
        _peer_topk_body,
        grid=(n // tl, PEER_HEADS // TOPK_HEADS),
        in_specs=[pl.BlockSpec((tl, TOPK_HEADS * 2 * PEER_KEYS), lambda i, h: (i, h)),
                  pl.BlockSpec((TOPK_HEADS, 2, PEER_KEYS, PEER_KEYS), lambda i, h: (h, 0, 0, 0))],
        out_specs=[blk, blk],
        out_shape=[jax.ShapeDtypeStruct((PEER_PICKS, n), jnp.int32),
                   jax.ShapeDtypeStruct((PEER_PICKS, n), F32)],
        compiler_params=_params("parallel", "parallel"),
        name="peer_topk",
    )(q, keys)


PEER_SLOTS = 8
LANES = 128
PITCH_PAD = 4


def _pack_body(u_ref, v_ref, o_ref):
    hi = lax.bitcast_convert_type(u_ref[0].astype(BF16).astype(F32), jnp.uint32)
    lo = lax.bitcast_convert_type(v_ref[0].astype(BF16).astype(F32), jnp.uint32)
    words = hi | (lo >> 16)
    for c in range(o_ref.shape[1]):
        o_ref[:, c, :] = words[:, c * LANES:(c + 1) * LANES]


def _pack_expert_table(u, v, layer, te=256):
    _, n_exp, d = u.shape
    chunks = d // LANES
    te = min(te, n_exp)
    packed = pl.pallas_call(
        _pack_body,
        grid=(n_exp // te,),
        in_specs=[pl.BlockSpec((1, te, d), lambda i: (layer, i, 0))] * 2,
        out_specs=pl.BlockSpec((te, chunks, LANES), lambda i: (i, 0, 0)),
        out_shape=jax.ShapeDtypeStruct((n_exp, chunks, LANES), jnp.uint32),
        compiler_params=_params("parallel"),
        name="pack_expert_table",
    )(u, v)
    return packed.reshape(n_exp * chunks, LANES)


def _peer_ffn_body(eidx_ref, enext_ref, gate_ref, hn_ref, x_ref, tab_ref, *rest, tb, d, out_norm):
    norm_ref, o_ref, scratch = (rest[0], rest[1], rest[2:]) if out_norm else (None, rest[0], rest[1:])
    rows_refs, (sem_ref, w_ref, stage_ref) = scratch[:PEER_SLOTS], scratch[PEER_SLOTS:]
    chunks = d // LANES
    pitch = chunks + PITCH_PAD
    block = pl.program_id(0)

    def issue(t, slot, ids_ref=eidx_ref):
        for e in range(PEER_PICKS):
            src = tab_ref.at[pl.ds(pl.multiple_of(ids_ref[t, e] * chunks, chunks), chunks), :]
            dst = rows_refs[slot].at[pl.ds(e * pitch, chunks), :]
            pltpu.make_async_copy(src, dst, sem_ref.at[slot]).start(priority=e % 2)

    def wait(slot):
        total = PEER_PICKS * chunks
        pltpu.make_async_copy(tab_ref.at[pl.ds(0, total), :],
                              rows_refs[slot].at[pl.ds(0, total), :], sem_ref.at[slot]).wait()

    tok_lane = lax.broadcasted_iota(jnp.int32, (PEER_PICKS, tb), 1)
    hi_mask = jnp.uint32(0xFFFF0000)

    def words(slot, c):
        return rows_refs[slot][pl.ds(c, PEER_PICKS, stride=pitch), :]

    def pick_weights(t, slot):
        xt = hn_ref[pl.ds(t, 1), :]
        acc = jnp.zeros((PEER_PICKS, LANES), F32)
        for c in range(chunks):
            u = lax.bitcast_convert_type(words(slot, c) & hi_mask, F32)
            acc = acc + u * xt[:, c * LANES:(c + 1) * LANES]
        act = jnp.sum(acc, axis=1, keepdims=True)
        act = 0.5 * act * (1.0 + lax.erf(act * (2.0 ** -0.5)))
        gate = jnp.sum(jnp.where(tok_lane == t, gate_ref[...], 0.0), axis=1, keepdims=True)
        w_ref[slot] = jnp.broadcast_to(gate * act, (PEER_PICKS, LANES))

    def combine(slot):
        for c in range(chunks):
            v = lax.bitcast_convert_type(words(slot, c) << 16, F32)
            stage_ref[slot:slot + 1, c * LANES:(c + 1) * LANES] = jnp.sum(v * w_ref[slot], axis=0, keepdims=True)

    def store_group(g):
        rows = pl.ds(pl.multiple_of(g * PEER_SLOTS, PEER_SLOTS), PEER_SLOTS)
        y = x_ref[rows, :] + stage_ref[...]
        if norm_ref is not None:
            y = y * lax.rsqrt(jnp.mean(y * y, axis=-1, keepdims=True) + RMS_EPS) * norm_ref[...]
        o_ref[rows, :] = y

    ahead = PEER_SLOTS - 1
    groups = tb // PEER_SLOTS

    @pl.when(block == 0)
    def _():
        for t in range(ahead):
            issue(t, t)

    wait(0)
    pick_weights(0, 0)

    def group(g, carry):
        for s in range(PEER_SLOTS):
            t = g * PEER_SLOTS + s
            wait((s + 1) % PEER_SLOTS)
            pick_weights(t + 1, (s + 1) % PEER_SLOTS)
            combine(s)
            issue(t + ahead, (s + ahead) % PEER_SLOTS)
        store_group(g)
        return carry

    lax.fori_loop(0, groups - 1, group, 0)

    for s in range(PEER_SLOTS):
        t = (groups - 1) * PEER_SLOTS + s
        if t + ahead < tb:
            issue(t + ahead, (s + ahead) % PEER_SLOTS)
        if t + 1 < tb:
            wait((s + 1) % PEER_SLOTS)
            pick_weights(t + 1, (s + 1) % PEER_SLOTS)
        combine(s)
        if s < ahead:

            @pl.when(block + 1 < pl.num_programs(0))
            def _(s=s):
                issue(s, s, enext_ref)
    store_group(groups - 1)


def peer_expert_ffn(eidx, gate, hn, x, table, tb=128, out_norm_g=None):
    n, d = x.shape
    tb = min(tb, n)
    assert tb % PEER_SLOTS == 0 and table.shape[1] == LANES
    assert table.shape[0] >= PEER_PICKS * (d // LANES)
    slot_rows = PEER_PICKS * (d // LANES + PITCH_PAD)
    per_block = tb // PEER_SLOTS
    last_rows = n // PEER_SLOTS - 1
    out_norm = out_norm_g is not None
    in_specs = [pl.BlockSpec((tb, PEER_PICKS), lambda i: (i, 0), memory_space=pltpu.SMEM),
                pl.BlockSpec((PEER_SLOTS, PEER_PICKS), lambda i: (jnp.minimum((i + 1) * per_block, last_rows), 0),
                             memory_space=pltpu.SMEM),
                pl.BlockSpec((PEER_PICKS, tb), lambda i: (0, i)),
                pl.BlockSpec((tb, d), lambda i: (i, 0)),
                pl.BlockSpec((tb, d), lambda i: (i, 0)),
                pl.BlockSpec(memory_space=pl.ANY)]
    operands = [eidx, eidx, gate, hn, x, table]
    if out_norm:
        in_specs.append(pl.BlockSpec((1, d), lambda i: (0, 0)))
        operands.append(out_norm_g.reshape(1, d))
    return pl.pallas_call(
        functools.partial(_peer_ffn_body, tb=tb, d=d, out_norm=out_norm),
        grid=(n // tb,),
        in_specs=in_specs,
        out_specs=pl.BlockSpec((tb, d), lambda i: (i, 0)),
        out_shape=jax.ShapeDtypeStruct((n, d), F32),
        scratch_shapes=[pltpu.VMEM((slot_rows, LANES), jnp.uint32)] * PEER_SLOTS
                       + [pltpu.SemaphoreType.DMA((PEER_SLOTS,)),
                          pltpu.VMEM((PEER_SLOTS, PEER_PICKS, LANES), F32),
                          pltpu.VMEM((PEER_SLOTS, d), F32)],
        compiler_params=_params("arbitrary"),
        name="peer_expert_ffn",
    )(*operands)


def peer_layer(x, norm_g, wq, keys, u_all, v_all, layer, tb=256, out_norm_g=None):
    q, hn = norm_matmul(x, norm_g, wq.astype(BF16), emit_hn=True)
    eidx, gate = peer_topk(q, keys)
    return peer_expert_ffn(eidx.T, gate, hn, x, _pack_expert_table(u_all, v_all, layer), tb=tb,
                           out_norm_g=out_norm_g)


def _band_bias(rel_bias):
    pad = LEFT_CHUNKS * CHUNK
    dist = pad + np.arange(CHUNK)[:, None] - np.arange(BAND)[None, :]
    idx = np.clip(dist, -(CHUNK - 1), REL_MAX) + (CHUNK - 1)
    return rel_bias[:, idx].astype(F32)


def kernel(x, mem, norm_mix, norm_ffn, norm_mem, w_mem_kv, w_out, peer_wq, peer_keys, peer_u, peer_v, a_mix, a_w_in, a_w0, a_w1, a_w2, a_a0, a_a1, a_a2, a_g1, a_g2, a_k_k, a_k_a, a_r_k, a_ln_g, a_ln_b, kv_norm, w_kv_shared, b_w_in, b_rel_bias, final_norm):
    bsz, seq, d = x.shape
    n = bsz * seq
    n_mem = mem.shape[1]
    seq_width = a_w0.shape[1]
    n_pairs = seq_width // PAIR
    mem_width = MEM_HEADS * MEM_HEAD_DIM
    x = x.reshape(n, d)
    mem2 = mem.reshape(bsz * n_mem, d)

    def mixer_tail(x, seq_out, proj, q_col_block, layer, out_norm_g=None):
        mkv = norm_matmul(mem2, norm_mem[layer], w_mem_kv[layer].astype(BF16), tm=256)
        mem_out = memory_attention(proj, q_col_block, mkv, seq, n_mem)
        wo = w_out[layer].astype(BF16)
        x = out_projection(seq_out, mem_out, wo[:seq_width], wo[seq_width:], x)
        return peer_layer(x, norm_ffn[layer], peer_wq[layer], peer_keys[layer], peer_u, peer_v, layer,
                          out_norm_g=out_norm_g)

    h = rmsnorm(x, norm_mix[0])
    mix_tab = jnp.concatenate([a_mix[0], jnp.zeros((1, d), F32)], axis=0)
    tiles = seq_width // 512
    gid_main = jnp.asarray([0] * tiles + [2] * tiles + [3] * tiles + [6] * (mem_width // 512), jnp.int32)
    proj = mix_matmul(h, mix_tab, gid_main, a_w_in[0].astype(BF16), seq, tn=512)
    rank = a_w1.shape[2]
    padc = lambda w: jnp.pad(w, ((0, 0), (0, LORA_PAD - rank)))
    padr = lambda w: jnp.pad(w, ((0, LORA_PAD - rank), (0, 0)))
    w_l1 = jnp.concatenate([padc(a_w1[0]), padc(a_a1[0]), a_g1[0]], axis=1).astype(BF16)
    gid_l1 = jnp.asarray([1, 4] + [5] * (a_g1.shape[2] // LORA_PAD), jnp.int32)
    t1 = mix_matmul(h, mix_tab, gid_l1, w_l1, seq, tn=LORA_PAD)
    lw, a_iclr, gate = lora_stage2(t1, padr(a_w2[0]).astype(BF16), padr(a_a2[0]).astype(BF16),
                                   a_g2[0].astype(BF16), a_w0[0], a_a0[0])
    prep = wkv_chunk_prepare(proj, lw, a_iclr, a_k_k[0], a_k_a[0], n_pairs)
    seq_out = wkv_chunk_scan(prep, proj, a_iclr, gate, a_k_a[0], a_r_k[0].reshape(-1),
                             a_ln_g[0], a_ln_b[0], n_pairs, seq)
    x = mixer_tail(x, seq_out, proj, (3 * seq_width) // mem_width, 0)

    kv = norm_matmul(x, kv_norm, w_kv_shared.astype(BF16))

    proj = norm_matmul(x, norm_mix[1], b_w_in[0].astype(BF16))
    seq_out = band_attention(proj, kv, _band_bias(b_rel_bias[0]), n_pairs, seq)
    x = mixer_tail(x, seq_out, proj, seq_width // mem_width, 1, out_norm_g=final_norm)
    return x.reshape(bsz, seq, d)
```

```python
import functools

import numpy as np
import jax
import jax.numpy as jnp
from jax import lax
from jax.experimental import pallas as pl
from jax.experimental.pallas import tpu as pltpu

F32 = jnp.float32
BF16 = jnp.bfloat16
HI = lax.Precision.HIGHEST

HEAD_DIM = 64
PAIR = 2 * HEAD_DIM
CHUNK = 64
SCAN_PAIRS = 2
LEFT_CHUNKS = 8
BAND = (LEFT_CHUNKS + 1) * CHUNK
REL_MAX = 128
MEM_HEADS = 4
MEM_HEAD_DIM = 128
PEER_KEYS = 128
PEER_HEADS = 8
PEER_TOPK = 16
PEER_PICKS = PEER_HEADS * PEER_TOPK
GN_EPS = 64e-5
RMS_EPS = 1e-6
LORA_PAD = 128
VMEM_LIMIT = 48 * 1024 * 1024


def _params(*sem):
    return pltpu.CompilerParams(dimension_semantics=sem, vmem_limit_bytes=VMEM_LIMIT)


def _dot(a, b):
    return jnp.dot(a, b, preferred_element_type=F32)


def _dot_nt(a, b, precision=None):
    return lax.dot_general(a, b, (((1,), (1,)), ((), ())), precision=precision,
                           preferred_element_type=F32)


def _dot_tn(a, b, precision=None):
    return lax.dot_general(a, b, (((0,), (0,)), ((), ())), precision=precision,
                           preferred_element_type=F32)


def _rmsnorm_body(x_ref, g_ref, o_ref):
    x = x_ref[...]
    ms = jnp.mean(x * x, axis=-1, keepdims=True)
    o_ref[...] = (x * lax.rsqrt(ms + RMS_EPS) * g_ref[...]).astype(o_ref.dtype)


def rmsnorm(x, g, tm=512):
    n, d = x.shape
    tm = min(tm, n)
    return pl.pallas_call(
        _rmsnorm_body,
        grid=(n // tm,),
        in_specs=[pl.BlockSpec((tm, d), lambda i: (i, 0)), pl.BlockSpec((1, d), lambda i: (0, 0))],
        out_specs=pl.BlockSpec((tm, d), lambda i: (i, 0)),
        out_shape=jax.ShapeDtypeStruct((n, d), F32),
        compiler_params=_params("parallel"),
        name="rmsnorm",
    )(x, g.reshape(1, d))


def _normmm_body(x_ref, g_ref, w_ref, o_ref, *rest, emit_hn):
    lhs_ref = rest[-1]

    @pl.when(pl.program_id(1) == 0)
    def _():
        x = x_ref[...]
        ms = jnp.mean(x * x, axis=-1, keepdims=True)
        hn = x * lax.rsqrt(ms + RMS_EPS) * g_ref[...]
        lhs_ref[...] = hn.astype(BF16)
        if emit_hn:
            rest[0][...] = hn

    o_ref[...] = _dot(lhs_ref[...], w_ref[...]).astype(o_ref.dtype)


def norm_matmul(x, g, w_bf16, emit_hn=False, tm=512, tn=1024, out_dtype=BF16):
    n, d = x.shape
    nc = w_bf16.shape[1]
    tm, tn = min(tm, n), min(tn, nc)
    out_shape = [jax.ShapeDtypeStruct((n, nc), out_dtype)]
    out_specs = [pl.BlockSpec((tm, tn), lambda i, j: (i, j))]
    if emit_hn:
        out_shape.append(jax.ShapeDtypeStruct((n, d), F32))
        out_specs.append(pl.BlockSpec((tm, d), lambda i, j: (i, 0)))
    res = pl.pallas_call(
        functools.partial(_normmm_body, emit_hn=emit_hn),
        grid=(n // tm, nc // tn),
        in_specs=[pl.BlockSpec((tm, d), lambda i, j: (i, 0)),
                  pl.BlockSpec((1, d), lambda i, j: (0, 0)),
                  pl.BlockSpec((d, tn), lambda i, j: (0, j))],
        out_specs=out_specs,
        out_shape=out_shape,
        scratch_shapes=[pltpu.VMEM((tm, d), BF16)],
        compiler_params=_params("parallel", "arbitrary"),
        name="norm_matmul",
    )(x, g.reshape(1, d), w_bf16)
    return res if emit_hn else res[0]


def _mixmm_body(gid_ref, h_ref, hp_ref, mix_ref, w_ref, o_ref, lhs_ref, *, tm, seq):
    i = pl.program_id(0)
    j = pl.program_id(1)
    new_group = jnp.logical_or(j == 0, gid_ref[j] != gid_ref[jnp.maximum(j - 1, 0)])

    @pl.when(new_group)
    def _():
        h = h_ref[...]
        prev = jnp.where((i * tm) % seq == 0, 0.0, hp_ref[7:8, :])
        row = lax.broadcasted_iota(jnp.int32, h.shape, 0)
        shifted = jnp.where(row == 0, prev, pltpu.roll(h, 1, 0))
        lhs_ref[...] = (h + (shifted - h) * mix_ref[0]).astype(BF16)

    o_ref[...] = _dot(lhs_ref[...], w_ref[...])


def mix_matmul(h, mix_tab, gid, w_bf16, seq, tn, tm=1024):
    n, d = h.shape
    nc = w_bf16.shape[1]
    tm = min(tm, seq)
    sub = tm // 8
    grid_spec = pltpu.PrefetchScalarGridSpec(
        num_scalar_prefetch=1,
        grid=(n // tm, nc // tn),
        in_specs=[pl.BlockSpec((tm, d), lambda i, j, g: (i, 0)),
                  pl.BlockSpec((8, d), lambda i, j, g: (jnp.maximum(i * sub - 1, 0), 0)),
                  pl.BlockSpec((1, 1, d), lambda i, j, g: (g[j], 0, 0)),
                  pl.BlockSpec((d, tn), lambda i, j, g: (0, j))],
        out_specs=pl.BlockSpec((tm, tn), lambda i, j, g: (i, j)),
        scratch_shapes=[pltpu.VMEM((tm, d), BF16)],
    )
    return pl.pallas_call(
        functools.partial(_mixmm_body, tm=tm, seq=seq),
        grid_spec=grid_spec,
        out_shape=jax.ShapeDtypeStruct((n, nc), F32),
        compiler_params=_params("parallel", "arbitrary"),
        name="mix_matmul",
    )(gid, h, h, mix_tab.reshape(mix_tab.shape[0], 1, d), w_bf16)


def _sigmoid(x):
    return 1.0 / (1.0 + jnp.exp(-x))


def _lora2_body(t_ref, w2_ref, a2_ref, g2_ref, w0_ref, a0_ref, lw_ref, a_ref, g_ref):
    t = t_ref[...]
    tw = jnp.tanh(t[:, :LORA_PAD]).astype(BF16)
    ta = t[:, LORA_PAD:2 * LORA_PAD].astype(BF16)
    tg = _sigmoid(t[:, 2 * LORA_PAD:]).astype(BF16)
    u = w0_ref[...] + _dot(tw, w2_ref[...])
    softplus_neg_u = jnp.maximum(-u, 0.0) + jnp.log(1.0 + jnp.exp(-jnp.abs(u)))
    lw_ref[...] = -jnp.exp(-softplus_neg_u - 0.5)
    a_ref[...] = _sigmoid(a0_ref[...] + _dot(ta, a2_ref[...]))
    g_ref[...] = _dot(tg, g2_ref[...])


def lora_stage2(t1, w2p, a2p, g2, w0, a0, tm=256):
    n = t1.shape[0]
    width = w2p.shape[1]
    tm = min(tm, n)
    full = lambda a: pl.BlockSpec(a.shape, lambda i: (0, 0))
    row = pl.BlockSpec((tm, width), lambda i: (i, 0))
    w0 = w0.reshape(1, width)
    a0 = a0.reshape(1, width)
    return pl.pallas_call(
        _lora2_body,
        grid=(n // tm,),
        in_specs=[pl.BlockSpec((tm, t1.shape[1]), lambda i: (i, 0)), full(w2p), full(a2p), full(g2),
                  full(w0), full(a0)],
        out_specs=[row, row, row],
        out_shape=[jax.ShapeDtypeStruct((n, width), F32)] * 3,
        compiler_params=_params("parallel"),
        name="lora_stage2",
    )(t1, w2p, a2p, g2, w0, a0)


def _head_group_sum(x):
    width = x.shape[-1]
    r = lax.broadcasted_iota(jnp.int32, (width, width), 0) // HEAD_DIM
    c = lax.broadcasted_iota(jnp.int32, (width, width), 1) // HEAD_DIM
    ones = jnp.where(r == c, 1.0, 0.0).astype(F32)
    return jnp.dot(x, ones, precision=HI, preferred_element_type=F32)


def _wkv_prepare_body(r_ref, k_ref, v_ref, lw_ref, a_ref, kk_ref, ka_ref,
                      g_ref, s0c_ref, rp_ref, y0_ref, *, tb):
    r = r_ref[...]
    k = k_ref[...]
    v = v_ref[...]
    lw = lw_ref[...]
    a = a_ref[...]
    kk = k * kk_ref[...]
    norm = jnp.sqrt(_head_group_sum(kk * kk))
    kk = kk / jnp.maximum(norm, 1e-12)
    kmod = k * (1.0 + (a - 1.0) * ka_ref[...])
    avec = -kk
    bvec = kk * a

    tr = lax.broadcasted_iota(jnp.int32, (CHUNK, CHUNK), 0)
    tc = lax.broadcasted_iota(jnp.int32, (CHUNK, CHUNK), 1)
    tri = jnp.where(tr >= tc, 1.0, 0.0).astype(F32)

    lane = lax.broadcasted_iota(jnp.int32, (CHUNK, PAIR), 1)
    head0 = lane < HEAD_DIM
    row = lax.broadcasted_iota(jnp.int32, (PAIR, PAIR), 0)
    col = lax.broadcasted_iota(jnp.int32, (PAIR, PAIR), 1)
    strict = row > col
    lower = row >= col
    eye = jnp.where(row == col, 1.0, 0.0).astype(F32)

    def stack(x):
        return jnp.concatenate([jnp.where(head0, x, 0.0), jnp.where(head0, 0.0, x)], axis=0)

    chunks = range(tb // CHUNK)
    pre = []
    for c in chunks:
        sl = slice(c * CHUNK, (c + 1) * CHUNK)
        cm = jnp.dot(tri, lw[sl], precision=HI, preferred_element_type=F32)
        cend = cm[CHUNK - 1:CHUNK]
        e_in = jnp.exp(cm)
        e_out = jnp.exp(-cm)
        e_tail = jnp.exp(cend - cm)
        a_s = stack(avec[sl] * jnp.exp(cm - lw[sl]))
        r_s = stack(r[sl] * e_in)
        b_s = stack(bvec[sl] * e_out)
        k_s = stack(kmod[sl] * e_out)
        bt_s = stack(bvec[sl] * e_tail)
        kt_s = stack(kmod[sl] * e_tail)
        v_s = stack(v[sl])
        p = _dot_nt(jnp.concatenate([a_s, r_s], axis=0).astype(BF16),
                    jnp.concatenate([b_s, k_s], axis=0).astype(BF16))
        pre.append(dict(cend=cend, a_s=a_s, r_s=r_s, bt_s=bt_s, kt_s=kt_s, v_s=v_s,
                        l_ab=jnp.where(strict, p[:PAIR, :PAIR], 0.0),
                        l_ak=jnp.where(strict, p[:PAIR, PAIR:], 0.0),
                        a_rb=jnp.where(lower, p[PAIR:, :PAIR], 0.0),
                        a_rk=jnp.where(lower, p[PAIR:, PAIR:], 0.0)))

    ms = [q["l_ab"] for q in pre]
    ts = [eye + m for m in ms]
    for _ in range(5):
        ms = [_dot(m.astype(BF16), m.astype(BF16)) for m in ms]
        ts = [t + _dot(t.astype(BF16), m.astype(BF16)) for t, m in zip(ts, ms)]

    lvs = [_dot(q["l_ak"].astype(BF16), q["v_s"].astype(BF16)) for q in pre]
    aus = [_dot(t.astype(BF16), jnp.concatenate([q["a_s"], lv], axis=1).astype(BF16))
           for t, q, lv in zip(ts, pre, lvs)]
    zero = jnp.zeros((PAIR, PAIR), F32)
    xs = [_dot(jnp.concatenate([q["a_rb"], q["a_rk"]], axis=1).astype(BF16),
               jnp.concatenate([au, jnp.concatenate([zero, q["v_s"]], axis=1)], axis=0).astype(BF16))
          for q, au in zip(pre, aus)]
    for c, q, au, x in zip(chunks, pre, aus, xs):
        rp_ref[0, c] = q["r_s"] + x[:, :PAIR]
        y0_ref[0, c] = x[:, PAIR:]
        ap = au[:, :PAIR]
        u0 = au[:, PAIR:]
        g_ref[0, c] = eye * jnp.exp(q["cend"]) + _dot_tn(ap.astype(BF16), q["bt_s"].astype(BF16))
        s0c_ref[0, c] = _dot_tn(jnp.concatenate([u0, q["v_s"]], axis=0).astype(BF16),
                                jnp.concatenate([q["bt_s"], q["kt_s"]], axis=0).astype(BF16))


def wkv_chunk_prepare(proj, lw, a, k_k, k_a, n_pairs, tb=512):
    n = proj.shape[0]
    tb = min(tb, n)
    cpb = tb // CHUNK
    col = lambda off: pl.BlockSpec((tb, PAIR), lambda i, p: (i, off + p))
    par = pl.BlockSpec((1, PAIR), lambda i, p: (0, p))
    blk = pl.BlockSpec((1, cpb, PAIR, PAIR), lambda i, p: (p, i, 0, 0))
    shp = jax.ShapeDtypeStruct((n_pairs, n // CHUNK, PAIR, PAIR), F32)
    return pl.pallas_call(
        functools.partial(_wkv_prepare_body, tb=tb),
        grid=(n // tb, n_pairs),
        in_specs=[col(0), col(n_pairs), col(2 * n_pairs), col(0), col(0), par, par],
        out_specs=[blk] * 4,
        out_shape=[shp] * 4,
        compiler_params=_params("parallel", "parallel"),
        name="wkv_chunk_prepare",
    )(proj, proj, proj, lw, a, k_k.reshape(1, -1), k_a.reshape(1, -1))


def _wkv_scan_body(g_ref, s0c_ref, rp_ref, y0_ref, r_ref, k_ref, v_ref, a_ref, gate_ref,
                   ka_ref, rk_ref, lng_ref, lnb_ref, o_ref, s_ref, y_ref, *, cpb):
    @pl.when(pl.program_id(2) == 0)
    def _():
        s_ref[...] = jnp.zeros_like(s_ref)

    pairs = range(SCAN_PAIRS)
    states = [[s_ref[w].astype(BF16)] for w in pairs]
    for c in range(cpb):
        for w in pairs:
            s = _dot(states[w][c], g_ref[w, c].astype(BF16)) + s0c_ref[w, c]
            if c + 1 < cpb:
                states[w].append(s.astype(BF16))
            else:
                s_ref[w] = s
    for c in range(cpb):
        for w in pairs:
            y_st = _dot_nt(rp_ref[w, c].astype(BF16), states[w][c]) + y0_ref[w, c]
            y_ref[c * CHUNK:(c + 1) * CHUNK, w * PAIR:(w + 1) * PAIR] = y_st[:CHUNK] + y_st[CHUNK:]

    y = y_ref[...]
    inv = 1.0 / HEAD_DIM
    mu = _head_group_sum(y) * inv
    yc = y - mu
    var = _head_group_sum(yc * yc) * inv
    yn = yc * lax.rsqrt(var + GN_EPS) * lng_ref[...] + lnb_ref[...]
    r = r_ref[...]
    kmod = k_ref[...] * (1.0 + (a_ref[...] - 1.0) * ka_ref[...])
    bonus = _head_group_sum(r * kmod * rk_ref[...]) * v_ref[...]
    o_ref[...] = ((yn + bonus) * gate_ref[...]).astype(o_ref.dtype)


def wkv_chunk_scan(prep, proj, a, gate, k_a, r_k, ln_g, ln_b, n_pairs, seq, tb=512):
    g_all, s0c_all, rp_all, y0_all = prep
    n = proj.shape[0]
    tb = min(tb, seq)
    cpb = tb // CHUNK
    nblk = seq // tb
    assert n_pairs % SCAN_PAIRS == 0
    width = SCAN_PAIRS * PAIR
    groups = n_pairs // SCAN_PAIRS
    blk = pl.BlockSpec((SCAN_PAIRS, cpb, PAIR, PAIR), lambda b, p, t: (p, b * nblk + t, 0, 0))
    col = lambda off: pl.BlockSpec((tb, width), lambda b, p, t: (b * nblk + t, off + p))
    par = pl.BlockSpec((1, width), lambda b, p, t: (0, p))
    vec = lambda z: z.reshape(1, -1)
    return pl.pallas_call(
        functools.partial(_wkv_scan_body, cpb=cpb),
        grid=(n // seq, groups, nblk),
        in_specs=[blk] * 4 + [col(0), col(groups), col(2 * groups), col(0), col(0)] + [par] * 4,
        out_specs=col(0),
        out_shape=jax.ShapeDtypeStruct((n, n_pairs * PAIR), BF16),
        scratch_shapes=[pltpu.VMEM((SCAN_PAIRS, PAIR, PAIR), F32), pltpu.VMEM((tb, width), F32)],
        compiler_params=_params("parallel", "parallel", "arbitrary"),
        name="wkv_chunk_scan",
    )(g_all, s0c_all, rp_all, y0_all, proj, proj, proj, a, gate,
      vec(k_a), vec(r_k), vec(ln_g), vec(ln_b))


def _memattn_body(q_ref, m_ref, o_ref):
    width = MEM_HEADS * MEM_HEAD_DIM
    scale = MEM_HEAD_DIM ** -0.5
    for h in range(MEM_HEADS):
        sl = slice(h * MEM_HEAD_DIM, (h + 1) * MEM_HEAD_DIM)
        q = q_ref[:, sl].astype(BF16)
        mk = m_ref[:, sl].astype(BF16)
        mv = m_ref[:, width + h * MEM_HEAD_DIM:width + (h + 1) * MEM_HEAD_DIM].astype(BF16)
        s = _dot_nt(q, mk) * scale
        e = jnp.exp(s - jnp.max(s, axis=-1, keepdims=True))
        p = e / jnp.sum(e, axis=-1, keepdims=True)
        o_ref[:, sl] = _dot(p.astype(BF16), mv).astype(o_ref.dtype)


def memory_attention(proj, q_col_block, mkv, seq, n_mem, tm=512):
    n = proj.shape[0]
    width = MEM_HEADS * MEM_HEAD_DIM
    tm = min(tm, seq)
    nblk = seq // tm
    return pl.pallas_call(
        _memattn_body,
        grid=(n // seq, nblk),
        in_specs=[pl.BlockSpec((tm, width), lambda b, t: (b * nblk + t, q_col_block)),
                  pl.BlockSpec((n_mem, 2 * width), lambda b, t: (b, 0))],
        out_specs=pl.BlockSpec((tm, width), lambda b, t: (b * nblk + t, 0)),
        out_shape=jax.ShapeDtypeStruct((n, width), BF16),
        compiler_params=_params("parallel", "parallel"),
        name="memory_attention",
    )(proj, mkv)


def _outproj_body(s_ref, m_ref, w1_ref, w2_ref, x_ref, o_ref):
    o_ref[...] = x_ref[...] + _dot(s_ref[...], w1_ref[...]) + _dot(m_ref[...], w2_ref[...])


def out_projection(seq_out, mem_out, w_seq, w_mem, x, tm=512, tn=512):
    n, d = x.shape
    tm, tn = min(tm, n), min(tn, d)
    ws, wm = seq_out.shape[1], mem_out.shape[1]
    return pl.pallas_call(
        _outproj_body,
        grid=(n // tm, d // tn),
        in_specs=[pl.BlockSpec((tm, ws), lambda i, j: (i, 0)),
                  pl.BlockSpec((tm, wm), lambda i, j: (i, 0)),
                  pl.BlockSpec((ws, tn), lambda i, j: (0, j)),
                  pl.BlockSpec((wm, tn), lambda i, j: (0, j)),
                  pl.BlockSpec((tm, tn), lambda i, j: (i, j))],
        out_specs=pl.BlockSpec((tm, tn), lambda i, j: (i, j)),
        out_shape=jax.ShapeDtypeStruct((n, d), F32),
        compiler_params=_params("parallel", "parallel"),
        name="out_projection",
    )(seq_out, mem_out, w_seq, w_mem, x)


def _bandattn_body(q_ref, kp_ref, kc_ref, vp_ref, vc_ref, bias_ref, o_ref, *, tq):
    qi = pl.program_id(2)
    scale = HEAD_DIM ** -0.5
    k_all = jnp.concatenate([kp_ref[...], kc_ref[...]], axis=0).astype(BF16)
    v_all = jnp.concatenate([vp_ref[...], vc_ref[...]], axis=0).astype(BF16)
    lane = lax.broadcasted_iota(jnp.int32, (CHUNK, PAIR), 1)
    head0 = lane < HEAD_DIM
    kcol = lax.broadcasted_iota(jnp.int32, (2 * CHUNK, BAND), 1)
    pad = LEFT_CHUNKS * CHUNK
    bias = bias_ref[...]
    chunks = range(tq // CHUNK)
    starts = [tq - pad + j * CHUNK for j in chunks]
    scores = []
    for j, start in zip(chunks, starts):
        q = q_ref[j * CHUNK:(j + 1) * CHUNK, :]
        q2 = jnp.concatenate([jnp.where(head0, q, 0.0), jnp.where(head0, 0.0, q)], axis=0).astype(BF16)
        scores.append(_dot_nt(q2, k_all[start:start + BAND]))
    exps = []
    for start, s in zip(starts, scores):
        valid = jnp.logical_or(qi > 0, kcol + start >= tq)
        s = jnp.where(valid, s * scale + bias, -jnp.inf)
        exps.append(jnp.exp(s - jnp.max(s, axis=-1, keepdims=True)))
    outs = [_dot(e.astype(BF16), v_all[start:start + BAND]) for start, e in zip(starts, exps)]
    for j, e, o in zip(chunks, exps, outs):
        o = o / jnp.sum(e, axis=-1, keepdims=True)
        o_ref[j * CHUNK:(j + 1) * CHUNK, :] = jnp.where(head0, o[:CHUNK], o[CHUNK:]).astype(o_ref.dtype)


def band_attention(proj, kv, bias, n_pairs, seq, tq=512):
    n = proj.shape[0]
    tq = min(tq, seq)
    assert tq >= LEFT_CHUNKS * CHUNK
    nblk = seq // tq
    cur = lambda off: pl.BlockSpec((tq, PAIR), lambda b, p, t: (b * nblk + t, off + p))
    prev = lambda off: pl.BlockSpec((tq, PAIR), lambda b, p, t: (b * nblk + jnp.maximum(t - 1, 0), off + p))
    return pl.pallas_call(
        functools.partial(_bandattn_body, tq=tq),
        grid=(n // seq, n_pairs, nblk),
        in_specs=[cur(0), prev(0), cur(0), prev(n_pairs), cur(n_pairs),
                  pl.BlockSpec((2 * CHUNK, BAND), lambda b, p, t: (p, 0))],
        out_specs=cur(0),
        out_shape=jax.ShapeDtypeStruct((n, n_pairs * PAIR), BF16),
        compiler_params=_params("parallel", "parallel", "parallel"),
        name="band_attention",
    )(proj, kv, kv, kv, kv, bias.reshape(-1, BAND))


def _top16(scores, payloads):
    nl = scores[0].shape[1]
    rids = [lax.broadcasted_iota(jnp.int32, s.shape, 0).astype(F32) for s in scores]
    slot = lax.broadcasted_iota(jnp.int32, (PEER_TOPK, nl), 0)

    def body(i, carry):
        sel = slot == i
        out = []
        for (s, vals, picks), rid, payload in zip(carry, rids, payloads):
            m = jnp.max(s, axis=0, keepdims=True)
            am = jnp.min(jnp.where(s == m, rid, float(s.shape[0])), axis=0, keepdims=True)
            hit = rid == am
            if payload is None:
                pick = am
            else:
                pick = jnp.sum(jnp.where(hit, payload, 0.0), axis=0, keepdims=True)
            out.append((jnp.where(hit, -jnp.inf, s), jnp.where(sel, m, vals), jnp.where(sel, pick, picks)))
        return tuple(out)

    zero = jnp.zeros((PEER_TOPK, nl), F32)
    res = lax.fori_loop(0, PEER_TOPK, body, tuple((s, zero, zero) for s in scores))
    return [(vals, picks) for _, vals, picks in res]


def _pair_candidates(a, b, combine):
    half = PEER_TOPK // 2
    rows = [combine(a[0:1], b)]
    rows += [combine(a[i:i + 1], b[:half]) for i in range(1, half)]
    rows.append(combine(a[half:], b[0:1]))
    return jnp.concatenate(rows, axis=0)


TOPK_HEADS = 2


def _peer_topk_body(q_ref, keys_ref, eidx_ref, gate_ref):
    q = q_ref[...].astype(BF16)
    scores = []
    for h in range(TOPK_HEADS):
        for half in range(2):
            col = (2 * h + half) * PEER_KEYS
            scores.append(_dot_nt(keys_ref[h, half].astype(BF16), q[:, col:col + PEER_KEYS]))
    cands, cidxs = [], []
    for h in range(TOPK_HEADS):
        (a, i1), (b, i2) = _top16(scores[2 * h:2 * h + 2], [None, None])
        cands.append(_pair_candidates(a, b, lambda x, y: x + y))
        cidxs.append(_pair_candidates(i1, i2, lambda x, y: x * PEER_KEYS + y))
    for h, (top, eidx) in enumerate(_top16(cands, cidxs)):
        rows = slice(h * PEER_TOPK, (h + 1) * PEER_TOPK)
        e = jnp.exp(top - top[0:1])
        gate_ref[rows, :] = e / jnp.sum(e, axis=0, keepdims=True)
        eidx_ref[rows, :] = eidx.astype(jnp.int32)


def peer_topk(q, keys, tl=128):
    n = q.shape[0]
    tl = min(tl, n)
    blk = pl.BlockSpec((TOPK_HEADS * PEER_TOPK, tl), lambda i, h: (h, i))
    return pl.pallas_call(
        _peer_topk_body,
        grid=(n // tl, PEER_HEADS // TOPK_HEADS),
        in_specs=[pl.BlockSpec((tl, TOPK_HEADS * 2 * PEER_KEYS), lambda i, h: (i, h)),
                  pl.BlockSpec((TOPK_HEADS, 2, PEER_KEYS, PEER_KEYS), lambda i, h: (h, 0, 0, 0))],
        out_specs=[blk, blk],
        out_shape=[jax.ShapeDtypeStruct((PEER_PICKS, n), jnp.int32),
                   jax.ShapeDtypeStruct((PEER_PICKS, n), F32)],
        compiler_params=_params("parallel", "parallel"),
        name="peer_topk",
    )(q, keys)


PEER_SLOTS = 8
LANES = 128
PITCH_PAD = 4


def _pack_body(u_ref, v_ref, o_ref):
    hi = lax.bitcast_convert_type(u_ref[0].astype(BF16).astype(F32), jnp.uint32)
    lo = lax.bitcast_convert_type(v_ref[0].astype(BF16).astype(F32), jnp.uint32)
    words = hi | (lo >> 16)
    for c in range(o_ref.shape[1]):
        o_ref[:, c, :] = words[:, c * LANES:(c + 1) * LANES]


def _pack_expert_table(u, v, layer, te=256):
    _, n_exp, d = u.shape
    chunks = d // LANES
    te = min(te, n_exp)
    packed = pl.pallas_call(
        _pack_body,
        grid=(n_exp // te,),
        in_specs=[pl.BlockSpec((1, te, d), lambda i: (layer, i, 0))] * 2,
        out_specs=pl.BlockSpec((te, chunks, LANES), lambda i: (i, 0, 0)),
        out_shape=jax.ShapeDtypeStruct((n_exp, chunks, LANES), jnp.uint32),
        compiler_params=_params("parallel"),
        name="pack_expert_table",
    )(u, v)
    return packed.reshape(n_exp * chunks, LANES)


def _peer_ffn_body(eidx_ref, gate_ref, hn_ref, x_ref, tab_ref, *rest, tb, d, out_norm):
    norm_ref, o_ref, scratch = (rest[0], rest[1], rest[2:]) if out_norm else (None, rest[0], rest[1:])
    rows_refs, (sem_ref, w_ref, stage_ref) = scratch[:PEER_SLOTS], scratch[PEER_SLOTS:]
    chunks = d // LANES
    pitch = chunks + PITCH_PAD

    def issue(t, slot):
        for e in range(PEER_PICKS):
            src = tab_ref.at[pl.ds(pl.multiple_of(eidx_ref[t, e] * chunks, chunks), chunks), :]
            dst = rows_refs[slot].at[pl.ds(e * pitch, chunks), :]
            pltpu.make_async_copy(src, dst, sem_ref.at[slot]).start(priority=e % 2)

    def wait(slot):
        total = PEER_PICKS * chunks
        pltpu.make_async_copy(tab_ref.at[pl.ds(0, total), :],
                              rows_refs[slot].at[pl.ds(0, total), :], sem_ref.at[slot]).wait()

    tok_lane = lax.broadcasted_iota(jnp.int32, (PEER_PICKS, tb), 1)
    hi_mask = jnp.uint32(0xFFFF0000)

    def words(slot, c):
        return rows_refs[slot][pl.ds(c, PEER_PICKS, stride=pitch), :]

    def pick_weights(t, slot):
        xt = hn_ref[pl.ds(t, 1), :]
        acc = jnp.zeros((PEER_PICKS, LANES), F32)
        for c in range(chunks):
            u = lax.bitcast_convert_type(words(slot, c) & hi_mask, F32)
            acc = acc + u * xt[:, c * LANES:(c + 1) * LANES]
        act = jnp.sum(acc, axis=1, keepdims=True)
        act = 0.5 * act * (1.0 + lax.erf(act * (2.0 ** -0.5)))
        gate = jnp.sum(jnp.where(tok_lane == t, gate_ref[...], 0.0), axis=1, keepdims=True)
        w_ref[slot] = jnp.broadcast_to(gate * act, (PEER_PICKS, LANES))

    def combine(slot):
        for c in range(chunks):
            v = lax.bitcast_convert_type(words(slot, c) << 16, F32)
            stage_ref[slot:slot + 1, c * LANES:(c + 1) * LANES] = jnp.sum(v * w_ref[slot], axis=0, keepdims=True)

    def store_group(g):
        rows = pl.ds(pl.multiple_of(g * PEER_SLOTS, PEER_SLOTS), PEER_SLOTS)
        y = x_ref[rows, :] + stage_ref[...]
        if norm_ref is not None:
            y = y * lax.rsqrt(jnp.mean(y * y, axis=-1, keepdims=True) + RMS_EPS) * norm_ref[...]
        o_ref[rows, :] = y

    ahead = PEER_SLOTS - 1
    groups = tb // PEER_SLOTS
    for t in range(ahead):
        issue(t, t)
    wait(0)
    pick_weights(0, 0)

    def group(g, carry):
        for s in range(PEER_SLOTS):
            t = g * PEER_SLOTS + s
            wait((s + 1) % PEER_SLOTS)
            pick_weights(t + 1, (s + 1) % PEER_SLOTS)
            combine(s)
            issue(t + ahead, (s + ahead) % PEER_SLOTS)
        store_group(g)
        return carry

    lax.fori_loop(0, groups - 1, group, 0)

    for s in range(PEER_SLOTS):
        t = (groups - 1) * PEER_SLOTS + s
        if t + ahead < tb:
            issue(t + ahead, (s + ahead) % PEER_SLOTS)
        if t + 1 < tb:
            wait((s + 1) % PEER_SLOTS)
            pick_weights(t + 1, (s + 1) % PEER_SLOTS)
        combine(s)
    store_group(groups - 1)


def peer_expert_ffn(eidx, gate, hn, x, table, tb=128, out_norm_g=None):
    n, d = x.shape
    tb = min(tb, n)
    assert tb % PEER_SLOTS == 0 and table.shape[1] == LANES
    assert table.shape[0] >= PEER_PICKS * (d // LANES)
    slot_rows = PEER_PICKS * (d // LANES + PITCH_PAD)
    out_norm = out_norm_g is not None
    in_specs = [pl.BlockSpec((tb, PEER_PICKS), lambda i: (i, 0), memory_space=pltpu.SMEM),
                pl.BlockSpec((PEER_PICKS, tb), lambda i: (0, i)),
                pl.BlockSpec((tb, d), lambda i: (i, 0)),
                pl.BlockSpec((tb, d), lambda i: (i, 0)),
                pl.BlockSpec(memory_space=pl.ANY)]
    operands = [eidx, gate, hn, x, table]
    if out_norm:
        in_specs.append(pl.BlockSpec((1, d), lambda i: (0, 0)))
        operands.append(out_norm_g.reshape(1, d))
    return pl.pallas_call(
        functools.partial(_peer_ffn_body, tb=tb, d=d, out_norm=out_norm),
        grid=(n // tb,),
        in_specs=in_specs,
        out_specs=pl.BlockSpec((tb, d), lambda i: (i, 0)),
        out_shape=jax.ShapeDtypeStruct((n, d), F32),
        scratch_shapes=[pltpu.VMEM((slot_rows, LANES), jnp.uint32)] * PEER_SLOTS
                       + [pltpu.SemaphoreType.DMA((PEER_SLOTS,)),
                          pltpu.VMEM((PEER_SLOTS, PEER_PICKS, LANES), F32),
                          pltpu.VMEM((PEER_SLOTS, d), F32)],
        compiler_params=_params("arbitrary"),
        name="peer_expert_ffn",
    )(*operands)


def peer_layer(x, norm_g, wq, keys, u_all, v_all, layer, tb=256, out_norm_g=None):
    q, hn = norm_matmul(x, norm_g, wq.astype(BF16), emit_hn=True)
    eidx, gate = peer_topk(q, keys)
    return peer_expert_ffn(eidx.T, gate, hn, x, _pack_expert_table(u_all, v_all, layer), tb=tb,
                           out_norm_g=out_norm_g)


def _band_bias(rel_bias):
    pad = LEFT_CHUNKS * CHUNK
    dist = pad + np.arange(CHUNK)[:, None] - np.arange(BAND)[None, :]
    idx = np.clip(dist, -(CHUNK - 1), REL_MAX) + (CHUNK - 1)
    return rel_bias[:, idx].astype(F32)


def kernel(x, mem, norm_mix, norm_ffn, norm_mem, w_mem_kv, w_out, peer_wq, peer_keys, peer_u, peer_v, a_mix, a_w_in, a_w0, a_w1, a_w2, a_a0, a_a1, a_a2, a_g1, a_g2, a_k_k, a_k_a, a_r_k, a_ln_g, a_ln_b, kv_norm, w_kv_shared, b_w_in, b_rel_bias, final_norm):
    bsz, seq, d = x.shape
    n = bsz * seq
    n_mem = mem.shape[1]
    seq_width = a_w0.shape[1]
    n_pairs = seq_width // PAIR
    mem_width = MEM_HEADS * MEM_HEAD_DIM
    x = x.reshape(n, d)
    mem2 = mem.reshape(bsz * n_mem, d)

    def mixer_tail(x, seq_out, proj, q_col_block, layer, out_norm_g=None):
        mkv = norm_matmul(mem2, norm_mem[layer], w_mem_kv[layer].astype(BF16), tm=256)
        mem_out = memory_attention(proj, q_col_block, mkv, seq, n_mem)
        wo = w_out[layer].astype(BF16)
        x = out_projection(seq_out, mem_out, wo[:seq_width], wo[seq_width:], x)
        return peer_layer(x, norm_ffn[layer], peer_wq[layer], peer_keys[layer], peer_u, peer_v, layer,
                          out_norm_g=out_norm_g)

    h = rmsnorm(x, norm_mix[0])
    mix_tab = jnp.concatenate([a_mix[0], jnp.zeros((1, d), F32)], axis=0)
    tiles = seq_width // 512
    gid_main = jnp.asarray([0] * tiles + [2] * tiles + [3] * tiles + [6] * (mem_width // 512), jnp.int32)
    proj = mix_matmul(h, mix_tab, gid_main, a_w_in[0].astype(BF16), seq, tn=512)
    rank = a_w1.shape[2]
    padc = lambda w: jnp.pad(w, ((0, 0), (0, LORA_PAD - rank)))
    padr = lambda w: jnp.pad(w, ((0, LORA_PAD - rank), (0, 0)))
    w_l1 = jnp.concatenate([padc(a_w1[0]), padc(a_a1[0]), a_g1[0]], axis=1).astype(BF16)
    gid_l1 = jnp.asarray([1, 4] + [5] * (a_g1.shape[2] // LORA_PAD), jnp.int32)
    t1 = mix_matmul(h, mix_tab, gid_l1, w_l1, seq, tn=LORA_PAD)
    lw, a_iclr, gate = lora_stage2(t1, padr(a_w2[0]).astype(BF16), padr(a_a2[0]).astype(BF16),
                                   a_g2[0].astype(BF16), a_w0[0], a_a0[0])
    prep = wkv_chunk_prepare(proj, lw, a_iclr, a_k_k[0], a_k_a[0], n_pairs)
    seq_out = wkv_chunk_scan(prep, proj, a_iclr, gate, a_k_a[0], a_r_k[0].reshape(-1),
                             a_ln_g[0], a_ln_b[0], n_pairs, seq)
    x = mixer_tail(x, seq_out, proj, (3 * seq_width) // mem_width, 0)

    kv = norm_matmul(x, kv_norm, w_kv_shared.astype(BF16))

    proj = norm_matmul(x, norm_mix[1], b_w_in[0].astype(BF16))
    seq_out = band_attention(proj, kv, _band_bias(b_rel_bias[0]), n_pairs, seq)
    x = mixer_tail(x, seq_out, proj, seq_width // mem_width, 1, out_norm_g=final_norm)
    return x.reshape(bsz, seq, d)
```

```python
import functools

import numpy as np
import jax
import jax.numpy as jnp
from jax import lax
from jax.experimental import pallas as pl
from jax.experimental.pallas import tpu as pltpu

F32 = jnp.float32
BF16 = jnp.bfloat16
HI = lax.Precision.HIGHEST

HEAD_DIM = 64
PAIR = 2 * HEAD_DIM
CHUNK = 64
SCAN_PAIRS = 2
LEFT_CHUNKS = 8
BAND = (LEFT_CHUNKS + 1) * CHUNK
REL_MAX = 128
MEM_HEADS = 4
MEM_HEAD_DIM = 128
PEER_KEYS = 128
PEER_HEADS = 8
PEER_TOPK = 16
PEER_PICKS = PEER_HEADS * PEER_TOPK
GN_EPS = 64e-5
RMS_EPS = 1e-6
LORA_PAD = 128
VMEM_LIMIT = 48 * 1024 * 1024


def _params(*sem):
    return pltpu.CompilerParams(dimension_semantics=sem, vmem_limit_bytes=VMEM_LIMIT)


def _dot(a, b):
    return jnp.dot(a, b, preferred_element_type=F32)


def _dot_nt(a, b, precision=None):
    return lax.dot_general(a, b, (((1,), (1,)), ((), ())), precision=precision,
                           preferred_element_type=F32)


def _dot_tn(a, b, precision=None):
    return lax.dot_general(a, b, (((0,), (0,)), ((), ())), precision=precision,
                           preferred_element_type=F32)


def _rmsnorm_body(x_ref, g_ref, o_ref):
    x = x_ref[...]
    ms = jnp.mean(x * x, axis=-1, keepdims=True)
    o_ref[...] = (x * lax.rsqrt(ms + RMS_EPS) * g_ref[...]).astype(o_ref.dtype)


def rmsnorm(x, g, tm=512):
    n, d = x.shape
    tm = min(tm, n)
    return pl.pallas_call(
        _rmsnorm_body,
        grid=(n // tm,),
        in_specs=[pl.BlockSpec((tm, d), lambda i: (i, 0)), pl.BlockSpec((1, d), lambda i: (0, 0))],
        out_specs=pl.BlockSpec((tm, d), lambda i: (i, 0)),
        out_shape=jax.ShapeDtypeStruct((n, d), F32),
        compiler_params=_params("parallel"),
        name="rmsnorm",
    )(x, g.reshape(1, d))


def _normmm_body(x_ref, g_ref, w_ref, o_ref, *rest, emit_hn):
    lhs_ref = rest[-1]

    @pl.when(pl.program_id(1) == 0)
    def _():
        x = x_ref[...]
        ms = jnp.mean(x * x, axis=-1, keepdims=True)
        hn = x * lax.rsqrt(ms + RMS_EPS) * g_ref[...]
        lhs_ref[...] = hn.astype(BF16)
        if emit_hn:
            rest[0][...] = hn

    o_ref[...] = _dot(lhs_ref[...], w_ref[...]).astype(o_ref.dtype)


def norm_matmul(x, g, w_bf16, emit_hn=False, tm=512, tn=1024, out_dtype=BF16):
    n, d = x.shape
    nc = w_bf16.shape[1]
    tm, tn = min(tm, n), min(tn, nc)
    out_shape = [jax.ShapeDtypeStruct((n, nc), out_dtype)]
    out_specs = [pl.BlockSpec((tm, tn), lambda i, j: (i, j))]
    if emit_hn:
        out_shape.append(jax.ShapeDtypeStruct((n, d), F32))
        out_specs.append(pl.BlockSpec((tm, d), lambda i, j: (i, 0)))
    res = pl.pallas_call(
        functools.partial(_normmm_body, emit_hn=emit_hn),
        grid=(n // tm, nc // tn),
        in_specs=[pl.BlockSpec((tm, d), lambda i, j: (i, 0)),
                  pl.BlockSpec((1, d), lambda i, j: (0, 0)),
                  pl.BlockSpec((d, tn), lambda i, j: (0, j))],
        out_specs=out_specs,
        out_shape=out_shape,
        scratch_shapes=[pltpu.VMEM((tm, d), BF16)],
        compiler_params=_params("parallel", "arbitrary"),
        name="norm_matmul",
    )(x, g.reshape(1, d), w_bf16)
    return res if emit_hn else res[0]


def _mixmm_body(gid_ref, h_ref, hp_ref, mix_ref, w_ref, o_ref, lhs_ref, *, tm, seq):
    i = pl.program_id(0)
    j = pl.program_id(1)
    new_group = jnp.logical_or(j == 0, gid_ref[j] != gid_ref[jnp.maximum(j - 1, 0)])

    @pl.when(new_group)
    def _():
        h = h_ref[...]
        prev = jnp.where((i * tm) % seq == 0, 0.0, hp_ref[7:8, :])
        row = lax.broadcasted_iota(jnp.int32, h.shape, 0)
        shifted = jnp.where(row == 0, prev, pltpu.roll(h, 1, 0))
        lhs_ref[...] = (h + (shifted - h) * mix_ref[0]).astype(BF16)

    o_ref[...] = _dot(lhs_ref[...], w_ref[...])


def mix_matmul(h, mix_tab, gid, w_bf16, seq, tn, tm=1024):
    n, d = h.shape
    nc = w_bf16.shape[1]
    tm = min(tm, seq)
    sub = tm // 8
    grid_spec = pltpu.PrefetchScalarGridSpec(
        num_scalar_prefetch=1,
        grid=(n // tm, nc // tn),
        in_specs=[pl.BlockSpec((tm, d), lambda i, j, g: (i, 0)),
                  pl.BlockSpec((8, d), lambda i, j, g: (jnp.maximum(i * sub - 1, 0), 0)),
                  pl.BlockSpec((1, 1, d), lambda i, j, g: (g[j], 0, 0)),
                  pl.BlockSpec((d, tn), lambda i, j, g: (0, j))],
        out_specs=pl.BlockSpec((tm, tn), lambda i, j, g: (i, j)),
        scratch_shapes=[pltpu.VMEM((tm, d), BF16)],
    )
    return pl.pallas_call(
        functools.partial(_mixmm_body, tm=tm, seq=seq),
        grid_spec=grid_spec,
        out_shape=jax.ShapeDtypeStruct((n, nc), F32),
        compiler_params=_params("parallel", "arbitrary"),
        name="mix_matmul",
    )(gid, h, h, mix_tab.reshape(mix_tab.shape[0], 1, d), w_bf16)


def _sigmoid(x):
    return 1.0 / (1.0 + jnp.exp(-x))


def _lora2_body(t_ref, w2_ref, a2_ref, g2_ref, w0_ref, a0_ref, lw_ref, a_ref, g_ref):
    t = t_ref[...]
    tw = jnp.tanh(t[:, :LORA_PAD]).astype(BF16)
    ta = t[:, LORA_PAD:2 * LORA_PAD].astype(BF16)
    tg = _sigmoid(t[:, 2 * LORA_PAD:]).astype(BF16)
    u = w0_ref[...] + _dot(tw, w2_ref[...])
    softplus_neg_u = jnp.maximum(-u, 0.0) + jnp.log(1.0 + jnp.exp(-jnp.abs(u)))
    lw_ref[...] = -jnp.exp(-softplus_neg_u - 0.5)
    a_ref[...] = _sigmoid(a0_ref[...] + _dot(ta, a2_ref[...]))
    g_ref[...] = _dot(tg, g2_ref[...])


def lora_stage2(t1, w2p, a2p, g2, w0, a0, tm=256):
    n = t1.shape[0]
    width = w2p.shape[1]
    tm = min(tm, n)
    full = lambda a: pl.BlockSpec(a.shape, lambda i: (0, 0))
    row = pl.BlockSpec((tm, width), lambda i: (i, 0))
    w0 = w0.reshape(1, width)
    a0 = a0.reshape(1, width)
    return pl.pallas_call(
        _lora2_body,
        grid=(n // tm,),
        in_specs=[pl.BlockSpec((tm, t1.shape[1]), lambda i: (i, 0)), full(w2p), full(a2p), full(g2),
                  full(w0), full(a0)],
        out_specs=[row, row, row],
        out_shape=[jax.ShapeDtypeStruct((n, width), F32)] * 3,
        compiler_params=_params("parallel"),
        name="lora_stage2",
    )(t1, w2p, a2p, g2, w0, a0)


def _head_group_sum(x):
    width = x.shape[-1]
    r = lax.broadcasted_iota(jnp.int32, (width, width), 0) // HEAD_DIM
    c = lax.broadcasted_iota(jnp.int32, (width, width), 1) // HEAD_DIM
    ones = jnp.where(r == c, 1.0, 0.0).astype(F32)
    return jnp.dot(x, ones, precision=HI, preferred_element_type=F32)


def _wkv_prepare_body(r_ref, k_ref, v_ref, lw_ref, a_ref, kk_ref, ka_ref,
                      g_ref, s0c_ref, rp_ref, y0_ref, *, tb):
    r = r_ref[...]
    k = k_ref[...]
    v = v_ref[...]
    lw = lw_ref[...]
    a = a_ref[...]
    kk = k * kk_ref[...]
    norm = jnp.sqrt(_head_group_sum(kk * kk))
    kk = kk / jnp.maximum(norm, 1e-12)
    kmod = k * (1.0 + (a - 1.0) * ka_ref[...])
    avec = -kk
    bvec = kk * a

    tr = lax.broadcasted_iota(jnp.int32, (CHUNK, CHUNK), 0)
    tc = lax.broadcasted_iota(jnp.int32, (CHUNK, CHUNK), 1)
    tri = jnp.where(tr >= tc, 1.0, 0.0).astype(F32)

    lane = lax.broadcasted_iota(jnp.int32, (CHUNK, PAIR), 1)
    head0 = lane < HEAD_DIM
    row = lax.broadcasted_iota(jnp.int32, (PAIR, PAIR), 0)
    col = lax.broadcasted_iota(jnp.int32, (PAIR, PAIR), 1)
    strict = row > col
    lower = row >= col
    eye = jnp.where(row == col, 1.0, 0.0).astype(F32)

    def stack(x):
        return jnp.concatenate([jnp.where(head0, x, 0.0), jnp.where(head0, 0.0, x)], axis=0)

    chunks = range(tb // CHUNK)
    pre = []
    for c in chunks:
        sl = slice(c * CHUNK, (c + 1) * CHUNK)
        cm = jnp.dot(tri, lw[sl], precision=HI, preferred_element_type=F32)
        cend = cm[CHUNK - 1:CHUNK]
        e_in = jnp.exp(cm)
        e_out = jnp.exp(-cm)
        e_tail = jnp.exp(cend - cm)
        a_s = stack(avec[sl] * jnp.exp(cm - lw[sl]))
        r_s = stack(r[sl] * e_in)
        b_s = stack(bvec[sl] * e_out)
        k_s = stack(kmod[sl] * e_out)
        bt_s = stack(bvec[sl] * e_tail)
        kt_s = stack(kmod[sl] * e_tail)
        v_s = stack(v[sl])
        p = _dot_nt(jnp.concatenate([a_s, r_s], axis=0).astype(BF16),
                    jnp.concatenate([b_s, k_s], axis=0).astype(BF16))
        pre.append(dict(cend=cend, a_s=a_s, r_s=r_s, bt_s=bt_s, kt_s=kt_s, v_s=v_s,
                        l_ab=jnp.where(strict, p[:PAIR, :PAIR], 0.0),
                        l_ak=jnp.where(strict, p[:PAIR, PAIR:], 0.0),
                        a_rb=jnp.where(lower, p[PAIR:, :PAIR], 0.0),
                        a_rk=jnp.where(lower, p[PAIR:, PAIR:], 0.0)))

    ms = [q["l_ab"] for q in pre]
    ts = [eye + m for m in ms]
    for _ in range(5):
        ms = [_dot(m.astype(BF16), m.astype(BF16)) for m in ms]
        ts = [t + _dot(t.astype(BF16), m.astype(BF16)) for t, m in zip(ts, ms)]

    lvs = [_dot(q["l_ak"].astype(BF16), q["v_s"].astype(BF16)) for q in pre]
    aus = [_dot(t.astype(BF16), jnp.concatenate([q["a_s"], lv], axis=1).astype(BF16))
           for t, q, lv in zip(ts, pre, lvs)]
    zero = jnp.zeros((PAIR, PAIR), F32)
    xs = [_dot(jnp.concatenate([q["a_rb"], q["a_rk"]], axis=1).astype(BF16),
               jnp.concatenate([au, jnp.concatenate([zero, q["v_s"]], axis=1)], axis=0).astype(BF16))
          for q, au in zip(pre, aus)]
    for c, q, au, x in zip(chunks, pre, aus, xs):
        rp_ref[0, c] = q["r_s"] + x[:, :PAIR]
        y0_ref[0, c] = x[:, PAIR:]
        ap = au[:, :PAIR]
        u0 = au[:, PAIR:]
        g_ref[0, c] = eye * jnp.exp(q["cend"]) + _dot_tn(ap.astype(BF16), q["bt_s"].astype(BF16))
        s0c_ref[0, c] = _dot_tn(jnp.concatenate([u0, q["v_s"]], axis=0).astype(BF16),
                                jnp.concatenate([q["bt_s"], q["kt_s"]], axis=0).astype(BF16))


def wkv_chunk_prepare(proj, lw, a, k_k, k_a, n_pairs, tb=512):
    n = proj.shape[0]
    tb = min(tb, n)
    cpb = tb // CHUNK
    col = lambda off: pl.BlockSpec((tb, PAIR), lambda i, p: (i, off + p))
    par = pl.BlockSpec((1, PAIR), lambda i, p: (0, p))
    blk = pl.BlockSpec((1, cpb, PAIR, PAIR), lambda i, p: (p, i, 0, 0))
    shp = jax.ShapeDtypeStruct((n_pairs, n // CHUNK, PAIR, PAIR), F32)
    return pl.pallas_call(
        functools.partial(_wkv_prepare_body, tb=tb),
        grid=(n // tb, n_pairs),
        in_specs=[col(0), col(n_pairs), col(2 * n_pairs), col(0), col(0), par, par],
        out_specs=[blk] * 4,
        out_shape=[shp] * 4,
        compiler_params=_params("parallel", "parallel"),
        name="wkv_chunk_prepare",
    )(proj, proj, proj, lw, a, k_k.reshape(1, -1), k_a.reshape(1, -1))


def _wkv_scan_body(g_ref, s0c_ref, rp_ref, y0_ref, r_ref, k_ref, v_ref, a_ref, gate_ref,
                   ka_ref, rk_ref, lng_ref, lnb_ref, o_ref, s_ref, y_ref, *, cpb):
    @pl.when(pl.program_id(2) == 0)
    def _():
        s_ref[...] = jnp.zeros_like(s_ref)

    pairs = range(SCAN_PAIRS)
    states = [[s_ref[w].astype(BF16)] for w in pairs]
    for c in range(cpb):
        for w in pairs:
            s = _dot(states[w][c], g_ref[w, c].astype(BF16)) + s0c_ref[w, c]
            if c + 1 < cpb:
                states[w].append(s.astype(BF16))
            else:
                s_ref[w] = s
    for c in range(cpb):
        for w in pairs:
            y_st = _dot_nt(rp_ref[w, c].astype(BF16), states[w][c]) + y0_ref[w, c]
            y_ref[c * CHUNK:(c + 1) * CHUNK, w * PAIR:(w + 1) * PAIR] = y_st[:CHUNK] + y_st[CHUNK:]

    y = y_ref[...]
    inv = 1.0 / HEAD_DIM
    mu = _head_group_sum(y) * inv
    yc = y - mu
    var = _head_group_sum(yc * yc) * inv
    yn = yc * lax.rsqrt(var + GN_EPS) * lng_ref[...] + lnb_ref[...]
    r = r_ref[...]
    kmod = k_ref[...] * (1.0 + (a_ref[...] - 1.0) * ka_ref[...])
    bonus = _head_group_sum(r * kmod * rk_ref[...]) * v_ref[...]
    o_ref[...] = ((yn + bonus) * gate_ref[...]).astype(o_ref.dtype)


def wkv_chunk_scan(prep, proj, a, gate, k_a, r_k, ln_g, ln_b, n_pairs, seq, tb=512):
    g_all, s0c_all, rp_all, y0_all = prep
    n = proj.shape[0]
    tb = min(tb, seq)
    cpb = tb // CHUNK
    nblk = seq // tb
    assert n_pairs % SCAN_PAIRS == 0
    width = SCAN_PAIRS * PAIR
    groups = n_pairs // SCAN_PAIRS
    blk = pl.BlockSpec((SCAN_PAIRS, cpb, PAIR, PAIR), lambda b, p, t: (p, b * nblk + t, 0, 0))
    col = lambda off: pl.BlockSpec((tb, width), lambda b, p, t: (b * nblk + t, off + p))
    par = pl.BlockSpec((1, width), lambda b, p, t: (0, p))
    vec = lambda z: z.reshape(1, -1)
    return pl.pallas_call(
        functools.partial(_wkv_scan_body, cpb=cpb),
        grid=(n // seq, groups, nblk),
        in_specs=[blk] * 4 + [col(0), col(groups), col(2 * groups), col(0), col(0)] + [par] * 4,
        out_specs=col(0),
        out_shape=jax.ShapeDtypeStruct((n, n_pairs * PAIR), BF16),
        scratch_shapes=[pltpu.VMEM((SCAN_PAIRS, PAIR, PAIR), F32), pltpu.VMEM((tb, width), F32)],
        compiler_params=_params("parallel", "parallel", "arbitrary"),
        name="wkv_chunk_scan",
    )(g_all, s0c_all, rp_all, y0_all, proj, proj, proj, a, gate,
      vec(k_a), vec(r_k), vec(ln_g), vec(ln_b))


def _memattn_body(q_ref, m_ref, o_ref):
    width = MEM_HEADS * MEM_HEAD_DIM
    scale = MEM_HEAD_DIM ** -0.5
    for h in range(MEM_HEADS):
        sl = slice(h * MEM_HEAD_DIM, (h + 1) * MEM_HEAD_DIM)
        q = q_ref[:, sl].astype(BF16)
        mk = m_ref[:, sl].astype(BF16)
        mv = m_ref[:, width + h * MEM_HEAD_DIM:width + (h + 1) * MEM_HEAD_DIM].astype(BF16)
        s = _dot_nt(q, mk) * scale
        e = jnp.exp(s - jnp.max(s, axis=-1, keepdims=True))
        p = e / jnp.sum(e, axis=-1, keepdims=True)
        o_ref[:, sl] = _dot(p.astype(BF16), mv).astype(o_ref.dtype)


def memory_attention(proj, q_col_block, mkv, seq, n_mem, tm=512):
    n = proj.shape[0]
    width = MEM_HEADS * MEM_HEAD_DIM
    tm = min(tm, seq)
    nblk = seq // tm
    return pl.pallas_call(
        _memattn_body,
        grid=(n // seq, nblk),
        in_specs=[pl.BlockSpec((tm, width), lambda b, t: (b * nblk + t, q_col_block)),
                  pl.BlockSpec((n_mem, 2 * width), lambda b, t: (b, 0))],
        out_specs=pl.BlockSpec((tm, width), lambda b, t: (b * nblk + t, 0)),
        out_shape=jax.ShapeDtypeStruct((n, width), BF16),
        compiler_params=_params("parallel", "parallel"),
        name="memory_attention",
    )(proj, mkv)


def _outproj_body(s_ref, m_ref, w1_ref, w2_ref, x_ref, o_ref):
    o_ref[...] = x_ref[...] + _dot(s_ref[...], w1_ref[...]) + _dot(m_ref[...], w2_ref[...])


def out_projection(seq_out, mem_out, w_seq, w_mem, x, tm=512, tn=512):
    n, d = x.shape
    tm, tn = min(tm, n), min(tn, d)
    ws, wm = seq_out.shape[1], mem_out.shape[1]
    return pl.pallas_call(
        _outproj_body,
        grid=(n // tm, d // tn),
        in_specs=[pl.BlockSpec((tm, ws), lambda i, j: (i, 0)),
                  pl.BlockSpec((tm, wm), lambda i, j: (i, 0)),
                  pl.BlockSpec((ws, tn), lambda i, j: (0, j)),
                  pl.BlockSpec((wm, tn), lambda i, j: (0, j)),
                  pl.BlockSpec((tm, tn), lambda i, j: (i, j))],
        out_specs=pl.BlockSpec((tm, tn), lambda i, j: (i, j)),
        out_shape=jax.ShapeDtypeStruct((n, d), F32),
        compiler_params=_params("parallel", "parallel"),
        name="out_projection",
    )(seq_out, mem_out, w_seq, w_mem, x)


def _bandattn_body(q_ref, kp_ref, kc_ref, vp_ref, vc_ref, bias_ref, o_ref, *, tq):
    qi = pl.program_id(2)
    scale = HEAD_DIM ** -0.5
    k_all = jnp.concatenate([kp_ref[...], kc_ref[...]], axis=0).astype(BF16)
    v_all = jnp.concatenate([vp_ref[...], vc_ref[...]], axis=0).astype(BF16)
    lane = lax.broadcasted_iota(jnp.int32, (CHUNK, PAIR), 1)
    head0 = lane < HEAD_DIM
    kcol = lax.broadcasted_iota(jnp.int32, (2 * CHUNK, BAND), 1)
    pad = LEFT_CHUNKS * CHUNK
    bias = bias_ref[...]
    chunks = range(tq // CHUNK)
    starts = [tq - pad + j * CHUNK for j in chunks]
    scores = []
    for j, start in zip(chunks, starts):
        q = q_ref[j * CHUNK:(j + 1) * CHUNK, :]
        q2 = jnp.concatenate([jnp.where(head0, q, 0.0), jnp.where(head0, 0.0, q)], axis=0).astype(BF16)
        scores.append(_dot_nt(q2, k_all[start:start + BAND]))
    exps = []
    for start, s in zip(starts, scores):
        valid = jnp.logical_or(qi > 0, kcol + start >= tq)
        s = jnp.where(valid, s * scale + bias, -jnp.inf)
        exps.append(jnp.exp(s - jnp.max(s, axis=-1, keepdims=True)))
    outs = [_dot(e.astype(BF16), v_all[start:start + BAND]) for start, e in zip(starts, exps)]
    for j, e, o in zip(chunks, exps, outs):
        o = o / jnp.sum(e, axis=-1, keepdims=True)
        o_ref[j * CHUNK:(j + 1) * CHUNK, :] = jnp.where(head0, o[:CHUNK], o[CHUNK:]).astype(o_ref.dtype)


def band_attention(proj, kv, bias, n_pairs, seq, tq=512):
    n = proj.shape[0]
    tq = min(tq, seq)
    assert tq >= LEFT_CHUNKS * CHUNK
    nblk = seq // tq
    cur = lambda off: pl.BlockSpec((tq, PAIR), lambda b, p, t: (b * nblk + t, off + p))
    prev = lambda off: pl.BlockSpec((tq, PAIR), lambda b, p, t: (b * nblk + jnp.maximum(t - 1, 0), off + p))
    return pl.pallas_call(
        functools.partial(_bandattn_body, tq=tq),
        grid=(n // seq, n_pairs, nblk),
        in_specs=[cur(0), prev(0), cur(0), prev(n_pairs), cur(n_pairs),
                  pl.BlockSpec((2 * CHUNK, BAND), lambda b, p, t: (p, 0))],
        out_specs=cur(0),
        out_shape=jax.ShapeDtypeStruct((n, n_pairs * PAIR), BF16),
        compiler_params=_params("parallel", "parallel", "parallel"),
        name="band_attention",
    )(proj, kv, kv, kv, kv, bias.reshape(-1, BAND))


def _top16(scores, payloads):
    nl = scores[0].shape[1]
    rids = [lax.broadcasted_iota(jnp.int32, s.shape, 0).astype(F32) for s in scores]
    slot = lax.broadcasted_iota(jnp.int32, (PEER_TOPK, nl), 0)

    def body(i, carry):
        sel = slot == i
        out = []
        for (s, vals, picks), rid, payload in zip(carry, rids, payloads):
            m = jnp.max(s, axis=0, keepdims=True)
            am = jnp.min(jnp.where(s == m, rid, float(s.shape[0])), axis=0, keepdims=True)
            hit = rid == am
            if payload is None:
                pick = am
            else:
                pick = jnp.sum(jnp.where(hit, payload, 0.0), axis=0, keepdims=True)
            out.append((jnp.where(hit, -jnp.inf, s), jnp.where(sel, m, vals), jnp.where(sel, pick, picks)))
        return tuple(out)

    zero = jnp.zeros((PEER_TOPK, nl), F32)
    res = lax.fori_loop(0, PEER_TOPK, body, tuple((s, zero, zero) for s in scores))
    return [(vals, picks) for _, vals, picks in res]


def _pair_candidates(a, b, combine):
    half = PEER_TOPK // 2
    rows = [combine(a[0:1], b)]
    rows += [combine(a[i:i + 1], b[:half]) for i in range(1, half)]
    rows.append(combine(a[half:], b[0:1]))
    return jnp.concatenate(rows, axis=0)


TOPK_HEADS = 2


def _peer_topk_body(q_ref, keys_ref, eidx_ref, gate_ref):
    q = q_ref[...].astype(BF16)
    scores = []
    for h in range(TOPK_HEADS):
        for half in range(2):
            col = (2 * h + half) * PEER_KEYS
            scores.append(_dot_nt(keys_ref[h, half].astype(BF16), q[:, col:col + PEER_KEYS]))
    cands, cidxs = [], []
    for h in range(TOPK_HEADS):
        (a, i1), (b, i2) = _top16(scores[2 * h:2 * h + 2], [None, None])
        cands.append(_pair_candidates(a, b, lambda x, y: x + y))
        cidxs.append(_pair_candidates(i1, i2, lambda x, y: x * PEER_KEYS + y))
    for h, (top, eidx) in enumerate(_top16(cands, cidxs)):
        rows = slice(h * PEER_TOPK, (h + 1) * PEER_TOPK)
        e = jnp.exp(top - top[0:1])
        gate_ref[rows, :] = e / jnp.sum(e, axis=0, keepdims=True)
        eidx_ref[rows, :] = eidx.astype(jnp.int32)


def peer_topk(q, keys, tl=128):
    n = q.shape[0]
    tl = min(tl, n)
    blk = pl.BlockSpec((TOPK_HEADS * PEER_TOPK, tl), lambda i, h: (h, i))
    return pl.pallas_call(
        _peer_topk_body,
        grid=(n // tl, PEER_HEADS // TOPK_HEADS),
        in_specs=[pl.BlockSpec((tl, TOPK_HEADS * 2 * PEER_KEYS), lambda i, h: (i, h)),
                  pl.BlockSpec((TOPK_HEADS, 2, PEER_KEYS, PEER_KEYS), lambda i, h: (h, 0, 0, 0))],
        out_specs=[blk, blk],
        out_shape=[jax.ShapeDtypeStruct((PEER_PICKS, n), jnp.int32),
                   jax.ShapeDtypeStruct((PEER_PICKS, n), F32)],
        compiler_params=_params("parallel", "parallel"),
        name="peer_topk",
    )(q, keys)


PEER_SLOTS = 8
LANES = 128
PITCH_PAD = 4


def _pack_body(u_ref, v_ref, o_ref):
    hi = lax.bitcast_convert_type(u_ref[0].astype(BF16).astype(F32), jnp.uint32)
    lo = lax.bitcast_convert_type(v_ref[0].astype(BF16).astype(F32), jnp.uint32)
    words = hi | (lo >> 16)
    for c in range(o_ref.shape[1]):
        o_ref[:, c, :] = words[:, c * LANES:(c + 1) * LANES]


def _pack_expert_table(u, v, layer, te=256):
    _, n_exp, d = u.shape
    chunks = d // LANES
    te = min(te, n_exp)
    packed = pl.pallas_call(
        _pack_body,
        grid=(n_exp // te,),
        in_specs=[pl.BlockSpec((1, te, d), lambda i: (layer, i, 0))] * 2,
        out_specs=pl.BlockSpec((te, chunks, LANES), lambda i: (i, 0, 0)),
        out_shape=jax.ShapeDtypeStruct((n_exp, chunks, LANES), jnp.uint32),
        compiler_params=_params("parallel"),
        name="pack_expert_table",
    )(u, v)
    return packed.reshape(n_exp * chunks, LANES)


def _peer_ffn_body(eidx_ref, gate_ref, hn_ref, x_ref, tab_ref, *rest, tb, d, out_norm):
    norm_ref, o_ref, scratch = (rest[0], rest[1], rest[2:]) if out_norm else (None, rest[0], rest[1:])
    rows_refs = scratch[:PEER_SLOTS]
    sem_ref, w_ref, stage_ref, hn_rows_ref, gate_rows_ref = scratch[PEER_SLOTS:]
    chunks = d // LANES
    pitch = chunks + PITCH_PAD

    def issue(t, slot):
        for e in range(PEER_PICKS):
            src = tab_ref.at[pl.ds(pl.multiple_of(eidx_ref[t, e] * chunks, chunks), chunks), :]
            dst = rows_refs[slot].at[pl.ds(e * pitch, chunks), :]
            pltpu.make_async_copy(src, dst, sem_ref.at[slot]).start(priority=e % 2)

    def wait(slot):
        total = PEER_PICKS * chunks
        pltpu.make_async_copy(tab_ref.at[pl.ds(0, total), :],
                              rows_refs[slot].at[pl.ds(0, total), :], sem_ref.at[slot]).wait()

    pick_diag = (lax.broadcasted_iota(jnp.int32, (PEER_PICKS, PEER_PICKS), 0)
                 == lax.broadcasted_iota(jnp.int32, (PEER_PICKS, PEER_PICKS), 1))
    hi_mask = jnp.uint32(0xFFFF0000)

    def words(slot, c):
        return rows_refs[slot][pl.ds(c, PEER_PICKS, stride=pitch), :]

    def pick_weights(slot, hn_rows, gate_rows, row):
        acc = jnp.zeros((PEER_PICKS, LANES), F32)
        for c in range(chunks):
            u = lax.bitcast_convert_type(words(slot, c) & hi_mask, F32)
            acc = acc + u * hn_rows[row:row + 1, c * LANES:(c + 1) * LANES]
        act = jnp.sum(acc, axis=1, keepdims=True)
        act = 0.5 * act * (1.0 + lax.erf(act * (2.0 ** -0.5)))
        gate = jnp.sum(jnp.where(pick_diag, gate_rows[row:row + 1, :], 0.0), axis=1, keepdims=True)
        w_ref[slot] = jnp.broadcast_to(gate * act, (PEER_PICKS, LANES))
        return acc

    def combine(slot, after):
        last = lax.bitcast_convert_type(after[PEER_PICKS - 8:], jnp.int32) == -1
        never = jnp.concatenate([last] * (PEER_PICKS // 8), axis=0)
        w = jnp.where(never, 0.0, w_ref[slot])
        for c in range(chunks):
            v = lax.bitcast_convert_type(words(slot, c) << 16, F32)
            stage_ref[slot:slot + 1, c * LANES:(c + 1) * LANES] = jnp.sum(v * w, axis=0, keepdims=True)

    def store_group(g):
        rows = pl.ds(pl.multiple_of(g * PEER_SLOTS, PEER_SLOTS), PEER_SLOTS)
        y = x_ref[rows, :] + stage_ref[...]
        if norm_ref is not None:
            y = y * lax.rsqrt(jnp.mean(y * y, axis=-1, keepdims=True) + RMS_EPS) * norm_ref[...]
        o_ref[rows, :] = y

    ahead = PEER_SLOTS - 1
    groups = tb // PEER_SLOTS
    for t in range(ahead):
        issue(t, t)
    wait(0)
    first_pass = pick_weights(0, hn_ref, gate_ref, 0)

    def group(g, carry):
        rows = pl.ds(pl.multiple_of(g * PEER_SLOTS, PEER_SLOTS), 2 * PEER_SLOTS)
        hn_rows_ref[...] = hn_ref[rows, :]
        gate_rows_ref[...] = gate_ref[rows, :]
        for s in range(PEER_SLOTS):
            t = g * PEER_SLOTS + s
            wait((s + 1) % PEER_SLOTS)
            combine(s, pick_weights((s + 1) % PEER_SLOTS, hn_rows_ref, gate_rows_ref, s + 1))
            issue(t + ahead, (s + ahead) % PEER_SLOTS)
        store_group(g)
        return carry

    lax.fori_loop(0, groups - 1, group, 0)

    for s in range(PEER_SLOTS):
        t = (groups - 1) * PEER_SLOTS + s
        if t + ahead < tb:
            issue(t + ahead, (s + ahead) % PEER_SLOTS)
        if t + 1 < tb:
            wait((s + 1) % PEER_SLOTS)
            first_pass = pick_weights((s + 1) % PEER_SLOTS, hn_ref, gate_ref, t + 1)
        combine(s, first_pass)
    store_group(groups - 1)


def peer_expert_ffn(eidx, gate, hn, x, table, tb=128, out_norm_g=None):
    n, d = x.shape
    tb = min(tb, n)
    assert tb % PEER_SLOTS == 0 and table.shape[1] == LANES
    assert table.shape[0] >= PEER_PICKS * (d // LANES)
    slot_rows = PEER_PICKS * (d // LANES + PITCH_PAD)
    out_norm = out_norm_g is not None
    in_specs = [pl.BlockSpec((tb, PEER_PICKS), lambda i: (i, 0), memory_space=pltpu.SMEM),
                pl.BlockSpec((tb, PEER_PICKS), lambda i: (i, 0)),
                pl.BlockSpec((tb, d), lambda i: (i, 0)),
                pl.BlockSpec((tb, d), lambda i: (i, 0)),
                pl.BlockSpec(memory_space=pl.ANY)]
    operands = [eidx, gate, hn, x, table]
    if out_norm:
        in_specs.append(pl.BlockSpec((1, d), lambda i: (0, 0)))
        operands.append(out_norm_g.reshape(1, d))
    return pl.pallas_call(
        functools.partial(_peer_ffn_body, tb=tb, d=d, out_norm=out_norm),
        grid=(n // tb,),
        in_specs=in_specs,
        out_specs=pl.BlockSpec((tb, d), lambda i: (i, 0)),
        out_shape=jax.ShapeDtypeStruct((n, d), F32),
        scratch_shapes=[pltpu.VMEM((slot_rows, LANES), jnp.uint32)] * PEER_SLOTS
                       + [pltpu.SemaphoreType.DMA((PEER_SLOTS,)),
                          pltpu.VMEM((PEER_SLOTS, PEER_PICKS, LANES), F32),
                          pltpu.VMEM((PEER_SLOTS, d), F32),
                          pltpu.VMEM((2 * PEER_SLOTS, d), F32),
                          pltpu.VMEM((2 * PEER_SLOTS, PEER_PICKS), F32)],
        compiler_params=_params("arbitrary"),
        name="peer_expert_ffn",
    )(*operands)


def peer_layer(x, norm_g, wq, keys, u_all, v_all, layer, tb=256, out_norm_g=None):
    q, hn = norm_matmul(x, norm_g, wq.astype(BF16), emit_hn=True)
    eidx, gate = peer_topk(q, keys)
    return peer_expert_ffn(eidx.T, gate.T, hn, x, _pack_expert_table(u_all, v_all, layer), tb=tb,
                           out_norm_g=out_norm_g)


def _band_bias(rel_bias):
    pad = LEFT_CHUNKS * CHUNK
    dist = pad + np.arange(CHUNK)[:, None] - np.arange(BAND)[None, :]
    idx = np.clip(dist, -(CHUNK - 1), REL_MAX) + (CHUNK - 1)
    return rel_bias[:, idx].astype(F32)


def kernel(x, mem, norm_mix, norm_ffn, norm_mem, w_mem_kv, w_out, peer_wq, peer_keys, peer_u, peer_v, a_mix, a_w_in, a_w0, a_w1, a_w2, a_a0, a_a1, a_a2, a_g1, a_g2, a_k_k, a_k_a, a_r_k, a_ln_g, a_ln_b, kv_norm, w_kv_shared, b_w_in, b_rel_bias, final_norm):
    bsz, seq, d = x.shape
    n = bsz * seq
    n_mem = mem.shape[1]
    seq_width = a_w0.shape[1]
    n_pairs = seq_width // PAIR
    mem_width = MEM_HEADS * MEM_HEAD_DIM
    x = x.reshape(n, d)
    mem2 = mem.reshape(bsz * n_mem, d)

    def mixer_tail(x, seq_out, proj, q_col_block, layer, out_norm_g=None):
        mkv = norm_matmul(mem2, norm_mem[layer], w_mem_kv[layer].astype(BF16), tm=256)
        mem_out = memory_attention(proj, q_col_block, mkv, seq, n_mem)
        wo = w_out[layer].astype(BF16)
        x = out_projection(seq_out, mem_out, wo[:seq_width], wo[seq_width:], x)
        return peer_layer(x, norm_ffn[layer], peer_wq[layer], peer_keys[layer], peer_u, peer_v, layer,
                          out_norm_g=out_norm_g)

    h = rmsnorm(x, norm_mix[0])
    mix_tab = jnp.concatenate([a_mix[0], jnp.zeros((1, d), F32)], axis=0)
    tiles = seq_width // 512
    gid_main = jnp.asarray([0] * tiles + [2] * tiles + [3] * tiles + [6] * (mem_width // 512), jnp.int32)
    proj = mix_matmul(h, mix_tab, gid_main, a_w_in[0].astype(BF16), seq, tn=512)
    rank = a_w1.shape[2]
    padc = lambda w: jnp.pad(w, ((0, 0), (0, LORA_PAD - rank)))
    padr = lambda w: jnp.pad(w, ((0, LORA_PAD - rank), (0, 0)))
    w_l1 = jnp.concatenate([padc(a_w1[0]), padc(a_a1[0]), a_g1[0]], axis=1).astype(BF16)
    gid_l1 = jnp.asarray([1, 4] + [5] * (a_g1.shape[2] // LORA_PAD), jnp.int32)
    t1 = mix_matmul(h, mix_tab, gid_l1, w_l1, seq, tn=LORA_PAD)
    lw, a_iclr, gate = lora_stage2(t1, padr(a_w2[0]).astype(BF16), padr(a_a2[0]).astype(BF16),
                                   a_g2[0].astype(BF16), a_w0[0], a_a0[0])
    prep = wkv_chunk_prepare(proj, lw, a_iclr, a_k_k[0], a_k_a[0], n_pairs)
    seq_out = wkv_chunk_scan(prep, proj, a_iclr, gate, a_k_a[0], a_r_k[0].reshape(-1),
                             a_ln_g[0], a_ln_b[0], n_pairs, seq)
    x = mixer_tail(x, seq_out, proj, (3 * seq_width) // mem_width, 0)

    kv = norm_matmul(x, kv_norm, w_kv_shared.astype(BF16))

    proj = norm_matmul(x, norm_mix[1], b_w_in[0].astype(BF16))
    seq_out = band_attention(proj, kv, _band_bias(b_rel_bias[0]), n_pairs, seq)
    x = mixer_tail(x, seq_out, proj, seq_width // mem_width, 1, out_norm_g=final_norm)
    return x.reshape(bsz, seq, d)
```

```python
import functools

import numpy as np
import jax
import jax.numpy as jnp
from jax import lax
from jax.experimental import pallas as pl
from jax.experimental.pallas import tpu as pltpu

F32 = jnp.float32
BF16 = jnp.bfloat16
HI = lax.Precision.HIGHEST

HEAD_DIM = 64
PAIR = 2 * HEAD_DIM
CHUNK = 64
SCAN_PAIRS = 2
LEFT_CHUNKS = 8
BAND = (LEFT_CHUNKS + 1) * CHUNK
REL_MAX = 128
MEM_HEADS = 4
MEM_HEAD_DIM = 128
PEER_KEYS = 128
PEER_HEADS = 8
PEER_TOPK = 16
PEER_PICKS = PEER_HEADS * PEER_TOPK
GN_EPS = 64e-5
RMS_EPS = 1e-6
LORA_PAD = 128
VMEM_LIMIT = 48 * 1024 * 1024


def _params(*sem):
    return pltpu.CompilerParams(dimension_semantics=sem, vmem_limit_bytes=VMEM_LIMIT)


def _dot(a, b):
    return jnp.dot(a, b, preferred_element_type=F32)


def _dot_nt(a, b, precision=None):
    return lax.dot_general(a, b, (((1,), (1,)), ((), ())), precision=precision,
                           preferred_element_type=F32)


def _dot_tn(a, b, precision=None):
    return lax.dot_general(a, b, (((0,), (0,)), ((), ())), precision=precision,
                           preferred_element_type=F32)


def _rmsnorm_body(x_ref, g_ref, o_ref):
    x = x_ref[...]
    ms = jnp.mean(x * x, axis=-1, keepdims=True)
    o_ref[...] = (x * lax.rsqrt(ms + RMS_EPS) * g_ref[...]).astype(o_ref.dtype)


def rmsnorm(x, g, tm=512):
    n, d = x.shape
    tm = min(tm, n)
    return pl.pallas_call(
        _rmsnorm_body,
        grid=(n // tm,),
        in_specs=[pl.BlockSpec((tm, d), lambda i: (i, 0)), pl.BlockSpec((1, d), lambda i: (0, 0))],
        out_specs=pl.BlockSpec((tm, d), lambda i: (i, 0)),
        out_shape=jax.ShapeDtypeStruct((n, d), F32),
        compiler_params=_params("parallel"),
        name="rmsnorm",
    )(x, g.reshape(1, d))


def _normmm_body(x_ref, g_ref, w_ref, o_ref, *rest, emit_hn):
    lhs_ref = rest[-1]

    @pl.when(pl.program_id(1) == 0)
    def _():
        x = x_ref[...]
        ms = jnp.mean(x * x, axis=-1, keepdims=True)
        hn = x * lax.rsqrt(ms + RMS_EPS) * g_ref[...]
        lhs_ref[...] = hn.astype(BF16)
        if emit_hn:
            rest[0][...] = hn

    o_ref[...] = _dot(lhs_ref[...], w_ref[...]).astype(o_ref.dtype)


def norm_matmul(x, g, w_bf16, emit_hn=False, tm=512, tn=1024, out_dtype=BF16):
    n, d = x.shape
    nc = w_bf16.shape[1]
    tm, tn = min(tm, n), min(tn, nc)
    out_shape = [jax.ShapeDtypeStruct((n, nc), out_dtype)]
    out_specs = [pl.BlockSpec((tm, tn), lambda i, j: (i, j))]
    if emit_hn:
        out_shape.append(jax.ShapeDtypeStruct((n, d), F32))
        out_specs.append(pl.BlockSpec((tm, d), lambda i, j: (i, 0)))
    res = pl.pallas_call(
        functools.partial(_normmm_body, emit_hn=emit_hn),
        grid=(n // tm, nc // tn),
        in_specs=[pl.BlockSpec((tm, d), lambda i, j: (i, 0)),
                  pl.BlockSpec((1, d), lambda i, j: (0, 0)),
                  pl.BlockSpec((d, tn), lambda i, j: (0, j))],
        out_specs=out_specs,
        out_shape=out_shape,
        scratch_shapes=[pltpu.VMEM((tm, d), BF16)],
        compiler_params=_params("parallel", "arbitrary"),
        name="norm_matmul",
    )(x, g.reshape(1, d), w_bf16)
    return res if emit_hn else res[0]


def _mixmm_body(gid_ref, h_ref, hp_ref, mix_ref, w_ref, o_ref, lhs_ref, *, tm, seq):
    i = pl.program_id(0)
    j = pl.program_id(1)
    new_group = jnp.logical_or(j == 0, gid_ref[j] != gid_ref[jnp.maximum(j - 1, 0)])

    @pl.when(new_group)
    def _():
        h = h_ref[...]
        prev = jnp.where((i * tm) % seq == 0, 0.0, hp_ref[7:8, :])
        row = lax.broadcasted_iota(jnp.int32, h.shape, 0)
        shifted = jnp.where(row == 0, prev, pltpu.roll(h, 1, 0))
        lhs_ref[...] = (h + (shifted - h) * mix_ref[0]).astype(BF16)

    o_ref[...] = _dot(lhs_ref[...], w_ref[...])


def mix_matmul(h, mix_tab, gid, w_bf16, seq, tn, tm=1024):
    n, d = h.shape
    nc = w_bf16.shape[1]
    tm = min(tm, seq)
    sub = tm // 8
    grid_spec = pltpu.PrefetchScalarGridSpec(
        num_scalar_prefetch=1,
        grid=(n // tm, nc // tn),
        in_specs=[pl.BlockSpec((tm, d), lambda i, j, g: (i, 0)),
                  pl.BlockSpec((8, d), lambda i, j, g: (jnp.maximum(i * sub - 1, 0), 0)),
                  pl.BlockSpec((1, 1, d), lambda i, j, g: (g[j], 0, 0)),
                  pl.BlockSpec((d, tn), lambda i, j, g: (0, j))],
        out_specs=pl.BlockSpec((tm, tn), lambda i, j, g: (i, j)),
        scratch_shapes=[pltpu.VMEM((tm, d), BF16)],
    )
    return pl.pallas_call(
        functools.partial(_mixmm_body, tm=tm, seq=seq),
        grid_spec=grid_spec,
        out_shape=jax.ShapeDtypeStruct((n, nc), F32),
        compiler_params=_params("parallel", "arbitrary"),
        name="mix_matmul",
    )(gid, h, h, mix_tab.reshape(mix_tab.shape[0], 1, d), w_bf16)


def _sigmoid(x):
    return 1.0 / (1.0 + jnp.exp(-x))


def _lora2_body(t_ref, w2_ref, a2_ref, g2_ref, w0_ref, a0_ref, lw_ref, a_ref, g_ref):
    t = t_ref[...]
    tw = jnp.tanh(t[:, :LORA_PAD]).astype(BF16)
    ta = t[:, LORA_PAD:2 * LORA_PAD].astype(BF16)
    tg = _sigmoid(t[:, 2 * LORA_PAD:]).astype(BF16)
    u = w0_ref[...] + _dot(tw, w2_ref[...])
    softplus_neg_u = jnp.maximum(-u, 0.0) + jnp.log(1.0 + jnp.exp(-jnp.abs(u)))
    lw_ref[...] = -jnp.exp(-softplus_neg_u - 0.5)
    a_ref[...] = _sigmoid(a0_ref[...] + _dot(ta, a2_ref[...]))
    g_ref[...] = _dot(tg, g2_ref[...])


def lora_stage2(t1, w2p, a2p, g2, w0, a0, tm=256):
    n = t1.shape[0]
    width = w2p.shape[1]
    tm = min(tm, n)
    full = lambda a: pl.BlockSpec(a.shape, lambda i: (0, 0))
    row = pl.BlockSpec((tm, width), lambda i: (i, 0))
    w0 = w0.reshape(1, width)
    a0 = a0.reshape(1, width)
    return pl.pallas_call(
        _lora2_body,
        grid=(n // tm,),
        in_specs=[pl.BlockSpec((tm, t1.shape[1]), lambda i: (i, 0)), full(w2p), full(a2p), full(g2),
                  full(w0), full(a0)],
        out_specs=[row, row, row],
        out_shape=[jax.ShapeDtypeStruct((n, width), F32)] * 3,
        compiler_params=_params("parallel"),
        name="lora_stage2",
    )(t1, w2p, a2p, g2, w0, a0)


def _head_group_sum(x):
    width = x.shape[-1]
    r = lax.broadcasted_iota(jnp.int32, (width, width), 0) // HEAD_DIM
    c = lax.broadcasted_iota(jnp.int32, (width, width), 1) // HEAD_DIM
    ones = jnp.where(r == c, 1.0, 0.0).astype(F32)
    return jnp.dot(x, ones, precision=HI, preferred_element_type=F32)


def _wkv_prepare_body(r_ref, k_ref, v_ref, lw_ref, a_ref, kk_ref, ka_ref,
                      g_ref, s0c_ref, rp_ref, y0_ref, *, tb):
    r = r_ref[...]
    k = k_ref[...]
    v = v_ref[...]
    lw = lw_ref[...]
    a = a_ref[...]
    kk = k * kk_ref[...]
    norm = jnp.sqrt(_head_group_sum(kk * kk))
    kk = kk / jnp.maximum(norm, 1e-12)
    kmod = k * (1.0 + (a - 1.0) * ka_ref[...])
    avec = -kk
    bvec = kk * a

    tr = lax.broadcasted_iota(jnp.int32, (CHUNK, CHUNK), 0)
    tc = lax.broadcasted_iota(jnp.int32, (CHUNK, CHUNK), 1)
    tri = jnp.where(tr >= tc, 1.0, 0.0).astype(F32)

    lane = lax.broadcasted_iota(jnp.int32, (CHUNK, PAIR), 1)
    head0 = lane < HEAD_DIM
    row = lax.broadcasted_iota(jnp.int32, (PAIR, PAIR), 0)
    col = lax.broadcasted_iota(jnp.int32, (PAIR, PAIR), 1)
    strict = row > col
    lower = row >= col
    eye = jnp.where(row == col, 1.0, 0.0).astype(F32)

    def stack(x):
        return jnp.concatenate([jnp.where(head0, x, 0.0), jnp.where(head0, 0.0, x)], axis=0)

    chunks = range(tb // CHUNK)
    pre = []
    for c in chunks:
        sl = slice(c * CHUNK, (c + 1) * CHUNK)
        cm = jnp.dot(tri, lw[sl], precision=HI, preferred_element_type=F32)
        cend = cm[CHUNK - 1:CHUNK]
        e_in = jnp.exp(cm)
        e_out = jnp.exp(-cm)
        e_tail = jnp.exp(cend - cm)
        a_s = stack(avec[sl] * jnp.exp(cm - lw[sl]))
        r_s = stack(r[sl] * e_in)
        b_s = stack(bvec[sl] * e_out)
        k_s = stack(kmod[sl] * e_out)
        bt_s = stack(bvec[sl] * e_tail)
        kt_s = stack(kmod[sl] * e_tail)
        v_s = stack(v[sl])
        p = _dot_nt(jnp.concatenate([a_s, r_s], axis=0).astype(BF16),
                    jnp.concatenate([b_s, k_s], axis=0).astype(BF16))
        pre.append(dict(cend=cend, a_s=a_s, r_s=r_s, bt_s=bt_s, kt_s=kt_s, v_s=v_s,
                        l_ab=jnp.where(strict, p[:PAIR, :PAIR], 0.0),
                        l_ak=jnp.where(strict, p[:PAIR, PAIR:], 0.0),
                        a_rb=jnp.where(lower, p[PAIR:, :PAIR], 0.0),
                        a_rk=jnp.where(lower, p[PAIR:, PAIR:], 0.0)))

    ms = [q["l_ab"] for q in pre]
    ts = [eye + m for m in ms]
    for _ in range(5):
        ms = [_dot(m.astype(BF16), m.astype(BF16)) for m in ms]
        ts = [t + _dot(t.astype(BF16), m.astype(BF16)) for t, m in zip(ts, ms)]

    lvs = [_dot(q["l_ak"].astype(BF16), q["v_s"].astype(BF16)) for q in pre]
    aus = [_dot(t.astype(BF16), jnp.concatenate([q["a_s"], lv], axis=1).astype(BF16))
           for t, q, lv in zip(ts, pre, lvs)]
    zero = jnp.zeros((PAIR, PAIR), F32)
    xs = [_dot(jnp.concatenate([q["a_rb"], q["a_rk"]], axis=1).astype(BF16),
               jnp.concatenate([au, jnp.concatenate([zero, q["v_s"]], axis=1)], axis=0).astype(BF16))
          for q, au in zip(pre, aus)]
    for c, q, au, x in zip(chunks, pre, aus, xs):
        rp_ref[0, c] = (q["r_s"] + x[:, :PAIR]).astype(rp_ref.dtype)
        y0_ref[0, c] = x[:, PAIR:]
        ap = au[:, :PAIR]
        u0 = au[:, PAIR:]
        g_ref[0, c] = (eye * jnp.exp(q["cend"])
                       + _dot_tn(ap.astype(BF16), q["bt_s"].astype(BF16))).astype(g_ref.dtype)
        s0c_ref[0, c] = _dot_tn(jnp.concatenate([u0, q["v_s"]], axis=0).astype(BF16),
                                jnp.concatenate([q["bt_s"], q["kt_s"]], axis=0).astype(BF16))


def wkv_chunk_prepare(proj, lw, a, k_k, k_a, n_pairs, tb=512):
    n = proj.shape[0]
    tb = min(tb, n)
    cpb = tb // CHUNK
    col = lambda off: pl.BlockSpec((tb, PAIR), lambda i, p: (i, off + p))
    par = pl.BlockSpec((1, PAIR), lambda i, p: (0, p))
    blk = pl.BlockSpec((1, cpb, PAIR, PAIR), lambda i, p: (p, i, 0, 0))
    shp = lambda dt: jax.ShapeDtypeStruct((n_pairs, n // CHUNK, PAIR, PAIR), dt)
    return pl.pallas_call(
        functools.partial(_wkv_prepare_body, tb=tb),
        grid=(n // tb, n_pairs),
        in_specs=[col(0), col(n_pairs), col(2 * n_pairs), col(0), col(0), par, par],
        out_specs=[blk] * 4,
        out_shape=[shp(BF16), shp(F32), shp(BF16), shp(F32)],
        compiler_params=_params("parallel", "parallel"),
        name="wkv_chunk_prepare",
    )(proj, proj, proj, lw, a, k_k.reshape(1, -1), k_a.reshape(1, -1))


def _wkv_scan_body(g_ref, s0c_ref, rp_ref, y0_ref, r_ref, k_ref, v_ref, a_ref, gate_ref,
                   ka_ref, rk_ref, lng_ref, lnb_ref, o_ref, s_ref, y_ref, *, cpb):
    @pl.when(pl.program_id(2) == 0)
    def _():
        s_ref[...] = jnp.zeros_like(s_ref)

    pairs = range(SCAN_PAIRS)
    states = [[s_ref[w].astype(BF16)] for w in pairs]
    for c in range(cpb):
        for w in pairs:
            s = _dot(states[w][c], g_ref[w, c].astype(BF16)) + s0c_ref[w, c]
            if c + 1 < cpb:
                states[w].append(s.astype(BF16))
            else:
                s_ref[w] = s
    for c in range(cpb):
        for w in pairs:
            y_st = _dot_nt(rp_ref[w, c].astype(BF16), states[w][c]) + y0_ref[w, c]
            y_ref[c * CHUNK:(c + 1) * CHUNK, w * PAIR:(w + 1) * PAIR] = y_st[:CHUNK] + y_st[CHUNK:]

    y = y_ref[...]
    inv = 1.0 / HEAD_DIM
    mu = _head_group_sum(y) * inv
    yc = y - mu
    var = _head_group_sum(yc * yc) * inv
    yn = yc * lax.rsqrt(var + GN_EPS) * lng_ref[...] + lnb_ref[...]
    r = r_ref[...]
    kmod = k_ref[...] * (1.0 + (a_ref[...] - 1.0) * ka_ref[...])
    bonus = _head_group_sum(r * kmod * rk_ref[...]) * v_ref[...]
    o_ref[...] = ((yn + bonus) * gate_ref[...]).astype(o_ref.dtype)


def wkv_chunk_scan(prep, proj, a, gate, k_a, r_k, ln_g, ln_b, n_pairs, seq, tb=512):
    g_all, s0c_all, rp_all, y0_all = prep
    n = proj.shape[0]
    tb = min(tb, seq)
    cpb = tb // CHUNK
    nblk = seq // tb
    assert n_pairs % SCAN_PAIRS == 0
    width = SCAN_PAIRS * PAIR
    groups = n_pairs // SCAN_PAIRS
    blk = pl.BlockSpec((SCAN_PAIRS, cpb, PAIR, PAIR), lambda b, p, t: (p, b * nblk + t, 0, 0))
    col = lambda off: pl.BlockSpec((tb, width), lambda b, p, t: (b * nblk + t, off + p))
    par = pl.BlockSpec((1, width), lambda b, p, t: (0, p))
    vec = lambda z: z.reshape(1, -1)
    return pl.pallas_call(
        functools.partial(_wkv_scan_body, cpb=cpb),
        grid=(n // seq, groups, nblk),
        in_specs=[blk] * 4 + [col(0), col(groups), col(2 * groups), col(0), col(0)] + [par] * 4,
        out_specs=col(0),
        out_shape=jax.ShapeDtypeStruct((n, n_pairs * PAIR), BF16),
        scratch_shapes=[pltpu.VMEM((SCAN_PAIRS, PAIR, PAIR), F32), pltpu.VMEM((tb, width), F32)],
        compiler_params=_params("parallel", "parallel", "arbitrary"),
        name="wkv_chunk_scan",
    )(g_all, s0c_all, rp_all, y0_all, proj, proj, proj, a, gate,
      vec(k_a), vec(r_k), vec(ln_g), vec(ln_b))


def _memattn_body(q_ref, m_ref, o_ref):
    width = MEM_HEADS * MEM_HEAD_DIM
    scale = MEM_HEAD_DIM ** -0.5
    for h in range(MEM_HEADS):
        sl = slice(h * MEM_HEAD_DIM, (h + 1) * MEM_HEAD_DIM)
        q = q_ref[:, sl].astype(BF16)
        mk = m_ref[:, sl].astype(BF16)
        mv = m_ref[:, width + h * MEM_HEAD_DIM:width + (h + 1) * MEM_HEAD_DIM].astype(BF16)
        s = _dot_nt(q, mk) * scale
        e = jnp.exp(s - jnp.max(s, axis=-1, keepdims=True))
        p = e / jnp.sum(e, axis=-1, keepdims=True)
        o_ref[:, sl] = _dot(p.astype(BF16), mv).astype(o_ref.dtype)


def memory_attention(proj, q_col_block, mkv, seq, n_mem, tm=512):
    n = proj.shape[0]
    width = MEM_HEADS * MEM_HEAD_DIM
    tm = min(tm, seq)
    nblk = seq // tm
    return pl.pallas_call(
        _memattn_body,
        grid=(n // seq, nblk),
        in_specs=[pl.BlockSpec((tm, width), lambda b, t: (b * nblk + t, q_col_block)),
                  pl.BlockSpec((n_mem, 2 * width), lambda b, t: (b, 0))],
        out_specs=pl.BlockSpec((tm, width), lambda b, t: (b * nblk + t, 0)),
        out_shape=jax.ShapeDtypeStruct((n, width), BF16),
        compiler_params=_params("parallel", "parallel"),
        name="memory_attention",
    )(proj, mkv)


def _outproj_body(s_ref, m_ref, w1_ref, w2_ref, x_ref, o_ref):
    o_ref[...] = x_ref[...] + _dot(s_ref[...], w1_ref[...]) + _dot(m_ref[...], w2_ref[...])


def out_projection(seq_out, mem_out, w_seq, w_mem, x, tm=512, tn=1024):
    n, d = x.shape
    tm, tn = min(tm, n), min(tn, d)
    ws, wm = seq_out.shape[1], mem_out.shape[1]
    return pl.pallas_call(
        _outproj_body,
        grid=(n // tm, d // tn),
        in_specs=[pl.BlockSpec((tm, ws), lambda i, j: (i, 0)),
                  pl.BlockSpec((tm, wm), lambda i, j: (i, 0)),
                  pl.BlockSpec((ws, tn), lambda i, j: (0, j)),
                  pl.BlockSpec((wm, tn), lambda i, j: (0, j)),
                  pl.BlockSpec((tm, tn), lambda i, j: (i, j))],
        out_specs=pl.BlockSpec((tm, tn), lambda i, j: (i, j)),
        out_shape=jax.ShapeDtypeStruct((n, d), F32),
        compiler_params=_params("parallel", "parallel"),
        name="out_projection",
    )(seq_out, mem_out, w_seq, w_mem, x)


def _bandattn_body(q_ref, kp_ref, kc_ref, vp_ref, vc_ref, bias_ref, o_ref, *, tq):
    qi = pl.program_id(2)
    scale = HEAD_DIM ** -0.5
    k_all = jnp.concatenate([kp_ref[...], kc_ref[...]], axis=0).astype(BF16)
    v_all = jnp.concatenate([vp_ref[...], vc_ref[...]], axis=0).astype(BF16)
    lane = lax.broadcasted_iota(jnp.int32, (CHUNK, PAIR), 1)
    head0 = lane < HEAD_DIM
    kcol = lax.broadcasted_iota(jnp.int32, (2 * CHUNK, BAND), 1)
    pad = LEFT_CHUNKS * CHUNK
    bias = bias_ref[...]
    chunks = range(tq // CHUNK)
    starts = [tq - pad + j * CHUNK for j in chunks]
    scores = []
    for j, start in zip(chunks, starts):
        q = q_ref[j * CHUNK:(j + 1) * CHUNK, :]
        q2 = jnp.concatenate([jnp.where(head0, q, 0.0), jnp.where(head0, 0.0, q)], axis=0).astype(BF16)
        scores.append(_dot_nt(q2, k_all[start:start + BAND]))
    exps = []
    for start, s in zip(starts, scores):
        valid = jnp.logical_or(qi > 0, kcol + start >= tq)
        s = jnp.where(valid, s * scale + bias, -jnp.inf)
        exps.append(jnp.exp(s - jnp.max(s, axis=-1, keepdims=True)))
    outs = [_dot(e.astype(BF16), v_all[start:start + BAND]) for start, e in zip(starts, exps)]
    for j, e, o in zip(chunks, exps, outs):
        o = o / jnp.sum(e, axis=-1, keepdims=True)
        o_ref[j * CHUNK:(j + 1) * CHUNK, :] = jnp.where(head0, o[:CHUNK], o[CHUNK:]).astype(o_ref.dtype)


def band_attention(proj, kv, bias, n_pairs, seq, tq=512):
    n = proj.shape[0]
    tq = min(tq, seq)
    assert tq >= LEFT_CHUNKS * CHUNK
    nblk = seq // tq
    cur = lambda off: pl.BlockSpec((tq, PAIR), lambda b, p, t: (b * nblk + t, off + p))
    prev = lambda off: pl.BlockSpec((tq, PAIR), lambda b, p, t: (b * nblk + jnp.maximum(t - 1, 0), off + p))
    return pl.pallas_call(
        functools.partial(_bandattn_body, tq=tq),
        grid=(n // seq, n_pairs, nblk),
        in_specs=[cur(0), prev(0), cur(0), prev(n_pairs), cur(n_pairs),
                  pl.BlockSpec((2 * CHUNK, BAND), lambda b, p, t: (p, 0))],
        out_specs=cur(0),
        out_shape=jax.ShapeDtypeStruct((n, n_pairs * PAIR), BF16),
        compiler_params=_params("parallel", "parallel", "parallel"),
        name="band_attention",
    )(proj, kv, kv, kv, kv, bias.reshape(-1, BAND))


def _top16(scores, payloads):
    nl = scores[0].shape[1]
    rids = [lax.broadcasted_iota(jnp.int32, s.shape, 0).astype(F32) for s in scores]
    slot = lax.broadcasted_iota(jnp.int32, (PEER_TOPK, nl), 0)
    tile = 8

    def winner(s, rid, payload):
        items = [(s[g:g + tile], rid[g:g + tile], None if payload is None else payload[g:g + tile])
                 for g in range(0, s.shape[0], tile)]
        while len(items) > 1:
            nxt = []
            for k in range(0, len(items) - 1, 2):
                (va, ia, pa), (vb, ib, pb) = items[k], items[k + 1]
                keep = va >= vb
                nxt.append((jnp.maximum(va, vb), jnp.where(keep, ia, ib),
                            None if pa is None else jnp.where(keep, pa, pb)))
            if len(items) % 2:
                nxt.append(items[-1])
            items = nxt
        v, idx, p = items[0]
        for shift in (4, 2, 1):
            pv, pi = pltpu.roll(v, shift, 0), pltpu.roll(idx, shift, 0)
            take = (pv > v) | ((pv == v) & (pi < idx))
            if p is not None:
                p = jnp.where(take, pltpu.roll(p, shift, 0), p)
            v, idx = jnp.where(take, pv, v), jnp.where(take, pi, idx)
        return v, idx, p

    def body(i, carry):
        sel = slot == i
        out = []
        for (s, vals, picks), rid, payload in zip(carry, rids, payloads):
            m, am, p = winner(s, rid, payload)
            hit = rid == jnp.concatenate([am] * (s.shape[0] // tile), axis=0)
            pick = am if payload is None else p
            out.append((jnp.where(hit, -jnp.inf, s), jnp.where(sel, m[0:1], vals), jnp.where(sel, pick[0:1], picks)))
        return tuple(out)

    zero = jnp.zeros((PEER_TOPK, nl), F32)
    res = lax.fori_loop(0, PEER_TOPK, body, tuple((s, zero, zero) for s in scores))
    return [(vals, picks) for _, vals, picks in res]


def _pair_candidates(a, b, combine):
    half = PEER_TOPK // 2
    rows = [combine(a[0:1], b)]
    rows += [combine(a[i:i + 1], b[:half]) for i in range(1, half)]
    rows.append(combine(a[half:], b[0:1]))
    return jnp.concatenate(rows, axis=0)


TOPK_HEADS = 2


def _peer_topk_body(q_ref, keys_ref, eidx_ref, gate_ref):
    q = q_ref[...].astype(BF16)
    scores = []
    for h in range(TOPK_HEADS):
        for half in range(2):
            col = (2 * h + half) * PEER_KEYS
            scores.append(_dot_nt(keys_ref[h, half].astype(BF16), q[:, col:col + PEER_KEYS]))
    cands, cidxs = [], []
    for h in range(TOPK_HEADS):
        (a, i1), (b, i2) = _top16(scores[2 * h:2 * h + 2], [None, None])
        cands.append(_pair_candidates(a, b, lambda x, y: x + y))
        cidxs.append(_pair_candidates(i1, i2, lambda x, y: x * PEER_KEYS + y))
    for h, (top, eidx) in enumerate(_top16(cands, cidxs)):
        rows = slice(h * PEER_TOPK, (h + 1) * PEER_TOPK)
        e = jnp.exp(top - top[0:1])
        gate_ref[rows, :] = e / jnp.sum(e, axis=0, keepdims=True)
        eidx_ref[rows, :] = eidx.astype(jnp.int32)


def peer_topk(q, keys, tl=128):
    n = q.shape[0]
    tl = min(tl, n)
    blk = pl.BlockSpec((TOPK_HEADS * PEER_TOPK, tl), lambda i, h: (h, i))
    return pl.pallas_call(
        _peer_topk_body,
        grid=(n // tl, PEER_HEADS // TOPK_HEADS),
        in_specs=[pl.BlockSpec((tl, TOPK_HEADS * 2 * PEER_KEYS), lambda i, h: (i, h)),
                  pl.BlockSpec((TOPK_HEADS, 2, PEER_KEYS, PEER_KEYS), lambda i, h: (h, 0, 0, 0))],
        out_specs=[blk, blk],
        out_shape=[jax.ShapeDtypeStruct((PEER_PICKS, n), jnp.int32),
                   jax.ShapeDtypeStruct((PEER_PICKS, n), F32)],
        compiler_params=_params("parallel", "parallel"),
        name="peer_topk",
    )(q, keys)


PEER_SLOTS = 8
LANES = 128
PITCH_PAD = 4


def _pack_body(u_ref, v_ref, o_ref):
    hi = lax.bitcast_convert_type(u_ref[0].astype(BF16).astype(F32), jnp.uint32)
    lo = lax.bitcast_convert_type(v_ref[0].astype(BF16).astype(F32), jnp.uint32)
    words = hi | (lo >> 16)
    for c in range(o_ref.shape[1]):
        o_ref[:, c, :] = words[:, c * LANES:(c + 1) * LANES]


def _pack_expert_table(u, v, layer, te=256):
    _, n_exp, d = u.shape
    chunks = d // LANES
    te = min(te, n_exp)
    packed = pl.pallas_call(
        _pack_body,
        grid=(n_exp // te,),
        in_specs=[pl.BlockSpec((1, te, d), lambda i: (layer, i, 0))] * 2,
        out_specs=pl.BlockSpec((te, chunks, LANES), lambda i: (i, 0, 0)),
        out_shape=jax.ShapeDtypeStruct((n_exp, chunks, LANES), jnp.uint32),
        compiler_params=_params("parallel"),
        name="pack_expert_table",
    )(u, v)
    return packed.reshape(n_exp * chunks, LANES)


def _peer_ffn_body(eidx_ref, gate_ref, hn_ref, x_ref, tab_ref, *rest, tb, d, out_norm):
    norm_ref, o_ref, scratch = (rest[0], rest[1], rest[2:]) if out_norm else (None, rest[0], rest[1:])
    rows_refs = scratch[:PEER_SLOTS]
    sem_ref, w_ref, stage_ref, hn_rows_ref, gate_rows_ref = scratch[PEER_SLOTS:]
    chunks = d // LANES
    pitch = chunks + PITCH_PAD

    def issue(t, slot):
        for e in range(PEER_PICKS):
            src = tab_ref.at[pl.ds(pl.multiple_of(eidx_ref[t, e] * chunks, chunks), chunks), :]
            dst = rows_refs[slot].at[pl.ds(e * pitch, chunks), :]
            pltpu.make_async_copy(src, dst, sem_ref.at[slot]).start(priority=e % 2)

    def wait(slot):
        total = PEER_PICKS * chunks
        pltpu.make_async_copy(tab_ref.at[pl.ds(0, total), :],
                              rows_refs[slot].at[pl.ds(0, total), :], sem_ref.at[slot]).wait()

    pick_diag = (lax.broadcasted_iota(jnp.int32, (PEER_PICKS, PEER_PICKS), 0)
                 == lax.broadcasted_iota(jnp.int32, (PEER_PICKS, PEER_PICKS), 1))
    hi_mask = jnp.uint32(0xFFFF0000)

    def words(slot, c):
        return rows_refs[slot][pl.ds(c, PEER_PICKS, stride=pitch), :]

    def pick_weights(slot, hn_rows, gate_rows, row):
        acc = jnp.zeros((PEER_PICKS, LANES), F32)
        for c in range(chunks):
            u = lax.bitcast_convert_type(words(slot, c) & hi_mask, F32)
            acc = acc + u * hn_rows[row:row + 1, c * LANES:(c + 1) * LANES]
        act = jnp.sum(acc, axis=1, keepdims=True)
        act = 0.5 * act * (1.0 + lax.erf(act * (2.0 ** -0.5)))
        gate = jnp.sum(jnp.where(pick_diag, gate_rows[row:row + 1, :], 0.0), axis=1, keepdims=True)
        w_ref[slot] = jnp.broadcast_to(gate * act, (PEER_PICKS, LANES))
        return acc

    def combine(slot, after):
        last = lax.bitcast_convert_type(after[PEER_PICKS - 8:], jnp.int32) == -1
        never = jnp.concatenate([last] * (PEER_PICKS // 8), axis=0)
        w = jnp.where(never, 0.0, w_ref[slot])
        for c in range(chunks):
            v = lax.bitcast_convert_type(words(slot, c) << 16, F32)
            stage_ref[slot:slot + 1, c * LANES:(c + 1) * LANES] = jnp.sum(v * w, axis=0, keepdims=True)

    def store_group(g):
        rows = pl.ds(pl.multiple_of(g * PEER_SLOTS, PEER_SLOTS), PEER_SLOTS)
        y = x_ref[rows, :] + stage_ref[...]
        if norm_ref is not None:
            y = y * lax.rsqrt(jnp.mean(y * y, axis=-1, keepdims=True) + RMS_EPS) * norm_ref[...]
        o_ref[rows, :] = y

    ahead = PEER_SLOTS - 1
    groups = tb // PEER_SLOTS
    for t in range(ahead):
        issue(t, t)
    wait(0)
    first_pass = pick_weights(0, hn_ref, gate_ref, 0)

    def group(g, carry):
        rows = pl.ds(pl.multiple_of(g * PEER_SLOTS, PEER_SLOTS), 2 * PEER_SLOTS)
        hn_rows_ref[...] = hn_ref[rows, :]
        gate_rows_ref[...] = gate_ref[rows, :]
        for s in range(PEER_SLOTS):
            t = g * PEER_SLOTS + s
            wait((s + 1) % PEER_SLOTS)
            combine(s, pick_weights((s + 1) % PEER_SLOTS, hn_rows_ref, gate_rows_ref, s + 1))
            issue(t + ahead, (s + ahead) % PEER_SLOTS)
        store_group(g)
        return carry

    lax.fori_loop(0, groups - 1, group, 0)

    for s in range(PEER_SLOTS):
        t = (groups - 1) * PEER_SLOTS + s
        if t + ahead < tb:
            issue(t + ahead, (s + ahead) % PEER_SLOTS)
        if t + 1 < tb:
            wait((s + 1) % PEER_SLOTS)
            first_pass = pick_weights((s + 1) % PEER_SLOTS, hn_ref, gate_ref, t + 1)
        combine(s, first_pass)
    store_group(groups - 1)


def peer_expert_ffn(eidx, gate, hn, x, table, tb=128, out_norm_g=None):
    n, d = x.shape
    tb = min(tb, n)
    assert tb % PEER_SLOTS == 0 and table.shape[1] == LANES
    assert table.shape[0] >= PEER_PICKS * (d // LANES)
    slot_rows = PEER_PICKS * (d // LANES + PITCH_PAD)
    out_norm = out_norm_g is not None
    in_specs = [pl.BlockSpec((tb, PEER_PICKS), lambda i: (i, 0), memory_space=pltpu.SMEM),
                pl.BlockSpec((tb, PEER_PICKS), lambda i: (i, 0)),
                pl.BlockSpec((tb, d), lambda i: (i, 0)),
                pl.BlockSpec((tb, d), lambda i: (i, 0)),
                pl.BlockSpec(memory_space=pl.ANY)]
    operands = [eidx, gate, hn, x, table]
    if out_norm:
        in_specs.append(pl.BlockSpec((1, d), lambda i: (0, 0)))
        operands.append(out_norm_g.reshape(1, d))
    return pl.pallas_call(
        functools.partial(_peer_ffn_body, tb=tb, d=d, out_norm=out_norm),
        grid=(n // tb,),
        in_specs=in_specs,
        out_specs=pl.BlockSpec((tb, d), lambda i: (i, 0)),
        out_shape=jax.ShapeDtypeStruct((n, d), F32),
        scratch_shapes=[pltpu.VMEM((slot_rows, LANES), jnp.uint32)] * PEER_SLOTS
                       + [pltpu.SemaphoreType.DMA((PEER_SLOTS,)),
                          pltpu.VMEM((PEER_SLOTS, PEER_PICKS, LANES), F32),
                          pltpu.VMEM((PEER_SLOTS, d), F32),
                          pltpu.VMEM((2 * PEER_SLOTS, d), F32),
                          pltpu.VMEM((2 * PEER_SLOTS, PEER_PICKS), F32)],
        compiler_params=_params("arbitrary"),
        name="peer_expert_ffn",
    )(*operands)


def peer_layer(x, norm_g, wq, keys, u_all, v_all, layer, tb=256, out_norm_g=None):
    q, hn = norm_matmul(x, norm_g, wq.astype(BF16), emit_hn=True)
    eidx, gate = peer_topk(q, keys)
    return peer_expert_ffn(eidx.T, gate.T, hn, x, _pack_expert_table(u_all, v_all, layer), tb=tb,
                           out_norm_g=out_norm_g)


def _band_bias(rel_bias):
    n_rel = rel_bias.shape[1]
    far = BAND - n_rel + CHUNK - 1
    long_row = jnp.concatenate([jnp.broadcast_to(rel_bias[:, n_rel - 1:], (rel_bias.shape[0], far)),
                                rel_bias[:, ::-1]], axis=1)
    rows = [long_row[:, CHUNK - 1 - i:CHUNK - 1 - i + BAND] for i in range(CHUNK)]
    return jnp.stack(rows, axis=1).astype(F32)


def kernel(x, mem, norm_mix, norm_ffn, norm_mem, w_mem_kv, w_out, peer_wq, peer_keys, peer_u, peer_v, a_mix, a_w_in, a_w0, a_w1, a_w2, a_a0, a_a1, a_a2, a_g1, a_g2, a_k_k, a_k_a, a_r_k, a_ln_g, a_ln_b, kv_norm, w_kv_shared, b_w_in, b_rel_bias, final_norm):
    bsz, seq, d = x.shape
    n = bsz * seq
    n_mem = mem.shape[1]
    seq_width = a_w0.shape[1]
    n_pairs = seq_width // PAIR
    mem_width = MEM_HEADS * MEM_HEAD_DIM
    x = x.reshape(n, d)
    mem2 = mem.reshape(bsz * n_mem, d)

    def mixer_tail(x, seq_out, proj, q_col_block, layer, out_norm_g=None):
        mkv = norm_matmul(mem2, norm_mem[layer], w_mem_kv[layer].astype(BF16), tm=256)
        mem_out = memory_attention(proj, q_col_block, mkv, seq, n_mem)
        wo = w_out[layer].astype(BF16)
        x = out_projection(seq_out, mem_out, wo[:seq_width], wo[seq_width:], x)
        return peer_layer(x, norm_ffn[layer], peer_wq[layer], peer_keys[layer], peer_u, peer_v, layer,
                          out_norm_g=out_norm_g)

    h = rmsnorm(x, norm_mix[0])
    mix_tab = jnp.concatenate([a_mix[0], jnp.zeros((1, d), F32)], axis=0)
    tiles = seq_width // 512
    gid_main = jnp.asarray([0] * tiles + [2] * tiles + [3] * tiles + [6] * (mem_width // 512), jnp.int32)
    proj = mix_matmul(h, mix_tab, gid_main, a_w_in[0].astype(BF16), seq, tn=512)
    rank = a_w1.shape[2]
    padc = lambda w: jnp.pad(w, ((0, 0), (0, LORA_PAD - rank)))
    padr = lambda w: jnp.pad(w, ((0, LORA_PAD - rank), (0, 0)))
    w_l1 = jnp.concatenate([padc(a_w1[0]), padc(a_a1[0]), a_g1[0]], axis=1).astype(BF16)
    gid_l1 = jnp.asarray([1, 4] + [5] * (a_g1.shape[2] // LORA_PAD), jnp.int32)
    t1 = mix_matmul(h, mix_tab, gid_l1, w_l1, seq, tn=LORA_PAD)
    lw, a_iclr, gate = lora_stage2(t1, padr(a_w2[0]).astype(BF16), padr(a_a2[0]).astype(BF16),
                                   a_g2[0].astype(BF16), a_w0[0], a_a0[0])
    prep = wkv_chunk_prepare(proj, lw, a_iclr, a_k_k[0], a_k_a[0], n_pairs)
    seq_out = wkv_chunk_scan(prep, proj, a_iclr, gate, a_k_a[0], a_r_k[0].reshape(-1),
                             a_ln_g[0], a_ln_b[0], n_pairs, seq)
    x = mixer_tail(x, seq_out, proj, (3 * seq_width) // mem_width, 0)

    kv = norm_matmul(x, kv_norm, w_kv_shared.astype(BF16))

    proj = norm_matmul(x, norm_mix[1], b_w_in[0].astype(BF16))
    seq_out = band_attention(proj, kv, _band_bias(b_rel_bias[0]), n_pairs, seq)
    x = mixer_tail(x, seq_out, proj, seq_width // mem_width, 1, out_norm_g=final_norm)
    return x.reshape(bsz, seq, d)
```

```python
import functools

import numpy as np
import jax
import jax.numpy as jnp
from jax import lax
from jax.experimental import pallas as pl
from jax.experimental.pallas import tpu as pltpu

F32 = jnp.float32
BF16 = jnp.bfloat16
HI = lax.Precision.HIGHEST

HEAD_DIM = 64
PAIR = 2 * HEAD_DIM
CHUNK = 64
SCAN_PAIRS = 2
LEFT_CHUNKS = 8
BAND = (LEFT_CHUNKS + 1) * CHUNK
REL_MAX = 128
MEM_HEADS = 4
MEM_HEAD_DIM = 128
PEER_KEYS = 128
PEER_HEADS = 8
PEER_TOPK = 16
PEER_PICKS = PEER_HEADS * PEER_TOPK
GN_EPS = 64e-5
RMS_EPS = 1e-6
LORA_PAD = 128
VMEM_LIMIT = 48 * 1024 * 1024


def _params(*sem):
    return pltpu.CompilerParams(dimension_semantics=sem, vmem_limit_bytes=VMEM_LIMIT)


def _dot(a, b):
    return jnp.dot(a, b, preferred_element_type=F32)


def _dot_nt(a, b, precision=None):
    return lax.dot_general(a, b, (((1,), (1,)), ((), ())), precision=precision,
                           preferred_element_type=F32)


def _dot_tn(a, b, precision=None):
    return lax.dot_general(a, b, (((0,), (0,)), ((), ())), precision=precision,
                           preferred_element_type=F32)


def _rmsnorm_body(x_ref, g_ref, o_ref):
    x = x_ref[...]
    ms = jnp.mean(x * x, axis=-1, keepdims=True)
    o_ref[...] = (x * lax.rsqrt(ms + RMS_EPS) * g_ref[...]).astype(o_ref.dtype)


def rmsnorm(x, g, tm=512):
    n, d = x.shape
    tm = min(tm, n)
    return pl.pallas_call(
        _rmsnorm_body,
        grid=(n // tm,),
        in_specs=[pl.BlockSpec((tm, d), lambda i: (i, 0)), pl.BlockSpec((1, d), lambda i: (0, 0))],
        out_specs=pl.BlockSpec((tm, d), lambda i: (i, 0)),
        out_shape=jax.ShapeDtypeStruct((n, d), F32),
        compiler_params=_params("parallel"),
        name="rmsnorm",
    )(x, g.reshape(1, d))


def _normmm_body(x_ref, g_ref, w_ref, o_ref, *rest, emit_hn):
    lhs_ref = rest[-1]

    @pl.when(pl.program_id(1) == 0)
    def _():
        x = x_ref[...]
        ms = jnp.mean(x * x, axis=-1, keepdims=True)
        hn = x * lax.rsqrt(ms + RMS_EPS) * g_ref[...]
        lhs_ref[...] = hn.astype(BF16)
        if emit_hn:
            rest[0][...] = hn

    o_ref[...] = _dot(lhs_ref[...], w_ref[...]).astype(o_ref.dtype)


def norm_matmul(x, g, w_bf16, emit_hn=False, tm=512, tn=1024, out_dtype=BF16):
    n, d = x.shape
    nc = w_bf16.shape[1]
    tm, tn = min(tm, n), min(tn, nc)
    out_shape = [jax.ShapeDtypeStruct((n, nc), out_dtype)]
    out_specs = [pl.BlockSpec((tm, tn), lambda i, j: (i, j))]
    if emit_hn:
        out_shape.append(jax.ShapeDtypeStruct((n, d), F32))
        out_specs.append(pl.BlockSpec((tm, d), lambda i, j: (i, 0)))
    res = pl.pallas_call(
        functools.partial(_normmm_body, emit_hn=emit_hn),
        grid=(n // tm, nc // tn),
        in_specs=[pl.BlockSpec((tm, d), lambda i, j: (i, 0)),
                  pl.BlockSpec((1, d), lambda i, j: (0, 0)),
                  pl.BlockSpec((d, tn), lambda i, j: (0, j))],
        out_specs=out_specs,
        out_shape=out_shape,
        scratch_shapes=[pltpu.VMEM((tm, d), BF16)],
        compiler_params=_params("parallel", "arbitrary"),
        name="norm_matmul",
    )(x, g.reshape(1, d), w_bf16)
    return res if emit_hn else res[0]


def _mixmm_body(gid_ref, h_ref, hp_ref, mix_ref, w_ref, o_ref, lhs_ref, *, tm, seq):
    i = pl.program_id(0)
    j = pl.program_id(1)
    new_group = jnp.logical_or(j == 0, gid_ref[j] != gid_ref[jnp.maximum(j - 1, 0)])

    @pl.when(new_group)
    def _():
        h = h_ref[...]
        prev = jnp.where((i * tm) % seq == 0, 0.0, hp_ref[7:8, :])
        row = lax.broadcasted_iota(jnp.int32, h.shape, 0)
        shifted = jnp.where(row == 0, prev, pltpu.roll(h, 1, 0))
        lhs_ref[...] = (h + (shifted - h) * mix_ref[0]).astype(BF16)

    o_ref[...] = _dot(lhs_ref[...], w_ref[...])


def mix_matmul(h, mix_tab, gid, w_bf16, seq, tn, tm=1024):
    n, d = h.shape
    nc = w_bf16.shape[1]
    tm = min(tm, seq)
    sub = tm // 8
    grid_spec = pltpu.PrefetchScalarGridSpec(
        num_scalar_prefetch=1,
        grid=(n // tm, nc // tn),
        in_specs=[pl.BlockSpec((tm, d), lambda i, j, g: (i, 0)),
                  pl.BlockSpec((8, d), lambda i, j, g: (jnp.maximum(i * sub - 1, 0), 0)),
                  pl.BlockSpec((1, 1, d), lambda i, j, g: (g[j], 0, 0)),
                  pl.BlockSpec((d, tn), lambda i, j, g: (0, j))],
        out_specs=pl.BlockSpec((tm, tn), lambda i, j, g: (i, j)),
        scratch_shapes=[pltpu.VMEM((tm, d), BF16)],
    )
    return pl.pallas_call(
        functools.partial(_mixmm_body, tm=tm, seq=seq),
        grid_spec=grid_spec,
        out_shape=jax.ShapeDtypeStruct((n, nc), F32),
        compiler_params=_params("parallel", "arbitrary"),
        name="mix_matmul",
    )(gid, h, h, mix_tab.reshape(mix_tab.shape[0], 1, d), w_bf16)


def _sigmoid(x):
    return 1.0 / (1.0 + jnp.exp(-x))


def _lora2_body(t_ref, w2_ref, a2_ref, g2_ref, w0_ref, a0_ref, lw_ref, a_ref, g_ref):
    t = t_ref[...]
    tw = jnp.tanh(t[:, :LORA_PAD]).astype(BF16)
    ta = t[:, LORA_PAD:2 * LORA_PAD].astype(BF16)
    tg = _sigmoid(t[:, 2 * LORA_PAD:]).astype(BF16)
    u = w0_ref[...] + _dot(tw, w2_ref[...])
    softplus_neg_u = jnp.maximum(-u, 0.0) + jnp.log(1.0 + jnp.exp(-jnp.abs(u)))
    lw_ref[...] = -jnp.exp(-softplus_neg_u - 0.5)
    a_ref[...] = _sigmoid(a0_ref[...] + _dot(ta, a2_ref[...]))
    g_ref[...] = _dot(tg, g2_ref[...])


def lora_stage2(t1, w2p, a2p, g2, w0, a0, tm=256):
    n = t1.shape[0]
    width = w2p.shape[1]
    tm = min(tm, n)
    full = lambda a: pl.BlockSpec(a.shape, lambda i: (0, 0))
    row = pl.BlockSpec((tm, width), lambda i: (i, 0))
    w0 = w0.reshape(1, width)
    a0 = a0.reshape(1, width)
    return pl.pallas_call(
        _lora2_body,
        grid=(n // tm,),
        in_specs=[pl.BlockSpec((tm, t1.shape[1]), lambda i: (i, 0)), full(w2p), full(a2p), full(g2),
                  full(w0), full(a0)],
        out_specs=[row, row, row],
        out_shape=[jax.ShapeDtypeStruct((n, width), F32)] * 3,
        compiler_params=_params("parallel"),
        name="lora_stage2",
    )(t1, w2p, a2p, g2, w0, a0)


def _head_group_sum(x):
    width = x.shape[-1]
    r = lax.broadcasted_iota(jnp.int32, (width, width), 0) // HEAD_DIM
    c = lax.broadcasted_iota(jnp.int32, (width, width), 1) // HEAD_DIM
    ones = jnp.where(r == c, 1.0, 0.0).astype(F32)
    return jnp.dot(x, ones, precision=HI, preferred_element_type=F32)


def _wkv_prepare_body(r_ref, k_ref, v_ref, lw_ref, a_ref, kk_ref, ka_ref,
                      g_ref, s0c_ref, rp_ref, y0_ref, *, tb):
    r = r_ref[...]
    k = k_ref[...]
    v = v_ref[...]
    lw = lw_ref[...]
    a = a_ref[...]
    kk = k * kk_ref[...]
    norm = jnp.sqrt(_head_group_sum(kk * kk))
    kk = kk / jnp.maximum(norm, 1e-12)
    kmod = k * (1.0 + (a - 1.0) * ka_ref[...])
    avec = -kk
    bvec = kk * a

    tr = lax.broadcasted_iota(jnp.int32, (CHUNK, CHUNK), 0)
    tc = lax.broadcasted_iota(jnp.int32, (CHUNK, CHUNK), 1)
    tri = jnp.where(tr >= tc, 1.0, 0.0).astype(F32)

    lane = lax.broadcasted_iota(jnp.int32, (CHUNK, PAIR), 1)
    head0 = lane < HEAD_DIM
    row = lax.broadcasted_iota(jnp.int32, (PAIR, PAIR), 0)
    col = lax.broadcasted_iota(jnp.int32, (PAIR, PAIR), 1)
    strict = row > col
    lower = row >= col
    eye = jnp.where(row == col, 1.0, 0.0).astype(F32)

    def stack(x):
        return jnp.concatenate([jnp.where(head0, x, 0.0), jnp.where(head0, 0.0, x)], axis=0)

    chunks = range(tb // CHUNK)
    pre = []
    for c in chunks:
        sl = slice(c * CHUNK, (c + 1) * CHUNK)
        cm = jnp.dot(tri, lw[sl], precision=HI, preferred_element_type=F32)
        cend = cm[CHUNK - 1:CHUNK]
        e_in = jnp.exp(cm)
        e_out = jnp.exp(-cm)
        e_tail = jnp.exp(cend - cm)
        a_s = stack(avec[sl] * jnp.exp(cm - lw[sl]))
        r_s = stack(r[sl] * e_in)
        b_s = stack(bvec[sl] * e_out)
        k_s = stack(kmod[sl] * e_out)
        bt_s = stack(bvec[sl] * e_tail)
        kt_s = stack(kmod[sl] * e_tail)
        v_s = stack(v[sl])
        p = _dot_nt(jnp.concatenate([a_s, r_s], axis=0).astype(BF16),
                    jnp.concatenate([b_s, k_s], axis=0).astype(BF16))
        pre.append(dict(cend=cend, a_s=a_s, r_s=r_s, bt_s=bt_s, kt_s=kt_s, v_s=v_s,
                        l_ab=jnp.where(strict, p[:PAIR, :PAIR], 0.0),
                        l_ak=jnp.where(strict, p[:PAIR, PAIR:], 0.0),
                        a_rb=jnp.where(lower, p[PAIR:, :PAIR], 0.0),
                        a_rk=jnp.where(lower, p[PAIR:, PAIR:], 0.0)))

    ms = [q["l_ab"] for q in pre]
    ts = [eye + m for m in ms]
    ms = [_dot(m.astype(BF16), m.astype(BF16)) for m in ms]
    for _ in range(4):
        both = [_dot(m.astype(BF16), jnp.concatenate([t, m], axis=1).astype(BF16)) for t, m in zip(ts, ms)]
        ts = [t + b[:, :PAIR] for t, b in zip(ts, both)]
        ms = [b[:, PAIR:] for b in both]
    ts = [t + _dot(m.astype(BF16), t.astype(BF16)) for t, m in zip(ts, ms)]

    lvs = [_dot(q["l_ak"].astype(BF16), q["v_s"].astype(BF16)) for q in pre]
    aus = [_dot(t.astype(BF16), jnp.concatenate([q["a_s"], lv], axis=1).astype(BF16))
           for t, q, lv in zip(ts, pre, lvs)]
    zero = jnp.zeros((PAIR, PAIR), F32)
    xs = [_dot(jnp.concatenate([q["a_rb"], q["a_rk"]], axis=1).astype(BF16),
               jnp.concatenate([au, jnp.concatenate([zero, q["v_s"]], axis=1)], axis=0).astype(BF16))
          for q, au in zip(pre, aus)]
    for c, q, au, x in zip(chunks, pre, aus, xs):
        rp_ref[0, c] = (q["r_s"] + x[:, :PAIR]).astype(rp_ref.dtype)
        y0_ref[0, c] = x[:, PAIR:]
        ap = au[:, :PAIR]
        u0 = au[:, PAIR:]
        g_ref[0, c] = (eye * jnp.exp(q["cend"])
                       + _dot_tn(ap.astype(BF16), q["bt_s"].astype(BF16))).astype(g_ref.dtype)
        s0c_ref[0, c] = _dot_tn(jnp.concatenate([u0, q["v_s"]], axis=0).astype(BF16),
                                jnp.concatenate([q["bt_s"], q["kt_s"]], axis=0).astype(BF16))


def wkv_chunk_prepare(proj, lw, a, k_k, k_a, n_pairs, tb=512):
    n = proj.shape[0]
    tb = min(tb, n)
    cpb = tb // CHUNK
    col = lambda off: pl.BlockSpec((tb, PAIR), lambda i, p: (i, off + p))
    par = pl.BlockSpec((1, PAIR), lambda i, p: (0, p))
    blk = pl.BlockSpec((1, cpb, PAIR, PAIR), lambda i, p: (p, i, 0, 0))
    shp = lambda dt: jax.ShapeDtypeStruct((n_pairs, n // CHUNK, PAIR, PAIR), dt)
    return pl.pallas_call(
        functools.partial(_wkv_prepare_body, tb=tb),
        grid=(n // tb, n_pairs),
        in_specs=[col(0), col(n_pairs), col(2 * n_pairs), col(0), col(0), par, par],
        out_specs=[blk] * 4,
        out_shape=[shp(BF16), shp(F32), shp(BF16), shp(F32)],
        compiler_params=_params("parallel", "parallel"),
        name="wkv_chunk_prepare",
    )(proj, proj, proj, lw, a, k_k.reshape(1, -1), k_a.reshape(1, -1))


def _wkv_scan_body(g_ref, s0c_ref, rp_ref, y0_ref, r_ref, k_ref, v_ref, a_ref, gate_ref,
                   ka_ref, rk_ref, lng_ref, lnb_ref, o_ref, s_ref, y_ref, *, cpb):
    @pl.when(pl.program_id(2) == 0)
    def _():
        s_ref[...] = jnp.zeros_like(s_ref)

    pairs = range(SCAN_PAIRS)
    states = [[s_ref[w].astype(BF16)] for w in pairs]
    for c in range(cpb):
        for w in pairs:
            s = _dot(states[w][c], g_ref[w, c].astype(BF16)) + s0c_ref[w, c]
            if c + 1 < cpb:
                states[w].append(s.astype(BF16))
            else:
                s_ref[w] = s
    for c in range(cpb):
        for w in pairs:
            y_st = _dot_nt(rp_ref[w, c].astype(BF16), states[w][c]) + y0_ref[w, c]
            y_ref[c * CHUNK:(c + 1) * CHUNK, w * PAIR:(w + 1) * PAIR] = y_st[:CHUNK] + y_st[CHUNK:]

    y = y_ref[...]
    inv = 1.0 / HEAD_DIM
    mu = _head_group_sum(y) * inv
    yc = y - mu
    var = _head_group_sum(yc * yc) * inv
    yn = yc * lax.rsqrt(var + GN_EPS) * lng_ref[...] + lnb_ref[...]
    r = r_ref[...]
    kmod = k_ref[...] * (1.0 + (a_ref[...] - 1.0) * ka_ref[...])
    bonus = _head_group_sum(r * kmod * rk_ref[...]) * v_ref[...]
    o_ref[...] = ((yn + bonus) * gate_ref[...]).astype(o_ref.dtype)


def wkv_chunk_scan(prep, proj, a, gate, k_a, r_k, ln_g, ln_b, n_pairs, seq, tb=512):
    g_all, s0c_all, rp_all, y0_all = prep
    n = proj.shape[0]
    tb = min(tb, seq)
    cpb = tb // CHUNK
    nblk = seq // tb
    assert n_pairs % SCAN_PAIRS == 0
    width = SCAN_PAIRS * PAIR
    groups = n_pairs // SCAN_PAIRS
    blk = pl.BlockSpec((SCAN_PAIRS, cpb, PAIR, PAIR), lambda b, p, t: (p, b * nblk + t, 0, 0))
    col = lambda off: pl.BlockSpec((tb, width), lambda b, p, t: (b * nblk + t, off + p))
    par = pl.BlockSpec((1, width), lambda b, p, t: (0, p))
    vec = lambda z: z.reshape(1, -1)
    return pl.pallas_call(
        functools.partial(_wkv_scan_body, cpb=cpb),
        grid=(n // seq, groups, nblk),
        in_specs=[blk] * 4 + [col(0), col(groups), col(2 * groups), col(0), col(0)] + [par] * 4,
        out_specs=col(0),
        out_shape=jax.ShapeDtypeStruct((n, n_pairs * PAIR), BF16),
        scratch_shapes=[pltpu.VMEM((SCAN_PAIRS, PAIR, PAIR), F32), pltpu.VMEM((tb, width), F32)],
        compiler_params=_params("parallel", "parallel", "arbitrary"),
        name="wkv_chunk_scan",
    )(g_all, s0c_all, rp_all, y0_all, proj, proj, proj, a, gate,
      vec(k_a), vec(r_k), vec(ln_g), vec(ln_b))


def _memattn_body(q_ref, m_ref, o_ref):
    width = MEM_HEADS * MEM_HEAD_DIM
    scale = MEM_HEAD_DIM ** -0.5
    for h in range(MEM_HEADS):
        sl = slice(h * MEM_HEAD_DIM, (h + 1) * MEM_HEAD_DIM)
        q = q_ref[:, sl].astype(BF16)
        mk = m_ref[:, sl].astype(BF16)
        mv = m_ref[:, width + h * MEM_HEAD_DIM:width + (h + 1) * MEM_HEAD_DIM].astype(BF16)
        s = _dot_nt(q, mk) * scale
        e = jnp.exp(s - jnp.max(s, axis=-1, keepdims=True))
        p = e / jnp.sum(e, axis=-1, keepdims=True)
        o_ref[:, sl] = _dot(p.astype(BF16), mv).astype(o_ref.dtype)


def memory_attention(proj, q_col_block, mkv, seq, n_mem, tm=512):
    n = proj.shape[0]
    width = MEM_HEADS * MEM_HEAD_DIM
    tm = min(tm, seq)
    nblk = seq // tm
    return pl.pallas_call(
        _memattn_body,
        grid=(n // seq, nblk),
        in_specs=[pl.BlockSpec((tm, width), lambda b, t: (b * nblk + t, q_col_block)),
                  pl.BlockSpec((n_mem, 2 * width), lambda b, t: (b, 0))],
        out_specs=pl.BlockSpec((tm, width), lambda b, t: (b * nblk + t, 0)),
        out_shape=jax.ShapeDtypeStruct((n, width), BF16),
        compiler_params=_params("parallel", "parallel"),
        name="memory_attention",
    )(proj, mkv)


def _outproj_body(s_ref, m_ref, w1_ref, w2_ref, x_ref, o_ref):
    o_ref[...] = x_ref[...] + _dot(s_ref[...], w1_ref[...]) + _dot(m_ref[...], w2_ref[...])


def out_projection(seq_out, mem_out, w_seq, w_mem, x, tm=512, tn=1024):
    n, d = x.shape
    tm, tn = min(tm, n), min(tn, d)
    ws, wm = seq_out.shape[1], mem_out.shape[1]
    return pl.pallas_call(
        _outproj_body,
        grid=(n // tm, d // tn),
        in_specs=[pl.BlockSpec((tm, ws), lambda i, j: (i, 0)),
                  pl.BlockSpec((tm, wm), lambda i, j: (i, 0)),
                  pl.BlockSpec((ws, tn), lambda i, j: (0, j)),
                  pl.BlockSpec((wm, tn), lambda i, j: (0, j)),
                  pl.BlockSpec((tm, tn), lambda i, j: (i, j))],
        out_specs=pl.BlockSpec((tm, tn), lambda i, j: (i, j)),
        out_shape=jax.ShapeDtypeStruct((n, d), F32),
        compiler_params=_params("parallel", "parallel"),
        name="out_projection",
    )(seq_out, mem_out, w_seq, w_mem, x)


def _bandattn_body(q_ref, kp_ref, kc_ref, vp_ref, vc_ref, bias_ref, o_ref, *, tq):
    qi = pl.program_id(2)
    scale = HEAD_DIM ** -0.5
    k_all = jnp.concatenate([kp_ref[...], kc_ref[...]], axis=0).astype(BF16)
    v_all = jnp.concatenate([vp_ref[...], vc_ref[...]], axis=0).astype(BF16)
    lane = lax.broadcasted_iota(jnp.int32, (CHUNK, PAIR), 1)
    head0 = lane < HEAD_DIM
    kcol = lax.broadcasted_iota(jnp.int32, (2 * CHUNK, BAND), 1)
    pad = LEFT_CHUNKS * CHUNK
    bias = bias_ref[...]
    chunks = range(tq // CHUNK)
    starts = [tq - pad + j * CHUNK for j in chunks]
    scores = []
    for j, start in zip(chunks, starts):
        q = q_ref[j * CHUNK:(j + 1) * CHUNK, :]
        q2 = jnp.concatenate([jnp.where(head0, q, 0.0), jnp.where(head0, 0.0, q)], axis=0).astype(BF16)
        scores.append(_dot_nt(q2, k_all[start:start + BAND]))
    exps = []
    for start, s in zip(starts, scores):
        valid = jnp.logical_or(qi > 0, kcol + start >= tq)
        s = jnp.where(valid, s * scale + bias, -jnp.inf)
        exps.append(jnp.exp(s - jnp.max(s, axis=-1, keepdims=True)))
    outs = [_dot(e.astype(BF16), v_all[start:start + BAND]) for start, e in zip(starts, exps)]
    for j, e, o in zip(chunks, exps, outs):
        o = o / jnp.sum(e, axis=-1, keepdims=True)
        o_ref[j * CHUNK:(j + 1) * CHUNK, :] = jnp.where(head0, o[:CHUNK], o[CHUNK:]).astype(o_ref.dtype)


def band_attention(proj, kv, bias, n_pairs, seq, tq=512):
    n = proj.shape[0]
    tq = min(tq, seq)
    assert tq >= LEFT_CHUNKS * CHUNK
    nblk = seq // tq
    cur = lambda off: pl.BlockSpec((tq, PAIR), lambda b, p, t: (b * nblk + t, off + p))
    prev = lambda off: pl.BlockSpec((tq, PAIR), lambda b, p, t: (b * nblk + jnp.maximum(t - 1, 0), off + p))
    return pl.pallas_call(
        functools.partial(_bandattn_body, tq=tq),
        grid=(n // seq, n_pairs, nblk),
        in_specs=[cur(0), prev(0), cur(0), prev(n_pairs), cur(n_pairs),
                  pl.BlockSpec((2 * CHUNK, BAND), lambda b, p, t: (p, 0))],
        out_specs=cur(0),
        out_shape=jax.ShapeDtypeStruct((n, n_pairs * PAIR), BF16),
        compiler_params=_params("parallel", "parallel", "parallel"),
        name="band_attention",
    )(proj, kv, kv, kv, kv, bias.reshape(-1, BAND))


def _top16(scores, payloads):
    nl = scores[0].shape[1]
    rids = [lax.broadcasted_iota(jnp.int32, s.shape, 0).astype(F32) for s in scores]
    slot = lax.broadcasted_iota(jnp.int32, (PEER_TOPK, nl), 0)
    tile = 8

    def winner(s, rid, payload):
        items = [(s[g:g + tile], rid[g:g + tile], None if payload is None else payload[g:g + tile])
                 for g in range(0, s.shape[0], tile)]
        while len(items) > 1:
            nxt = []
            for k in range(0, len(items) - 1, 2):
                (va, ia, pa), (vb, ib, pb) = items[k], items[k + 1]
                keep = va >= vb
                nxt.append((jnp.maximum(va, vb), jnp.where(keep, ia, ib),
                            None if pa is None else jnp.where(keep, pa, pb)))
            if len(items) % 2:
                nxt.append(items[-1])
            items = nxt
        v, idx, p = items[0]
        for shift in (4, 2, 1):
            pv, pi = pltpu.roll(v, shift, 0), pltpu.roll(idx, shift, 0)
            take = (pv > v) | ((pv == v) & (pi < idx))
            if p is not None:
                p = jnp.where(take, pltpu.roll(p, shift, 0), p)
            v, idx = jnp.where(take, pv, v), jnp.where(take, pi, idx)
        return v, idx, p

    def body(i, carry):
        sel = slot == i
        out = []
        for (s, vals, picks), rid, payload in zip(carry, rids, payloads):
            m, am, p = winner(s, rid, payload)
            hit = rid == jnp.concatenate([am] * (s.shape[0] // tile), axis=0)
            pick = am if payload is None else p
            out.append((jnp.where(hit, -jnp.inf, s), jnp.where(sel, m[0:1], vals), jnp.where(sel, pick[0:1], picks)))
        return tuple(out)

    zero = jnp.zeros((PEER_TOPK, nl), F32)
    res = lax.fori_loop(0, PEER_TOPK, body, tuple((s, zero, zero) for s in scores))
    return [(vals, picks) for _, vals, picks in res]


def _pair_candidates(a, b, combine):
    half = PEER_TOPK // 2
    rows = [combine(a[0:1], b)]
    rows += [combine(a[i:i + 1], b[:half]) for i in range(1, half)]
    rows.append(combine(a[half:], b[0:1]))
    return jnp.concatenate(rows, axis=0)


TOPK_HEADS = 2


def _peer_topk_body(q_ref, keys_ref, eidx_ref, gate_ref):
    q = q_ref[...].astype(BF16)
    scores = []
    for h in range(TOPK_HEADS):
        for half in range(2):
            col = (2 * h + half) * PEER_KEYS
            scores.append(_dot_nt(keys_ref[h, half].astype(BF16), q[:, col:col + PEER_KEYS]))
    cands, cidxs = [], []
    for h in range(TOPK_HEADS):
        (a, i1), (b, i2) = _top16(scores[2 * h:2 * h + 2], [None, None])
        cands.append(_pair_candidates(a, b, lambda x, y: x + y))
        cidxs.append(_pair_candidates(i1, i2, lambda x, y: x * PEER_KEYS + y))
    for h, (top, eidx) in enumerate(_top16(cands, cidxs)):
        rows = slice(h * PEER_TOPK, (h + 1) * PEER_TOPK)
        e = jnp.exp(top - top[0:1])
        gate_ref[rows, :] = e / jnp.sum(e, axis=0, keepdims=True)
        eidx_ref[rows, :] = eidx.astype(jnp.int32)


def peer_topk(q, keys, tl=128):
    n = q.shape[0]
    tl = min(tl, n)
    blk = pl.BlockSpec((TOPK_HEADS * PEER_TOPK, tl), lambda i, h: (h, i))
    return pl.pallas_call(
        _peer_topk_body,
        grid=(n // tl, PEER_HEADS // TOPK_HEADS),
        in_specs=[pl.BlockSpec((tl, TOPK_HEADS * 2 * PEER_KEYS), lambda i, h: (i, h)),
                  pl.BlockSpec((TOPK_HEADS, 2, PEER_KEYS, PEER_KEYS), lambda i, h: (h, 0, 0, 0))],
        out_specs=[blk, blk],
        out_shape=[jax.ShapeDtypeStruct((PEER_PICKS, n), jnp.int32),
                   jax.ShapeDtypeStruct((PEER_PICKS, n), F32)],
        compiler_params=_params("parallel", "parallel"),
        name="peer_topk",
    )(q, keys)


PEER_SLOTS = 8
LANES = 128
PITCH_PAD = 4


def _pack_body(u_ref, v_ref, o_ref):
    hi = lax.bitcast_convert_type(u_ref[0].astype(BF16).astype(F32), jnp.uint32)
    lo = lax.bitcast_convert_type(v_ref[0].astype(BF16).astype(F32), jnp.uint32)
    words = hi | (lo >> 16)
    for c in range(o_ref.shape[1]):
        o_ref[:, c, :] = words[:, c * LANES:(c + 1) * LANES]


def _pack_expert_table(u, v, layer, te=256):
    _, n_exp, d = u.shape
    chunks = d // LANES
    te = min(te, n_exp)
    packed = pl.pallas_call(
        _pack_body,
        grid=(n_exp // te,),
        in_specs=[pl.BlockSpec((1, te, d), lambda i: (layer, i, 0))] * 2,
        out_specs=pl.BlockSpec((te, chunks, LANES), lambda i: (i, 0, 0)),
        out_shape=jax.ShapeDtypeStruct((n_exp, chunks, LANES), jnp.uint32),
        compiler_params=_params("parallel"),
        name="pack_expert_table",
    )(u, v)
    return packed.reshape(n_exp * chunks, LANES)


def _peer_ffn_body(eidx_ref, gate_ref, hn_ref, x_ref, tab_ref, *rest, tb, d, out_norm):
    norm_ref, o_ref, scratch = (rest[0], rest[1], rest[2:]) if out_norm else (None, rest[0], rest[1:])
    rows_refs = scratch[:PEER_SLOTS]
    sem_ref, w_ref, stage_ref, hn_rows_ref, gate_rows_ref = scratch[PEER_SLOTS:]
    chunks = d // LANES
    pitch = chunks + PITCH_PAD

    def issue(t, slot):
        for e in range(PEER_PICKS):
            src = tab_ref.at[pl.ds(pl.multiple_of(eidx_ref[t, e] * chunks, chunks), chunks), :]
            dst = rows_refs[slot].at[pl.ds(e * pitch, chunks), :]
            pltpu.make_async_copy(src, dst, sem_ref.at[slot]).start(priority=e % 2)

    def wait(slot):
        total = PEER_PICKS * chunks
        pltpu.make_async_copy(tab_ref.at[pl.ds(0, total), :],
                              rows_refs[slot].at[pl.ds(0, total), :], sem_ref.at[slot]).wait()

    pick_diag = (lax.broadcasted_iota(jnp.int32, (PEER_PICKS, PEER_PICKS), 0)
                 == lax.broadcasted_iota(jnp.int32, (PEER_PICKS, PEER_PICKS), 1))
    hi_mask = jnp.uint32(0xFFFF0000)

    def words(slot, c):
        return rows_refs[slot][pl.ds(c, PEER_PICKS, stride=pitch), :]

    def pick_weights(slot, hn_rows, gate_rows, row):
        acc = jnp.zeros((PEER_PICKS, LANES), F32)
        for c in range(chunks):
            u = lax.bitcast_convert_type(words(slot, c) & hi_mask, F32)
            acc = acc + u * hn_rows[row:row + 1, c * LANES:(c + 1) * LANES]
        act = jnp.sum(acc, axis=1, keepdims=True)
        act = 0.5 * act * (1.0 + lax.erf(act * (2.0 ** -0.5)))
        gate = jnp.sum(jnp.where(pick_diag, gate_rows[row:row + 1, :], 0.0), axis=1, keepdims=True)
        w_ref[slot] = jnp.broadcast_to(gate * act, (PEER_PICKS, LANES))
        return acc

    def combine(slot, after):
        last = lax.bitcast_convert_type(after[PEER_PICKS - 8:], jnp.int32) == -1
        never = jnp.concatenate([last] * (PEER_PICKS // 8), axis=0)
        w = jnp.where(never, 0.0, w_ref[slot])
        for c in range(chunks):
            v = lax.bitcast_convert_type(words(slot, c) << 16, F32)
            stage_ref[slot:slot + 1, c * LANES:(c + 1) * LANES] = jnp.sum(v * w, axis=0, keepdims=True)

    def store_group(g):
        rows = pl.ds(pl.multiple_of(g * PEER_SLOTS, PEER_SLOTS), PEER_SLOTS)
        y = x_ref[rows, :] + stage_ref[...]
        if norm_ref is not None:
            y = y * lax.rsqrt(jnp.mean(y * y, axis=-1, keepdims=True) + RMS_EPS) * norm_ref[...]
        o_ref[rows, :] = y

    ahead = PEER_SLOTS - 1
    groups = tb // PEER_SLOTS
    for t in range(ahead):
        issue(t, t)
    wait(0)
    first_pass = pick_weights(0, hn_ref, gate_ref, 0)

    def group(g, carry):
        rows = pl.ds(pl.multiple_of(g * PEER_SLOTS, PEER_SLOTS), 2 * PEER_SLOTS)
        hn_rows_ref[...] = hn_ref[rows, :]
        gate_rows_ref[...] = gate_ref[rows, :]
        for s in range(PEER_SLOTS):
            t = g * PEER_SLOTS + s
            wait((s + 1) % PEER_SLOTS)
            combine(s, pick_weights((s + 1) % PEER_SLOTS, hn_rows_ref, gate_rows_ref, s + 1))
            issue(t + ahead, (s + ahead) % PEER_SLOTS)
        store_group(g)
        return carry

    lax.fori_loop(0, groups - 1, group, 0)

    for s in range(PEER_SLOTS):
        t = (groups - 1) * PEER_SLOTS + s
        if t + ahead < tb:
            issue(t + ahead, (s + ahead) % PEER_SLOTS)
        if t + 1 < tb:
            wait((s + 1) % PEER_SLOTS)
            first_pass = pick_weights((s + 1) % PEER_SLOTS, hn_ref, gate_ref, t + 1)
        combine(s, first_pass)
    store_group(groups - 1)


def peer_expert_ffn(eidx, gate, hn, x, table, tb=128, out_norm_g=None):
    n, d = x.shape
    tb = min(tb, n)
    assert tb % PEER_SLOTS == 0 and table.shape[1] == LANES
    assert table.shape[0] >= PEER_PICKS * (d // LANES)
    slot_rows = PEER_PICKS * (d // LANES + PITCH_PAD)
    out_norm = out_norm_g is not None
    in_specs = [pl.BlockSpec((tb, PEER_PICKS), lambda i: (i, 0), memory_space=pltpu.SMEM),
                pl.BlockSpec((tb, PEER_PICKS), lambda i: (i, 0)),
                pl.BlockSpec((tb, d), lambda i: (i, 0)),
                pl.BlockSpec((tb, d), lambda i: (i, 0)),
                pl.BlockSpec(memory_space=pl.ANY)]
    operands = [eidx, gate, hn, x, table]
    if out_norm:
        in_specs.append(pl.BlockSpec((1, d), lambda i: (0, 0)))
        operands.append(out_norm_g.reshape(1, d))
    return pl.pallas_call(
        functools.partial(_peer_ffn_body, tb=tb, d=d, out_norm=out_norm),
        grid=(n // tb,),
        in_specs=in_specs,
        out_specs=pl.BlockSpec((tb, d), lambda i: (i, 0)),
        out_shape=jax.ShapeDtypeStruct((n, d), F32),
        scratch_shapes=[pltpu.VMEM((slot_rows, LANES), jnp.uint32)] * PEER_SLOTS
                       + [pltpu.SemaphoreType.DMA((PEER_SLOTS,)),
                          pltpu.VMEM((PEER_SLOTS, PEER_PICKS, LANES), F32),
                          pltpu.VMEM((PEER_SLOTS, d), F32),
                          pltpu.VMEM((2 * PEER_SLOTS, d), F32),
                          pltpu.VMEM((2 * PEER_SLOTS, PEER_PICKS), F32)],
        compiler_params=_params("arbitrary"),
        name="peer_expert_ffn",
    )(*operands)


def peer_layer(x, norm_g, wq, keys, u_all, v_all, layer, tb=256, out_norm_g=None):
    q, hn = norm_matmul(x, norm_g, wq.astype(BF16), emit_hn=True)
    eidx, gate = peer_topk(q, keys)
    return peer_expert_ffn(eidx.T, gate.T, hn, x, _pack_expert_table(u_all, v_all, layer), tb=tb,
                           out_norm_g=out_norm_g)


def _band_bias(rel_bias):
    n_rel = rel_bias.shape[1]
    far = BAND - n_rel + CHUNK - 1
    long_row = jnp.concatenate([jnp.broadcast_to(rel_bias[:, n_rel - 1:], (rel_bias.shape[0], far)),
                                rel_bias[:, ::-1]], axis=1)
    rows = [long_row[:, CHUNK - 1 - i:CHUNK - 1 - i + BAND] for i in range(CHUNK)]
    return jnp.stack(rows, axis=1).astype(F32)


def kernel(x, mem, norm_mix, norm_ffn, norm_mem, w_mem_kv, w_out, peer_wq, peer_keys, peer_u, peer_v, a_mix, a_w_in, a_w0, a_w1, a_w2, a_a0, a_a1, a_a2, a_g1, a_g2, a_k_k, a_k_a, a_r_k, a_ln_g, a_ln_b, kv_norm, w_kv_shared, b_w_in, b_rel_bias, final_norm):
    bsz, seq, d = x.shape
    n = bsz * seq
    n_mem = mem.shape[1]
    seq_width = a_w0.shape[1]
    n_pairs = seq_width // PAIR
    mem_width = MEM_HEADS * MEM_HEAD_DIM
    x = x.reshape(n, d)
    mem2 = mem.reshape(bsz * n_mem, d)

    def mixer_tail(x, seq_out, proj, q_col_block, layer, out_norm_g=None):
        mkv = norm_matmul(mem2, norm_mem[layer], w_mem_kv[layer].astype(BF16), tm=256)
        mem_out = memory_attention(proj, q_col_block, mkv, seq, n_mem)
        wo = w_out[layer].astype(BF16)
        x = out_projection(seq_out, mem_out, wo[:seq_width], wo[seq_width:], x)
        return peer_layer(x, norm_ffn[layer], peer_wq[layer], peer_keys[layer], peer_u, peer_v, layer,
                          out_norm_g=out_norm_g)

    h = rmsnorm(x, norm_mix[0])
    mix_tab = jnp.concatenate([a_mix[0], jnp.zeros((1, d), F32)], axis=0)
    tiles = seq_width // 512
    gid_main = jnp.asarray([0] * tiles + [2] * tiles + [3] * tiles + [6] * (mem_width // 512), jnp.int32)
    proj = mix_matmul(h, mix_tab, gid_main, a_w_in[0].astype(BF16), seq, tn=512)
    rank = a_w1.shape[2]
    padc = lambda w: jnp.pad(w, ((0, 0), (0, LORA_PAD - rank)))
    padr = lambda w: jnp.pad(w, ((0, LORA_PAD - rank), (0, 0)))
    w_l1 = jnp.concatenate([padc(a_w1[0]), padc(a_a1[0]), a_g1[0]], axis=1).astype(BF16)
    gid_l1 = jnp.asarray([1, 4] + [5] * (a_g1.shape[2] // LORA_PAD), jnp.int32)
    t1 = mix_matmul(h, mix_tab, gid_l1, w_l1, seq, tn=LORA_PAD)
    lw, a_iclr, gate = lora_stage2(t1, padr(a_w2[0]).astype(BF16), padr(a_a2[0]).astype(BF16),
                                   a_g2[0].astype(BF16), a_w0[0], a_a0[0])
    prep = wkv_chunk_prepare(proj, lw, a_iclr, a_k_k[0], a_k_a[0], n_pairs)
    seq_out = wkv_chunk_scan(prep, proj, a_iclr, gate, a_k_a[0], a_r_k[0].reshape(-1),
                             a_ln_g[0], a_ln_b[0], n_pairs, seq)
    x = mixer_tail(x, seq_out, proj, (3 * seq_width) // mem_width, 0)

    kv = norm_matmul(x, kv_norm, w_kv_shared.astype(BF16), tm=1024)

    proj = norm_matmul(x, norm_mix[1], b_w_in[0].astype(BF16), tm=1024)
    seq_out = band_attention(proj, kv, _band_bias(b_rel_bias[0]), n_pairs, seq)
    x = mixer_tail(x, seq_out, proj, seq_width // mem_width, 1, out_norm_g=final_norm)
    return x.reshape(bsz, seq, d)
```

```python
import functools

import numpy as np
import jax
import jax.numpy as jnp
from jax import lax
from jax.experimental import pallas as pl
from jax.experimental.pallas import tpu as pltpu

F32 = jnp.float32
BF16 = jnp.bfloat16

HEAD_DIM = 64
PAIR = 2 * HEAD_DIM
CHUNK = 64
SCAN_PAIRS = 2
LEFT_CHUNKS = 8
BAND = (LEFT_CHUNKS + 1) * CHUNK
REL_MAX = 128
MEM_HEADS = 4
MEM_HEAD_DIM = 128
PEER_KEYS = 128
PEER_HEADS = 8
PEER_TOPK = 16
PEER_PICKS = PEER_HEADS * PEER_TOPK
GN_EPS = 64e-5
RMS_EPS = 1e-6
LORA_PAD = 128
VMEM_LIMIT = 48 * 1024 * 1024


def _params(*sem):
    return pltpu.CompilerParams(dimension_semantics=sem, vmem_limit_bytes=VMEM_LIMIT)


def _dot(a, b):
    return jnp.dot(a, b, preferred_element_type=F32)


def _dot_nt(a, b, precision=None):
    return lax.dot_general(a, b, (((1,), (1,)), ((), ())), precision=precision,
                           preferred_element_type=F32)


def _dot_tn(a, b, precision=None):
    return lax.dot_general(a, b, (((0,), (0,)), ((), ())), precision=precision,
                           preferred_element_type=F32)


def _rmsnorm_body(x_ref, g_ref, o_ref):
    x = x_ref[...]
    ms = jnp.mean(x * x, axis=-1, keepdims=True)
    o_ref[...] = (x * lax.rsqrt(ms + RMS_EPS) * g_ref[...]).astype(o_ref.dtype)


def rmsnorm(x, g, tm=512):
    n, d = x.shape
    tm = min(tm, n)
    return pl.pallas_call(
        _rmsnorm_body,
        grid=(n // tm,),
        in_specs=[pl.BlockSpec((tm, d), lambda i: (i, 0)), pl.BlockSpec((1, d), lambda i: (0, 0))],
        out_specs=pl.BlockSpec((tm, d), lambda i: (i, 0)),
        out_shape=jax.ShapeDtypeStruct((n, d), F32),
        compiler_params=_params("parallel"),
        name="rmsnorm",
    )(x, g.reshape(1, d))


def _normmm_body(x_ref, g_ref, w_ref, o_ref, *rest, emit_hn):
    lhs_ref = rest[-1]

    @pl.when(pl.program_id(1) == 0)
    def _():
        x = x_ref[...]
        ms = jnp.mean(x * x, axis=-1, keepdims=True)
        hn = x * lax.rsqrt(ms + RMS_EPS) * g_ref[...]
        lhs_ref[...] = hn.astype(BF16)
        if emit_hn:
            rest[0][...] = hn

    o_ref[...] = _dot(lhs_ref[...], w_ref[...]).astype(o_ref.dtype)


def norm_matmul(x, g, w_bf16, emit_hn=False, tm=512, tn=1024, out_dtype=BF16):
    n, d = x.shape
    nc = w_bf16.shape[1]
    tm, tn = min(tm, n), min(tn, nc)
    out_shape = [jax.ShapeDtypeStruct((n, nc), out_dtype)]
    out_specs = [pl.BlockSpec((tm, tn), lambda i, j: (i, j))]
    if emit_hn:
        out_shape.append(jax.ShapeDtypeStruct((n, d), F32))
        out_specs.append(pl.BlockSpec((tm, d), lambda i, j: (i, 0)))
    res = pl.pallas_call(
        functools.partial(_normmm_body, emit_hn=emit_hn),
        grid=(n // tm, nc // tn),
        in_specs=[pl.BlockSpec((tm, d), lambda i, j: (i, 0)),
                  pl.BlockSpec((1, d), lambda i, j: (0, 0)),
                  pl.BlockSpec((d, tn), lambda i, j: (0, j))],
        out_specs=out_specs,
        out_shape=out_shape,
        scratch_shapes=[pltpu.VMEM((tm, d), BF16)],
        compiler_params=_params("parallel", "arbitrary"),
        name="norm_matmul",
    )(x, g.reshape(1, d), w_bf16)
    return res if emit_hn else res[0]


def _mixmm_body(gid_ref, h_ref, hp_ref, mix_ref, w_ref, o_ref, lhs_ref, *, tm, seq):
    i = pl.program_id(0)
    j = pl.program_id(1)
    new_group = jnp.logical_or(j == 0, gid_ref[j] != gid_ref[jnp.maximum(j - 1, 0)])

    @pl.when(new_group)
    def _():
        h = h_ref[...]
        prev = jnp.where((i * tm) % seq == 0, 0.0, hp_ref[7:8, :])
        row = lax.broadcasted_iota(jnp.int32, h.shape, 0)
        shifted = jnp.where(row == 0, prev, pltpu.roll(h, 1, 0))
        lhs_ref[...] = (h + (shifted - h) * mix_ref[0]).astype(BF16)

    o_ref[...] = _dot(lhs_ref[...], w_ref[...])


def mix_matmul(h, mix_tab, gid, w_bf16, seq, tn, tm=1024):
    n, d = h.shape
    nc = w_bf16.shape[1]
    tm = min(tm, seq)
    sub = tm // 8
    grid_spec = pltpu.PrefetchScalarGridSpec(
        num_scalar_prefetch=1,
        grid=(n // tm, nc // tn),
        in_specs=[pl.BlockSpec((tm, d), lambda i, j, g: (i, 0)),
                  pl.BlockSpec((8, d), lambda i, j, g: (jnp.maximum(i * sub - 1, 0), 0)),
                  pl.BlockSpec((1, 1, d), lambda i, j, g: (g[j], 0, 0)),
                  pl.BlockSpec((d, tn), lambda i, j, g: (0, j))],
        out_specs=pl.BlockSpec((tm, tn), lambda i, j, g: (i, j)),
        scratch_shapes=[pltpu.VMEM((tm, d), BF16)],
    )
    return pl.pallas_call(
        functools.partial(_mixmm_body, tm=tm, seq=seq),
        grid_spec=grid_spec,
        out_shape=jax.ShapeDtypeStruct((n, nc), F32),
        compiler_params=_params("parallel", "arbitrary"),
        name="mix_matmul",
    )(gid, h, h, mix_tab.reshape(mix_tab.shape[0], 1, d), w_bf16)


def _sigmoid(x):
    return 1.0 / (1.0 + jnp.exp(-x))


def _lora2_body(t_ref, w2_ref, a2_ref, g2_ref, w0_ref, a0_ref, lw_ref, a_ref, g_ref):
    t = t_ref[...]
    tw = jnp.tanh(t[:, :LORA_PAD]).astype(BF16)
    ta = t[:, LORA_PAD:2 * LORA_PAD].astype(BF16)
    tg = _sigmoid(t[:, 2 * LORA_PAD:]).astype(BF16)
    u = w0_ref[...] + _dot(tw, w2_ref[...])
    softplus_neg_u = jnp.maximum(-u, 0.0) + jnp.log(1.0 + jnp.exp(-jnp.abs(u)))
    lw_ref[...] = -jnp.exp(-softplus_neg_u - 0.5)
    a_ref[...] = _sigmoid(a0_ref[...] + _dot(ta, a2_ref[...]))
    g_ref[...] = _dot(tg, g2_ref[...])


def lora_stage2(t1, w2p, a2p, g2, w0, a0, tm=256):
    n = t1.shape[0]
    width = w2p.shape[1]
    tm = min(tm, n)
    full = lambda a: pl.BlockSpec(a.shape, lambda i: (0, 0))
    row = pl.BlockSpec((tm, width), lambda i: (i, 0))
    w0 = w0.reshape(1, width)
    a0 = a0.reshape(1, width)
    return pl.pallas_call(
        _lora2_body,
        grid=(n // tm,),
        in_specs=[pl.BlockSpec((tm, t1.shape[1]), lambda i: (i, 0)), full(w2p), full(a2p), full(g2),
                  full(w0), full(a0)],
        out_specs=[row, row, row],
        out_shape=[jax.ShapeDtypeStruct((n, width), F32)] * 3,
        compiler_params=_params("parallel"),
        name="lora_stage2",
    )(t1, w2p, a2p, g2, w0, a0)


def _head_group_sum(x):
    width = x.shape[-1]
    r = lax.broadcasted_iota(jnp.int32, (width, width), 0) // HEAD_DIM
    c = lax.broadcasted_iota(jnp.int32, (width, width), 1) // HEAD_DIM
    ones = jnp.where(r == c, 1.0, 0.0).astype(BF16)
    hi = x.astype(BF16)
    rest = x - hi.astype(F32)
    mid = rest.astype(BF16)
    lo = (rest - mid.astype(F32)).astype(BF16)
    return _dot(hi, ones) + _dot(mid, ones) + _dot(lo, ones)


def _wkv_prepare_body(r_ref, k_ref, v_ref, lw_ref, a_ref, kk_ref, ka_ref,
                      g_ref, s0c_ref, rp_ref, y0_ref, *, tb):
    r = r_ref[...]
    k = k_ref[...]
    v = v_ref[...]
    lw = lw_ref[...]
    a = a_ref[...]
    kk = k * kk_ref[...]
    norm = jnp.sqrt(_head_group_sum(kk * kk))
    kk = kk / jnp.maximum(norm, 1e-12)
    kmod = k * (1.0 + (a - 1.0) * ka_ref[...])
    avec = -kk
    bvec = kk * a

    tr = lax.broadcasted_iota(jnp.int32, (CHUNK, CHUNK), 0)
    tc = lax.broadcasted_iota(jnp.int32, (CHUNK, CHUNK), 1)
    tri = jnp.where(tr >= tc, 1.0, 0.0).astype(BF16)
    lw_hi = lw.astype(BF16)
    lw_rest = lw - lw_hi.astype(F32)
    lw_mid = lw_rest.astype(BF16)
    lw_lo = (lw_rest - lw_mid.astype(F32)).astype(BF16)

    lane = lax.broadcasted_iota(jnp.int32, (CHUNK, PAIR), 1)
    head0 = lane < HEAD_DIM
    row = lax.broadcasted_iota(jnp.int32, (PAIR, PAIR), 0)
    col = lax.broadcasted_iota(jnp.int32, (PAIR, PAIR), 1)
    strict = row > col
    lower = row >= col
    eye = jnp.where(row == col, 1.0, 0.0).astype(F32)

    def stack(x):
        return jnp.concatenate([jnp.where(head0, x, 0.0), jnp.where(head0, 0.0, x)], axis=0)

    chunks = range(tb // CHUNK)
    pre = []
    for c in chunks:
        sl = slice(c * CHUNK, (c + 1) * CHUNK)
        cm = _dot(tri, lw_hi[sl]) + _dot(tri, lw_mid[sl]) + _dot(tri, lw_lo[sl])
        cend = cm[CHUNK - 1:CHUNK]
        e_in = jnp.exp(cm)
        e_out = jnp.exp(-cm)
        e_tail = jnp.exp(cend - cm)
        a_s = stack(avec[sl] * jnp.exp(cm - lw[sl]))
        r_s = stack(r[sl] * e_in)
        b_s = stack(bvec[sl] * e_out)
        k_s = stack(kmod[sl] * e_out)
        bt_s = stack(bvec[sl] * e_tail)
        kt_s = stack(kmod[sl] * e_tail)
        v_s = stack(v[sl])
        p = _dot_nt(jnp.concatenate([a_s, r_s], axis=0).astype(BF16),
                    jnp.concatenate([b_s, k_s], axis=0).astype(BF16))
        pre.append(dict(cend=cend, a_s=a_s, r_s=r_s, bt_s=bt_s, kt_s=kt_s, v_s=v_s,
                        l_ab=jnp.where(strict, p[:PAIR, :PAIR], 0.0),
                        l_ak=jnp.where(strict, p[:PAIR, PAIR:], 0.0),
                        a_rb=jnp.where(lower, p[PAIR:, :PAIR], 0.0),
                        a_rk=jnp.where(lower, p[PAIR:, PAIR:], 0.0)))

    ms = [q["l_ab"] for q in pre]
    ts = [eye + m for m in ms]
    ms = [_dot(m.astype(BF16), m.astype(BF16)) for m in ms]
    for _ in range(4):
        both = [_dot(m.astype(BF16), jnp.concatenate([t, m], axis=1).astype(BF16)) for t, m in zip(ts, ms)]
        ts = [t + b[:, :PAIR] for t, b in zip(ts, both)]
        ms = [b[:, PAIR:] for b in both]
    ts = [t + _dot(m.astype(BF16), t.astype(BF16)) for t, m in zip(ts, ms)]

    lvs = [_dot(q["l_ak"].astype(BF16), q["v_s"].astype(BF16)) for q in pre]
    aus = [_dot(t.astype(BF16), jnp.concatenate([q["a_s"], lv], axis=1).astype(BF16))
           for t, q, lv in zip(ts, pre, lvs)]
    zero = jnp.zeros((PAIR, PAIR), F32)
    xs = [_dot(jnp.concatenate([q["a_rb"], q["a_rk"]], axis=1).astype(BF16),
               jnp.concatenate([au, jnp.concatenate([zero, q["v_s"]], axis=1)], axis=0).astype(BF16))
          for q, au in zip(pre, aus)]
    for c, q, au, x in zip(chunks, pre, aus, xs):
        rp_ref[0, c] = (q["r_s"] + x[:, :PAIR]).astype(rp_ref.dtype)
        y0_ref[0, c] = x[:, PAIR:]
        ap = au[:, :PAIR]
        u0 = au[:, PAIR:]
        g_ref[0, c] = (eye * jnp.exp(q["cend"])
                       + _dot_tn(ap.astype(BF16), q["bt_s"].astype(BF16))).astype(g_ref.dtype)
        s0c_ref[0, c] = _dot_tn(jnp.concatenate([u0, q["v_s"]], axis=0).astype(BF16),
                                jnp.concatenate([q["bt_s"], q["kt_s"]], axis=0).astype(BF16))


def wkv_chunk_prepare(proj, lw, a, k_k, k_a, n_pairs, tb=512):
    n = proj.shape[0]
    tb = min(tb, n)
    cpb = tb // CHUNK
    col = lambda off: pl.BlockSpec((tb, PAIR), lambda i, p: (i, off + p))
    par = pl.BlockSpec((1, PAIR), lambda i, p: (0, p))
    blk = pl.BlockSpec((1, cpb, PAIR, PAIR), lambda i, p: (p, i, 0, 0))
    shp = lambda dt: jax.ShapeDtypeStruct((n_pairs, n // CHUNK, PAIR, PAIR), dt)
    return pl.pallas_call(
        functools.partial(_wkv_prepare_body, tb=tb),
        grid=(n // tb, n_pairs),
        in_specs=[col(0), col(n_pairs), col(2 * n_pairs), col(0), col(0), par, par],
        out_specs=[blk] * 4,
        out_shape=[shp(BF16), shp(F32), shp(BF16), shp(F32)],
        compiler_params=_params("parallel", "parallel"),
        name="wkv_chunk_prepare",
    )(proj, proj, proj, lw, a, k_k.reshape(1, -1), k_a.reshape(1, -1))


def _wkv_scan_body(g_ref, s0c_ref, rp_ref, y0_ref, r_ref, k_ref, v_ref, a_ref, gate_ref,
                   ka_ref, rk_ref, lng_ref, lnb_ref, o_ref, s_ref, y_ref, *, cpb):
    @pl.when(pl.program_id(2) == 0)
    def _():
        s_ref[...] = jnp.zeros_like(s_ref)

    pairs = range(SCAN_PAIRS)
    states = [[s_ref[w].astype(BF16)] for w in pairs]
    for c in range(cpb):
        for w in pairs:
            s = _dot(states[w][c], g_ref[w, c].astype(BF16)) + s0c_ref[w, c]
            if c + 1 < cpb:
                states[w].append(s.astype(BF16))
            else:
                s_ref[w] = s
    for c in range(cpb):
        for w in pairs:
            y_st = _dot_nt(rp_ref[w, c].astype(BF16), states[w][c]) + y0_ref[w, c]
            y_ref[c * CHUNK:(c + 1) * CHUNK, w * PAIR:(w + 1) * PAIR] = y_st[:CHUNK] + y_st[CHUNK:]

    y = y_ref[...]
    inv = 1.0 / HEAD_DIM
    mu = _head_group_sum(y) * inv
    yc = y - mu
    var = _head_group_sum(yc * yc) * inv
    yn = yc * lax.rsqrt(var + GN_EPS) * lng_ref[...] + lnb_ref[...]
    r = r_ref[...]
    kmod = k_ref[...] * (1.0 + (a_ref[...] - 1.0) * ka_ref[...])
    bonus = _head_group_sum(r * kmod * rk_ref[...]) * v_ref[...]
    o_ref[...] = ((yn + bonus) * gate_ref[...]).astype(o_ref.dtype)


def wkv_chunk_scan(prep, proj, a, gate, k_a, r_k, ln_g, ln_b, n_pairs, seq, tb=512):
    g_all, s0c_all, rp_all, y0_all = prep
    n = proj.shape[0]
    tb = min(tb, seq)
    cpb = tb // CHUNK
    nblk = seq // tb
    assert n_pairs % SCAN_PAIRS == 0
    width = SCAN_PAIRS * PAIR
    groups = n_pairs // SCAN_PAIRS
    blk = pl.BlockSpec((SCAN_PAIRS, cpb, PAIR, PAIR), lambda b, p, t: (p, b * nblk + t, 0, 0))
    col = lambda off: pl.BlockSpec((tb, width), lambda b, p, t: (b * nblk + t, off + p))
    par = pl.BlockSpec((1, width), lambda b, p, t: (0, p))
    vec = lambda z: z.reshape(1, -1)
    return pl.pallas_call(
        functools.partial(_wkv_scan_body, cpb=cpb),
        grid=(n // seq, groups, nblk),
        in_specs=[blk] * 4 + [col(0), col(groups), col(2 * groups), col(0), col(0)] + [par] * 4,
        out_specs=col(0),
        out_shape=jax.ShapeDtypeStruct((n, n_pairs * PAIR), BF16),
        scratch_shapes=[pltpu.VMEM((SCAN_PAIRS, PAIR, PAIR), F32), pltpu.VMEM((tb, width), F32)],
        compiler_params=_params("parallel", "parallel", "arbitrary"),
        name="wkv_chunk_scan",
    )(g_all, s0c_all, rp_all, y0_all, proj, proj, proj, a, gate,
      vec(k_a), vec(r_k), vec(ln_g), vec(ln_b))


def _memattn_body(q_ref, m_ref, o_ref):
    width = MEM_HEADS * MEM_HEAD_DIM
    scale = MEM_HEAD_DIM ** -0.5
    for h in range(MEM_HEADS):
        sl = slice(h * MEM_HEAD_DIM, (h + 1) * MEM_HEAD_DIM)
        q = q_ref[:, sl].astype(BF16)
        mk = m_ref[:, sl].astype(BF16)
        mv = m_ref[:, width + h * MEM_HEAD_DIM:width + (h + 1) * MEM_HEAD_DIM].astype(BF16)
        s = _dot_nt(q, mk) * scale
        e = jnp.exp(s - jnp.max(s, axis=-1, keepdims=True))
        p = e / jnp.sum(e, axis=-1, keepdims=True)
        o_ref[:, sl] = _dot(p.astype(BF16), mv).astype(o_ref.dtype)


def memory_attention(proj, q_col_block, mkv, seq, n_mem, tm=512):
    n = proj.shape[0]
    width = MEM_HEADS * MEM_HEAD_DIM
    tm = min(tm, seq)
    nblk = seq // tm
    return pl.pallas_call(
        _memattn_body,
        grid=(n // seq, nblk),
        in_specs=[pl.BlockSpec((tm, width), lambda b, t: (b * nblk + t, q_col_block)),
                  pl.BlockSpec((n_mem, 2 * width), lambda b, t: (b, 0))],
        out_specs=pl.BlockSpec((tm, width), lambda b, t: (b * nblk + t, 0)),
        out_shape=jax.ShapeDtypeStruct((n, width), BF16),
        compiler_params=_params("parallel", "parallel"),
        name="memory_attention",
    )(proj, mkv)


def _outproj_body(s_ref, m_ref, w1_ref, w2_ref, x_ref, o_ref):
    o_ref[...] = x_ref[...] + _dot(s_ref[...], w1_ref[...]) + _dot(m_ref[...], w2_ref[...])


def out_projection(seq_out, mem_out, w_seq, w_mem, x, tm=512, tn=1024):
    n, d = x.shape
    tm, tn = min(tm, n), min(tn, d)
    ws, wm = seq_out.shape[1], mem_out.shape[1]
    return pl.pallas_call(
        _outproj_body,
        grid=(n // tm, d // tn),
        in_specs=[pl.BlockSpec((tm, ws), lambda i, j: (i, 0)),
                  pl.BlockSpec((tm, wm), lambda i, j: (i, 0)),
                  pl.BlockSpec((ws, tn), lambda i, j: (0, j)),
                  pl.BlockSpec((wm, tn), lambda i, j: (0, j)),
                  pl.BlockSpec((tm, tn), lambda i, j: (i, j))],
        out_specs=pl.BlockSpec((tm, tn), lambda i, j: (i, j)),
        out_shape=jax.ShapeDtypeStruct((n, d), F32),
        compiler_params=_params("parallel", "parallel"),
        name="out_projection",
    )(seq_out, mem_out, w_seq, w_mem, x)


def _bandattn_body(q_ref, kp_ref, kc_ref, vp_ref, vc_ref, bias_ref, o_ref, *, tq):
    qi = pl.program_id(2)
    scale = HEAD_DIM ** -0.5
    k_all = jnp.concatenate([kp_ref[...], kc_ref[...]], axis=0).astype(BF16)
    v_all = jnp.concatenate([vp_ref[...], vc_ref[...]], axis=0).astype(BF16)
    lane = lax.broadcasted_iota(jnp.int32, (CHUNK, PAIR), 1)
    head0 = lane < HEAD_DIM
    kcol = lax.broadcasted_iota(jnp.int32, (2 * CHUNK, BAND), 1)
    pad = LEFT_CHUNKS * CHUNK
    bias = bias_ref[...]
    chunks = range(tq // CHUNK)
    starts = [tq - pad + j * CHUNK for j in chunks]
    scores = []
    for j, start in zip(chunks, starts):
        q = q_ref[j * CHUNK:(j + 1) * CHUNK, :]
        q2 = jnp.concatenate([jnp.where(head0, q, 0.0), jnp.where(head0, 0.0, q)], axis=0).astype(BF16)
        scores.append(_dot_nt(q2, k_all[start:start + BAND]))
    exps = []
    for start, s in zip(starts, scores):
        valid = jnp.logical_or(qi > 0, kcol + start >= tq)
        s = jnp.where(valid, s * scale + bias, -jnp.inf)
        exps.append(jnp.exp(s - jnp.max(s, axis=-1, keepdims=True)))
    outs = [_dot(e.astype(BF16), v_all[start:start + BAND]) for start, e in zip(starts, exps)]
    for j, e, o in zip(chunks, exps, outs):
        o = o / jnp.sum(e, axis=-1, keepdims=True)
        o_ref[j * CHUNK:(j + 1) * CHUNK, :] = jnp.where(head0, o[:CHUNK], o[CHUNK:]).astype(o_ref.dtype)


def band_attention(proj, kv, bias, n_pairs, seq, tq=512):
    n = proj.shape[0]
    tq = min(tq, seq)
    assert tq >= LEFT_CHUNKS * CHUNK
    nblk = seq // tq
    cur = lambda off: pl.BlockSpec((tq, PAIR), lambda b, p, t: (b * nblk + t, off + p))
    prev = lambda off: pl.BlockSpec((tq, PAIR), lambda b, p, t: (b * nblk + jnp.maximum(t - 1, 0), off + p))
    return pl.pallas_call(
        functools.partial(_bandattn_body, tq=tq),
        grid=(n // seq, n_pairs, nblk),
        in_specs=[cur(0), prev(0), cur(0), prev(n_pairs), cur(n_pairs),
                  pl.BlockSpec((2 * CHUNK, BAND), lambda b, p, t: (p, 0))],
        out_specs=cur(0),
        out_shape=jax.ShapeDtypeStruct((n, n_pairs * PAIR), BF16),
        compiler_params=_params("parallel", "parallel", "parallel"),
        name="band_attention",
    )(proj, kv, kv, kv, kv, bias.reshape(-1, BAND))


def _top16(scores, payloads):
    nl = scores[0].shape[1]
    rids = [lax.broadcasted_iota(jnp.int32, s.shape, 0).astype(F32) for s in scores]
    slot = lax.broadcasted_iota(jnp.int32, (PEER_TOPK, nl), 0)
    tile = 8

    def winner(s, rid, payload):
        items = [(s[g:g + tile], rid[g:g + tile], None if payload is None else payload[g:g + tile])
                 for g in range(0, s.shape[0], tile)]
        while len(items) > 1:
            nxt = []
            for k in range(0, len(items) - 1, 2):
                (va, ia, pa), (vb, ib, pb) = items[k], items[k + 1]
                keep = va >= vb
                nxt.append((jnp.maximum(va, vb), jnp.where(keep, ia, ib),
                            None if pa is None else jnp.where(keep, pa, pb)))
            if len(items) % 2:
                nxt.append(items[-1])
            items = nxt
        v, idx, p = items[0]
        for shift in (4, 2, 1):
            pv, pi = pltpu.roll(v, shift, 0), pltpu.roll(idx, shift, 0)
            take = (pv > v) | ((pv == v) & (pi < idx))
            if p is not None:
                p = jnp.where(take, pltpu.roll(p, shift, 0), p)
            v, idx = jnp.where(take, pv, v), jnp.where(take, pi, idx)
        return v, idx, p

    def body(i, carry):
        sel = slot == i
        out = []
        for (s, vals, picks), rid, payload in zip(carry, rids, payloads):
            m, am, p = winner(s, rid, payload)
            hit = rid == jnp.concatenate([am] * (s.shape[0] // tile), axis=0)
            pick = am if payload is None else p
            out.append((jnp.where(hit, -jnp.inf, s), jnp.where(sel, m[0:1], vals), jnp.where(sel, pick[0:1], picks)))
        return tuple(out)

    zero = jnp.zeros((PEER_TOPK, nl), F32)
    res = lax.fori_loop(0, PEER_TOPK, body, tuple((s, zero, zero) for s in scores))
    return [(vals, picks) for _, vals, picks in res]


def _pair_candidates(a, b, combine):
    half = PEER_TOPK // 2
    rows = [combine(a[0:1], b)]
    rows += [combine(a[i:i + 1], b[:half]) for i in range(1, half)]
    rows.append(combine(a[half:], b[0:1]))
    return jnp.concatenate(rows, axis=0)


TOPK_HEADS = 2


def _peer_topk_body(q_ref, keys_ref, eidx_ref, gate_ref):
    q = q_ref[...].astype(BF16)
    scores = []
    for h in range(TOPK_HEADS):
        for half in range(2):
            col = (2 * h + half) * PEER_KEYS
            scores.append(_dot_nt(keys_ref[h, half].astype(BF16), q[:, col:col + PEER_KEYS]))
    cands, cidxs = [], []
    for h in range(TOPK_HEADS):
        (a, i1), (b, i2) = _top16(scores[2 * h:2 * h + 2], [None, None])
        cands.append(_pair_candidates(a, b, lambda x, y: x + y))
        cidxs.append(_pair_candidates(i1, i2, lambda x, y: x * PEER_KEYS + y))
    for h, (top, eidx) in enumerate(_top16(cands, cidxs)):
        rows = slice(h * PEER_TOPK, (h + 1) * PEER_TOPK)
        e = jnp.exp(top - top[0:1])
        gate_ref[rows, :] = e / jnp.sum(e, axis=0, keepdims=True)
        eidx_ref[rows, :] = eidx.astype(jnp.int32)


def peer_topk(q, keys, tl=128):
    n = q.shape[0]
    tl = min(tl, n)
    blk = pl.BlockSpec((TOPK_HEADS * PEER_TOPK, tl), lambda i, h: (h, i))
    return pl.pallas_call(
        _peer_topk_body,
        grid=(n // tl, PEER_HEADS // TOPK_HEADS),
        in_specs=[pl.BlockSpec((tl, TOPK_HEADS * 2 * PEER_KEYS), lambda i, h: (i, h)),
                  pl.BlockSpec((TOPK_HEADS, 2, PEER_KEYS, PEER_KEYS), lambda i, h: (h, 0, 0, 0))],
        out_specs=[blk, blk],
        out_shape=[jax.ShapeDtypeStruct((PEER_PICKS, n), jnp.int32),
                   jax.ShapeDtypeStruct((PEER_PICKS, n), F32)],
        compiler_params=_params("parallel", "parallel"),
        name="peer_topk",
    )(q, keys)


PEER_SLOTS = 8
LANES = 128
PITCH_PAD = 4


def _pack_body(u_ref, v_ref, o_ref):
    hi = lax.bitcast_convert_type(u_ref[0].astype(BF16).astype(F32), jnp.uint32)
    lo = lax.bitcast_convert_type(v_ref[0].astype(BF16).astype(F32), jnp.uint32)
    words = hi | (lo >> 16)
    for c in range(o_ref.shape[1]):
        o_ref[:, c, :] = words[:, c * LANES:(c + 1) * LANES]


def _pack_expert_table(u, v, layer, te=256):
    _, n_exp, d = u.shape
    chunks = d // LANES
    te = min(te, n_exp)
    packed = pl.pallas_call(
        _pack_body,
        grid=(n_exp // te,),
        in_specs=[pl.BlockSpec((1, te, d), lambda i: (layer, i, 0))] * 2,
        out_specs=pl.BlockSpec((te, chunks, LANES), lambda i: (i, 0, 0)),
        out_shape=jax.ShapeDtypeStruct((n_exp, chunks, LANES), jnp.uint32),
        compiler_params=_params("parallel"),
        name="pack_expert_table",
    )(u, v)
    return packed.reshape(n_exp * chunks, LANES)


def _peer_ffn_body(eidx_ref, gate_ref, hn_ref, x_ref, tab_ref, *rest, tb, d, out_norm):
    norm_ref, o_ref, scratch = (rest[0], rest[1], rest[2:]) if out_norm else (None, rest[0], rest[1:])
    rows_refs = scratch[:PEER_SLOTS]
    sem_ref, w_ref, stage_ref, hn_rows_ref, gate_rows_ref = scratch[PEER_SLOTS:]
    chunks = d // LANES
    pitch = chunks + PITCH_PAD

    def issue(t, slot):
        for e in range(PEER_PICKS):
            src = tab_ref.at[pl.ds(pl.multiple_of(eidx_ref[t, e] * chunks, chunks), chunks), :]
            dst = rows_refs[slot].at[pl.ds(e * pitch, chunks), :]
            pltpu.make_async_copy(src, dst, sem_ref.at[slot]).start(priority=e % 2)

    def wait(slot):
        total = PEER_PICKS * chunks
        pltpu.make_async_copy(tab_ref.at[pl.ds(0, total), :],
                              rows_refs[slot].at[pl.ds(0, total), :], sem_ref.at[slot]).wait()

    pick_diag = (lax.broadcasted_iota(jnp.int32, (PEER_PICKS, PEER_PICKS), 0)
                 == lax.broadcasted_iota(jnp.int32, (PEER_PICKS, PEER_PICKS), 1))
    hi_mask = jnp.uint32(0xFFFF0000)

    def words(slot, c):
        return rows_refs[slot][pl.ds(c, PEER_PICKS, stride=pitch), :]

    def pick_weights(slot, hn_rows, gate_rows, row):
        acc = jnp.zeros((PEER_PICKS, LANES), F32)
        for c in range(chunks):
            u = lax.bitcast_convert_type(words(slot, c) & hi_mask, F32)
            acc = acc + u * hn_rows[row:row + 1, c * LANES:(c + 1) * LANES]
        act = jnp.sum(acc, axis=1, keepdims=True)
        act = 0.5 * act * (1.0 + lax.erf(act * (2.0 ** -0.5)))
        gate = jnp.sum(jnp.where(pick_diag, gate_rows[row:row + 1, :], 0.0), axis=1, keepdims=True)
        w_ref[slot] = jnp.broadcast_to(gate * act, (PEER_PICKS, LANES))
        return acc

    def combine(slot, after):
        last = lax.bitcast_convert_type(after[PEER_PICKS - 8:], jnp.int32) == -1
        never = jnp.concatenate([last] * (PEER_PICKS // 8), axis=0)
        w = jnp.where(never, 0.0, w_ref[slot])
        for c in range(chunks):
            v = lax.bitcast_convert_type(words(slot, c) << 16, F32)
            stage_ref[slot:slot + 1, c * LANES:(c + 1) * LANES] = jnp.sum(v * w, axis=0, keepdims=True)

    def store_group(g):
        rows = pl.ds(pl.multiple_of(g * PEER_SLOTS, PEER_SLOTS), PEER_SLOTS)
        y = x_ref[rows, :] + stage_ref[...]
        if norm_ref is not None:
            y = y * lax.rsqrt(jnp.mean(y * y, axis=-1, keepdims=True) + RMS_EPS) * norm_ref[...]
        o_ref[rows, :] = y

    ahead = PEER_SLOTS - 1
    groups = tb // PEER_SLOTS
    for t in range(ahead):
        issue(t, t)
    wait(0)
    first_pass = pick_weights(0, hn_ref, gate_ref, 0)

    def group(g, carry):
        rows = pl.ds(pl.multiple_of(g * PEER_SLOTS, PEER_SLOTS), 2 * PEER_SLOTS)
        hn_rows_ref[...] = hn_ref[rows, :]
        gate_rows_ref[...] = gate_ref[rows, :]
        for s in range(PEER_SLOTS):
            t = g * PEER_SLOTS + s
            wait((s + 1) % PEER_SLOTS)
            combine(s, pick_weights((s + 1) % PEER_SLOTS, hn_rows_ref, gate_rows_ref, s + 1))
            issue(t + ahead, (s + ahead) % PEER_SLOTS)
        store_group(g)
        return carry

    lax.fori_loop(0, groups - 1, group, 0)

    for s in range(PEER_SLOTS):
        t = (groups - 1) * PEER_SLOTS + s
        if t + ahead < tb:
            issue(t + ahead, (s + ahead) % PEER_SLOTS)
        if t + 1 < tb:
            wait((s + 1) % PEER_SLOTS)
            first_pass = pick_weights((s + 1) % PEER_SLOTS, hn_ref, gate_ref, t + 1)
        combine(s, first_pass)
    store_group(groups - 1)


def peer_expert_ffn(eidx, gate, hn, x, table, tb=128, out_norm_g=None):
    n, d = x.shape
    tb = min(tb, n)
    assert tb % PEER_SLOTS == 0 and table.shape[1] == LANES
    assert table.shape[0] >= PEER_PICKS * (d // LANES)
    slot_rows = PEER_PICKS * (d // LANES + PITCH_PAD)
    out_norm = out_norm_g is not None
    in_specs = [pl.BlockSpec((tb, PEER_PICKS), lambda i: (i, 0), memory_space=pltpu.SMEM),
                pl.BlockSpec((tb, PEER_PICKS), lambda i: (i, 0)),
                pl.BlockSpec((tb, d), lambda i: (i, 0)),
                pl.BlockSpec((tb, d), lambda i: (i, 0)),
                pl.BlockSpec(memory_space=pl.ANY)]
    operands = [eidx, gate, hn, x, table]
    if out_norm:
        in_specs.append(pl.BlockSpec((1, d), lambda i: (0, 0)))
        operands.append(out_norm_g.reshape(1, d))
    return pl.pallas_call(
        functools.partial(_peer_ffn_body, tb=tb, d=d, out_norm=out_norm),
        grid=(n // tb,),
        in_specs=in_specs,
        out_specs=pl.BlockSpec((tb, d), lambda i: (i, 0)),
        out_shape=jax.ShapeDtypeStruct((n, d), F32),
        scratch_shapes=[pltpu.VMEM((slot_rows, LANES), jnp.uint32)] * PEER_SLOTS
                       + [pltpu.SemaphoreType.DMA((PEER_SLOTS,)),
                          pltpu.VMEM((PEER_SLOTS, PEER_PICKS, LANES), F32),
                          pltpu.VMEM((PEER_SLOTS, d), F32),
                          pltpu.VMEM((2 * PEER_SLOTS, d), F32),
                          pltpu.VMEM((2 * PEER_SLOTS, PEER_PICKS), F32)],
        compiler_params=_params("arbitrary"),
        name="peer_expert_ffn",
    )(*operands)


def peer_layer(x, norm_g, wq, keys, u_all, v_all, layer, tb=256, out_norm_g=None):
    q, hn = norm_matmul(x, norm_g, wq.astype(BF16), emit_hn=True)
    eidx, gate = peer_topk(q, keys)
    return peer_expert_ffn(eidx.T, gate.T, hn, x, _pack_expert_table(u_all, v_all, layer), tb=tb,
                           out_norm_g=out_norm_g)


def _band_bias(rel_bias):
    n_rel = rel_bias.shape[1]
    far = BAND - n_rel + CHUNK - 1
    long_row = jnp.concatenate([jnp.broadcast_to(rel_bias[:, n_rel - 1:], (rel_bias.shape[0], far)),
                                rel_bias[:, ::-1]], axis=1)
    rows = [long_row[:, CHUNK - 1 - i:CHUNK - 1 - i + BAND] for i in range(CHUNK)]
    return jnp.stack(rows, axis=1).astype(F32)


def kernel(x, mem, norm_mix, norm_ffn, norm_mem, w_mem_kv, w_out, peer_wq, peer_keys, peer_u, peer_v, a_mix, a_w_in, a_w0, a_w1, a_w2, a_a0, a_a1, a_a2, a_g1, a_g2, a_k_k, a_k_a, a_r_k, a_ln_g, a_ln_b, kv_norm, w_kv_shared, b_w_in, b_rel_bias, final_norm):
    bsz, seq, d = x.shape
    n = bsz * seq
    n_mem = mem.shape[1]
    seq_width = a_w0.shape[1]
    n_pairs = seq_width // PAIR
    mem_width = MEM_HEADS * MEM_HEAD_DIM
    x = x.reshape(n, d)
    mem2 = mem.reshape(bsz * n_mem, d)

    def mixer_tail(x, seq_out, proj, q_col_block, layer, out_norm_g=None):
        mkv = norm_matmul(mem2, norm_mem[layer], w_mem_kv[layer].astype(BF16), tm=256)
        mem_out = memory_attention(proj, q_col_block, mkv, seq, n_mem)
        wo = w_out[layer].astype(BF16)
        x = out_projection(seq_out, mem_out, wo[:seq_width], wo[seq_width:], x)
        return peer_layer(x, norm_ffn[layer], peer_wq[layer], peer_keys[layer], peer_u, peer_v, layer,
                          out_norm_g=out_norm_g)

    h = rmsnorm(x, norm_mix[0])
    mix_tab = jnp.concatenate([a_mix[0], jnp.zeros((1, d), F32)], axis=0)
    tiles = seq_width // 512
    gid_main = jnp.asarray([0] * tiles + [2] * tiles + [3] * tiles + [6] * (mem_width // 512), jnp.int32)
    proj = mix_matmul(h, mix_tab, gid_main, a_w_in[0].astype(BF16), seq, tn=512)
    rank = a_w1.shape[2]
    padc = lambda w: jnp.pad(w, ((0, 0), (0, LORA_PAD - rank)))
    padr = lambda w: jnp.pad(w, ((0, LORA_PAD - rank), (0, 0)))
    w_l1 = jnp.concatenate([padc(a_w1[0]), padc(a_a1[0]), a_g1[0]], axis=1).astype(BF16)
    gid_l1 = jnp.asarray([1, 4] + [5] * (a_g1.shape[2] // LORA_PAD), jnp.int32)
    t1 = mix_matmul(h, mix_tab, gid_l1, w_l1, seq, tn=LORA_PAD)
    lw, a_iclr, gate = lora_stage2(t1, padr(a_w2[0]).astype(BF16), padr(a_a2[0]).astype(BF16),
                                   a_g2[0].astype(BF16), a_w0[0], a_a0[0])
    prep = wkv_chunk_prepare(proj, lw, a_iclr, a_k_k[0], a_k_a[0], n_pairs)
    seq_out = wkv_chunk_scan(prep, proj, a_iclr, gate, a_k_a[0], a_r_k[0].reshape(-1),
                             a_ln_g[0], a_ln_b[0], n_pairs, seq)
    x = mixer_tail(x, seq_out, proj, (3 * seq_width) // mem_width, 0)

    kv = norm_matmul(x, kv_norm, w_kv_shared.astype(BF16), tm=1024)

    proj = norm_matmul(x, norm_mix[1], b_w_in[0].astype(BF16), tm=1024)
    seq_out = band_attention(proj, kv, _band_bias(b_rel_bias[0]), n_pairs, seq)
    x = mixer_tail(x, seq_out, proj, seq_width // mem_width, 1, out_norm_g=final_norm)
    return x.reshape(bsz, seq, d)
```

```python
import functools

import numpy as np
import jax
import jax.numpy as jnp
from jax import lax
from jax.experimental import pallas as pl
from jax.experimental.pallas import tpu as pltpu

F32 = jnp.float32
BF16 = jnp.bfloat16

HEAD_DIM = 64
PAIR = 2 * HEAD_DIM
CHUNK = 64
SCAN_PAIRS = 2
LEFT_CHUNKS = 8
BAND = (LEFT_CHUNKS + 1) * CHUNK
REL_MAX = 128
MEM_HEADS = 4
MEM_HEAD_DIM = 128
PEER_KEYS = 128
PEER_HEADS = 8
PEER_TOPK = 16
PEER_PICKS = PEER_HEADS * PEER_TOPK
GN_EPS = 64e-5
RMS_EPS = 1e-6
LORA_PAD = 128
VMEM_LIMIT = 48 * 1024 * 1024


def _params(*sem):
    return pltpu.CompilerParams(dimension_semantics=sem, vmem_limit_bytes=VMEM_LIMIT)


def _dot(a, b):
    return jnp.dot(a, b, preferred_element_type=F32)


def _dot_nt(a, b, precision=None):
    return lax.dot_general(a, b, (((1,), (1,)), ((), ())), precision=precision,
                           preferred_element_type=F32)


def _dot_tn(a, b, precision=None):
    return lax.dot_general(a, b, (((0,), (0,)), ((), ())), precision=precision,
                           preferred_element_type=F32)


def _rmsnorm_body(x_ref, g_ref, o_ref):
    x = x_ref[...]
    ms = jnp.mean(x * x, axis=-1, keepdims=True)
    o_ref[...] = (x * lax.rsqrt(ms + RMS_EPS) * g_ref[...]).astype(o_ref.dtype)


def rmsnorm(x, g, tm=512):
    n, d = x.shape
    tm = min(tm, n)
    return pl.pallas_call(
        _rmsnorm_body,
        grid=(n // tm,),
        in_specs=[pl.BlockSpec((tm, d), lambda i: (i, 0)), pl.BlockSpec((1, d), lambda i: (0, 0))],
        out_specs=pl.BlockSpec((tm, d), lambda i: (i, 0)),
        out_shape=jax.ShapeDtypeStruct((n, d), F32),
        compiler_params=_params("parallel"),
        name="rmsnorm",
    )(x, g.reshape(1, d))


def _normmm_body(x_ref, g_ref, w_ref, o_ref, *rest, emit_hn):
    lhs_ref = rest[-1]

    @pl.when(pl.program_id(1) == 0)
    def _():
        x = x_ref[...]
        ms = jnp.mean(x * x, axis=-1, keepdims=True)
        hn = x * lax.rsqrt(ms + RMS_EPS) * g_ref[...]
        lhs_ref[...] = hn.astype(BF16)
        if emit_hn:
            rest[0][...] = hn

    o_ref[...] = _dot(lhs_ref[...], w_ref[...]).astype(o_ref.dtype)


def norm_matmul(x, g, w_bf16, emit_hn=False, tm=512, tn=1024, out_dtype=BF16):
    n, d = x.shape
    nc = w_bf16.shape[1]
    tm, tn = min(tm, n), min(tn, nc)
    out_shape = [jax.ShapeDtypeStruct((n, nc), out_dtype)]
    out_specs = [pl.BlockSpec((tm, tn), lambda i, j: (i, j))]
    if emit_hn:
        out_shape.append(jax.ShapeDtypeStruct((n, d), F32))
        out_specs.append(pl.BlockSpec((tm, d), lambda i, j: (i, 0)))
    res = pl.pallas_call(
        functools.partial(_normmm_body, emit_hn=emit_hn),
        grid=(n // tm, nc // tn),
        in_specs=[pl.BlockSpec((tm, d), lambda i, j: (i, 0)),
                  pl.BlockSpec((1, d), lambda i, j: (0, 0)),
                  pl.BlockSpec((d, tn), lambda i, j: (0, j))],
        out_specs=out_specs,
        out_shape=out_shape,
        scratch_shapes=[pltpu.VMEM((tm, d), BF16)],
        compiler_params=_params("parallel", "arbitrary"),
        name="norm_matmul",
    )(x, g.reshape(1, d), w_bf16)
    return res if emit_hn else res[0]


def _mixmm_body(gid_ref, h_ref, hp_ref, mix_ref, w_ref, o_ref, lhs_ref, *, tm, seq):
    i = pl.program_id(0)
    j = pl.program_id(1)
    new_group = jnp.logical_or(j == 0, gid_ref[j] != gid_ref[jnp.maximum(j - 1, 0)])

    @pl.when(new_group)
    def _():
        h = h_ref[...]
        prev = jnp.where((i * tm) % seq == 0, 0.0, hp_ref[7:8, :])
        row = lax.broadcasted_iota(jnp.int32, h.shape, 0)
        shifted = jnp.where(row == 0, prev, pltpu.roll(h, 1, 0))
        lhs_ref[...] = (h + (shifted - h) * mix_ref[0]).astype(BF16)

    o_ref[...] = _dot(lhs_ref[...], w_ref[...])


def mix_matmul(h, mix_tab, gid, w_bf16, seq, tn, tm=1024):
    n, d = h.shape
    nc = w_bf16.shape[1]
    tm = min(tm, seq)
    sub = tm // 8
    grid_spec = pltpu.PrefetchScalarGridSpec(
        num_scalar_prefetch=1,
        grid=(n // tm, nc // tn),
        in_specs=[pl.BlockSpec((tm, d), lambda i, j, g: (i, 0)),
                  pl.BlockSpec((8, d), lambda i, j, g: (jnp.maximum(i * sub - 1, 0), 0)),
                  pl.BlockSpec((1, 1, d), lambda i, j, g: (g[j], 0, 0)),
                  pl.BlockSpec((d, tn), lambda i, j, g: (0, j))],
        out_specs=pl.BlockSpec((tm, tn), lambda i, j, g: (i, j)),
        scratch_shapes=[pltpu.VMEM((tm, d), BF16)],
    )
    return pl.pallas_call(
        functools.partial(_mixmm_body, tm=tm, seq=seq),
        grid_spec=grid_spec,
        out_shape=jax.ShapeDtypeStruct((n, nc), F32),
        compiler_params=_params("parallel", "arbitrary"),
        name="mix_matmul",
    )(gid, h, h, mix_tab.reshape(mix_tab.shape[0], 1, d), w_bf16)


def _sigmoid(x):
    return 1.0 / (1.0 + jnp.exp(-x))


def _lora2_body(t_ref, w2_ref, a2_ref, g2_ref, w0_ref, a0_ref, lw_ref, a_ref, g_ref):
    t = t_ref[...]
    tw = jnp.tanh(t[:, :LORA_PAD]).astype(BF16)
    ta = t[:, LORA_PAD:2 * LORA_PAD].astype(BF16)
    tg = _sigmoid(t[:, 2 * LORA_PAD:]).astype(BF16)
    u = w0_ref[...] + _dot(tw, w2_ref[...])
    softplus_neg_u = jnp.maximum(-u, 0.0) + jnp.log(1.0 + jnp.exp(-jnp.abs(u)))
    lw_ref[...] = -jnp.exp(-softplus_neg_u - 0.5)
    a_ref[...] = _sigmoid(a0_ref[...] + _dot(ta, a2_ref[...]))
    g_ref[...] = _dot(tg, g2_ref[...])


def lora_stage2(t1, w2p, a2p, g2, w0, a0, tm=256):
    n = t1.shape[0]
    width = w2p.shape[1]
    tm = min(tm, n)
    full = lambda a: pl.BlockSpec(a.shape, lambda i: (0, 0))
    row = pl.BlockSpec((tm, width), lambda i: (i, 0))
    w0 = w0.reshape(1, width)
    a0 = a0.reshape(1, width)
    return pl.pallas_call(
        _lora2_body,
        grid=(n // tm,),
        in_specs=[pl.BlockSpec((tm, t1.shape[1]), lambda i: (i, 0)), full(w2p), full(a2p), full(g2),
                  full(w0), full(a0)],
        out_specs=[row, row, row],
        out_shape=[jax.ShapeDtypeStruct((n, width), F32)] * 3,
        compiler_params=_params("parallel"),
        name="lora_stage2",
    )(t1, w2p, a2p, g2, w0, a0)


def _head_group_sum(x):
    width = x.shape[-1]
    r = lax.broadcasted_iota(jnp.int32, (width, width), 0) // HEAD_DIM
    c = lax.broadcasted_iota(jnp.int32, (width, width), 1) // HEAD_DIM
    ones = jnp.where(r == c, 1.0, 0.0).astype(BF16)
    hi = x.astype(BF16)
    rest = x - hi.astype(F32)
    mid = rest.astype(BF16)
    lo = (rest - mid.astype(F32)).astype(BF16)
    return _dot(hi, ones) + _dot(mid, ones) + _dot(lo, ones)


def _wkv_prepare_body(r_ref, k_ref, v_ref, lw_ref, a_ref, kk_ref, ka_ref,
                      g_ref, s0c_ref, rp_ref, y0_ref, *, tb):
    r = r_ref[...]
    k = k_ref[...]
    v = v_ref[...]
    lw = lw_ref[...]
    a = a_ref[...]
    kk = k * kk_ref[...]
    norm = jnp.sqrt(_head_group_sum(kk * kk))
    kk = kk / jnp.maximum(norm, 1e-12)
    kmod = k * (1.0 + (a - 1.0) * ka_ref[...])
    avec = -kk
    bvec = kk * a

    tr = lax.broadcasted_iota(jnp.int32, (CHUNK, CHUNK), 0)
    tc = lax.broadcasted_iota(jnp.int32, (CHUNK, CHUNK), 1)
    tri = jnp.where(tr >= tc, 1.0, 0.0).astype(BF16)
    lw_hi = lw.astype(BF16)
    lw_rest = lw - lw_hi.astype(F32)
    lw_mid = lw_rest.astype(BF16)
    lw_lo = (lw_rest - lw_mid.astype(F32)).astype(BF16)

    lane = lax.broadcasted_iota(jnp.int32, (CHUNK, PAIR), 1)
    head0 = lane < HEAD_DIM
    row = lax.broadcasted_iota(jnp.int32, (PAIR, PAIR), 0)
    col = lax.broadcasted_iota(jnp.int32, (PAIR, PAIR), 1)
    strict = row > col
    lower = row >= col
    eye = jnp.where(row == col, 1.0, 0.0).astype(F32)

    def stack(x):
        return jnp.concatenate([jnp.where(head0, x, 0.0), jnp.where(head0, 0.0, x)], axis=0)

    chunks = range(tb // CHUNK)
    pre = []
    for c in chunks:
        sl = slice(c * CHUNK, (c + 1) * CHUNK)
        cm = _dot(tri, lw_hi[sl]) + _dot(tri, lw_mid[sl]) + _dot(tri, lw_lo[sl])
        cend = cm[CHUNK - 1:CHUNK]
        e_in = jnp.exp(cm)
        e_out = jnp.exp(-cm)
        e_tail = jnp.exp(cend - cm)
        a_s = stack(avec[sl] * jnp.exp(cm - lw[sl]))
        r_s = stack(r[sl] * e_in)
        b_s = stack(bvec[sl] * e_out)
        k_s = stack(kmod[sl] * e_out)
        bt_s = stack(bvec[sl] * e_tail)
        kt_s = stack(kmod[sl] * e_tail)
        v_s = stack(v[sl])
        p = _dot_nt(jnp.concatenate([a_s, r_s], axis=0).astype(BF16),
                    jnp.concatenate([b_s, k_s], axis=0).astype(BF16))
        pre.append(dict(cend=cend, a_s=a_s, r_s=r_s, bt_s=bt_s, kt_s=kt_s, v_s=v_s,
                        l_ab=jnp.where(strict, p[:PAIR, :PAIR], 0.0),
                        l_ak=jnp.where(strict, p[:PAIR, PAIR:], 0.0),
                        a_rb=jnp.where(lower, p[PAIR:, :PAIR], 0.0),
                        a_rk=jnp.where(lower, p[PAIR:, PAIR:], 0.0)))

    ms = [q["l_ab"] for q in pre]
    ts = [eye + m for m in ms]
    ms = [_dot(m.astype(BF16), m.astype(BF16)) for m in ms]
    for _ in range(4):
        both = [_dot(m.astype(BF16), jnp.concatenate([t, m], axis=1).astype(BF16)) for t, m in zip(ts, ms)]
        ts = [t + b[:, :PAIR] for t, b in zip(ts, both)]
        ms = [b[:, PAIR:] for b in both]
    ts = [t + _dot(m.astype(BF16), t.astype(BF16)) for t, m in zip(ts, ms)]

    lvs = [_dot(q["l_ak"].astype(BF16), q["v_s"].astype(BF16)) for q in pre]
    aus = [_dot(t.astype(BF16), jnp.concatenate([q["a_s"], lv], axis=1).astype(BF16))
           for t, q, lv in zip(ts, pre, lvs)]
    zero = jnp.zeros((PAIR, PAIR), F32)
    xs = [_dot(jnp.concatenate([q["a_rb"], q["a_rk"]], axis=1).astype(BF16),
               jnp.concatenate([au, jnp.concatenate([zero, q["v_s"]], axis=1)], axis=0).astype(BF16))
          for q, au in zip(pre, aus)]
    for c, q, au, x in zip(chunks, pre, aus, xs):
        rp_ref[0, c] = (q["r_s"] + x[:, :PAIR]).astype(rp_ref.dtype)
        y0_ref[0, c] = x[:, PAIR:]
        ap = au[:, :PAIR]
        u0 = au[:, PAIR:]
        g_ref[0, c] = (eye * jnp.exp(q["cend"])
                       + _dot_tn(ap.astype(BF16), q["bt_s"].astype(BF16))).astype(g_ref.dtype)
        s0c_ref[0, c] = _dot_tn(jnp.concatenate([u0, q["v_s"]], axis=0).astype(BF16),
                                jnp.concatenate([q["bt_s"], q["kt_s"]], axis=0).astype(BF16))


def wkv_chunk_prepare(proj, lw, a, k_k, k_a, n_pairs, tb=512):
    n = proj.shape[0]
    tb = min(tb, n)
    cpb = tb // CHUNK
    col = lambda off: pl.BlockSpec((tb, PAIR), lambda i, p: (i, off + p))
    par = pl.BlockSpec((1, PAIR), lambda i, p: (0, p))
    blk = pl.BlockSpec((1, cpb, PAIR, PAIR), lambda i, p: (p, i, 0, 0))
    shp = lambda dt: jax.ShapeDtypeStruct((n_pairs, n // CHUNK, PAIR, PAIR), dt)
    return pl.pallas_call(
        functools.partial(_wkv_prepare_body, tb=tb),
        grid=(n // tb, n_pairs),
        in_specs=[col(0), col(n_pairs), col(2 * n_pairs), col(0), col(0), par, par],
        out_specs=[blk] * 4,
        out_shape=[shp(BF16), shp(F32), shp(BF16), shp(F32)],
        compiler_params=_params("parallel", "parallel"),
        name="wkv_chunk_prepare",
    )(proj, proj, proj, lw, a, k_k.reshape(1, -1), k_a.reshape(1, -1))


def _wkv_scan_body(g_ref, s0c_ref, rp_ref, y0_ref, r_ref, k_ref, v_ref, a_ref, gate_ref,
                   ka_ref, rk_ref, lng_ref, lnb_ref, o_ref, s_ref, y_ref, *, cpb):
    @pl.when(pl.program_id(2) == 0)
    def _():
        s_ref[...] = jnp.zeros_like(s_ref)

    pairs = range(SCAN_PAIRS)
    states = [[s_ref[w].astype(BF16)] for w in pairs]
    for c in range(cpb):
        for w in pairs:
            s = _dot(states[w][c], g_ref[w, c].astype(BF16)) + s0c_ref[w, c]
            if c + 1 < cpb:
                states[w].append(s.astype(BF16))
            else:
                s_ref[w] = s
    for c in range(cpb):
        for w in pairs:
            y_st = _dot_nt(rp_ref[w, c].astype(BF16), states[w][c]) + y0_ref[w, c]
            y_ref[c * CHUNK:(c + 1) * CHUNK, w * PAIR:(w + 1) * PAIR] = y_st[:CHUNK] + y_st[CHUNK:]

    y = y_ref[...]
    inv = 1.0 / HEAD_DIM
    mu = _head_group_sum(y) * inv
    yc = y - mu
    var = _head_group_sum(yc * yc) * inv
    yn = yc * lax.rsqrt(var + GN_EPS) * lng_ref[...] + lnb_ref[...]
    r = r_ref[...]
    kmod = k_ref[...] * (1.0 + (a_ref[...] - 1.0) * ka_ref[...])
    bonus = _head_group_sum(r * kmod * rk_ref[...]) * v_ref[...]
    o_ref[...] = ((yn + bonus) * gate_ref[...]).astype(o_ref.dtype)


def wkv_chunk_scan(prep, proj, a, gate, k_a, r_k, ln_g, ln_b, n_pairs, seq, tb=512):
    g_all, s0c_all, rp_all, y0_all = prep
    n = proj.shape[0]
    tb = min(tb, seq)
    cpb = tb // CHUNK
    nblk = seq // tb
    assert n_pairs % SCAN_PAIRS == 0
    width = SCAN_PAIRS * PAIR
    groups = n_pairs // SCAN_PAIRS
    blk = pl.BlockSpec((SCAN_PAIRS, cpb, PAIR, PAIR), lambda b, p, t: (p, b * nblk + t, 0, 0))
    col = lambda off: pl.BlockSpec((tb, width), lambda b, p, t: (b * nblk + t, off + p))
    par = pl.BlockSpec((1, width), lambda b, p, t: (0, p))
    vec = lambda z: z.reshape(1, -1)
    return pl.pallas_call(
        functools.partial(_wkv_scan_body, cpb=cpb),
        grid=(n // seq, groups, nblk),
        in_specs=[blk] * 4 + [col(0), col(groups), col(2 * groups), col(0), col(0)] + [par] * 4,
        out_specs=col(0),
        out_shape=jax.ShapeDtypeStruct((n, n_pairs * PAIR), BF16),
        scratch_shapes=[pltpu.VMEM((SCAN_PAIRS, PAIR, PAIR), F32), pltpu.VMEM((tb, width), F32)],
        compiler_params=_params("parallel", "parallel", "arbitrary"),
        name="wkv_chunk_scan",
    )(g_all, s0c_all, rp_all, y0_all, proj, proj, proj, a, gate,
      vec(k_a), vec(r_k), vec(ln_g), vec(ln_b))


def _memattn_body(q_ref, m_ref, o_ref):
    width = MEM_HEADS * MEM_HEAD_DIM
    scale = MEM_HEAD_DIM ** -0.5
    cols = [slice(h * MEM_HEAD_DIM, (h + 1) * MEM_HEAD_DIM) for h in range(MEM_HEADS)]
    scores = [_dot_nt(q_ref[:, sl].astype(BF16), m_ref[:, sl].astype(BF16)) * scale for sl in cols]
    exps = [jnp.exp(s - jnp.max(s, axis=-1, keepdims=True)) for s in scores]
    outs = [_dot(e.astype(BF16), m_ref[:, width + sl.start:width + sl.stop].astype(BF16))
            for sl, e in zip(cols, exps)]
    for sl, e, o in zip(cols, exps, outs):
        o_ref[:, sl] = (o / jnp.sum(e, axis=-1, keepdims=True)).astype(o_ref.dtype)


def memory_attention(proj, q_col_block, mkv, seq, n_mem, tm=512):
    n = proj.shape[0]
    width = MEM_HEADS * MEM_HEAD_DIM
    tm = min(tm, seq)
    nblk = seq // tm
    return pl.pallas_call(
        _memattn_body,
        grid=(n // seq, nblk),
        in_specs=[pl.BlockSpec((tm, width), lambda b, t: (b * nblk + t, q_col_block)),
                  pl.BlockSpec((n_mem, 2 * width), lambda b, t: (b, 0))],
        out_specs=pl.BlockSpec((tm, width), lambda b, t: (b * nblk + t, 0)),
        out_shape=jax.ShapeDtypeStruct((n, width), BF16),
        compiler_params=_params("parallel", "parallel"),
        name="memory_attention",
    )(proj, mkv)


def _outproj_body(s_ref, m_ref, w1_ref, w2_ref, x_ref, o_ref):
    o_ref[...] = x_ref[...] + _dot(s_ref[...], w1_ref[...]) + _dot(m_ref[...], w2_ref[...])


def out_projection(seq_out, mem_out, w_seq, w_mem, x, tm=512, tn=1024):
    n, d = x.shape
    tm, tn = min(tm, n), min(tn, d)
    ws, wm = seq_out.shape[1], mem_out.shape[1]
    return pl.pallas_call(
        _outproj_body,
        grid=(n // tm, d // tn),
        in_specs=[pl.BlockSpec((tm, ws), lambda i, j: (i, 0)),
                  pl.BlockSpec((tm, wm), lambda i, j: (i, 0)),
                  pl.BlockSpec((ws, tn), lambda i, j: (0, j)),
                  pl.BlockSpec((wm, tn), lambda i, j: (0, j)),
                  pl.BlockSpec((tm, tn), lambda i, j: (i, j))],
        out_specs=pl.BlockSpec((tm, tn), lambda i, j: (i, j)),
        out_shape=jax.ShapeDtypeStruct((n, d), F32),
        compiler_params=_params("parallel", "parallel"),
        name="out_projection",
    )(seq_out, mem_out, w_seq, w_mem, x)


def _bandattn_body(q_ref, kp_ref, kc_ref, vp_ref, vc_ref, bias_ref, o_ref, *, tq):
    qi = pl.program_id(2)
    scale = HEAD_DIM ** -0.5
    k_all = jnp.concatenate([kp_ref[...], kc_ref[...]], axis=0).astype(BF16)
    v_all = jnp.concatenate([vp_ref[...], vc_ref[...]], axis=0).astype(BF16)
    lane = lax.broadcasted_iota(jnp.int32, (CHUNK, PAIR), 1)
    head0 = lane < HEAD_DIM
    kcol = lax.broadcasted_iota(jnp.int32, (2 * CHUNK, BAND), 1)
    pad = LEFT_CHUNKS * CHUNK
    bias = bias_ref[...]
    chunks = range(tq // CHUNK)
    starts = [tq - pad + j * CHUNK for j in chunks]
    scores = []
    for j, start in zip(chunks, starts):
        q = q_ref[j * CHUNK:(j + 1) * CHUNK, :] * scale
        q2 = jnp.concatenate([jnp.where(head0, q, 0.0), jnp.where(head0, 0.0, q)], axis=0).astype(BF16)
        scores.append(_dot_nt(q2, k_all[start:start + BAND]))
    exps = []
    for start, s in zip(starts, scores):
        valid = jnp.logical_or(qi > 0, kcol + start >= tq)
        s = jnp.where(valid, s + bias, -jnp.inf)
        exps.append(jnp.exp(s - jnp.max(s, axis=-1, keepdims=True)))
    outs = [_dot(e.astype(BF16), v_all[start:start + BAND]) for start, e in zip(starts, exps)]
    for j, e, o in zip(chunks, exps, outs):
        o = o / jnp.sum(e, axis=-1, keepdims=True)
        o_ref[j * CHUNK:(j + 1) * CHUNK, :] = jnp.where(head0, o[:CHUNK], o[CHUNK:]).astype(o_ref.dtype)


def band_attention(proj, kv, bias, n_pairs, seq, tq=512):
    n = proj.shape[0]
    tq = min(tq, seq)
    assert tq >= LEFT_CHUNKS * CHUNK
    nblk = seq // tq
    cur = lambda off: pl.BlockSpec((tq, PAIR), lambda b, p, t: (b * nblk + t, off + p))
    prev = lambda off: pl.BlockSpec((tq, PAIR), lambda b, p, t: (b * nblk + jnp.maximum(t - 1, 0), off + p))
    return pl.pallas_call(
        functools.partial(_bandattn_body, tq=tq),
        grid=(n // seq, n_pairs, nblk),
        in_specs=[cur(0), prev(0), cur(0), prev(n_pairs), cur(n_pairs),
                  pl.BlockSpec((2 * CHUNK, BAND), lambda b, p, t: (p, 0))],
        out_specs=cur(0),
        out_shape=jax.ShapeDtypeStruct((n, n_pairs * PAIR), BF16),
        compiler_params=_params("parallel", "parallel", "parallel"),
        name="band_attention",
    )(proj, kv, kv, kv, kv, bias.reshape(-1, BAND))


def _top16(scores, payloads):
    nl = scores[0].shape[1]
    rids = [lax.broadcasted_iota(jnp.int32, s.shape, 0).astype(F32) for s in scores]
    slot = lax.broadcasted_iota(jnp.int32, (PEER_TOPK, nl), 0)
    tile = 8

    def winner(s, rid, payload):
        items = [(s[g:g + tile], rid[g:g + tile], None if payload is None else payload[g:g + tile])
                 for g in range(0, s.shape[0], tile)]
        while len(items) > 1:
            nxt = []
            for k in range(0, len(items) - 1, 2):
                (va, ia, pa), (vb, ib, pb) = items[k], items[k + 1]
                keep = va >= vb
                nxt.append((jnp.maximum(va, vb), jnp.where(keep, ia, ib),
                            None if pa is None else jnp.where(keep, pa, pb)))
            if len(items) % 2:
                nxt.append(items[-1])
            items = nxt
        v, idx, p = items[0]
        for shift in (4, 2, 1):
            pv, pi = pltpu.roll(v, shift, 0), pltpu.roll(idx, shift, 0)
            take = (pv > v) | ((pv == v) & (pi < idx))
            if p is not None:
                p = jnp.where(take, pltpu.roll(p, shift, 0), p)
            v, idx = jnp.where(take, pv, v), jnp.where(take, pi, idx)
        return v, idx, p

    def body(i, carry):
        sel = slot == i
        out = []
        for (s, vals, picks), rid, payload in zip(carry, rids, payloads):
            m, am, p = winner(s, rid, payload)
            hit = rid == jnp.concatenate([am] * (s.shape[0] // tile), axis=0)
            pick = am if payload is None else p
            out.append((jnp.where(hit, -jnp.inf, s), jnp.where(sel, m[0:1], vals), jnp.where(sel, pick[0:1], picks)))
        return tuple(out)

    zero = jnp.zeros((PEER_TOPK, nl), F32)
    res = lax.fori_loop(0, PEER_TOPK, body, tuple((s, zero, zero) for s in scores))
    return [(vals, picks) for _, vals, picks in res]


def _pair_candidates(a, b, combine):
    half = PEER_TOPK // 2
    rows = [combine(a[0:1], b)]
    rows += [combine(a[i:i + 1], b[:half]) for i in range(1, half)]
    rows.append(combine(a[half:], b[0:1]))
    return jnp.concatenate(rows, axis=0)


TOPK_HEADS = 2


def _peer_topk_body(q_ref, keys_ref, eidx_ref, gate_ref):
    q = q_ref[...].astype(BF16)
    scores = []
    for h in range(TOPK_HEADS):
        for half in range(2):
            col = (2 * h + half) * PEER_KEYS
            scores.append(_dot_nt(keys_ref[h, half].astype(BF16), q[:, col:col + PEER_KEYS]))
    cands, cidxs = [], []
    for h in range(TOPK_HEADS):
        (a, i1), (b, i2) = _top16(scores[2 * h:2 * h + 2], [None, None])
        cands.append(_pair_candidates(a, b, lambda x, y: x + y))
        cidxs.append(_pair_candidates(i1, i2, lambda x, y: x * PEER_KEYS + y))
    for h, (top, eidx) in enumerate(_top16(cands, cidxs)):
        rows = slice(h * PEER_TOPK, (h + 1) * PEER_TOPK)
        e = jnp.exp(top - top[0:1])
        gate_ref[rows, :] = e / jnp.sum(e, axis=0, keepdims=True)
        eidx_ref[rows, :] = eidx.astype(jnp.int32)


def peer_topk(q, keys, tl=128):
    n = q.shape[0]
    tl = min(tl, n)
    blk = pl.BlockSpec((TOPK_HEADS * PEER_TOPK, tl), lambda i, h: (h, i))
    return pl.pallas_call(
        _peer_topk_body,
        grid=(n // tl, PEER_HEADS // TOPK_HEADS),
        in_specs=[pl.BlockSpec((tl, TOPK_HEADS * 2 * PEER_KEYS), lambda i, h: (i, h)),
                  pl.BlockSpec((TOPK_HEADS, 2, PEER_KEYS, PEER_KEYS), lambda i, h: (h, 0, 0, 0))],
        out_specs=[blk, blk],
        out_shape=[jax.ShapeDtypeStruct((PEER_PICKS, n), jnp.int32),
                   jax.ShapeDtypeStruct((PEER_PICKS, n), F32)],
        compiler_params=_params("parallel", "parallel"),
        name="peer_topk",
    )(q, keys)


PEER_SLOTS = 8
LANES = 128
PITCH_PAD = 4


def _pack_body(u_ref, v_ref, o_ref):
    hi = lax.bitcast_convert_type(u_ref[0].astype(BF16).astype(F32), jnp.uint32)
    lo = lax.bitcast_convert_type(v_ref[0].astype(BF16).astype(F32), jnp.uint32)
    words = hi | (lo >> 16)
    for c in range(o_ref.shape[1]):
        o_ref[:, c, :] = words[:, c * LANES:(c + 1) * LANES]


def _pack_expert_table(u, v, layer, te=256):
    _, n_exp, d = u.shape
    chunks = d // LANES
    te = min(te, n_exp)
    packed = pl.pallas_call(
        _pack_body,
        grid=(n_exp // te,),
        in_specs=[pl.BlockSpec((1, te, d), lambda i: (layer, i, 0))] * 2,
        out_specs=pl.BlockSpec((te, chunks, LANES), lambda i: (i, 0, 0)),
        out_shape=jax.ShapeDtypeStruct((n_exp, chunks, LANES), jnp.uint32),
        compiler_params=_params("parallel"),
        name="pack_expert_table",
    )(u, v)
    return packed.reshape(n_exp * chunks, LANES)


def _peer_ffn_body(eidx_ref, gate_ref, hn_ref, x_ref, tab_ref, *rest, tb, d, out_norm):
    norm_ref, o_ref, scratch = (rest[0], rest[1], rest[2:]) if out_norm else (None, rest[0], rest[1:])
    rows_refs = scratch[:PEER_SLOTS]
    sem_ref, w_ref, stage_ref, hn_rows_ref, gate_rows_ref = scratch[PEER_SLOTS:]
    chunks = d // LANES
    pitch = chunks + PITCH_PAD

    def issue(t, slot):
        for e in range(PEER_PICKS):
            src = tab_ref.at[pl.ds(pl.multiple_of(eidx_ref[t, e] * chunks, chunks), chunks), :]
            dst = rows_refs[slot].at[pl.ds(e * pitch, chunks), :]
            pltpu.make_async_copy(src, dst, sem_ref.at[slot]).start(priority=e % 2)

    def wait(slot):
        total = PEER_PICKS * chunks
        pltpu.make_async_copy(tab_ref.at[pl.ds(0, total), :],
                              rows_refs[slot].at[pl.ds(0, total), :], sem_ref.at[slot]).wait()

    pick_diag = (lax.broadcasted_iota(jnp.int32, (PEER_PICKS, PEER_PICKS), 0)
                 == lax.broadcasted_iota(jnp.int32, (PEER_PICKS, PEER_PICKS), 1))
    hi_mask = jnp.uint32(0xFFFF0000)

    def words(slot, c):
        return rows_refs[slot][pl.ds(c, PEER_PICKS, stride=pitch), :]

    def pick_weights(slot, hn_rows, gate_rows, row):
        acc = jnp.zeros((PEER_PICKS, LANES), F32)
        for c in range(chunks):
            u = lax.bitcast_convert_type(words(slot, c) & hi_mask, F32)
            acc = acc + u * hn_rows[row:row + 1, c * LANES:(c + 1) * LANES]
        act = jnp.sum(acc, axis=1, keepdims=True)
        act = 0.5 * act * (1.0 + lax.erf(act * (2.0 ** -0.5)))
        gate = jnp.sum(jnp.where(pick_diag, gate_rows[row:row + 1, :], 0.0), axis=1, keepdims=True)
        w_ref[slot] = jnp.broadcast_to(gate * act, (PEER_PICKS, LANES))
        return acc

    def combine(slot, after):
        last = lax.bitcast_convert_type(after[PEER_PICKS - 8:], jnp.int32) == -1
        never = jnp.concatenate([last] * (PEER_PICKS // 8), axis=0)
        w = jnp.where(never, 0.0, w_ref[slot])
        for c in range(chunks):
            v = lax.bitcast_convert_type(words(slot, c) << 16, F32)
            stage_ref[slot:slot + 1, c * LANES:(c + 1) * LANES] = jnp.sum(v * w, axis=0, keepdims=True)

    def store_group(g):
        rows = pl.ds(pl.multiple_of(g * PEER_SLOTS, PEER_SLOTS), PEER_SLOTS)
        y = x_ref[rows, :] + stage_ref[...]
        if norm_ref is not None:
            y = y * lax.rsqrt(jnp.mean(y * y, axis=-1, keepdims=True) + RMS_EPS) * norm_ref[...]
        o_ref[rows, :] = y

    ahead = PEER_SLOTS - 1
    groups = tb // PEER_SLOTS
    for t in range(ahead):
        issue(t, t)
    wait(0)
    first_pass = pick_weights(0, hn_ref, gate_ref, 0)

    def group(g, carry):
        rows = pl.ds(pl.multiple_of(g * PEER_SLOTS, PEER_SLOTS), 2 * PEER_SLOTS)
        hn_rows_ref[...] = hn_ref[rows, :]
        gate_rows_ref[...] = gate_ref[rows, :]
        for s in range(PEER_SLOTS):
            t = g * PEER_SLOTS + s
            wait((s + 1) % PEER_SLOTS)
            combine(s, pick_weights((s + 1) % PEER_SLOTS, hn_rows_ref, gate_rows_ref, s + 1))
            issue(t + ahead, (s + ahead) % PEER_SLOTS)
        store_group(g)
        return carry

    lax.fori_loop(0, groups - 1, group, 0)

    for s in range(PEER_SLOTS):
        t = (groups - 1) * PEER_SLOTS + s
        if t + ahead < tb:
            issue(t + ahead, (s + ahead) % PEER_SLOTS)
        if t + 1 < tb:
            wait((s + 1) % PEER_SLOTS)
            first_pass = pick_weights((s + 1) % PEER_SLOTS, hn_ref, gate_ref, t + 1)
        combine(s, first_pass)
    store_group(groups - 1)


def peer_expert_ffn(eidx, gate, hn, x, table, tb=128, out_norm_g=None):
    n, d = x.shape
    tb = min(tb, n)
    assert tb % PEER_SLOTS == 0 and table.shape[1] == LANES
    assert table.shape[0] >= PEER_PICKS * (d // LANES)
    slot_rows = PEER_PICKS * (d // LANES + PITCH_PAD)
    out_norm = out_norm_g is not None
    in_specs = [pl.BlockSpec((tb, PEER_PICKS), lambda i: (i, 0), memory_space=pltpu.SMEM),
                pl.BlockSpec((tb, PEER_PICKS), lambda i: (i, 0)),
                pl.BlockSpec((tb, d), lambda i: (i, 0)),
                pl.BlockSpec((tb, d), lambda i: (i, 0)),
                pl.BlockSpec(memory_space=pl.ANY)]
    operands = [eidx, gate, hn, x, table]
    if out_norm:
        in_specs.append(pl.BlockSpec((1, d), lambda i: (0, 0)))
        operands.append(out_norm_g.reshape(1, d))
    return pl.pallas_call(
        functools.partial(_peer_ffn_body, tb=tb, d=d, out_norm=out_norm),
        grid=(n // tb,),
        in_specs=in_specs,
        out_specs=pl.BlockSpec((tb, d), lambda i: (i, 0)),
        out_shape=jax.ShapeDtypeStruct((n, d), F32),
        scratch_shapes=[pltpu.VMEM((slot_rows, LANES), jnp.uint32)] * PEER_SLOTS
                       + [pltpu.SemaphoreType.DMA((PEER_SLOTS,)),
                          pltpu.VMEM((PEER_SLOTS, PEER_PICKS, LANES), F32),
                          pltpu.VMEM((PEER_SLOTS, d), F32),
                          pltpu.VMEM((2 * PEER_SLOTS, d), F32),
                          pltpu.VMEM((2 * PEER_SLOTS, PEER_PICKS), F32)],
        compiler_params=_params("arbitrary"),
        name="peer_expert_ffn",
    )(*operands)


def peer_layer(x, norm_g, wq, keys, u_all, v_all, layer, tb=256, out_norm_g=None):
    q, hn = norm_matmul(x, norm_g, wq.astype(BF16), emit_hn=True)
    eidx, gate = peer_topk(q, keys)
    return peer_expert_ffn(eidx.T, gate.T, hn, x, _pack_expert_table(u_all, v_all, layer), tb=tb,
                           out_norm_g=out_norm_g)


def _band_bias(rel_bias):
    n_rel = rel_bias.shape[1]
    far = BAND - n_rel + CHUNK - 1
    long_row = jnp.concatenate([jnp.broadcast_to(rel_bias[:, n_rel - 1:], (rel_bias.shape[0], far)),
                                rel_bias[:, ::-1]], axis=1)
    rows = [long_row[:, CHUNK - 1 - i:CHUNK - 1 - i + BAND] for i in range(CHUNK)]
    return jnp.stack(rows, axis=1).astype(F32)


def kernel(x, mem, norm_mix, norm_ffn, norm_mem, w_mem_kv, w_out, peer_wq, peer_keys, peer_u, peer_v, a_mix, a_w_in, a_w0, a_w1, a_w2, a_a0, a_a1, a_a2, a_g1, a_g2, a_k_k, a_k_a, a_r_k, a_ln_g, a_ln_b, kv_norm, w_kv_shared, b_w_in, b_rel_bias, final_norm):
    bsz, seq, d = x.shape
    n = bsz * seq
    n_mem = mem.shape[1]
    seq_width = a_w0.shape[1]
    n_pairs = seq_width // PAIR
    mem_width = MEM_HEADS * MEM_HEAD_DIM
    x = x.reshape(n, d)
    mem2 = mem.reshape(bsz * n_mem, d)

    def mixer_tail(x, seq_out, proj, q_col_block, layer, out_norm_g=None):
        mkv = norm_matmul(mem2, norm_mem[layer], w_mem_kv[layer].astype(BF16), tm=256)
        mem_out = memory_attention(proj, q_col_block, mkv, seq, n_mem)
        wo = w_out[layer].astype(BF16)
        x = out_projection(seq_out, mem_out, wo[:seq_width], wo[seq_width:], x)
        return peer_layer(x, norm_ffn[layer], peer_wq[layer], peer_keys[layer], peer_u, peer_v, layer,
                          out_norm_g=out_norm_g)

    h = rmsnorm(x, norm_mix[0])
    mix_tab = jnp.concatenate([a_mix[0], jnp.zeros((1, d), F32)], axis=0)
    tiles = seq_width // 512
    gid_main = jnp.asarray([0] * tiles + [2] * tiles + [3] * tiles + [6] * (mem_width // 512), jnp.int32)
    proj = mix_matmul(h, mix_tab, gid_main, a_w_in[0].astype(BF16), seq, tn=512)
    rank = a_w1.shape[2]
    padc = lambda w: jnp.pad(w, ((0, 0), (0, LORA_PAD - rank)))
    padr = lambda w: jnp.pad(w, ((0, LORA_PAD - rank), (0, 0)))
    w_l1 = jnp.concatenate([padc(a_w1[0]), padc(a_a1[0]), a_g1[0]], axis=1).astype(BF16)
    gid_l1 = jnp.asarray([1, 4] + [5] * (a_g1.shape[2] // LORA_PAD), jnp.int32)
    t1 = mix_matmul(h, mix_tab, gid_l1, w_l1, seq, tn=LORA_PAD)
    lw, a_iclr, gate = lora_stage2(t1, padr(a_w2[0]).astype(BF16), padr(a_a2[0]).astype(BF16),
                                   a_g2[0].astype(BF16), a_w0[0], a_a0[0])
    prep = wkv_chunk_prepare(proj, lw, a_iclr, a_k_k[0], a_k_a[0], n_pairs)
    seq_out = wkv_chunk_scan(prep, proj, a_iclr, gate, a_k_a[0], a_r_k[0].reshape(-1),
                             a_ln_g[0], a_ln_b[0], n_pairs, seq)
    x = mixer_tail(x, seq_out, proj, (3 * seq_width) // mem_width, 0)

    kv = norm_matmul(x, kv_norm, w_kv_shared.astype(BF16), tm=1024)

    proj = norm_matmul(x, norm_mix[1], b_w_in[0].astype(BF16), tm=1024)
    seq_out = band_attention(proj, kv, _band_bias(b_rel_bias[0]), n_pairs, seq)
    x = mixer_tail(x, seq_out, proj, seq_width // mem_width, 1, out_norm_g=final_norm)
    return x.reshape(bsz, seq, d)
```

```python
import functools

import numpy as np
import jax
import jax.numpy as jnp
from jax import lax
from jax.experimental import pallas as pl
from jax.experimental.pallas import tpu as pltpu

F32 = jnp.float32
BF16 = jnp.bfloat16

HEAD_DIM = 64
PAIR = 2 * HEAD_DIM
CHUNK = 64
SCAN_PAIRS = 2
LEFT_CHUNKS = 8
BAND = (LEFT_CHUNKS + 1) * CHUNK
REL_MAX = 128
MEM_HEADS = 4
MEM_HEAD_DIM = 128
PEER_KEYS = 128
PEER_HEADS = 8
PEER_TOPK = 16
PEER_PICKS = PEER_HEADS * PEER_TOPK
GN_EPS = 64e-5
RMS_EPS = 1e-6
LORA_PAD = 128
VMEM_LIMIT = 48 * 1024 * 1024


def _params(*sem):
    return pltpu.CompilerParams(dimension_semantics=sem, vmem_limit_bytes=VMEM_LIMIT)


def _dot(a, b):
    return jnp.dot(a, b, preferred_element_type=F32)


def _dot_nt(a, b, precision=None):
    return lax.dot_general(a, b, (((1,), (1,)), ((), ())), precision=precision,
                           preferred_element_type=F32)


def _dot_tn(a, b, precision=None):
    return lax.dot_general(a, b, (((0,), (0,)), ((), ())), precision=precision,
                           preferred_element_type=F32)


def _rmsnorm_body(x_ref, g_ref, o_ref):
    x = x_ref[...]
    ms = jnp.mean(x * x, axis=-1, keepdims=True)
    o_ref[...] = (x * lax.rsqrt(ms + RMS_EPS) * g_ref[...]).astype(o_ref.dtype)


def rmsnorm(x, g, tm=512):
    n, d = x.shape
    tm = min(tm, n)
    return pl.pallas_call(
        _rmsnorm_body,
        grid=(n // tm,),
        in_specs=[pl.BlockSpec((tm, d), lambda i: (i, 0)), pl.BlockSpec((1, d), lambda i: (0, 0))],
        out_specs=pl.BlockSpec((tm, d), lambda i: (i, 0)),
        out_shape=jax.ShapeDtypeStruct((n, d), F32),
        compiler_params=_params("parallel"),
        name="rmsnorm",
    )(x, g.reshape(1, d))


def _normmm_body(x_ref, g_ref, w_ref, o_ref, *rest, emit_hn):
    lhs_ref = rest[-1]

    @pl.when(pl.program_id(1) == 0)
    def _():
        x = x_ref[...]
        ms = jnp.mean(x * x, axis=-1, keepdims=True)
        hn = x * lax.rsqrt(ms + RMS_EPS) * g_ref[...]
        lhs_ref[...] = hn.astype(BF16)
        if emit_hn:
            rest[0][...] = hn

    o_ref[...] = _dot(lhs_ref[...], w_ref[...]).astype(o_ref.dtype)


def norm_matmul(x, g, w_bf16, emit_hn=False, tm=512, tn=1024, out_dtype=BF16):
    n, d = x.shape
    nc = w_bf16.shape[1]
    tm, tn = min(tm, n), min(tn, nc)
    out_shape = [jax.ShapeDtypeStruct((n, nc), out_dtype)]
    out_specs = [pl.BlockSpec((tm, tn), lambda i, j: (i, j))]
    if emit_hn:
        out_shape.append(jax.ShapeDtypeStruct((n, d), F32))
        out_specs.append(pl.BlockSpec((tm, d), lambda i, j: (i, 0)))
    res = pl.pallas_call(
        functools.partial(_normmm_body, emit_hn=emit_hn),
        grid=(n // tm, nc // tn),
        in_specs=[pl.BlockSpec((tm, d), lambda i, j: (i, 0)),
                  pl.BlockSpec((1, d), lambda i, j: (0, 0)),
                  pl.BlockSpec((d, tn), lambda i, j: (0, j))],
        out_specs=out_specs,
        out_shape=out_shape,
        scratch_shapes=[pltpu.VMEM((tm, d), BF16)],
        compiler_params=_params("parallel", "arbitrary"),
        name="norm_matmul",
    )(x, g.reshape(1, d), w_bf16)
    return res if emit_hn else res[0]


def _mixmm_body(gid_ref, h_ref, hp_ref, mix_ref, w_ref, o_ref, lhs_ref, *, tm, seq):
    i = pl.program_id(0)
    j = pl.program_id(1)
    new_group = jnp.logical_or(j == 0, gid_ref[j] != gid_ref[jnp.maximum(j - 1, 0)])

    @pl.when(new_group)
    def _():
        h = h_ref[...]
        prev = jnp.where((i * tm) % seq == 0, 0.0, hp_ref[7:8, :])
        row = lax.broadcasted_iota(jnp.int32, h.shape, 0)
        shifted = jnp.where(row == 0, prev, pltpu.roll(h, 1, 0))
        lhs_ref[...] = (h + (shifted - h) * mix_ref[0]).astype(BF16)

    o_ref[...] = _dot(lhs_ref[...], w_ref[...])


def mix_matmul(h, mix_tab, gid, w_bf16, seq, tn, tm=1024):
    n, d = h.shape
    nc = w_bf16.shape[1]
    tm = min(tm, seq)
    sub = tm // 8
    grid_spec = pltpu.PrefetchScalarGridSpec(
        num_scalar_prefetch=1,
        grid=(n // tm, nc // tn),
        in_specs=[pl.BlockSpec((tm, d), lambda i, j, g: (i, 0)),
                  pl.BlockSpec((8, d), lambda i, j, g: (jnp.maximum(i * sub - 1, 0), 0)),
                  pl.BlockSpec((1, 1, d), lambda i, j, g: (g[j], 0, 0)),
                  pl.BlockSpec((d, tn), lambda i, j, g: (0, j))],
        out_specs=pl.BlockSpec((tm, tn), lambda i, j, g: (i, j)),
        scratch_shapes=[pltpu.VMEM((tm, d), BF16)],
    )
    return pl.pallas_call(
        functools.partial(_mixmm_body, tm=tm, seq=seq),
        grid_spec=grid_spec,
        out_shape=jax.ShapeDtypeStruct((n, nc), F32),
        compiler_params=_params("parallel", "arbitrary"),
        name="mix_matmul",
    )(gid, h, h, mix_tab.reshape(mix_tab.shape[0], 1, d), w_bf16)


def _sigmoid(x):
    return 1.0 / (1.0 + jnp.exp(-x))


def _lora2_body(t_ref, w2_ref, a2_ref, g2_ref, w0_ref, a0_ref, lw_ref, a_ref, g_ref):
    t = t_ref[...]
    tw = jnp.tanh(t[:, :LORA_PAD]).astype(BF16)
    ta = t[:, LORA_PAD:2 * LORA_PAD].astype(BF16)
    tg = _sigmoid(t[:, 2 * LORA_PAD:]).astype(BF16)
    u = w0_ref[...] + _dot(tw, w2_ref[...])
    softplus_neg_u = jnp.maximum(-u, 0.0) + jnp.log(1.0 + jnp.exp(-jnp.abs(u)))
    lw_ref[...] = -jnp.exp(-softplus_neg_u - 0.5)
    a_ref[...] = _sigmoid(a0_ref[...] + _dot(ta, a2_ref[...]))
    g_ref[...] = _dot(tg, g2_ref[...])


def lora_stage2(t1, w2p, a2p, g2, w0, a0, tm=256):
    n = t1.shape[0]
    width = w2p.shape[1]
    tm = min(tm, n)
    full = lambda a: pl.BlockSpec(a.shape, lambda i: (0, 0))
    row = pl.BlockSpec((tm, width), lambda i: (i, 0))
    w0 = w0.reshape(1, width)
    a0 = a0.reshape(1, width)
    return pl.pallas_call(
        _lora2_body,
        grid=(n // tm,),
        in_specs=[pl.BlockSpec((tm, t1.shape[1]), lambda i: (i, 0)), full(w2p), full(a2p), full(g2),
                  full(w0), full(a0)],
        out_specs=[row, row, row],
        out_shape=[jax.ShapeDtypeStruct((n, width), F32)] * 3,
        compiler_params=_params("parallel"),
        name="lora_stage2",
    )(t1, w2p, a2p, g2, w0, a0)


def _head_group_sum(x):
    width = x.shape[-1]
    r = lax.broadcasted_iota(jnp.int32, (width, width), 0) // HEAD_DIM
    c = lax.broadcasted_iota(jnp.int32, (width, width), 1) // HEAD_DIM
    ones = jnp.where(r == c, 1.0, 0.0).astype(BF16)
    hi = x.astype(BF16)
    rest = x - hi.astype(F32)
    mid = rest.astype(BF16)
    lo = (rest - mid.astype(F32)).astype(BF16)
    return _dot(hi, ones) + _dot(mid, ones) + _dot(lo, ones)


def _wkv_prepare_body(r_ref, k_ref, v_ref, lw_ref, a_ref, kk_ref, ka_ref,
                      g_ref, s0c_ref, rp_ref, y0_ref, *, tb):
    r = r_ref[...]
    k = k_ref[...]
    v = v_ref[...]
    lw = lw_ref[...]
    a = a_ref[...]
    kk = k * kk_ref[...]
    norm = jnp.sqrt(_head_group_sum(kk * kk))
    kk = kk / jnp.maximum(norm, 1e-12)
    kmod = k * (1.0 + (a - 1.0) * ka_ref[...])
    avec = -kk
    bvec = kk * a

    tr = lax.broadcasted_iota(jnp.int32, (CHUNK, CHUNK), 0)
    tc = lax.broadcasted_iota(jnp.int32, (CHUNK, CHUNK), 1)
    tri = jnp.where(tr >= tc, 1.0, 0.0).astype(BF16)
    lw_hi = lw.astype(BF16)
    lw_rest = lw - lw_hi.astype(F32)
    lw_mid = lw_rest.astype(BF16)
    lw_lo = (lw_rest - lw_mid.astype(F32)).astype(BF16)

    lane = lax.broadcasted_iota(jnp.int32, (CHUNK, PAIR), 1)
    head0 = lane < HEAD_DIM
    row = lax.broadcasted_iota(jnp.int32, (PAIR, PAIR), 0)
    col = lax.broadcasted_iota(jnp.int32, (PAIR, PAIR), 1)
    strict = row > col
    lower = row >= col
    eye = jnp.where(row == col, 1.0, 0.0).astype(F32)

    def stack(x):
        return jnp.concatenate([jnp.where(head0, x, 0.0), jnp.where(head0, 0.0, x)], axis=0)

    chunks = range(tb // CHUNK)
    pre = []
    for c in chunks:
        sl = slice(c * CHUNK, (c + 1) * CHUNK)
        cm = _dot(tri, lw_hi[sl]) + _dot(tri, lw_mid[sl]) + _dot(tri, lw_lo[sl])
        cend = cm[CHUNK - 1:CHUNK]
        e_in = jnp.exp(cm)
        e_out = jnp.exp(-cm)
        e_tail = jnp.exp(cend - cm)
        a_s = stack(avec[sl] * jnp.exp(cm - lw[sl]))
        r_s = stack(r[sl] * e_in)
        b_s = stack(bvec[sl] * e_out)
        k_s = stack(kmod[sl] * e_out)
        bt_s = stack(bvec[sl] * e_tail)
        kt_s = stack(kmod[sl] * e_tail)
        v_s = stack(v[sl])
        p = _dot_nt(jnp.concatenate([a_s, r_s], axis=0).astype(BF16),
                    jnp.concatenate([b_s, k_s], axis=0).astype(BF16))
        pre.append(dict(cend=cend, a_s=a_s, r_s=r_s, bt_s=bt_s, kt_s=kt_s, v_s=v_s,
                        l_ab=jnp.where(strict, p[:PAIR, :PAIR], 0.0),
                        l_ak=jnp.where(strict, p[:PAIR, PAIR:], 0.0),
                        a_rb=jnp.where(lower, p[PAIR:, :PAIR], 0.0),
                        a_rk=jnp.where(lower, p[PAIR:, PAIR:], 0.0)))

    ms = [q["l_ab"] for q in pre]
    ts = [eye + m for m in ms]
    ms = [_dot(m.astype(BF16), m.astype(BF16)) for m in ms]
    for _ in range(4):
        both = [_dot(m.astype(BF16), jnp.concatenate([t, m], axis=1).astype(BF16)) for t, m in zip(ts, ms)]
        ts = [t + b[:, :PAIR] for t, b in zip(ts, both)]
        ms = [b[:, PAIR:] for b in both]
    ts = [t + _dot(m.astype(BF16), t.astype(BF16)) for t, m in zip(ts, ms)]

    lvs = [_dot(q["l_ak"].astype(BF16), q["v_s"].astype(BF16)) for q in pre]
    aus = [_dot(t.astype(BF16), jnp.concatenate([q["a_s"], lv], axis=1).astype(BF16))
           for t, q, lv in zip(ts, pre, lvs)]
    zero = jnp.zeros((PAIR, PAIR), F32)
    xs = [_dot(jnp.concatenate([q["a_rb"], q["a_rk"]], axis=1).astype(BF16),
               jnp.concatenate([au, jnp.concatenate([zero, q["v_s"]], axis=1)], axis=0).astype(BF16))
          for q, au in zip(pre, aus)]
    for c, q, au, x in zip(chunks, pre, aus, xs):
        rp_ref[0, c] = (q["r_s"] + x[:, :PAIR]).astype(rp_ref.dtype)
        y0_ref[0, c] = x[:, PAIR:]
        ap = au[:, :PAIR]
        u0 = au[:, PAIR:]
        g_ref[0, c] = (eye * jnp.exp(q["cend"])
                       + _dot_tn(ap.astype(BF16), q["bt_s"].astype(BF16))).astype(g_ref.dtype)
        s0c_ref[0, c] = _dot_tn(jnp.concatenate([u0, q["v_s"]], axis=0).astype(BF16),
                                jnp.concatenate([q["bt_s"], q["kt_s"]], axis=0).astype(BF16))


def wkv_chunk_prepare(proj, lw, a, k_k, k_a, n_pairs, tb=1024):
    n = proj.shape[0]
    tb = min(tb, n)
    cpb = tb // CHUNK
    col = lambda off: pl.BlockSpec((tb, PAIR), lambda i, p: (i, off + p))
    par = pl.BlockSpec((1, PAIR), lambda i, p: (0, p))
    blk = pl.BlockSpec((1, cpb, PAIR, PAIR), lambda i, p: (p, i, 0, 0))
    shp = lambda dt: jax.ShapeDtypeStruct((n_pairs, n // CHUNK, PAIR, PAIR), dt)
    return pl.pallas_call(
        functools.partial(_wkv_prepare_body, tb=tb),
        grid=(n // tb, n_pairs),
        in_specs=[col(0), col(n_pairs), col(2 * n_pairs), col(0), col(0), par, par],
        out_specs=[blk] * 4,
        out_shape=[shp(BF16), shp(F32), shp(BF16), shp(F32)],
        compiler_params=_params("parallel", "parallel"),
        name="wkv_chunk_prepare",
    )(proj, proj, proj, lw, a, k_k.reshape(1, -1), k_a.reshape(1, -1))


def _wkv_scan_body(g_ref, s0c_ref, rp_ref, y0_ref, r_ref, k_ref, v_ref, a_ref, gate_ref,
                   ka_ref, rk_ref, lng_ref, lnb_ref, o_ref, s_ref, y_ref, *, cpb):
    @pl.when(pl.program_id(2) == 0)
    def _():
        s_ref[...] = jnp.zeros_like(s_ref)

    pairs = range(SCAN_PAIRS)
    states = [[s_ref[w].astype(BF16)] for w in pairs]
    for c in range(cpb):
        for w in pairs:
            s = _dot(states[w][c], g_ref[w, c].astype(BF16)) + s0c_ref[w, c]
            if c + 1 < cpb:
                states[w].append(s.astype(BF16))
            else:
                s_ref[w] = s
    for c in range(cpb):
        for w in pairs:
            y_st = _dot_nt(rp_ref[w, c].astype(BF16), states[w][c]) + y0_ref[w, c]
            y_ref[c * CHUNK:(c + 1) * CHUNK, w * PAIR:(w + 1) * PAIR] = y_st[:CHUNK] + y_st[CHUNK:]

    y = y_ref[...]
    inv = 1.0 / HEAD_DIM
    mu = _head_group_sum(y) * inv
    yc = y - mu
    var = _head_group_sum(yc * yc) * inv
    yn = yc * lax.rsqrt(var + GN_EPS) * lng_ref[...] + lnb_ref[...]
    r = r_ref[...]
    kmod = k_ref[...] * (1.0 + (a_ref[...] - 1.0) * ka_ref[...])
    bonus = _head_group_sum(r * kmod * rk_ref[...]) * v_ref[...]
    o_ref[...] = ((yn + bonus) * gate_ref[...]).astype(o_ref.dtype)


def wkv_chunk_scan(prep, proj, a, gate, k_a, r_k, ln_g, ln_b, n_pairs, seq, tb=1024):
    g_all, s0c_all, rp_all, y0_all = prep
    n = proj.shape[0]
    tb = min(tb, seq)
    cpb = tb // CHUNK
    nblk = seq // tb
    assert n_pairs % SCAN_PAIRS == 0
    width = SCAN_PAIRS * PAIR
    groups = n_pairs // SCAN_PAIRS
    blk = pl.BlockSpec((SCAN_PAIRS, cpb, PAIR, PAIR), lambda b, p, t: (p, b * nblk + t, 0, 0))
    col = lambda off: pl.BlockSpec((tb, width), lambda b, p, t: (b * nblk + t, off + p))
    par = pl.BlockSpec((1, width), lambda b, p, t: (0, p))
    vec = lambda z: z.reshape(1, -1)
    return pl.pallas_call(
        functools.partial(_wkv_scan_body, cpb=cpb),
        grid=(n // seq, groups, nblk),
        in_specs=[blk] * 4 + [col(0), col(groups), col(2 * groups), col(0), col(0)] + [par] * 4,
        out_specs=col(0),
        out_shape=jax.ShapeDtypeStruct((n, n_pairs * PAIR), BF16),
        scratch_shapes=[pltpu.VMEM((SCAN_PAIRS, PAIR, PAIR), F32), pltpu.VMEM((tb, width), F32)],
        compiler_params=_params("parallel", "parallel", "arbitrary"),
        name="wkv_chunk_scan",
    )(g_all, s0c_all, rp_all, y0_all, proj, proj, proj, a, gate,
      vec(k_a), vec(r_k), vec(ln_g), vec(ln_b))


def _memattn_body(q_ref, m_ref, o_ref):
    width = MEM_HEADS * MEM_HEAD_DIM
    scale = MEM_HEAD_DIM ** -0.5
    cols = [slice(h * MEM_HEAD_DIM, (h + 1) * MEM_HEAD_DIM) for h in range(MEM_HEADS)]
    scores = [_dot_nt(q_ref[:, sl].astype(BF16), m_ref[:, sl].astype(BF16)) * scale for sl in cols]
    exps = [jnp.exp(s - jnp.max(s, axis=-1, keepdims=True)) for s in scores]
    outs = [_dot(e.astype(BF16), m_ref[:, width + sl.start:width + sl.stop].astype(BF16))
            for sl, e in zip(cols, exps)]
    for sl, e, o in zip(cols, exps, outs):
        o_ref[:, sl] = (o / jnp.sum(e, axis=-1, keepdims=True)).astype(o_ref.dtype)


def memory_attention(proj, q_col_block, mkv, seq, n_mem, tm=512):
    n = proj.shape[0]
    width = MEM_HEADS * MEM_HEAD_DIM
    tm = min(tm, seq)
    nblk = seq // tm
    return pl.pallas_call(
        _memattn_body,
        grid=(n // seq, nblk),
        in_specs=[pl.BlockSpec((tm, width), lambda b, t: (b * nblk + t, q_col_block)),
                  pl.BlockSpec((n_mem, 2 * width), lambda b, t: (b, 0))],
        out_specs=pl.BlockSpec((tm, width), lambda b, t: (b * nblk + t, 0)),
        out_shape=jax.ShapeDtypeStruct((n, width), BF16),
        compiler_params=_params("parallel", "parallel"),
        name="memory_attention",
    )(proj, mkv)


def _outproj_body(s_ref, m_ref, w1_ref, w2_ref, x_ref, o_ref):
    o_ref[...] = x_ref[...] + _dot(s_ref[...], w1_ref[...]) + _dot(m_ref[...], w2_ref[...])


def out_projection(seq_out, mem_out, w_seq, w_mem, x, tm=512, tn=1024):
    n, d = x.shape
    tm, tn = min(tm, n), min(tn, d)
    ws, wm = seq_out.shape[1], mem_out.shape[1]
    return pl.pallas_call(
        _outproj_body,
        grid=(n // tm, d // tn),
        in_specs=[pl.BlockSpec((tm, ws), lambda i, j: (i, 0)),
                  pl.BlockSpec((tm, wm), lambda i, j: (i, 0)),
                  pl.BlockSpec((ws, tn), lambda i, j: (0, j)),
                  pl.BlockSpec((wm, tn), lambda i, j: (0, j)),
                  pl.BlockSpec((tm, tn), lambda i, j: (i, j))],
        out_specs=pl.BlockSpec((tm, tn), lambda i, j: (i, j)),
        out_shape=jax.ShapeDtypeStruct((n, d), F32),
        compiler_params=_params("parallel", "parallel"),
        name="out_projection",
    )(seq_out, mem_out, w_seq, w_mem, x)


def _bandattn_body(q_ref, kp_ref, kc_ref, vp_ref, vc_ref, bias_ref, o_ref, *, tq):
    qi = pl.program_id(2)
    scale = HEAD_DIM ** -0.5
    k_all = jnp.concatenate([kp_ref[...], kc_ref[...]], axis=0).astype(BF16)
    v_all = jnp.concatenate([vp_ref[...], vc_ref[...]], axis=0).astype(BF16)
    lane = lax.broadcasted_iota(jnp.int32, (CHUNK, PAIR), 1)
    head0 = lane < HEAD_DIM
    kcol = lax.broadcasted_iota(jnp.int32, (2 * CHUNK, BAND), 1)
    pad = LEFT_CHUNKS * CHUNK
    bias = bias_ref[...]
    chunks = range(tq // CHUNK)
    starts = [tq - pad + j * CHUNK for j in chunks]
    scores = []
    for j, start in zip(chunks, starts):
        q = q_ref[j * CHUNK:(j + 1) * CHUNK, :] * scale
        q2 = jnp.concatenate([jnp.where(head0, q, 0.0), jnp.where(head0, 0.0, q)], axis=0).astype(BF16)
        scores.append(_dot_nt(q2, k_all[start:start + BAND]))
    exps = []
    for start, s in zip(starts, scores):
        valid = jnp.logical_or(qi > 0, kcol + start >= tq)
        s = jnp.where(valid, s + bias, -jnp.inf)
        exps.append(jnp.exp(s - jnp.max(s, axis=-1, keepdims=True)))
    outs = [_dot(e.astype(BF16), v_all[start:start + BAND]) for start, e in zip(starts, exps)]
    for j, e, o in zip(chunks, exps, outs):
        o = o / jnp.sum(e, axis=-1, keepdims=True)
        o_ref[j * CHUNK:(j + 1) * CHUNK, :] = jnp.where(head0, o[:CHUNK], o[CHUNK:]).astype(o_ref.dtype)


def band_attention(proj, kv, bias, n_pairs, seq, tq=512):
    n = proj.shape[0]
    tq = min(tq, seq)
    assert tq >= LEFT_CHUNKS * CHUNK
    nblk = seq // tq
    cur = lambda off: pl.BlockSpec((tq, PAIR), lambda b, p, t: (b * nblk + t, off + p))
    prev = lambda off: pl.BlockSpec((tq, PAIR), lambda b, p, t: (b * nblk + jnp.maximum(t - 1, 0), off + p))
    return pl.pallas_call(
        functools.partial(_bandattn_body, tq=tq),
        grid=(n // seq, n_pairs, nblk),
        in_specs=[cur(0), prev(0), cur(0), prev(n_pairs), cur(n_pairs),
                  pl.BlockSpec((2 * CHUNK, BAND), lambda b, p, t: (p, 0))],
        out_specs=cur(0),
        out_shape=jax.ShapeDtypeStruct((n, n_pairs * PAIR), BF16),
        compiler_params=_params("parallel", "parallel", "parallel"),
        name="band_attention",
    )(proj, kv, kv, kv, kv, bias.reshape(-1, BAND))


def _top16(scores, payloads):
    nl = scores[0].shape[1]
    rids = [lax.broadcasted_iota(jnp.int32, s.shape, 0).astype(F32) for s in scores]
    slot = lax.broadcasted_iota(jnp.int32, (PEER_TOPK, nl), 0)
    tile = 8

    def winner(s, rid, payload):
        items = [(s[g:g + tile], rid[g:g + tile], None if payload is None else payload[g:g + tile])
                 for g in range(0, s.shape[0], tile)]
        while len(items) > 1:
            nxt = []
            for k in range(0, len(items) - 1, 2):
                (va, ia, pa), (vb, ib, pb) = items[k], items[k + 1]
                keep = va >= vb
                nxt.append((jnp.maximum(va, vb), jnp.where(keep, ia, ib),
                            None if pa is None else jnp.where(keep, pa, pb)))
            if len(items) % 2:
                nxt.append(items[-1])
            items = nxt
        v, idx, p = items[0]
        for shift in (4, 2, 1):
            pv, pi = pltpu.roll(v, shift, 0), pltpu.roll(idx, shift, 0)
            take = (pv > v) | ((pv == v) & (pi < idx))
            if p is not None:
                p = jnp.where(take, pltpu.roll(p, shift, 0), p)
            v, idx = jnp.where(take, pv, v), jnp.where(take, pi, idx)
        return v, idx, p

    def body(i, carry):
        sel = slot == i
        out = []
        for (s, vals, picks), rid, payload in zip(carry, rids, payloads):
            m, am, p = winner(s, rid, payload)
            hit = rid == jnp.concatenate([am] * (s.shape[0] // tile), axis=0)
            pick = am if payload is None else p
            out.append((jnp.where(hit, -jnp.inf, s), jnp.where(sel, m[0:1], vals), jnp.where(sel, pick[0:1], picks)))
        return tuple(out)

    zero = jnp.zeros((PEER_TOPK, nl), F32)
    res = lax.fori_loop(0, PEER_TOPK, body, tuple((s, zero, zero) for s in scores))
    return [(vals, picks) for _, vals, picks in res]


def _pair_candidates(a, b, combine):
    half = PEER_TOPK // 2
    rows = [combine(a[0:1], b)]
    rows += [combine(a[i:i + 1], b[:half]) for i in range(1, half)]
    rows.append(combine(a[half:], b[0:1]))
    return jnp.concatenate(rows, axis=0)


TOPK_HEADS = 2


def _peer_topk_body(q_ref, keys_ref, eidx_ref, gate_ref):
    q = q_ref[...].astype(BF16)
    scores = []
    for h in range(TOPK_HEADS):
        for half in range(2):
            col = (2 * h + half) * PEER_KEYS
            scores.append(_dot_nt(keys_ref[h, half].astype(BF16), q[:, col:col + PEER_KEYS]))
    cands, cidxs = [], []
    for h in range(TOPK_HEADS):
        (a, i1), (b, i2) = _top16(scores[2 * h:2 * h + 2], [None, None])
        cands.append(_pair_candidates(a, b, lambda x, y: x + y))
        cidxs.append(_pair_candidates(i1, i2, lambda x, y: x * PEER_KEYS + y))
    for h, (top, eidx) in enumerate(_top16(cands, cidxs)):
        rows = slice(h * PEER_TOPK, (h + 1) * PEER_TOPK)
        e = jnp.exp(top - top[0:1])
        gate_ref[rows, :] = e / jnp.sum(e, axis=0, keepdims=True)
        eidx_ref[rows, :] = eidx.astype(jnp.int32)


def peer_topk(q, keys, tl=128):
    n = q.shape[0]
    tl = min(tl, n)
    blk = pl.BlockSpec((TOPK_HEADS * PEER_TOPK, tl), lambda i, h: (h, i))
    return pl.pallas_call(
        _peer_topk_body,
        grid=(n // tl, PEER_HEADS // TOPK_HEADS),
        in_specs=[pl.BlockSpec((tl, TOPK_HEADS * 2 * PEER_KEYS), lambda i, h: (i, h)),
                  pl.BlockSpec((TOPK_HEADS, 2, PEER_KEYS, PEER_KEYS), lambda i, h: (h, 0, 0, 0))],
        out_specs=[blk, blk],
        out_shape=[jax.ShapeDtypeStruct((PEER_PICKS, n), jnp.int32),
                   jax.ShapeDtypeStruct((PEER_PICKS, n), F32)],
        compiler_params=_params("parallel", "parallel"),
        name="peer_topk",
    )(q, keys)


PEER_SLOTS = 8
LANES = 128
PITCH_PAD = 4


def _pack_body(u_ref, v_ref, o_ref):
    hi = lax.bitcast_convert_type(u_ref[0].astype(BF16).astype(F32), jnp.uint32)
    lo = lax.bitcast_convert_type(v_ref[0].astype(BF16).astype(F32), jnp.uint32)
    words = hi | (lo >> 16)
    for c in range(o_ref.shape[1]):
        o_ref[:, c, :] = words[:, c * LANES:(c + 1) * LANES]


def _pack_expert_table(u, v, layer, te=256):
    _, n_exp, d = u.shape
    chunks = d // LANES
    te = min(te, n_exp)
    packed = pl.pallas_call(
        _pack_body,
        grid=(n_exp // te,),
        in_specs=[pl.BlockSpec((1, te, d), lambda i: (layer, i, 0))] * 2,
        out_specs=pl.BlockSpec((te, chunks, LANES), lambda i: (i, 0, 0)),
        out_shape=jax.ShapeDtypeStruct((n_exp, chunks, LANES), jnp.uint32),
        compiler_params=_params("parallel"),
        name="pack_expert_table",
    )(u, v)
    return packed.reshape(n_exp * chunks, LANES)


def _peer_ffn_body(eidx_ref, gate_ref, hn_ref, x_ref, tab_ref, *rest, tb, d, out_norm):
    norm_ref, o_ref, scratch = (rest[0], rest[1], rest[2:]) if out_norm else (None, rest[0], rest[1:])
    rows_refs = scratch[:PEER_SLOTS]
    sem_ref, w_ref, stage_ref, hn_rows_ref, gate_rows_ref = scratch[PEER_SLOTS:]
    chunks = d // LANES
    pitch = chunks + PITCH_PAD

    def issue(t, slot):
        for e in range(PEER_PICKS):
            src = tab_ref.at[pl.ds(pl.multiple_of(eidx_ref[t, e] * chunks, chunks), chunks), :]
            dst = rows_refs[slot].at[pl.ds(e * pitch, chunks), :]
            pltpu.make_async_copy(src, dst, sem_ref.at[slot]).start(priority=e % 2)

    def wait(slot):
        total = PEER_PICKS * chunks
        pltpu.make_async_copy(tab_ref.at[pl.ds(0, total), :],
                              rows_refs[slot].at[pl.ds(0, total), :], sem_ref.at[slot]).wait()

    pick_diag = (lax.broadcasted_iota(jnp.int32, (PEER_PICKS, PEER_PICKS), 0)
                 == lax.broadcasted_iota(jnp.int32, (PEER_PICKS, PEER_PICKS), 1))
    hi_mask = jnp.uint32(0xFFFF0000)

    def words(slot, c):
        return rows_refs[slot][pl.ds(c, PEER_PICKS, stride=pitch), :]

    def pick_weights(slot, hn_rows, gate_rows, row):
        acc = jnp.zeros((PEER_PICKS, LANES), F32)
        for c in range(chunks):
            u = lax.bitcast_convert_type(words(slot, c) & hi_mask, F32)
            acc = acc + u * hn_rows[row:row + 1, c * LANES:(c + 1) * LANES]
        act = jnp.sum(acc, axis=1, keepdims=True)
        act = 0.5 * act * (1.0 + lax.erf(act * (2.0 ** -0.5)))
        gate = jnp.sum(jnp.where(pick_diag, gate_rows[row:row + 1, :], 0.0), axis=1, keepdims=True)
        w_ref[slot] = jnp.broadcast_to(gate * act, (PEER_PICKS, LANES))
        return acc

    def combine(slot, after):
        last = lax.bitcast_convert_type(after[PEER_PICKS - 8:], jnp.int32) == -1
        never = jnp.concatenate([last] * (PEER_PICKS // 8), axis=0)
        w = jnp.where(never, 0.0, w_ref[slot])
        for c in range(chunks):
            v = lax.bitcast_convert_type(words(slot, c) << 16, F32)
            stage_ref[slot:slot + 1, c * LANES:(c + 1) * LANES] = jnp.sum(v * w, axis=0, keepdims=True)

    def store_group(g):
        rows = pl.ds(pl.multiple_of(g * PEER_SLOTS, PEER_SLOTS), PEER_SLOTS)
        y = x_ref[rows, :] + stage_ref[...]
        if norm_ref is not None:
            y = y * lax.rsqrt(jnp.mean(y * y, axis=-1, keepdims=True) + RMS_EPS) * norm_ref[...]
        o_ref[rows, :] = y

    ahead = PEER_SLOTS - 1
    groups = tb // PEER_SLOTS
    for t in range(ahead):
        issue(t, t)
    wait(0)
    first_pass = pick_weights(0, hn_ref, gate_ref, 0)

    def group(g, carry):
        rows = pl.ds(pl.multiple_of(g * PEER_SLOTS, PEER_SLOTS), 2 * PEER_SLOTS)
        hn_rows_ref[...] = hn_ref[rows, :]
        gate_rows_ref[...] = gate_ref[rows, :]
        for s in range(PEER_SLOTS):
            t = g * PEER_SLOTS + s
            wait((s + 1) % PEER_SLOTS)
            combine(s, pick_weights((s + 1) % PEER_SLOTS, hn_rows_ref, gate_rows_ref, s + 1))
            issue(t + ahead, (s + ahead) % PEER_SLOTS)
        store_group(g)
        return carry

    lax.fori_loop(0, groups - 1, group, 0)

    for s in range(PEER_SLOTS):
        t = (groups - 1) * PEER_SLOTS + s
        if t + ahead < tb:
            issue(t + ahead, (s + ahead) % PEER_SLOTS)
        if t + 1 < tb:
            wait((s + 1) % PEER_SLOTS)
            first_pass = pick_weights((s + 1) % PEER_SLOTS, hn_ref, gate_ref, t + 1)
        combine(s, first_pass)
    store_group(groups - 1)


def peer_expert_ffn(eidx, gate, hn, x, table, tb=128, out_norm_g=None):
    n, d = x.shape
    tb = min(tb, n)
    assert tb % PEER_SLOTS == 0 and table.shape[1] == LANES
    assert table.shape[0] >= PEER_PICKS * (d // LANES)
    slot_rows = PEER_PICKS * (d // LANES + PITCH_PAD)
    out_norm = out_norm_g is not None
    in_specs = [pl.BlockSpec((tb, PEER_PICKS), lambda i: (i, 0), memory_space=pltpu.SMEM),
                pl.BlockSpec((tb, PEER_PICKS), lambda i: (i, 0)),
                pl.BlockSpec((tb, d), lambda i: (i, 0)),
                pl.BlockSpec((tb, d), lambda i: (i, 0)),
                pl.BlockSpec(memory_space=pl.ANY)]
    operands = [eidx, gate, hn, x, table]
    if out_norm:
        in_specs.append(pl.BlockSpec((1, d), lambda i: (0, 0)))
        operands.append(out_norm_g.reshape(1, d))
    return pl.pallas_call(
        functools.partial(_peer_ffn_body, tb=tb, d=d, out_norm=out_norm),
        grid=(n // tb,),
        in_specs=in_specs,
        out_specs=pl.BlockSpec((tb, d), lambda i: (i, 0)),
        out_shape=jax.ShapeDtypeStruct((n, d), F32),
        scratch_shapes=[pltpu.VMEM((slot_rows, LANES), jnp.uint32)] * PEER_SLOTS
                       + [pltpu.SemaphoreType.DMA((PEER_SLOTS,)),
                          pltpu.VMEM((PEER_SLOTS, PEER_PICKS, LANES), F32),
                          pltpu.VMEM((PEER_SLOTS, d), F32),
                          pltpu.VMEM((2 * PEER_SLOTS, d), F32),
                          pltpu.VMEM((2 * PEER_SLOTS, PEER_PICKS), F32)],
        compiler_params=_params("arbitrary"),
        name="peer_expert_ffn",
    )(*operands)


def peer_layer(x, norm_g, wq, keys, u_all, v_all, layer, tb=256, out_norm_g=None):
    q, hn = norm_matmul(x, norm_g, wq.astype(BF16), emit_hn=True)
    eidx, gate = peer_topk(q, keys)
    return peer_expert_ffn(eidx.T, gate.T, hn, x, _pack_expert_table(u_all, v_all, layer), tb=tb,
                           out_norm_g=out_norm_g)


def _band_bias(rel_bias):
    n_rel = rel_bias.shape[1]
    far = BAND - n_rel + CHUNK - 1
    long_row = jnp.concatenate([jnp.broadcast_to(rel_bias[:, n_rel - 1:], (rel_bias.shape[0], far)),
                                rel_bias[:, ::-1]], axis=1)
    rows = [long_row[:, CHUNK - 1 - i:CHUNK - 1 - i + BAND] for i in range(CHUNK)]
    return jnp.stack(rows, axis=1).astype(F32)


def kernel(x, mem, norm_mix, norm_ffn, norm_mem, w_mem_kv, w_out, peer_wq, peer_keys, peer_u, peer_v, a_mix, a_w_in, a_w0, a_w1, a_w2, a_a0, a_a1, a_a2, a_g1, a_g2, a_k_k, a_k_a, a_r_k, a_ln_g, a_ln_b, kv_norm, w_kv_shared, b_w_in, b_rel_bias, final_norm):
    bsz, seq, d = x.shape
    n = bsz * seq
    n_mem = mem.shape[1]
    seq_width = a_w0.shape[1]
    n_pairs = seq_width // PAIR
    mem_width = MEM_HEADS * MEM_HEAD_DIM
    x = x.reshape(n, d)
    mem2 = mem.reshape(bsz * n_mem, d)

    def mixer_tail(x, seq_out, proj, q_col_block, layer, out_norm_g=None):
        mkv = norm_matmul(mem2, norm_mem[layer], w_mem_kv[layer].astype(BF16), tm=256)
        mem_out = memory_attention(proj, q_col_block, mkv, seq, n_mem)
        wo = w_out[layer].astype(BF16)
        x = out_projection(seq_out, mem_out, wo[:seq_width], wo[seq_width:], x)
        return peer_layer(x, norm_ffn[layer], peer_wq[layer], peer_keys[layer], peer_u, peer_v, layer,
                          out_norm_g=out_norm_g)

    h = rmsnorm(x, norm_mix[0])
    mix_tab = jnp.concatenate([a_mix[0], jnp.zeros((1, d), F32)], axis=0)
    tiles = seq_width // 512
    gid_main = jnp.asarray([0] * tiles + [2] * tiles + [3] * tiles + [6] * (mem_width // 512), jnp.int32)
    proj = mix_matmul(h, mix_tab, gid_main, a_w_in[0].astype(BF16), seq, tn=512)
    rank = a_w1.shape[2]
    padc = lambda w: jnp.pad(w, ((0, 0), (0, LORA_PAD - rank)))
    padr = lambda w: jnp.pad(w, ((0, LORA_PAD - rank), (0, 0)))
    w_l1 = jnp.concatenate([padc(a_w1[0]), padc(a_a1[0]), a_g1[0]], axis=1).astype(BF16)
    gid_l1 = jnp.asarray([1, 4] + [5] * (a_g1.shape[2] // LORA_PAD), jnp.int32)
    t1 = mix_matmul(h, mix_tab, gid_l1, w_l1, seq, tn=LORA_PAD)
    lw, a_iclr, gate = lora_stage2(t1, padr(a_w2[0]).astype(BF16), padr(a_a2[0]).astype(BF16),
                                   a_g2[0].astype(BF16), a_w0[0], a_a0[0])
    prep = wkv_chunk_prepare(proj, lw, a_iclr, a_k_k[0], a_k_a[0], n_pairs)
    seq_out = wkv_chunk_scan(prep, proj, a_iclr, gate, a_k_a[0], a_r_k[0].reshape(-1),
                             a_ln_g[0], a_ln_b[0], n_pairs, seq)
    x = mixer_tail(x, seq_out, proj, (3 * seq_width) // mem_width, 0)

    kv = norm_matmul(x, kv_norm, w_kv_shared.astype(BF16), tm=1024)

    proj = norm_matmul(x, norm_mix[1], b_w_in[0].astype(BF16), tm=1024)
    seq_out = band_attention(proj, kv, _band_bias(b_rel_bias[0]), n_pairs, seq)
    x = mixer_tail(x, seq_out, proj, seq_width // mem_width, 1, out_norm_g=final_norm)
    return x.reshape(bsz, seq, d)
```

```python
import functools

import numpy as np
import jax
import jax.numpy as jnp
from jax import lax
from jax.experimental import pallas as pl
from jax.experimental.pallas import tpu as pltpu

F32 = jnp.float32
BF16 = jnp.bfloat16

HEAD_DIM = 64
PAIR = 2 * HEAD_DIM
CHUNK = 64
SCAN_PAIRS = 2
LEFT_CHUNKS = 8
BAND = (LEFT_CHUNKS + 1) * CHUNK
REL_MAX = 128
MEM_HEADS = 4
MEM_HEAD_DIM = 128
PEER_KEYS = 128
PEER_HEADS = 8
PEER_TOPK = 16
PEER_PICKS = PEER_HEADS * PEER_TOPK
GN_EPS = 64e-5
RMS_EPS = 1e-6
LORA_PAD = 128
VMEM_LIMIT = 48 * 1024 * 1024


def _params(*sem):
    return pltpu.CompilerParams(dimension_semantics=sem, vmem_limit_bytes=VMEM_LIMIT)


def _dot(a, b):
    return jnp.dot(a, b, preferred_element_type=F32)


def _dot_nt(a, b, precision=None):
    return lax.dot_general(a, b, (((1,), (1,)), ((), ())), precision=precision,
                           preferred_element_type=F32)


def _dot_tn(a, b, precision=None):
    return lax.dot_general(a, b, (((0,), (0,)), ((), ())), precision=precision,
                           preferred_element_type=F32)


def _rmsnorm_body(x_ref, g_ref, o_ref):
    x = x_ref[...]
    ms = jnp.mean(x * x, axis=-1, keepdims=True)
    o_ref[...] = (x * lax.rsqrt(ms + RMS_EPS) * g_ref[...]).astype(o_ref.dtype)


def rmsnorm(x, g, tm=512):
    n, d = x.shape
    tm = min(tm, n)
    return pl.pallas_call(
        _rmsnorm_body,
        grid=(n // tm,),
        in_specs=[pl.BlockSpec((tm, d), lambda i: (i, 0)), pl.BlockSpec((1, d), lambda i: (0, 0))],
        out_specs=pl.BlockSpec((tm, d), lambda i: (i, 0)),
        out_shape=jax.ShapeDtypeStruct((n, d), F32),
        compiler_params=_params("parallel"),
        name="rmsnorm",
    )(x, g.reshape(1, d))


def _normmm_body(x_ref, g_ref, w_ref, o_ref, *rest, emit_hn):
    lhs_ref = rest[-1]

    @pl.when(pl.program_id(1) == 0)
    def _():
        x = x_ref[...]
        ms = jnp.mean(x * x, axis=-1, keepdims=True)
        hn = x * lax.rsqrt(ms + RMS_EPS) * g_ref[...]
        lhs_ref[...] = hn.astype(BF16)
        if emit_hn:
            rest[0][...] = hn

    o_ref[...] = _dot(lhs_ref[...], w_ref[...]).astype(o_ref.dtype)


def norm_matmul(x, g, w_bf16, emit_hn=False, tm=512, tn=1024, out_dtype=BF16):
    n, d = x.shape
    nc = w_bf16.shape[1]
    tm, tn = min(tm, n), min(tn, nc)
    out_shape = [jax.ShapeDtypeStruct((n, nc), out_dtype)]
    out_specs = [pl.BlockSpec((tm, tn), lambda i, j: (i, j))]
    if emit_hn:
        out_shape.append(jax.ShapeDtypeStruct((n, d), F32))
        out_specs.append(pl.BlockSpec((tm, d), lambda i, j: (i, 0)))
    res = pl.pallas_call(
        functools.partial(_normmm_body, emit_hn=emit_hn),
        grid=(n // tm, nc // tn),
        in_specs=[pl.BlockSpec((tm, d), lambda i, j: (i, 0)),
                  pl.BlockSpec((1, d), lambda i, j: (0, 0)),
                  pl.BlockSpec((d, tn), lambda i, j: (0, j))],
        out_specs=out_specs,
        out_shape=out_shape,
        scratch_shapes=[pltpu.VMEM((tm, d), BF16)],
        compiler_params=_params("parallel", "arbitrary"),
        name="norm_matmul",
    )(x, g.reshape(1, d), w_bf16)
    return res if emit_hn else res[0]


def _mixmm_body(gid_ref, h_ref, hp_ref, mix_ref, w_ref, o_ref, lhs_ref, diff_ref, *, tm, seq):
    i = pl.program_id(0)
    j = pl.program_id(1)

    @pl.when(j == 0)
    def _():
        h = h_ref[...]
        prev = jnp.where((i * tm) % seq == 0, 0.0, hp_ref[7:8, :])
        row = lax.broadcasted_iota(jnp.int32, h.shape, 0)
        diff_ref[...] = jnp.where(row == 0, prev, pltpu.roll(h, 1, 0)) - h

    new_group = jnp.logical_or(j == 0, gid_ref[j] != gid_ref[jnp.maximum(j - 1, 0)])

    @pl.when(new_group)
    def _():
        lhs_ref[...] = (h_ref[...] + diff_ref[...] * mix_ref[0]).astype(BF16)

    o_ref[...] = _dot(lhs_ref[...], w_ref[...])


def mix_matmul(h, mix_tab, gid, w_bf16, seq, tn, tm=1024):
    n, d = h.shape
    nc = w_bf16.shape[1]
    tm = min(tm, seq)
    sub = tm // 8
    grid_spec = pltpu.PrefetchScalarGridSpec(
        num_scalar_prefetch=1,
        grid=(n // tm, nc // tn),
        in_specs=[pl.BlockSpec((tm, d), lambda i, j, g: (i, 0)),
                  pl.BlockSpec((8, d), lambda i, j, g: (jnp.maximum(i * sub - 1, 0), 0)),
                  pl.BlockSpec((1, 1, d), lambda i, j, g: (g[j], 0, 0)),
                  pl.BlockSpec((d, tn), lambda i, j, g: (0, j))],
        out_specs=pl.BlockSpec((tm, tn), lambda i, j, g: (i, j)),
        scratch_shapes=[pltpu.VMEM((tm, d), BF16), pltpu.VMEM((tm, d), F32)],
    )
    return pl.pallas_call(
        functools.partial(_mixmm_body, tm=tm, seq=seq),
        grid_spec=grid_spec,
        out_shape=jax.ShapeDtypeStruct((n, nc), F32),
        compiler_params=_params("parallel", "arbitrary"),
        name="mix_matmul",
    )(gid, h, h, mix_tab.reshape(mix_tab.shape[0], 1, d), w_bf16)


def _sigmoid(x):
    return 1.0 / (1.0 + jnp.exp(-x))


def _lora2_body(t_ref, w2_ref, a2_ref, g2_ref, w0_ref, a0_ref, lw_ref, a_ref, g_ref):
    t = t_ref[...]
    tw = jnp.tanh(t[:, :LORA_PAD]).astype(BF16)
    ta = t[:, LORA_PAD:2 * LORA_PAD].astype(BF16)
    tg = _sigmoid(t[:, 2 * LORA_PAD:]).astype(BF16)
    u = w0_ref[...] + _dot(tw, w2_ref[...])
    softplus_neg_u = jnp.maximum(-u, 0.0) + jnp.log(1.0 + jnp.exp(-jnp.abs(u)))
    lw_ref[...] = -jnp.exp(-softplus_neg_u - 0.5)
    a_ref[...] = _sigmoid(a0_ref[...] + _dot(ta, a2_ref[...]))
    g_ref[...] = _dot(tg, g2_ref[...])


def lora_stage2(t1, w2p, a2p, g2, w0, a0, tm=256):
    n = t1.shape[0]
    width = w2p.shape[1]
    tm = min(tm, n)
    full = lambda a: pl.BlockSpec(a.shape, lambda i: (0, 0))
    row = pl.BlockSpec((tm, width), lambda i: (i, 0))
    w0 = w0.reshape(1, width)
    a0 = a0.reshape(1, width)
    return pl.pallas_call(
        _lora2_body,
        grid=(n // tm,),
        in_specs=[pl.BlockSpec((tm, t1.shape[1]), lambda i: (i, 0)), full(w2p), full(a2p), full(g2),
                  full(w0), full(a0)],
        out_specs=[row, row, row],
        out_shape=[jax.ShapeDtypeStruct((n, width), F32)] * 3,
        compiler_params=_params("parallel"),
        name="lora_stage2",
    )(t1, w2p, a2p, g2, w0, a0)


def _head_group_sum(x):
    width = x.shape[-1]
    r = lax.broadcasted_iota(jnp.int32, (width, width), 0) // HEAD_DIM
    c = lax.broadcasted_iota(jnp.int32, (width, width), 1) // HEAD_DIM
    ones = jnp.where(r == c, 1.0, 0.0).astype(BF16)
    hi = x.astype(BF16)
    rest = x - hi.astype(F32)
    mid = rest.astype(BF16)
    lo = (rest - mid.astype(F32)).astype(BF16)
    return _dot(hi, ones) + _dot(mid, ones) + _dot(lo, ones)


def _wkv_prepare_body(r_ref, k_ref, v_ref, lw_ref, a_ref, kk_ref, ka_ref,
                      g_ref, s0c_ref, rp_ref, y0_ref, *, tb):
    r = r_ref[...]
    k = k_ref[...]
    v = v_ref[...]
    lw = lw_ref[...]
    a = a_ref[...]
    kk = k * kk_ref[...]
    norm = jnp.sqrt(_head_group_sum(kk * kk))
    kk = kk / jnp.maximum(norm, 1e-12)
    kmod = k * (1.0 + (a - 1.0) * ka_ref[...])
    avec = -kk
    bvec = kk * a

    tr = lax.broadcasted_iota(jnp.int32, (CHUNK, CHUNK), 0)
    tc = lax.broadcasted_iota(jnp.int32, (CHUNK, CHUNK), 1)
    tri = jnp.where(tr >= tc, 1.0, 0.0).astype(BF16)
    lw_hi = lw.astype(BF16)
    lw_rest = lw - lw_hi.astype(F32)
    lw_mid = lw_rest.astype(BF16)
    lw_lo = (lw_rest - lw_mid.astype(F32)).astype(BF16)

    lane = lax.broadcasted_iota(jnp.int32, (CHUNK, PAIR), 1)
    head0 = lane < HEAD_DIM
    row = lax.broadcasted_iota(jnp.int32, (PAIR, PAIR), 0)
    col = lax.broadcasted_iota(jnp.int32, (PAIR, PAIR), 1)
    strict = row > col
    lower = row >= col
    eye = jnp.where(row == col, 1.0, 0.0).astype(F32)

    def stack(x):
        return jnp.concatenate([jnp.where(head0, x, 0.0), jnp.where(head0, 0.0, x)], axis=0)

    chunks = range(tb // CHUNK)
    pre = []
    for c in chunks:
        sl = slice(c * CHUNK, (c + 1) * CHUNK)
        cm = _dot(tri, lw_hi[sl]) + _dot(tri, lw_mid[sl]) + _dot(tri, lw_lo[sl])
        cend = cm[CHUNK - 1:CHUNK]
        e_in = jnp.exp(cm)
        e_out = jnp.exp(-cm)
        e_tail = jnp.exp(cend - cm)
        a_s = stack(avec[sl] * jnp.exp(cm - lw[sl]))
        r_s = stack(r[sl] * e_in)
        b_s = stack(bvec[sl] * e_out)
        k_s = stack(kmod[sl] * e_out)
        bt_s = stack(bvec[sl] * e_tail)
        kt_s = stack(kmod[sl] * e_tail)
        v_s = stack(v[sl])
        p = _dot_nt(jnp.concatenate([a_s, r_s], axis=0).astype(BF16),
                    jnp.concatenate([b_s, k_s], axis=0).astype(BF16))
        pre.append(dict(cend=cend, a_s=a_s, r_s=r_s, bt_s=bt_s, kt_s=kt_s, v_s=v_s,
                        l_ab=jnp.where(strict, p[:PAIR, :PAIR], 0.0),
                        l_ak=jnp.where(strict, p[:PAIR, PAIR:], 0.0),
                        a_rb=jnp.where(lower, p[PAIR:, :PAIR], 0.0),
                        a_rk=jnp.where(lower, p[PAIR:, PAIR:], 0.0)))

    ms = [q["l_ab"] for q in pre]
    ts = [eye + m for m in ms]
    ms = [_dot(m.astype(BF16), m.astype(BF16)) for m in ms]
    for _ in range(4):
        both = [_dot(m.astype(BF16), jnp.concatenate([t, m], axis=1).astype(BF16)) for t, m in zip(ts, ms)]
        ts = [t + b[:, :PAIR] for t, b in zip(ts, both)]
        ms = [b[:, PAIR:] for b in both]
    ts = [t + _dot(m.astype(BF16), t.astype(BF16)) for t, m in zip(ts, ms)]

    lvs = [_dot(q["l_ak"].astype(BF16), q["v_s"].astype(BF16)) for q in pre]
    aus = [_dot(t.astype(BF16), jnp.concatenate([q["a_s"], lv], axis=1).astype(BF16))
           for t, q, lv in zip(ts, pre, lvs)]
    zero = jnp.zeros((PAIR, PAIR), F32)
    xs = [_dot(jnp.concatenate([q["a_rb"], q["a_rk"]], axis=1).astype(BF16),
               jnp.concatenate([au, jnp.concatenate([zero, q["v_s"]], axis=1)], axis=0).astype(BF16))
          for q, au in zip(pre, aus)]
    for c, q, au, x in zip(chunks, pre, aus, xs):
        rp_ref[0, c] = (q["r_s"] + x[:, :PAIR]).astype(rp_ref.dtype)
        y0_ref[0, c] = x[:, PAIR:]
        ap = au[:, :PAIR]
        u0 = au[:, PAIR:]
        g_ref[0, c] = (eye * jnp.exp(q["cend"])
                       + _dot_tn(ap.astype(BF16), q["bt_s"].astype(BF16))).astype(g_ref.dtype)
        s0c_ref[0, c] = _dot_tn(jnp.concatenate([u0, q["v_s"]], axis=0).astype(BF16),
                                jnp.concatenate([q["bt_s"], q["kt_s"]], axis=0).astype(BF16))


def wkv_chunk_prepare(proj, lw, a, k_k, k_a, n_pairs, tb=1024):
    n = proj.shape[0]
    tb = min(tb, n)
    cpb = tb // CHUNK
    col = lambda off: pl.BlockSpec((tb, PAIR), lambda i, p: (i, off + p))
    par = pl.BlockSpec((1, PAIR), lambda i, p: (0, p))
    blk = pl.BlockSpec((1, cpb, PAIR, PAIR), lambda i, p: (p, i, 0, 0))
    shp = lambda dt: jax.ShapeDtypeStruct((n_pairs, n // CHUNK, PAIR, PAIR), dt)
    return pl.pallas_call(
        functools.partial(_wkv_prepare_body, tb=tb),
        grid=(n // tb, n_pairs),
        in_specs=[col(0), col(n_pairs), col(2 * n_pairs), col(0), col(0), par, par],
        out_specs=[blk] * 4,
        out_shape=[shp(BF16), shp(F32), shp(BF16), shp(F32)],
        compiler_params=_params("parallel", "parallel"),
        name="wkv_chunk_prepare",
    )(proj, proj, proj, lw, a, k_k.reshape(1, -1), k_a.reshape(1, -1))


def _wkv_scan_body(g_ref, s0c_ref, rp_ref, y0_ref, r_ref, k_ref, v_ref, a_ref, gate_ref,
                   ka_ref, rk_ref, lng_ref, lnb_ref, o_ref, s_ref, y_ref, *, cpb):
    @pl.when(pl.program_id(2) == 0)
    def _():
        s_ref[...] = jnp.zeros_like(s_ref)

    pairs = range(SCAN_PAIRS)
    states = [[s_ref[w].astype(BF16)] for w in pairs]
    for c in range(cpb):
        for w in pairs:
            s = _dot(states[w][c], g_ref[w, c].astype(BF16)) + s0c_ref[w, c]
            if c + 1 < cpb:
                states[w].append(s.astype(BF16))
            else:
                s_ref[w] = s
    for c in range(cpb):
        for w in pairs:
            y_st = _dot_nt(rp_ref[w, c].astype(BF16), states[w][c]) + y0_ref[w, c]
            y_ref[c * CHUNK:(c + 1) * CHUNK, w * PAIR:(w + 1) * PAIR] = y_st[:CHUNK] + y_st[CHUNK:]

    y = y_ref[...]
    inv = 1.0 / HEAD_DIM
    mu = _head_group_sum(y) * inv
    yc = y - mu
    var = _head_group_sum(yc * yc) * inv
    yn = yc * lax.rsqrt(var + GN_EPS) * lng_ref[...] + lnb_ref[...]
    r = r_ref[...]
    kmod = k_ref[...] * (1.0 + (a_ref[...] - 1.0) * ka_ref[...])
    bonus = _head_group_sum(r * kmod * rk_ref[...]) * v_ref[...]
    o_ref[...] = ((yn + bonus) * gate_ref[...]).astype(o_ref.dtype)


def wkv_chunk_scan(prep, proj, a, gate, k_a, r_k, ln_g, ln_b, n_pairs, seq, tb=1024):
    g_all, s0c_all, rp_all, y0_all = prep
    n = proj.shape[0]
    tb = min(tb, seq)
    cpb = tb // CHUNK
    nblk = seq // tb
    assert n_pairs % SCAN_PAIRS == 0
    width = SCAN_PAIRS * PAIR
    groups = n_pairs // SCAN_PAIRS
    blk = pl.BlockSpec((SCAN_PAIRS, cpb, PAIR, PAIR), lambda b, p, t: (p, b * nblk + t, 0, 0))
    col = lambda off: pl.BlockSpec((tb, width), lambda b, p, t: (b * nblk + t, off + p))
    par = pl.BlockSpec((1, width), lambda b, p, t: (0, p))
    vec = lambda z: z.reshape(1, -1)
    return pl.pallas_call(
        functools.partial(_wkv_scan_body, cpb=cpb),
        grid=(n // seq, groups, nblk),
        in_specs=[blk] * 4 + [col(0), col(groups), col(2 * groups), col(0), col(0)] + [par] * 4,
        out_specs=col(0),
        out_shape=jax.ShapeDtypeStruct((n, n_pairs * PAIR), BF16),
        scratch_shapes=[pltpu.VMEM((SCAN_PAIRS, PAIR, PAIR), F32), pltpu.VMEM((tb, width), F32)],
        compiler_params=_params("parallel", "parallel", "arbitrary"),
        name="wkv_chunk_scan",
    )(g_all, s0c_all, rp_all, y0_all, proj, proj, proj, a, gate,
      vec(k_a), vec(r_k), vec(ln_g), vec(ln_b))


def _memattn_body(q_ref, m_ref, o_ref):
    width = MEM_HEADS * MEM_HEAD_DIM
    scale = MEM_HEAD_DIM ** -0.5
    cols = [slice(h * MEM_HEAD_DIM, (h + 1) * MEM_HEAD_DIM) for h in range(MEM_HEADS)]
    scores = [_dot_nt(q_ref[:, sl].astype(BF16), m_ref[:, sl].astype(BF16)) * scale for sl in cols]
    exps = [jnp.exp(s - jnp.max(s, axis=-1, keepdims=True)) for s in scores]
    outs = [_dot(e.astype(BF16), m_ref[:, width + sl.start:width + sl.stop].astype(BF16))
            for sl, e in zip(cols, exps)]
    for sl, e, o in zip(cols, exps, outs):
        o_ref[:, sl] = (o / jnp.sum(e, axis=-1, keepdims=True)).astype(o_ref.dtype)


def memory_attention(proj, q_col_block, mkv, seq, n_mem, tm=512):
    n = proj.shape[0]
    width = MEM_HEADS * MEM_HEAD_DIM
    tm = min(tm, seq)
    nblk = seq // tm
    return pl.pallas_call(
        _memattn_body,
        grid=(n // seq, nblk),
        in_specs=[pl.BlockSpec((tm, width), lambda b, t: (b * nblk + t, q_col_block)),
                  pl.BlockSpec((n_mem, 2 * width), lambda b, t: (b, 0))],
        out_specs=pl.BlockSpec((tm, width), lambda b, t: (b * nblk + t, 0)),
        out_shape=jax.ShapeDtypeStruct((n, width), BF16),
        compiler_params=_params("parallel", "parallel"),
        name="memory_attention",
    )(proj, mkv)


def _outproj_body(s_ref, m_ref, w1_ref, w2_ref, x_ref, o_ref):
    o_ref[...] = x_ref[...] + _dot(s_ref[...], w1_ref[...]) + _dot(m_ref[...], w2_ref[...])


def out_projection(seq_out, mem_out, w_seq, w_mem, x, tm=512, tn=1024):
    n, d = x.shape
    tm, tn = min(tm, n), min(tn, d)
    ws, wm = seq_out.shape[1], mem_out.shape[1]
    return pl.pallas_call(
        _outproj_body,
        grid=(n // tm, d // tn),
        in_specs=[pl.BlockSpec((tm, ws), lambda i, j: (i, 0)),
                  pl.BlockSpec((tm, wm), lambda i, j: (i, 0)),
                  pl.BlockSpec((ws, tn), lambda i, j: (0, j)),
                  pl.BlockSpec((wm, tn), lambda i, j: (0, j)),
                  pl.BlockSpec((tm, tn), lambda i, j: (i, j))],
        out_specs=pl.BlockSpec((tm, tn), lambda i, j: (i, j)),
        out_shape=jax.ShapeDtypeStruct((n, d), F32),
        compiler_params=_params("parallel", "parallel"),
        name="out_projection",
    )(seq_out, mem_out, w_seq, w_mem, x)


def _bandattn_body(q_ref, kp_ref, kc_ref, vp_ref, vc_ref, bias_ref, o_ref, *, tq):
    qi = pl.program_id(2)
    scale = HEAD_DIM ** -0.5
    k_all = jnp.concatenate([kp_ref[...], kc_ref[...]], axis=0).astype(BF16)
    v_all = jnp.concatenate([vp_ref[...], vc_ref[...]], axis=0).astype(BF16)
    lane = lax.broadcasted_iota(jnp.int32, (CHUNK, PAIR), 1)
    head0 = lane < HEAD_DIM
    kcol = lax.broadcasted_iota(jnp.int32, (2 * CHUNK, BAND), 1)
    pad = LEFT_CHUNKS * CHUNK
    bias = bias_ref[...]
    chunks = range(tq // CHUNK)
    starts = [tq - pad + j * CHUNK for j in chunks]
    scores = []
    for j, start in zip(chunks, starts):
        q = q_ref[j * CHUNK:(j + 1) * CHUNK, :] * scale
        q2 = jnp.concatenate([jnp.where(head0, q, 0.0), jnp.where(head0, 0.0, q)], axis=0).astype(BF16)
        scores.append(_dot_nt(q2, k_all[start:start + BAND]))
    exps = []
    for start, s in zip(starts, scores):
        valid = jnp.logical_or(qi > 0, kcol + start >= tq)
        s = jnp.where(valid, s + bias, -jnp.inf)
        exps.append(jnp.exp(s - jnp.max(s, axis=-1, keepdims=True)))
    outs = [_dot(e.astype(BF16), v_all[start:start + BAND]) for start, e in zip(starts, exps)]
    for j, e, o in zip(chunks, exps, outs):
        o = o / jnp.sum(e, axis=-1, keepdims=True)
        o_ref[j * CHUNK:(j + 1) * CHUNK, :] = jnp.where(head0, o[:CHUNK], o[CHUNK:]).astype(o_ref.dtype)


def band_attention(proj, kv, bias, n_pairs, seq, tq=512):
    n = proj.shape[0]
    tq = min(tq, seq)
    assert tq >= LEFT_CHUNKS * CHUNK
    nblk = seq // tq
    cur = lambda off: pl.BlockSpec((tq, PAIR), lambda b, p, t: (b * nblk + t, off + p))
    prev = lambda off: pl.BlockSpec((tq, PAIR), lambda b, p, t: (b * nblk + jnp.maximum(t - 1, 0), off + p))
    return pl.pallas_call(
        functools.partial(_bandattn_body, tq=tq),
        grid=(n // seq, n_pairs, nblk),
        in_specs=[cur(0), prev(0), cur(0), prev(n_pairs), cur(n_pairs),
                  pl.BlockSpec((2 * CHUNK, BAND), lambda b, p, t: (p, 0))],
        out_specs=cur(0),
        out_shape=jax.ShapeDtypeStruct((n, n_pairs * PAIR), BF16),
        compiler_params=_params("parallel", "parallel", "parallel"),
        name="band_attention",
    )(proj, kv, kv, kv, kv, bias.reshape(-1, BAND))


def _top16(scores, payloads):
    nl = scores[0].shape[1]
    rids = [lax.broadcasted_iota(jnp.int32, s.shape, 0).astype(F32) for s in scores]
    slot = lax.broadcasted_iota(jnp.int32, (PEER_TOPK, nl), 0)
    tile = 8

    def winner(s, rid, payload):
        items = [(s[g:g + tile], rid[g:g + tile], None if payload is None else payload[g:g + tile])
                 for g in range(0, s.shape[0], tile)]
        while len(items) > 1:
            nxt = []
            for k in range(0, len(items) - 1, 2):
                (va, ia, pa), (vb, ib, pb) = items[k], items[k + 1]
                keep = va >= vb
                nxt.append((jnp.maximum(va, vb), jnp.where(keep, ia, ib),
                            None if pa is None else jnp.where(keep, pa, pb)))
            if len(items) % 2:
                nxt.append(items[-1])
            items = nxt
        v, idx, p = items[0]
        for shift in (4, 2, 1):
            pv, pi = pltpu.roll(v, shift, 0), pltpu.roll(idx, shift, 0)
            take = (pv > v) | ((pv == v) & (pi < idx))
            if p is not None:
                p = jnp.where(take, pltpu.roll(p, shift, 0), p)
            v, idx = jnp.where(take, pv, v), jnp.where(take, pi, idx)
        return v, idx, p

    def body(i, carry):
        sel = slot == i
        out = []
        for (s, vals, picks), rid, payload in zip(carry, rids, payloads):
            m, am, p = winner(s, rid, payload)
            hit = rid == jnp.concatenate([am] * (s.shape[0] // tile), axis=0)
            pick = am if payload is None else p
            out.append((jnp.where(hit, -jnp.inf, s), jnp.where(sel, m[0:1], vals), jnp.where(sel, pick[0:1], picks)))
        return tuple(out)

    zero = jnp.zeros((PEER_TOPK, nl), F32)
    res = lax.fori_loop(0, PEER_TOPK, body, tuple((s, zero, zero) for s in scores))
    return [(vals, picks) for _, vals, picks in res]


def _pair_candidates(a, b, combine):
    half = PEER_TOPK // 2
    rows = [combine(a[0:1], b)]
    rows += [combine(a[i:i + 1], b[:half]) for i in range(1, half)]
    rows.append(combine(a[half:], b[0:1]))
    return jnp.concatenate(rows, axis=0)


TOPK_HEADS = 2


def _peer_topk_body(q_ref, keys_ref, eidx_ref, gate_ref):
    q = q_ref[...].astype(BF16)
    scores = []
    for h in range(TOPK_HEADS):
        for half in range(2):
            col = (2 * h + half) * PEER_KEYS
            scores.append(_dot_nt(keys_ref[h, half].astype(BF16), q[:, col:col + PEER_KEYS]))
    cands, cidxs = [], []
    for h in range(TOPK_HEADS):
        (a, i1), (b, i2) = _top16(scores[2 * h:2 * h + 2], [None, None])
        cands.append(_pair_candidates(a, b, lambda x, y: x + y))
        cidxs.append(_pair_candidates(i1, i2, lambda x, y: x * PEER_KEYS + y))
    for h, (top, eidx) in enumerate(_top16(cands, cidxs)):
        rows = slice(h * PEER_TOPK, (h + 1) * PEER_TOPK)
        e = jnp.exp(top - top[0:1])
        gate_ref[rows, :] = e / jnp.sum(e, axis=0, keepdims=True)
        eidx_ref[rows, :] = eidx.astype(jnp.int32)


def peer_topk(q, keys, tl=128):
    n = q.shape[0]
    tl = min(tl, n)
    blk = pl.BlockSpec((TOPK_HEADS * PEER_TOPK, tl), lambda i, h: (h, i))
    return pl.pallas_call(
        _peer_topk_body,
        grid=(n // tl, PEER_HEADS // TOPK_HEADS),
        in_specs=[pl.BlockSpec((tl, TOPK_HEADS * 2 * PEER_KEYS), lambda i, h: (i, h)),
                  pl.BlockSpec((TOPK_HEADS, 2, PEER_KEYS, PEER_KEYS), lambda i, h: (h, 0, 0, 0))],
        out_specs=[blk, blk],
        out_shape=[jax.ShapeDtypeStruct((PEER_PICKS, n), jnp.int32),
                   jax.ShapeDtypeStruct((PEER_PICKS, n), F32)],
        compiler_params=_params("parallel", "parallel"),
        name="peer_topk",
    )(q, keys)


PEER_SLOTS = 8
LANES = 128
PITCH_PAD = 4


def _pack_body(u_ref, v_ref, o_ref):
    hi = lax.bitcast_convert_type(u_ref[0].astype(BF16).astype(F32), jnp.uint32)
    lo = lax.bitcast_convert_type(v_ref[0].astype(BF16).astype(F32), jnp.uint32)
    words = hi | (lo >> 16)
    for c in range(o_ref.shape[1]):
        o_ref[:, c, :] = words[:, c * LANES:(c + 1) * LANES]


def _pack_expert_table(u, v, layer, te=256):
    _, n_exp, d = u.shape
    chunks = d // LANES
    te = min(te, n_exp)
    packed = pl.pallas_call(
        _pack_body,
        grid=(n_exp // te,),
        in_specs=[pl.BlockSpec((1, te, d), lambda i: (layer, i, 0))] * 2,
        out_specs=pl.BlockSpec((te, chunks, LANES), lambda i: (i, 0, 0)),
        out_shape=jax.ShapeDtypeStruct((n_exp, chunks, LANES), jnp.uint32),
        compiler_params=_params("parallel"),
        name="pack_expert_table",
    )(u, v)
    return packed.reshape(n_exp * chunks, LANES)


def _peer_ffn_body(eidx_ref, gate_ref, hn_ref, x_ref, tab_ref, *rest, tb, d, out_norm):
    norm_ref, o_ref, scratch = (rest[0], rest[1], rest[2:]) if out_norm else (None, rest[0], rest[1:])
    rows_refs = scratch[:PEER_SLOTS]
    sem_ref, w_ref, stage_ref, hn_rows_ref, gate_rows_ref = scratch[PEER_SLOTS:]
    chunks = d // LANES
    pitch = chunks + PITCH_PAD

    def issue(t, slot):
        for e in range(PEER_PICKS):
            src = tab_ref.at[pl.ds(pl.multiple_of(eidx_ref[t, e] * chunks, chunks), chunks), :]
            dst = rows_refs[slot].at[pl.ds(e * pitch, chunks), :]
            pltpu.make_async_copy(src, dst, sem_ref.at[slot]).start(priority=e % 2)

    def wait(slot):
        total = PEER_PICKS * chunks
        pltpu.make_async_copy(tab_ref.at[pl.ds(0, total), :],
                              rows_refs[slot].at[pl.ds(0, total), :], sem_ref.at[slot]).wait()

    pick_diag = (lax.broadcasted_iota(jnp.int32, (PEER_PICKS, PEER_PICKS), 0)
                 == lax.broadcasted_iota(jnp.int32, (PEER_PICKS, PEER_PICKS), 1))
    hi_mask = jnp.uint32(0xFFFF0000)

    def words(slot, c):
        return rows_refs[slot][pl.ds(c, PEER_PICKS, stride=pitch), :]

    def pick_weights(slot, hn_rows, gate_rows, row):
        acc = jnp.zeros((PEER_PICKS, LANES), F32)
        for c in range(chunks):
            u = lax.bitcast_convert_type(words(slot, c) & hi_mask, F32)
            acc = acc + u * hn_rows[row:row + 1, c * LANES:(c + 1) * LANES]
        act = jnp.sum(acc, axis=1, keepdims=True)
        act = 0.5 * act * (1.0 + lax.erf(act * (2.0 ** -0.5)))
        gate = jnp.sum(jnp.where(pick_diag, gate_rows[row:row + 1, :], 0.0), axis=1, keepdims=True)
        w_ref[slot] = jnp.broadcast_to(gate * act, (PEER_PICKS, LANES))
        return acc

    def combine(slot, after):
        last = lax.bitcast_convert_type(after[PEER_PICKS - 8:], jnp.int32) == -1
        never = jnp.concatenate([last] * (PEER_PICKS // 8), axis=0)
        w = jnp.where(never, 0.0, w_ref[slot])
        for c in range(chunks):
            v = lax.bitcast_convert_type(words(slot, c) << 16, F32)
            stage_ref[slot:slot + 1, c * LANES:(c + 1) * LANES] = jnp.sum(v * w, axis=0, keepdims=True)

    def store_group(g):
        rows = pl.ds(pl.multiple_of(g * PEER_SLOTS, PEER_SLOTS), PEER_SLOTS)
        y = x_ref[rows, :] + stage_ref[...]
        if norm_ref is not None:
            y = y * lax.rsqrt(jnp.mean(y * y, axis=-1, keepdims=True) + RMS_EPS) * norm_ref[...]
        o_ref[rows, :] = y

    ahead = PEER_SLOTS - 1
    groups = tb // PEER_SLOTS
    for t in range(ahead):
        issue(t, t)
    wait(0)
    first_pass = pick_weights(0, hn_ref, gate_ref, 0)

    def group(g, carry):
        rows = pl.ds(pl.multiple_of(g * PEER_SLOTS, PEER_SLOTS), 2 * PEER_SLOTS)
        hn_rows_ref[...] = hn_ref[rows, :]
        gate_rows_ref[...] = gate_ref[rows, :]
        for s in range(PEER_SLOTS):
            t = g * PEER_SLOTS + s
            wait((s + 1) % PEER_SLOTS)
            combine(s, pick_weights((s + 1) % PEER_SLOTS, hn_rows_ref, gate_rows_ref, s + 1))
            issue(t + ahead, (s + ahead) % PEER_SLOTS)
        store_group(g)
        return carry

    lax.fori_loop(0, groups - 1, group, 0)

    for s in range(PEER_SLOTS):
        t = (groups - 1) * PEER_SLOTS + s
        if t + ahead < tb:
            issue(t + ahead, (s + ahead) % PEER_SLOTS)
        if t + 1 < tb:
            wait((s + 1) % PEER_SLOTS)
            first_pass = pick_weights((s + 1) % PEER_SLOTS, hn_ref, gate_ref, t + 1)
        combine(s, first_pass)
    store_group(groups - 1)


def peer_expert_ffn(eidx, gate, hn, x, table, tb=128, out_norm_g=None):
    n, d = x.shape
    tb = min(tb, n)
    assert tb % PEER_SLOTS == 0 and table.shape[1] == LANES
    assert table.shape[0] >= PEER_PICKS * (d // LANES)
    slot_rows = PEER_PICKS * (d // LANES + PITCH_PAD)
    out_norm = out_norm_g is not None
    in_specs = [pl.BlockSpec((tb, PEER_PICKS), lambda i: (i, 0), memory_space=pltpu.SMEM),
                pl.BlockSpec((tb, PEER_PICKS), lambda i: (i, 0)),
                pl.BlockSpec((tb, d), lambda i: (i, 0)),
                pl.BlockSpec((tb, d), lambda i: (i, 0)),
                pl.BlockSpec(memory_space=pl.ANY)]
    operands = [eidx, gate, hn, x, table]
    if out_norm:
        in_specs.append(pl.BlockSpec((1, d), lambda i: (0, 0)))
        operands.append(out_norm_g.reshape(1, d))
    return pl.pallas_call(
        functools.partial(_peer_ffn_body, tb=tb, d=d, out_norm=out_norm),
        grid=(n // tb,),
        in_specs=in_specs,
        out_specs=pl.BlockSpec((tb, d), lambda i: (i, 0)),
        out_shape=jax.ShapeDtypeStruct((n, d), F32),
        scratch_shapes=[pltpu.VMEM((slot_rows, LANES), jnp.uint32)] * PEER_SLOTS
                       + [pltpu.SemaphoreType.DMA((PEER_SLOTS,)),
                          pltpu.VMEM((PEER_SLOTS, PEER_PICKS, LANES), F32),
                          pltpu.VMEM((PEER_SLOTS, d), F32),
                          pltpu.VMEM((2 * PEER_SLOTS, d), F32),
                          pltpu.VMEM((2 * PEER_SLOTS, PEER_PICKS), F32)],
        compiler_params=_params("arbitrary"),
        name="peer_expert_ffn",
    )(*operands)


def peer_layer(x, norm_g, wq, keys, u_all, v_all, layer, tb=256, out_norm_g=None):
    q, hn = norm_matmul(x, norm_g, wq.astype(BF16), emit_hn=True)
    eidx, gate = peer_topk(q, keys)
    return peer_expert_ffn(eidx.T, gate.T, hn, x, _pack_expert_table(u_all, v_all, layer), tb=tb,
                           out_norm_g=out_norm_g)


def _band_bias(rel_bias):
    n_rel = rel_bias.shape[1]
    far = BAND - n_rel + CHUNK - 1
    long_row = jnp.concatenate([jnp.broadcast_to(rel_bias[:, n_rel - 1:], (rel_bias.shape[0], far)),
                                rel_bias[:, ::-1]], axis=1)
    rows = [long_row[:, CHUNK - 1 - i:CHUNK - 1 - i + BAND] for i in range(CHUNK)]
    return jnp.stack(rows, axis=1).astype(F32)


def kernel(x, mem, norm_mix, norm_ffn, norm_mem, w_mem_kv, w_out, peer_wq, peer_keys, peer_u, peer_v, a_mix, a_w_in, a_w0, a_w1, a_w2, a_a0, a_a1, a_a2, a_g1, a_g2, a_k_k, a_k_a, a_r_k, a_ln_g, a_ln_b, kv_norm, w_kv_shared, b_w_in, b_rel_bias, final_norm):
    bsz, seq, d = x.shape
    n = bsz * seq
    n_mem = mem.shape[1]
    seq_width = a_w0.shape[1]
    n_pairs = seq_width // PAIR
    mem_width = MEM_HEADS * MEM_HEAD_DIM
    x = x.reshape(n, d)
    mem2 = mem.reshape(bsz * n_mem, d)

    def mixer_tail(x, seq_out, proj, q_col_block, layer, out_norm_g=None):
        mkv = norm_matmul(mem2, norm_mem[layer], w_mem_kv[layer].astype(BF16), tm=256)
        mem_out = memory_attention(proj, q_col_block, mkv, seq, n_mem)
        wo = w_out[layer].astype(BF16)
        x = out_projection(seq_out, mem_out, wo[:seq_width], wo[seq_width:], x)
        return peer_layer(x, norm_ffn[layer], peer_wq[layer], peer_keys[layer], peer_u, peer_v, layer,
                          out_norm_g=out_norm_g)

    h = rmsnorm(x, norm_mix[0])
    mix_tab = jnp.concatenate([a_mix[0], jnp.zeros((1, d), F32)], axis=0)
    tiles = seq_width // 512
    gid_main = jnp.asarray([0] * tiles + [2] * tiles + [3] * tiles + [6] * (mem_width // 512), jnp.int32)
    proj = mix_matmul(h, mix_tab, gid_main, a_w_in[0].astype(BF16), seq, tn=512)
    rank = a_w1.shape[2]
    padc = lambda w: jnp.pad(w, ((0, 0), (0, LORA_PAD - rank)))
    padr = lambda w: jnp.pad(w, ((0, LORA_PAD - rank), (0, 0)))
    w_l1 = jnp.concatenate([padc(a_w1[0]), padc(a_a1[0]), a_g1[0]], axis=1).astype(BF16)
    gid_l1 = jnp.asarray([1, 4] + [5] * (a_g1.shape[2] // LORA_PAD), jnp.int32)
    t1 = mix_matmul(h, mix_tab, gid_l1, w_l1, seq, tn=LORA_PAD)
    lw, a_iclr, gate = lora_stage2(t1, padr(a_w2[0]).astype(BF16), padr(a_a2[0]).astype(BF16),
                                   a_g2[0].astype(BF16), a_w0[0], a_a0[0])
    prep = wkv_chunk_prepare(proj, lw, a_iclr, a_k_k[0], a_k_a[0], n_pairs)
    seq_out = wkv_chunk_scan(prep, proj, a_iclr, gate, a_k_a[0], a_r_k[0].reshape(-1),
                             a_ln_g[0], a_ln_b[0], n_pairs, seq)
    x = mixer_tail(x, seq_out, proj, (3 * seq_width) // mem_width, 0)

    kv = norm_matmul(x, kv_norm, w_kv_shared.astype(BF16), tm=1024)

    proj = norm_matmul(x, norm_mix[1], b_w_in[0].astype(BF16), tm=1024)
    seq_out = band_attention(proj, kv, _band_bias(b_rel_bias[0]), n_pairs, seq)
    x = mixer_tail(x, seq_out, proj, seq_width // mem_width, 1, out_norm_g=final_norm)
    return x.reshape(bsz, seq, d)
```

```python
import functools

import numpy as np
import jax
import jax.numpy as jnp
from jax import lax
from jax.experimental import pallas as pl
from jax.experimental.pallas import tpu as pltpu

F32 = jnp.float32
BF16 = jnp.bfloat16

HEAD_DIM = 64
PAIR = 2 * HEAD_DIM
CHUNK = 64
SCAN_PAIRS = 2
LEFT_CHUNKS = 8
BAND = (LEFT_CHUNKS + 1) * CHUNK
REL_MAX = 128
MEM_HEADS = 4
MEM_HEAD_DIM = 128
PEER_KEYS = 128
PEER_HEADS = 8
PEER_TOPK = 16
PEER_PICKS = PEER_HEADS * PEER_TOPK
GN_EPS = 64e-5
RMS_EPS = 1e-6
LORA_PAD = 128
VMEM_LIMIT = 48 * 1024 * 1024


def _params(*sem):
    return pltpu.CompilerParams(dimension_semantics=sem, vmem_limit_bytes=VMEM_LIMIT)


def _dot(a, b):
    return jnp.dot(a, b, preferred_element_type=F32)


def _dot_nt(a, b, precision=None):
    return lax.dot_general(a, b, (((1,), (1,)), ((), ())), precision=precision,
                           preferred_element_type=F32)


def _dot_tn(a, b, precision=None):
    return lax.dot_general(a, b, (((0,), (0,)), ((), ())), precision=precision,
                           preferred_element_type=F32)


def _rmsnorm_body(x_ref, g_ref, o_ref):
    x = x_ref[...]
    ms = jnp.mean(x * x, axis=-1, keepdims=True)
    o_ref[...] = (x * lax.rsqrt(ms + RMS_EPS) * g_ref[...]).astype(o_ref.dtype)


def rmsnorm(x, g, tm=512):
    n, d = x.shape
    tm = min(tm, n)
    return pl.pallas_call(
        _rmsnorm_body,
        grid=(n // tm,),
        in_specs=[pl.BlockSpec((tm, d), lambda i: (i, 0)), pl.BlockSpec((1, d), lambda i: (0, 0))],
        out_specs=pl.BlockSpec((tm, d), lambda i: (i, 0)),
        out_shape=jax.ShapeDtypeStruct((n, d), F32),
        compiler_params=_params("parallel"),
        name="rmsnorm",
    )(x, g.reshape(1, d))


def _normmm_body(x_ref, g_ref, w_ref, o_ref, *rest, emit_hn):
    lhs_ref = rest[-1]

    @pl.when(pl.program_id(1) == 0)
    def _():
        x = x_ref[...]
        ms = jnp.mean(x * x, axis=-1, keepdims=True)
        hn = x * lax.rsqrt(ms + RMS_EPS) * g_ref[...]
        lhs_ref[...] = hn.astype(BF16)
        if emit_hn:
            rest[0][...] = hn

    o_ref[...] = _dot(lhs_ref[...], w_ref[...]).astype(o_ref.dtype)


def norm_matmul(x, g, w_bf16, emit_hn=False, tm=512, tn=1024, out_dtype=BF16):
    n, d = x.shape
    nc = w_bf16.shape[1]
    tm, tn = min(tm, n), min(tn, nc)
    out_shape = [jax.ShapeDtypeStruct((n, nc), out_dtype)]
    out_specs = [pl.BlockSpec((tm, tn), lambda i, j: (i, j))]
    if emit_hn:
        out_shape.append(jax.ShapeDtypeStruct((n, d), F32))
        out_specs.append(pl.BlockSpec((tm, d), lambda i, j: (i, 0)))
    res = pl.pallas_call(
        functools.partial(_normmm_body, emit_hn=emit_hn),
        grid=(n // tm, nc // tn),
        in_specs=[pl.BlockSpec((tm, d), lambda i, j: (i, 0)),
                  pl.BlockSpec((1, d), lambda i, j: (0, 0)),
                  pl.BlockSpec((d, tn), lambda i, j: (0, j))],
        out_specs=out_specs,
        out_shape=out_shape,
        scratch_shapes=[pltpu.VMEM((tm, d), BF16)],
        compiler_params=_params("parallel", "arbitrary"),
        name="norm_matmul",
    )(x, g.reshape(1, d), w_bf16)
    return res if emit_hn else res[0]


def _mixmm_body(gid_ref, h_ref, hp_ref, mix_ref, w_ref, o_ref, lhs_ref, diff_ref, *, tm, seq):
    i = pl.program_id(0)
    j = pl.program_id(1)

    @pl.when(j == 0)
    def _():
        h = h_ref[...]
        prev = jnp.where((i * tm) % seq == 0, 0.0, hp_ref[7:8, :])
        row = lax.broadcasted_iota(jnp.int32, h.shape, 0)
        diff_ref[...] = jnp.where(row == 0, prev, pltpu.roll(h, 1, 0)) - h

    new_group = jnp.logical_or(j == 0, gid_ref[j] != gid_ref[jnp.maximum(j - 1, 0)])

    @pl.when(new_group)
    def _():
        lhs_ref[...] = (h_ref[...] + diff_ref[...] * mix_ref[0]).astype(BF16)

    o_ref[...] = _dot(lhs_ref[...], w_ref[...])


def mix_matmul(h, mix_tab, gid, w_bf16, seq, tn, tm=1024):
    n, d = h.shape
    nc = w_bf16.shape[1]
    tm = min(tm, seq)
    sub = tm // 8
    grid_spec = pltpu.PrefetchScalarGridSpec(
        num_scalar_prefetch=1,
        grid=(n // tm, nc // tn),
        in_specs=[pl.BlockSpec((tm, d), lambda i, j, g: (i, 0)),
                  pl.BlockSpec((8, d), lambda i, j, g: (jnp.maximum(i * sub - 1, 0), 0)),
                  pl.BlockSpec((1, 1, d), lambda i, j, g: (g[j], 0, 0)),
                  pl.BlockSpec((d, tn), lambda i, j, g: (0, j))],
        out_specs=pl.BlockSpec((tm, tn), lambda i, j, g: (i, j)),
        scratch_shapes=[pltpu.VMEM((tm, d), BF16), pltpu.VMEM((tm, d), F32)],
    )
    return pl.pallas_call(
        functools.partial(_mixmm_body, tm=tm, seq=seq),
        grid_spec=grid_spec,
        out_shape=jax.ShapeDtypeStruct((n, nc), F32),
        compiler_params=_params("parallel", "arbitrary"),
        name="mix_matmul",
    )(gid, h, h, mix_tab.reshape(mix_tab.shape[0], 1, d), w_bf16)


def _sigmoid(x):
    return 1.0 / (1.0 + jnp.exp(-x))


def _lora2_body(t_ref, w2_ref, a2_ref, g2_ref, w0_ref, a0_ref, lw_ref, a_ref, g_ref):
    t = t_ref[...]
    tw = jnp.tanh(t[:, :LORA_PAD]).astype(BF16)
    ta = t[:, LORA_PAD:2 * LORA_PAD].astype(BF16)
    tg = _sigmoid(t[:, 2 * LORA_PAD:]).astype(BF16)
    u = w0_ref[...] + _dot(tw, w2_ref[...])
    softplus_neg_u = jnp.maximum(-u, 0.0) + jnp.log(1.0 + jnp.exp(-jnp.abs(u)))
    lw_ref[...] = -jnp.exp(-softplus_neg_u - 0.5)
    a_ref[...] = _sigmoid(a0_ref[...] + _dot(ta, a2_ref[...]))
    g_ref[...] = _dot(tg, g2_ref[...])


def lora_stage2(t1, w2p, a2p, g2, w0, a0, tm=256):
    n = t1.shape[0]
    width = w2p.shape[1]
    tm = min(tm, n)
    full = lambda a: pl.BlockSpec(a.shape, lambda i: (0, 0))
    row = pl.BlockSpec((tm, width), lambda i: (i, 0))
    w0 = w0.reshape(1, width)
    a0 = a0.reshape(1, width)
    return pl.pallas_call(
        _lora2_body,
        grid=(n // tm,),
        in_specs=[pl.BlockSpec((tm, t1.shape[1]), lambda i: (i, 0)), full(w2p), full(a2p), full(g2),
                  full(w0), full(a0)],
        out_specs=[row, row, row],
        out_shape=[jax.ShapeDtypeStruct((n, width), F32)] * 3,
        compiler_params=_params("parallel"),
        name="lora_stage2",
    )(t1, w2p, a2p, g2, w0, a0)


def _head_group_sum(x):
    width = x.shape[-1]
    r = lax.broadcasted_iota(jnp.int32, (width, width), 0) // HEAD_DIM
    c = lax.broadcasted_iota(jnp.int32, (width, width), 1) // HEAD_DIM
    ones = jnp.where(r == c, 1.0, 0.0).astype(BF16)
    hi = x.astype(BF16)
    rest = x - hi.astype(F32)
    mid = rest.astype(BF16)
    lo = (rest - mid.astype(F32)).astype(BF16)
    return _dot(hi, ones) + _dot(mid, ones) + _dot(lo, ones)


def _wkv_prepare_body(r_ref, k_ref, v_ref, lw_ref, a_ref, kk_ref, ka_ref,
                      g_ref, s0c_ref, rp_ref, y0_ref, *, tb):
    r = r_ref[...]
    k = k_ref[...]
    v = v_ref[...]
    lw = lw_ref[...]
    a = a_ref[...]
    kk = k * kk_ref[...]
    norm = jnp.sqrt(_head_group_sum(kk * kk))
    kk = kk / jnp.maximum(norm, 1e-12)
    kmod = k * (1.0 + (a - 1.0) * ka_ref[...])
    avec = -kk
    bvec = kk * a

    tr = lax.broadcasted_iota(jnp.int32, (CHUNK, CHUNK), 0)
    tc = lax.broadcasted_iota(jnp.int32, (CHUNK, CHUNK), 1)
    tri = jnp.where(tr >= tc, 1.0, 0.0).astype(BF16)
    lw_hi = lw.astype(BF16)
    lw_rest = lw - lw_hi.astype(F32)
    lw_mid = lw_rest.astype(BF16)
    lw_lo = (lw_rest - lw_mid.astype(F32)).astype(BF16)

    lane = lax.broadcasted_iota(jnp.int32, (CHUNK, PAIR), 1)
    head0 = lane < HEAD_DIM
    row = lax.broadcasted_iota(jnp.int32, (PAIR, PAIR), 0)
    col = lax.broadcasted_iota(jnp.int32, (PAIR, PAIR), 1)
    strict = row > col
    lower = row >= col
    eye = jnp.where(row == col, 1.0, 0.0).astype(F32)

    def stack(x):
        return jnp.concatenate([jnp.where(head0, x, 0.0), jnp.where(head0, 0.0, x)], axis=0)

    chunks = range(tb // CHUNK)
    pre = []
    for c in chunks:
        sl = slice(c * CHUNK, (c + 1) * CHUNK)
        cm = _dot(tri, lw_hi[sl]) + _dot(tri, lw_mid[sl]) + _dot(tri, lw_lo[sl])
        cend = cm[CHUNK - 1:CHUNK]
        e_in = jnp.exp(cm)
        e_out = jnp.exp(-cm)
        e_tail = jnp.exp(cend - cm)
        a_s = stack(avec[sl] * jnp.exp(cm - lw[sl]))
        r_s = stack(r[sl] * e_in)
        b_s = stack(bvec[sl] * e_out)
        k_s = stack(kmod[sl] * e_out)
        bt_s = stack(bvec[sl] * e_tail)
        kt_s = stack(kmod[sl] * e_tail)
        v_s = stack(v[sl])
        p = _dot_nt(jnp.concatenate([a_s, r_s], axis=0).astype(BF16),
                    jnp.concatenate([b_s, k_s], axis=0).astype(BF16))
        pre.append(dict(cend=cend, a_s=a_s, r_s=r_s, bt_s=bt_s, kt_s=kt_s, v_s=v_s,
                        l_ab=jnp.where(strict, p[:PAIR, :PAIR], 0.0),
                        l_ak=jnp.where(strict, p[:PAIR, PAIR:], 0.0),
                        a_rb=jnp.where(lower, p[PAIR:, :PAIR], 0.0),
                        a_rk=jnp.where(lower, p[PAIR:, PAIR:], 0.0)))

    ms = [q["l_ab"] for q in pre]
    ts = [eye + m for m in ms]
    ms = [_dot(m.astype(BF16), m.astype(BF16)) for m in ms]
    for _ in range(4):
        both = [_dot(m.astype(BF16), jnp.concatenate([t, m], axis=1).astype(BF16)) for t, m in zip(ts, ms)]
        ts = [t + b[:, :PAIR] for t, b in zip(ts, both)]
        ms = [b[:, PAIR:] for b in both]
    ts = [t + _dot(m.astype(BF16), t.astype(BF16)) for t, m in zip(ts, ms)]

    lvs = [_dot(q["l_ak"].astype(BF16), q["v_s"].astype(BF16)) for q in pre]
    aus = [_dot(t.astype(BF16), jnp.concatenate([q["a_s"], lv], axis=1).astype(BF16))
           for t, q, lv in zip(ts, pre, lvs)]
    zero = jnp.zeros((PAIR, PAIR), F32)
    xs = [_dot(jnp.concatenate([q["a_rb"], q["a_rk"]], axis=1).astype(BF16),
               jnp.concatenate([au, jnp.concatenate([zero, q["v_s"]], axis=1)], axis=0).astype(BF16))
          for q, au in zip(pre, aus)]
    for c, q, au, x in zip(chunks, pre, aus, xs):
        rp_ref[0, c] = (q["r_s"] + x[:, :PAIR]).astype(rp_ref.dtype)
        y0_ref[0, c] = x[:, PAIR:]
        ap = au[:, :PAIR]
        u0 = au[:, PAIR:]
        g_ref[0, c] = (eye * jnp.exp(q["cend"])
                       + _dot_tn(ap.astype(BF16), q["bt_s"].astype(BF16))).astype(g_ref.dtype)
        s0c_ref[0, c] = _dot_tn(jnp.concatenate([u0, q["v_s"]], axis=0).astype(BF16),
                                jnp.concatenate([q["bt_s"], q["kt_s"]], axis=0).astype(BF16))


def wkv_chunk_prepare(proj, lw, a, k_k, k_a, n_pairs, tb=1024):
    n = proj.shape[0]
    tb = min(tb, n)
    cpb = tb // CHUNK
    col = lambda off: pl.BlockSpec((tb, PAIR), lambda i, p: (i, off + p))
    par = pl.BlockSpec((1, PAIR), lambda i, p: (0, p))
    blk = pl.BlockSpec((1, cpb, PAIR, PAIR), lambda i, p: (p, i, 0, 0))
    shp = lambda dt: jax.ShapeDtypeStruct((n_pairs, n // CHUNK, PAIR, PAIR), dt)
    return pl.pallas_call(
        functools.partial(_wkv_prepare_body, tb=tb),
        grid=(n // tb, n_pairs),
        in_specs=[col(0), col(n_pairs), col(2 * n_pairs), col(0), col(0), par, par],
        out_specs=[blk] * 4,
        out_shape=[shp(BF16), shp(F32), shp(BF16), shp(F32)],
        compiler_params=_params("parallel", "parallel"),
        name="wkv_chunk_prepare",
    )(proj, proj, proj, lw, a, k_k.reshape(1, -1), k_a.reshape(1, -1))


def _wkv_scan_body(g_ref, s0c_ref, rp_ref, y0_ref, r_ref, k_ref, v_ref, a_ref, gate_ref,
                   ka_ref, rk_ref, lng_ref, lnb_ref, o_ref, s_ref, y_ref, *, cpb):
    @pl.when(pl.program_id(2) == 0)
    def _():
        s_ref[...] = jnp.zeros_like(s_ref)

    pairs = range(SCAN_PAIRS)
    states = [[s_ref[w].astype(BF16)] for w in pairs]
    for c in range(cpb):
        for w in pairs:
            s = _dot(states[w][c], g_ref[w, c].astype(BF16)) + s0c_ref[w, c]
            if c + 1 < cpb:
                states[w].append(s.astype(BF16))
            else:
                s_ref[w] = s
    for c in range(cpb):
        for w in pairs:
            y_st = _dot_nt(rp_ref[w, c].astype(BF16), states[w][c]) + y0_ref[w, c]
            y_ref[c * CHUNK:(c + 1) * CHUNK, w * PAIR:(w + 1) * PAIR] = y_st[:CHUNK] + y_st[CHUNK:]

    y = y_ref[...]
    inv = 1.0 / HEAD_DIM
    mu = _head_group_sum(y) * inv
    yc = y - mu
    var = _head_group_sum(yc * yc) * inv
    yn = yc * lax.rsqrt(var + GN_EPS) * lng_ref[...] + lnb_ref[...]
    r = r_ref[...]
    kmod = k_ref[...] * (1.0 + (a_ref[...] - 1.0) * ka_ref[...])
    bonus = _head_group_sum(r * kmod * rk_ref[...]) * v_ref[...]
    o_ref[...] = ((yn + bonus) * gate_ref[...]).astype(o_ref.dtype)


def wkv_chunk_scan(prep, proj, a, gate, k_a, r_k, ln_g, ln_b, n_pairs, seq, tb=1024):
    g_all, s0c_all, rp_all, y0_all = prep
    n = proj.shape[0]
    tb = min(tb, seq)
    cpb = tb // CHUNK
    nblk = seq // tb
    assert n_pairs % SCAN_PAIRS == 0
    width = SCAN_PAIRS * PAIR
    groups = n_pairs // SCAN_PAIRS
    blk = pl.BlockSpec((SCAN_PAIRS, cpb, PAIR, PAIR), lambda b, p, t: (p, b * nblk + t, 0, 0))
    col = lambda off: pl.BlockSpec((tb, width), lambda b, p, t: (b * nblk + t, off + p))
    par = pl.BlockSpec((1, width), lambda b, p, t: (0, p))
    vec = lambda z: z.reshape(1, -1)
    return pl.pallas_call(
        functools.partial(_wkv_scan_body, cpb=cpb),
        grid=(n // seq, groups, nblk),
        in_specs=[blk] * 4 + [col(0), col(groups), col(2 * groups), col(0), col(0)] + [par] * 4,
        out_specs=col(0),
        out_shape=jax.ShapeDtypeStruct((n, n_pairs * PAIR), BF16),
        scratch_shapes=[pltpu.VMEM((SCAN_PAIRS, PAIR, PAIR), F32), pltpu.VMEM((tb, width), F32)],
        compiler_params=_params("parallel", "parallel", "arbitrary"),
        name="wkv_chunk_scan",
    )(g_all, s0c_all, rp_all, y0_all, proj, proj, proj, a, gate,
      vec(k_a), vec(r_k), vec(ln_g), vec(ln_b))


def _memattn_body(q_ref, m_ref, o_ref):
    width = MEM_HEADS * MEM_HEAD_DIM
    scale = MEM_HEAD_DIM ** -0.5
    cols = [slice(h * MEM_HEAD_DIM, (h + 1) * MEM_HEAD_DIM) for h in range(MEM_HEADS)]
    scores = [_dot_nt(q_ref[:, sl].astype(BF16), m_ref[:, sl].astype(BF16)) * scale for sl in cols]
    exps = [jnp.exp(s - jnp.max(s, axis=-1, keepdims=True)) for s in scores]
    outs = [_dot(e.astype(BF16), m_ref[:, width + sl.start:width + sl.stop].astype(BF16))
            for sl, e in zip(cols, exps)]
    for sl, e, o in zip(cols, exps, outs):
        o_ref[:, sl] = (o / jnp.sum(e, axis=-1, keepdims=True)).astype(o_ref.dtype)


def memory_attention(proj, q_col_block, mkv, seq, n_mem, tm=512):
    n = proj.shape[0]
    width = MEM_HEADS * MEM_HEAD_DIM
    tm = min(tm, seq)
    nblk = seq // tm
    return pl.pallas_call(
        _memattn_body,
        grid=(n // seq, nblk),
        in_specs=[pl.BlockSpec((tm, width), lambda b, t: (b * nblk + t, q_col_block)),
                  pl.BlockSpec((n_mem, 2 * width), lambda b, t: (b, 0))],
        out_specs=pl.BlockSpec((tm, width), lambda b, t: (b * nblk + t, 0)),
        out_shape=jax.ShapeDtypeStruct((n, width), BF16),
        compiler_params=_params("parallel", "parallel"),
        name="memory_attention",
    )(proj, mkv)


def _outproj_body(s_ref, m_ref, w1_ref, w2_ref, x_ref, o_ref):
    o_ref[...] = x_ref[...] + _dot(s_ref[...], w1_ref[...]) + _dot(m_ref[...], w2_ref[...])


def out_projection(seq_out, mem_out, w_seq, w_mem, x, tm=512, tn=1024):
    n, d = x.shape
    tm, tn = min(tm, n), min(tn, d)
    ws, wm = seq_out.shape[1], mem_out.shape[1]
    return pl.pallas_call(
        _outproj_body,
        grid=(n // tm, d // tn),
        in_specs=[pl.BlockSpec((tm, ws), lambda i, j: (i, 0)),
                  pl.BlockSpec((tm, wm), lambda i, j: (i, 0)),
                  pl.BlockSpec((ws, tn), lambda i, j: (0, j)),
                  pl.BlockSpec((wm, tn), lambda i, j: (0, j)),
                  pl.BlockSpec((tm, tn), lambda i, j: (i, j))],
        out_specs=pl.BlockSpec((tm, tn), lambda i, j: (i, j)),
        out_shape=jax.ShapeDtypeStruct((n, d), F32),
        compiler_params=_params("parallel", "parallel"),
        name="out_projection",
    )(seq_out, mem_out, w_seq, w_mem, x)


def _bandattn_body(q_ref, kp_ref, kc_ref, vp_ref, vc_ref, bias_ref, o_ref, *, tq):
    qi = pl.program_id(2)
    scale = HEAD_DIM ** -0.5
    k_all = jnp.concatenate([kp_ref[...], kc_ref[...]], axis=0).astype(BF16)
    v_all = jnp.concatenate([vp_ref[...], vc_ref[...]], axis=0).astype(BF16)
    lane = lax.broadcasted_iota(jnp.int32, (CHUNK, PAIR), 1)
    head0 = lane < HEAD_DIM
    kcol = lax.broadcasted_iota(jnp.int32, (2 * CHUNK, BAND), 1)
    pad = LEFT_CHUNKS * CHUNK
    bias = bias_ref[...]
    chunks = range(tq // CHUNK)
    starts = [tq - pad + j * CHUNK for j in chunks]
    scores = []
    for j, start in zip(chunks, starts):
        q = q_ref[j * CHUNK:(j + 1) * CHUNK, :] * scale
        q2 = jnp.concatenate([jnp.where(head0, q, 0.0), jnp.where(head0, 0.0, q)], axis=0).astype(BF16)
        scores.append(_dot_nt(q2, k_all[start:start + BAND]))
    exps = []
    for start, s in zip(starts, scores):
        valid = jnp.logical_or(qi > 0, kcol + start >= tq)
        s = jnp.where(valid, s + bias, -jnp.inf)
        exps.append(jnp.exp(s - jnp.max(s, axis=-1, keepdims=True)))
    outs = [_dot(e.astype(BF16), v_all[start:start + BAND]) for start, e in zip(starts, exps)]
    for j, e, o in zip(chunks, exps, outs):
        o = o / jnp.sum(e, axis=-1, keepdims=True)
        o_ref[j * CHUNK:(j + 1) * CHUNK, :] = jnp.where(head0, o[:CHUNK], o[CHUNK:]).astype(o_ref.dtype)


def band_attention(proj, kv, bias, n_pairs, seq, tq=512):
    n = proj.shape[0]
    tq = min(tq, seq)
    assert tq >= LEFT_CHUNKS * CHUNK
    nblk = seq // tq
    cur = lambda off: pl.BlockSpec((tq, PAIR), lambda b, p, t: (b * nblk + t, off + p))
    prev = lambda off: pl.BlockSpec((tq, PAIR), lambda b, p, t: (b * nblk + jnp.maximum(t - 1, 0), off + p))
    return pl.pallas_call(
        functools.partial(_bandattn_body, tq=tq),
        grid=(n // seq, n_pairs, nblk),
        in_specs=[cur(0), prev(0), cur(0), prev(n_pairs), cur(n_pairs),
                  pl.BlockSpec((2 * CHUNK, BAND), lambda b, p, t: (p, 0))],
        out_specs=cur(0),
        out_shape=jax.ShapeDtypeStruct((n, n_pairs * PAIR), BF16),
        compiler_params=_params("parallel", "parallel", "parallel"),
        name="band_attention",
    )(proj, kv, kv, kv, kv, bias.reshape(-1, BAND))


def _top16(scores, payloads):
    nl = scores[0].shape[1]
    rids = [lax.broadcasted_iota(jnp.int32, s.shape, 0).astype(F32) for s in scores]
    slot = lax.broadcasted_iota(jnp.int32, (PEER_TOPK, nl), 0)
    tile = 8

    def winner(s, rid, payload):
        items = [(s[g:g + tile], rid[g:g + tile], None if payload is None else payload[g:g + tile])
                 for g in range(0, s.shape[0], tile)]
        while len(items) > 1:
            nxt = []
            for k in range(0, len(items) - 1, 2):
                (va, ia, pa), (vb, ib, pb) = items[k], items[k + 1]
                keep = va >= vb
                nxt.append((jnp.maximum(va, vb), jnp.where(keep, ia, ib),
                            None if pa is None else jnp.where(keep, pa, pb)))
            if len(items) % 2:
                nxt.append(items[-1])
            items = nxt
        v, idx, p = items[0]
        for shift in (4, 2, 1):
            pv, pi = pltpu.roll(v, shift, 0), pltpu.roll(idx, shift, 0)
            take = (pv > v) | ((pv == v) & (pi < idx))
            if p is not None:
                p = jnp.where(take, pltpu.roll(p, shift, 0), p)
            v, idx = jnp.where(take, pv, v), jnp.where(take, pi, idx)
        return v, idx, p

    def body(i, carry):
        sel = slot == i
        out = []
        for (s, vals, picks), rid, payload in zip(carry, rids, payloads):
            m, am, p = winner(s, rid, payload)
            hit = rid == jnp.concatenate([am] * (s.shape[0] // tile), axis=0)
            pick = am if payload is None else p
            out.append((jnp.where(hit, -jnp.inf, s), jnp.where(sel, m[0:1], vals), jnp.where(sel, pick[0:1], picks)))
        return tuple(out)

    zero = jnp.zeros((PEER_TOPK, nl), F32)
    res = lax.fori_loop(0, PEER_TOPK, body, tuple((s, zero, zero) for s in scores))
    return [(vals, picks) for _, vals, picks in res]


def _pair_candidates(a, b, combine):
    half = PEER_TOPK // 2
    rows = [combine(a[0:1], b)]
    rows += [combine(a[i:i + 1], b[:half]) for i in range(1, half)]
    rows.append(combine(a[half:], b[0:1]))
    return jnp.concatenate(rows, axis=0)


TOPK_HEADS = 2


def _peer_topk_body(q_ref, keys_ref, eidx_ref, gate_ref):
    q = q_ref[...].astype(BF16)
    scores = []
    for h in range(TOPK_HEADS):
        for half in range(2):
            col = (2 * h + half) * PEER_KEYS
            scores.append(_dot_nt(keys_ref[h, half].astype(BF16), q[:, col:col + PEER_KEYS]))
    cands, cidxs = [], []
    for h in range(TOPK_HEADS):
        (a, i1), (b, i2) = _top16(scores[2 * h:2 * h + 2], [None, None])
        cands.append(_pair_candidates(a, b, lambda x, y: x + y))
        cidxs.append(_pair_candidates(i1, i2, lambda x, y: x * PEER_KEYS + y))
    for h, (top, eidx) in enumerate(_top16(cands, cidxs)):
        rows = slice(h * PEER_TOPK, (h + 1) * PEER_TOPK)
        e = jnp.exp(top - top[0:1])
        gate_ref[rows, :] = e / jnp.sum(e, axis=0, keepdims=True)
        eidx_ref[rows, :] = eidx.astype(jnp.int32)


def peer_topk(q, keys, tl=128):
    n = q.shape[0]
    tl = min(tl, n)
    blk = pl.BlockSpec((TOPK_HEADS * PEER_TOPK, tl), lambda i, h: (h, i))
    return pl.pallas_call(
        _peer_topk_body,
        grid=(n // tl, PEER_HEADS // TOPK_HEADS),
        in_specs=[pl.BlockSpec((tl, TOPK_HEADS * 2 * PEER_KEYS), lambda i, h: (i, h)),
                  pl.BlockSpec((TOPK_HEADS, 2, PEER_KEYS, PEER_KEYS), lambda i, h: (h, 0, 0, 0))],
        out_specs=[blk, blk],
        out_shape=[jax.ShapeDtypeStruct((PEER_PICKS, n), jnp.int32),
                   jax.ShapeDtypeStruct((PEER_PICKS, n), F32)],
        compiler_params=_params("parallel", "parallel"),
        name="peer_topk",
    )(q, keys)


PEER_SLOTS = 8
LANES = 128
PITCH_PAD = 4


def _pack_body(u_ref, v_ref, o_ref):
    hi = lax.bitcast_convert_type(u_ref[0].astype(BF16).astype(F32), jnp.uint32)
    lo = lax.bitcast_convert_type(v_ref[0].astype(BF16).astype(F32), jnp.uint32)
    words = hi | (lo >> 16)
    for c in range(o_ref.shape[1]):
        o_ref[:, c, :] = words[:, c * LANES:(c + 1) * LANES]


def _pack_expert_table(u, v, layer, te=256):
    _, n_exp, d = u.shape
    chunks = d // LANES
    te = min(te, n_exp)
    packed = pl.pallas_call(
        _pack_body,
        grid=(n_exp // te,),
        in_specs=[pl.BlockSpec((1, te, d), lambda i: (layer, i, 0))] * 2,
        out_specs=pl.BlockSpec((te, chunks, LANES), lambda i: (i, 0, 0)),
        out_shape=jax.ShapeDtypeStruct((n_exp, chunks, LANES), jnp.uint32),
        compiler_params=_params("parallel"),
        name="pack_expert_table",
    )(u, v)
    return packed.reshape(n_exp * chunks, LANES)


def _peer_ffn_body(eidx_ref, gate_ref, in_norm_ref, x_ref, tab_ref, *rest, tb, d, out_norm):
    norm_ref, o_ref, scratch = (rest[0], rest[1], rest[2:]) if out_norm else (None, rest[0], rest[1:])
    rows_refs = scratch[:PEER_SLOTS]
    sem_ref, w_ref, stage_ref, hn_rows_ref, gate_rows_ref = scratch[PEER_SLOTS:]
    chunks = d // LANES
    pitch = chunks + PITCH_PAD

    def issue(t, slot):
        for e in range(PEER_PICKS):
            src = tab_ref.at[pl.ds(pl.multiple_of(eidx_ref[t, e] * chunks, chunks), chunks), :]
            dst = rows_refs[slot].at[pl.ds(e * pitch, chunks), :]
            pltpu.make_async_copy(src, dst, sem_ref.at[slot]).start(priority=e % 2)

    def wait(slot):
        total = PEER_PICKS * chunks
        pltpu.make_async_copy(tab_ref.at[pl.ds(0, total), :],
                              rows_refs[slot].at[pl.ds(0, total), :], sem_ref.at[slot]).wait()

    pick_diag = (lax.broadcasted_iota(jnp.int32, (PEER_PICKS, PEER_PICKS), 0)
                 == lax.broadcasted_iota(jnp.int32, (PEER_PICKS, PEER_PICKS), 1))
    hi_mask = jnp.uint32(0xFFFF0000)

    def words(slot, c):
        return rows_refs[slot][pl.ds(c, PEER_PICKS, stride=pitch), :]

    def pick_weights(slot, hn_rows, gate_rows, row):
        acc = jnp.zeros((PEER_PICKS, LANES), F32)
        for c in range(chunks):
            u = lax.bitcast_convert_type(words(slot, c) & hi_mask, F32)
            acc = acc + u * hn_rows[row:row + 1, c * LANES:(c + 1) * LANES]
        act = jnp.sum(acc, axis=1, keepdims=True)
        act = 0.5 * act * (1.0 + lax.erf(act * (2.0 ** -0.5)))
        gate = jnp.sum(jnp.where(pick_diag, gate_rows[row:row + 1, :], 0.0), axis=1, keepdims=True)
        w_ref[slot] = jnp.broadcast_to(gate * act, (PEER_PICKS, LANES))
        return acc

    def combine(slot, after):
        last = lax.bitcast_convert_type(after[PEER_PICKS - 8:], jnp.int32) == -1
        never = jnp.concatenate([last] * (PEER_PICKS // 8), axis=0)
        w = jnp.where(never, 0.0, w_ref[slot])
        for c in range(chunks):
            v = lax.bitcast_convert_type(words(slot, c) << 16, F32)
            stage_ref[slot:slot + 1, c * LANES:(c + 1) * LANES] = jnp.sum(v * w, axis=0, keepdims=True)

    def store_group(g):
        rows = pl.ds(pl.multiple_of(g * PEER_SLOTS, PEER_SLOTS), PEER_SLOTS)
        y = x_ref[rows, :] + stage_ref[...]
        if norm_ref is not None:
            y = y * lax.rsqrt(jnp.mean(y * y, axis=-1, keepdims=True) + RMS_EPS) * norm_ref[...]
        o_ref[rows, :] = y

    ahead = PEER_SLOTS - 1
    groups = tb // PEER_SLOTS

    def stage_inputs(first_row):
        if not isinstance(first_row, int):
            first_row = pl.multiple_of(first_row, PEER_SLOTS)
        rows = pl.ds(first_row, 2 * PEER_SLOTS)
        xs = x_ref[rows, :]
        hn_rows_ref[...] = xs * lax.rsqrt(jnp.mean(xs * xs, axis=-1, keepdims=True) + RMS_EPS) * in_norm_ref[...]
        gate_rows_ref[...] = gate_ref[rows, :]

    for t in range(ahead):
        issue(t, t)
    stage_inputs(0)
    wait(0)
    first_pass = pick_weights(0, hn_rows_ref, gate_rows_ref, 0)

    def group(g, carry):
        stage_inputs(g * PEER_SLOTS)
        for s in range(PEER_SLOTS):
            t = g * PEER_SLOTS + s
            wait((s + 1) % PEER_SLOTS)
            combine(s, pick_weights((s + 1) % PEER_SLOTS, hn_rows_ref, gate_rows_ref, s + 1))
            issue(t + ahead, (s + ahead) % PEER_SLOTS)
        store_group(g)
        return carry

    lax.fori_loop(0, groups - 1, group, 0)

    tail_first = tb - 2 * PEER_SLOTS
    stage_inputs(tail_first)
    for s in range(PEER_SLOTS):
        t = (groups - 1) * PEER_SLOTS + s
        if t + ahead < tb:
            issue(t + ahead, (s + ahead) % PEER_SLOTS)
        if t + 1 < tb:
            wait((s + 1) % PEER_SLOTS)
            first_pass = pick_weights((s + 1) % PEER_SLOTS, hn_rows_ref, gate_rows_ref, t + 1 - tail_first)
        combine(s, first_pass)
    store_group(groups - 1)


def peer_expert_ffn(eidx, gate, in_norm_g, x, table, tb=128, out_norm_g=None):
    n, d = x.shape
    tb = min(tb, n)
    assert tb % PEER_SLOTS == 0 and table.shape[1] == LANES
    assert table.shape[0] >= PEER_PICKS * (d // LANES)
    slot_rows = PEER_PICKS * (d // LANES + PITCH_PAD)
    out_norm = out_norm_g is not None
    in_specs = [pl.BlockSpec((tb, PEER_PICKS), lambda i: (i, 0), memory_space=pltpu.SMEM),
                pl.BlockSpec((tb, PEER_PICKS), lambda i: (i, 0)),
                pl.BlockSpec((1, d), lambda i: (0, 0)),
                pl.BlockSpec((tb, d), lambda i: (i, 0)),
                pl.BlockSpec(memory_space=pl.ANY)]
    operands = [eidx, gate, in_norm_g.reshape(1, d), x, table]
    if out_norm:
        in_specs.append(pl.BlockSpec((1, d), lambda i: (0, 0)))
        operands.append(out_norm_g.reshape(1, d))
    return pl.pallas_call(
        functools.partial(_peer_ffn_body, tb=tb, d=d, out_norm=out_norm),
        grid=(n // tb,),
        in_specs=in_specs,
        out_specs=pl.BlockSpec((tb, d), lambda i: (i, 0)),
        out_shape=jax.ShapeDtypeStruct((n, d), F32),
        scratch_shapes=[pltpu.VMEM((slot_rows, LANES), jnp.uint32)] * PEER_SLOTS
                       + [pltpu.SemaphoreType.DMA((PEER_SLOTS,)),
                          pltpu.VMEM((PEER_SLOTS, PEER_PICKS, LANES), F32),
                          pltpu.VMEM((PEER_SLOTS, d), F32),
                          pltpu.VMEM((2 * PEER_SLOTS, d), F32),
                          pltpu.VMEM((2 * PEER_SLOTS, PEER_PICKS), F32)],
        compiler_params=_params("arbitrary"),
        name="peer_expert_ffn",
    )(*operands)


def peer_layer(x, norm_g, wq, keys, u_all, v_all, layer, tb=256, out_norm_g=None):
    q = norm_matmul(x, norm_g, wq.astype(BF16), tm=1024)
    eidx, gate = peer_topk(q, keys)
    return peer_expert_ffn(eidx.T, gate.T, norm_g, x, _pack_expert_table(u_all, v_all, layer), tb=tb,
                           out_norm_g=out_norm_g)


def _band_bias(rel_bias):
    n_rel = rel_bias.shape[1]
    far = BAND - n_rel + CHUNK - 1
    long_row = jnp.concatenate([jnp.broadcast_to(rel_bias[:, n_rel - 1:], (rel_bias.shape[0], far)),
                                rel_bias[:, ::-1]], axis=1)
    rows = [long_row[:, CHUNK - 1 - i:CHUNK - 1 - i + BAND] for i in range(CHUNK)]
    return jnp.stack(rows, axis=1).astype(F32)


def kernel(x, mem, norm_mix, norm_ffn, norm_mem, w_mem_kv, w_out, peer_wq, peer_keys, peer_u, peer_v, a_mix, a_w_in, a_w0, a_w1, a_w2, a_a0, a_a1, a_a2, a_g1, a_g2, a_k_k, a_k_a, a_r_k, a_ln_g, a_ln_b, kv_norm, w_kv_shared, b_w_in, b_rel_bias, final_norm):
    bsz, seq, d = x.shape
    n = bsz * seq
    n_mem = mem.shape[1]
    seq_width = a_w0.shape[1]
    n_pairs = seq_width // PAIR
    mem_width = MEM_HEADS * MEM_HEAD_DIM
    x = x.reshape(n, d)
    mem2 = mem.reshape(bsz * n_mem, d)

    def mixer_tail(x, seq_out, proj, q_col_block, layer, out_norm_g=None):
        mkv = norm_matmul(mem2, norm_mem[layer], w_mem_kv[layer].astype(BF16), tm=256)
        mem_out = memory_attention(proj, q_col_block, mkv, seq, n_mem)
        wo = w_out[layer].astype(BF16)
        x = out_projection(seq_out, mem_out, wo[:seq_width], wo[seq_width:], x)
        return peer_layer(x, norm_ffn[layer], peer_wq[layer], peer_keys[layer], peer_u, peer_v, layer,
                          out_norm_g=out_norm_g)

    h = rmsnorm(x, norm_mix[0])
    mix_tab = jnp.concatenate([a_mix[0], jnp.zeros((1, d), F32)], axis=0)
    tiles = seq_width // 512
    gid_main = jnp.asarray([0] * tiles + [2] * tiles + [3] * tiles + [6] * (mem_width // 512), jnp.int32)
    proj = mix_matmul(h, mix_tab, gid_main, a_w_in[0].astype(BF16), seq, tn=512)
    rank = a_w1.shape[2]
    padc = lambda w: jnp.pad(w, ((0, 0), (0, LORA_PAD - rank)))
    padr = lambda w: jnp.pad(w, ((0, LORA_PAD - rank), (0, 0)))
    w_l1 = jnp.concatenate([padc(a_w1[0]), padc(a_a1[0]), a_g1[0]], axis=1).astype(BF16)
    gid_l1 = jnp.asarray([1, 4] + [5] * (a_g1.shape[2] // LORA_PAD), jnp.int32)
    t1 = mix_matmul(h, mix_tab, gid_l1, w_l1, seq, tn=LORA_PAD)
    lw, a_iclr, gate = lora_stage2(t1, padr(a_w2[0]).astype(BF16), padr(a_a2[0]).astype(BF16),
                                   a_g2[0].astype(BF16), a_w0[0], a_a0[0])
    prep = wkv_chunk_prepare(proj, lw, a_iclr, a_k_k[0], a_k_a[0], n_pairs)
    seq_out = wkv_chunk_scan(prep, proj, a_iclr, gate, a_k_a[0], a_r_k[0].reshape(-1),
                             a_ln_g[0], a_ln_b[0], n_pairs, seq)
    x = mixer_tail(x, seq_out, proj, (3 * seq_width) // mem_width, 0)

    kv = norm_matmul(x, kv_norm, w_kv_shared.astype(BF16), tm=1024)

    proj = norm_matmul(x, norm_mix[1], b_w_in[0].astype(BF16), tm=1024)
    seq_out = band_attention(proj, kv, _band_bias(b_rel_bias[0]), n_pairs, seq)
    x = mixer_tail(x, seq_out, proj, seq_width // mem_width, 1, out_norm_g=final_norm)
    return x.reshape(bsz, seq, d)
```

```python
import functools

import numpy as np
import jax
import jax.numpy as jnp
from jax import lax
from jax.experimental import pallas as pl
from jax.experimental.pallas import tpu as pltpu

F32 = jnp.float32
BF16 = jnp.bfloat16

HEAD_DIM = 64
PAIR = 2 * HEAD_DIM
CHUNK = 64
SCAN_PAIRS = 2
LEFT_CHUNKS = 8
BAND = (LEFT_CHUNKS + 1) * CHUNK
REL_MAX = 128
MEM_HEADS = 4
MEM_HEAD_DIM = 128
PEER_KEYS = 128
PEER_HEADS = 8
PEER_TOPK = 16
PEER_PICKS = PEER_HEADS * PEER_TOPK
GN_EPS = 64e-5
RMS_EPS = 1e-6
LORA_PAD = 128
VMEM_LIMIT = 48 * 1024 * 1024


def _params(*sem):
    return pltpu.CompilerParams(dimension_semantics=sem, vmem_limit_bytes=VMEM_LIMIT)


def _dot(a, b):
    return jnp.dot(a, b, preferred_element_type=F32)


def _dot_nt(a, b, precision=None):
    return lax.dot_general(a, b, (((1,), (1,)), ((), ())), precision=precision,
                           preferred_element_type=F32)


def _dot_tn(a, b, precision=None):
    return lax.dot_general(a, b, (((0,), (0,)), ((), ())), precision=precision,
                           preferred_element_type=F32)


def _rmsnorm_body(x_ref, g_ref, o_ref):
    x = x_ref[...]
    ms = jnp.mean(x * x, axis=-1, keepdims=True)
    o_ref[...] = (x * lax.rsqrt(ms + RMS_EPS) * g_ref[...]).astype(o_ref.dtype)


def rmsnorm(x, g, tm=512):
    n, d = x.shape
    tm = min(tm, n)
    return pl.pallas_call(
        _rmsnorm_body,
        grid=(n // tm,),
        in_specs=[pl.BlockSpec((tm, d), lambda i: (i, 0)), pl.BlockSpec((1, d), lambda i: (0, 0))],
        out_specs=pl.BlockSpec((tm, d), lambda i: (i, 0)),
        out_shape=jax.ShapeDtypeStruct((n, d), F32),
        compiler_params=_params("parallel"),
        name="rmsnorm",
    )(x, g.reshape(1, d))


def _normmm_body(x_ref, g_ref, w_ref, o_ref, *rest, emit_hn):
    lhs_ref = rest[-1]

    @pl.when(pl.program_id(1) == 0)
    def _():
        x = x_ref[...]
        ms = jnp.mean(x * x, axis=-1, keepdims=True)
        hn = x * lax.rsqrt(ms + RMS_EPS) * g_ref[...]
        lhs_ref[...] = hn.astype(BF16)
        if emit_hn:
            rest[0][...] = hn

    o_ref[...] = _dot(lhs_ref[...], w_ref[...]).astype(o_ref.dtype)


def norm_matmul(x, g, w_bf16, emit_hn=False, tm=512, tn=1024, out_dtype=BF16):
    n, d = x.shape
    nc = w_bf16.shape[1]
    tm, tn = min(tm, n), min(tn, nc)
    out_shape = [jax.ShapeDtypeStruct((n, nc), out_dtype)]
    out_specs = [pl.BlockSpec((tm, tn), lambda i, j: (i, j))]
    if emit_hn:
        out_shape.append(jax.ShapeDtypeStruct((n, d), F32))
        out_specs.append(pl.BlockSpec((tm, d), lambda i, j: (i, 0)))
    res = pl.pallas_call(
        functools.partial(_normmm_body, emit_hn=emit_hn),
        grid=(n // tm, nc // tn),
        in_specs=[pl.BlockSpec((tm, d), lambda i, j: (i, 0)),
                  pl.BlockSpec((1, d), lambda i, j: (0, 0)),
                  pl.BlockSpec((d, tn), lambda i, j: (0, j))],
        out_specs=out_specs,
        out_shape=out_shape,
        scratch_shapes=[pltpu.VMEM((tm, d), BF16)],
        compiler_params=_params("parallel", "arbitrary"),
        name="norm_matmul",
    )(x, g.reshape(1, d), w_bf16)
    return res if emit_hn else res[0]


def _mixmm_body(gid_ref, h_ref, hp_ref, mix_ref, w_ref, o_ref, lhs_ref, diff_ref, *, tm, seq):
    i = pl.program_id(0)
    j = pl.program_id(1)

    @pl.when(j == 0)
    def _():
        h = h_ref[...]
        prev = jnp.where((i * tm) % seq == 0, 0.0, hp_ref[7:8, :])
        row = lax.broadcasted_iota(jnp.int32, h.shape, 0)
        diff_ref[...] = jnp.where(row == 0, prev, pltpu.roll(h, 1, 0)) - h

    new_group = jnp.logical_or(j == 0, gid_ref[j] != gid_ref[jnp.maximum(j - 1, 0)])

    @pl.when(new_group)
    def _():
        lhs_ref[...] = (h_ref[...] + diff_ref[...] * mix_ref[0]).astype(BF16)

    o_ref[...] = _dot(lhs_ref[...], w_ref[...])


def mix_matmul(h, mix_tab, gid, w_bf16, seq, tn, tm=1024):
    n, d = h.shape
    nc = w_bf16.shape[1]
    tm = min(tm, seq)
    sub = tm // 8
    grid_spec = pltpu.PrefetchScalarGridSpec(
        num_scalar_prefetch=1,
        grid=(n // tm, nc // tn),
        in_specs=[pl.BlockSpec((tm, d), lambda i, j, g: (i, 0)),
                  pl.BlockSpec((8, d), lambda i, j, g: (jnp.maximum(i * sub - 1, 0), 0)),
                  pl.BlockSpec((1, 1, d), lambda i, j, g: (g[j], 0, 0)),
                  pl.BlockSpec((d, tn), lambda i, j, g: (0, j))],
        out_specs=pl.BlockSpec((tm, tn), lambda i, j, g: (i, j)),
        scratch_shapes=[pltpu.VMEM((tm, d), BF16), pltpu.VMEM((tm, d), F32)],
    )
    return pl.pallas_call(
        functools.partial(_mixmm_body, tm=tm, seq=seq),
        grid_spec=grid_spec,
        out_shape=jax.ShapeDtypeStruct((n, nc), F32),
        compiler_params=_params("parallel", "arbitrary"),
        name="mix_matmul",
    )(gid, h, h, mix_tab.reshape(mix_tab.shape[0], 1, d), w_bf16)


def _sigmoid(x):
    return 1.0 / (1.0 + jnp.exp(-x))


def _lora2_body(t_ref, w2_ref, a2_ref, g2_ref, w0_ref, a0_ref, lw_ref, a_ref, g_ref):
    t = t_ref[...]
    tw = jnp.tanh(t[:, :LORA_PAD]).astype(BF16)
    ta = t[:, LORA_PAD:2 * LORA_PAD].astype(BF16)
    tg = _sigmoid(t[:, 2 * LORA_PAD:]).astype(BF16)
    u = w0_ref[...] + _dot(tw, w2_ref[...])
    softplus_neg_u = jnp.maximum(-u, 0.0) + jnp.log(1.0 + jnp.exp(-jnp.abs(u)))
    lw_ref[...] = -jnp.exp(-softplus_neg_u - 0.5)
    a_ref[...] = _sigmoid(a0_ref[...] + _dot(ta, a2_ref[...]))
    g_ref[...] = _dot(tg, g2_ref[...])


def lora_stage2(t1, w2p, a2p, g2, w0, a0, tm=256):
    n = t1.shape[0]
    width = w2p.shape[1]
    tm = min(tm, n)
    full = lambda a: pl.BlockSpec(a.shape, lambda i: (0, 0))
    row = pl.BlockSpec((tm, width), lambda i: (i, 0))
    w0 = w0.reshape(1, width)
    a0 = a0.reshape(1, width)
    return pl.pallas_call(
        _lora2_body,
        grid=(n // tm,),
        in_specs=[pl.BlockSpec((tm, t1.shape[1]), lambda i: (i, 0)), full(w2p), full(a2p), full(g2),
                  full(w0), full(a0)],
        out_specs=[row, row, row],
        out_shape=[jax.ShapeDtypeStruct((n, width), F32)] * 3,
        compiler_params=_params("parallel"),
        name="lora_stage2",
    )(t1, w2p, a2p, g2, w0, a0)


def _head_group_sum(x):
    width = x.shape[-1]
    r = lax.broadcasted_iota(jnp.int32, (width, width), 0) // HEAD_DIM
    c = lax.broadcasted_iota(jnp.int32, (width, width), 1) // HEAD_DIM
    ones = jnp.where(r == c, 1.0, 0.0).astype(BF16)
    hi = x.astype(BF16)
    rest = x - hi.astype(F32)
    mid = rest.astype(BF16)
    lo = (rest - mid.astype(F32)).astype(BF16)
    return _dot(hi, ones) + _dot(mid, ones) + _dot(lo, ones)


def _wkv_prepare_body(r_ref, k_ref, v_ref, lw_ref, a_ref, kk_ref, ka_ref,
                      g_ref, s0c_ref, rp_ref, y0_ref, *, tb):
    r = r_ref[...]
    k = k_ref[...]
    v = v_ref[...]
    lw = lw_ref[...]
    a = a_ref[...]
    kk = k * kk_ref[...]
    norm = jnp.sqrt(_head_group_sum(kk * kk))
    kk = kk / jnp.maximum(norm, 1e-12)
    kmod = k * (1.0 + (a - 1.0) * ka_ref[...])
    avec = -kk
    bvec = kk * a

    tr = lax.broadcasted_iota(jnp.int32, (CHUNK, CHUNK), 0)
    tc = lax.broadcasted_iota(jnp.int32, (CHUNK, CHUNK), 1)
    tri = jnp.where(tr >= tc, 1.0, 0.0).astype(BF16)
    lw_hi = lw.astype(BF16)
    lw_rest = lw - lw_hi.astype(F32)
    lw_mid = lw_rest.astype(BF16)
    lw_lo = (lw_rest - lw_mid.astype(F32)).astype(BF16)

    lane = lax.broadcasted_iota(jnp.int32, (CHUNK, PAIR), 1)
    head0 = lane < HEAD_DIM
    row = lax.broadcasted_iota(jnp.int32, (PAIR, PAIR), 0)
    col = lax.broadcasted_iota(jnp.int32, (PAIR, PAIR), 1)
    strict = row > col
    lower = row >= col
    eye = jnp.where(row == col, 1.0, 0.0).astype(F32)

    def stack(x):
        return jnp.concatenate([jnp.where(head0, x, 0.0), jnp.where(head0, 0.0, x)], axis=0)

    chunks = range(tb // CHUNK)
    pre = []
    for c in chunks:
        sl = slice(c * CHUNK, (c + 1) * CHUNK)
        cm = _dot(tri, lw_hi[sl]) + _dot(tri, lw_mid[sl]) + _dot(tri, lw_lo[sl])
        cend = cm[CHUNK - 1:CHUNK]
        e_in = jnp.exp(cm)
        e_out = jnp.exp(-cm)
        e_tail = jnp.exp(cend - cm)
        a_s = stack(avec[sl] * jnp.exp(cm - lw[sl]))
        r_s = stack(r[sl] * e_in)
        b_s = stack(bvec[sl] * e_out)
        k_s = stack(kmod[sl] * e_out)
        bt_s = stack(bvec[sl] * e_tail)
        kt_s = stack(kmod[sl] * e_tail)
        v_s = stack(v[sl])
        p = _dot_nt(jnp.concatenate([a_s, r_s], axis=0).astype(BF16),
                    jnp.concatenate([b_s, k_s], axis=0).astype(BF16))
        pre.append(dict(cend=cend, a_s=a_s, r_s=r_s, bt_s=bt_s, kt_s=kt_s, v_s=v_s,
                        l_ab=jnp.where(strict, p[:PAIR, :PAIR], 0.0),
                        l_ak=jnp.where(strict, p[:PAIR, PAIR:], 0.0),
                        a_rb=jnp.where(lower, p[PAIR:, :PAIR], 0.0),
                        a_rk=jnp.where(lower, p[PAIR:, PAIR:], 0.0)))

    ms = [q["l_ab"] for q in pre]
    ts = [eye + m for m in ms]
    ms = [_dot(m.astype(BF16), m.astype(BF16)) for m in ms]
    for _ in range(4):
        both = [_dot(m.astype(BF16), jnp.concatenate([t, m], axis=1).astype(BF16)) for t, m in zip(ts, ms)]
        ts = [t + b[:, :PAIR] for t, b in zip(ts, both)]
        ms = [b[:, PAIR:] for b in both]
    ts = [t + _dot(m.astype(BF16), t.astype(BF16)) for t, m in zip(ts, ms)]

    lvs = [_dot(q["l_ak"].astype(BF16), q["v_s"].astype(BF16)) for q in pre]
    aus = [_dot(t.astype(BF16), jnp.concatenate([q["a_s"], lv], axis=1).astype(BF16))
           for t, q, lv in zip(ts, pre, lvs)]
    zero = jnp.zeros((PAIR, PAIR), F32)
    xs = [_dot(jnp.concatenate([q["a_rb"], q["a_rk"]], axis=1).astype(BF16),
               jnp.concatenate([au, jnp.concatenate([zero, q["v_s"]], axis=1)], axis=0).astype(BF16))
          for q, au in zip(pre, aus)]
    for c, q, au, x in zip(chunks, pre, aus, xs):
        rp_ref[0, c] = (q["r_s"] + x[:, :PAIR]).astype(rp_ref.dtype)
        y0_ref[0, c] = x[:, PAIR:]
        ap = au[:, :PAIR]
        u0 = au[:, PAIR:]
        g_ref[0, c] = (eye * jnp.exp(q["cend"])
                       + _dot_tn(ap.astype(BF16), q["bt_s"].astype(BF16))).astype(g_ref.dtype)
        s0c_ref[0, c] = _dot_tn(jnp.concatenate([u0, q["v_s"]], axis=0).astype(BF16),
                                jnp.concatenate([q["bt_s"], q["kt_s"]], axis=0).astype(BF16))


def wkv_chunk_prepare(proj, lw, a, k_k, k_a, n_pairs, tb=1024):
    n = proj.shape[0]
    tb = min(tb, n)
    cpb = tb // CHUNK
    col = lambda off: pl.BlockSpec((tb, PAIR), lambda i, p: (i, off + p))
    par = pl.BlockSpec((1, PAIR), lambda i, p: (0, p))
    blk = pl.BlockSpec((1, cpb, PAIR, PAIR), lambda i, p: (p, i, 0, 0))
    shp = lambda dt: jax.ShapeDtypeStruct((n_pairs, n // CHUNK, PAIR, PAIR), dt)
    return pl.pallas_call(
        functools.partial(_wkv_prepare_body, tb=tb),
        grid=(n // tb, n_pairs),
        in_specs=[col(0), col(n_pairs), col(2 * n_pairs), col(0), col(0), par, par],
        out_specs=[blk] * 4,
        out_shape=[shp(BF16), shp(F32), shp(BF16), shp(F32)],
        compiler_params=_params("parallel", "parallel"),
        name="wkv_chunk_prepare",
    )(proj, proj, proj, lw, a, k_k.reshape(1, -1), k_a.reshape(1, -1))


def _wkv_scan_body(g_ref, s0c_ref, rp_ref, y0_ref, r_ref, k_ref, v_ref, a_ref, gate_ref,
                   ka_ref, rk_ref, lng_ref, lnb_ref, o_ref, s_ref, y_ref, *, cpb):
    @pl.when(pl.program_id(2) == 0)
    def _():
        s_ref[...] = jnp.zeros_like(s_ref)

    pairs = range(SCAN_PAIRS)
    states = [[s_ref[w].astype(BF16)] for w in pairs]
    for c in range(cpb):
        for w in pairs:
            s = _dot(states[w][c], g_ref[w, c].astype(BF16)) + s0c_ref[w, c]
            if c + 1 < cpb:
                states[w].append(s.astype(BF16))
            else:
                s_ref[w] = s
    for c in range(cpb):
        for w in pairs:
            y_st = _dot_nt(rp_ref[w, c].astype(BF16), states[w][c]) + y0_ref[w, c]
            y_ref[c * CHUNK:(c + 1) * CHUNK, w * PAIR:(w + 1) * PAIR] = y_st[:CHUNK] + y_st[CHUNK:]

    y = y_ref[...]
    inv = 1.0 / HEAD_DIM
    mu = _head_group_sum(y) * inv
    yc = y - mu
    var = _head_group_sum(yc * yc) * inv
    yn = yc * lax.rsqrt(var + GN_EPS) * lng_ref[...] + lnb_ref[...]
    r = r_ref[...]
    kmod = k_ref[...] * (1.0 + (a_ref[...] - 1.0) * ka_ref[...])
    bonus = _head_group_sum(r * kmod * rk_ref[...]) * v_ref[...]
    o_ref[...] = ((yn + bonus) * gate_ref[...]).astype(o_ref.dtype)


def wkv_chunk_scan(prep, proj, a, gate, k_a, r_k, ln_g, ln_b, n_pairs, seq, tb=1024):
    g_all, s0c_all, rp_all, y0_all = prep
    n = proj.shape[0]
    tb = min(tb, seq)
    cpb = tb // CHUNK
    nblk = seq // tb
    assert n_pairs % SCAN_PAIRS == 0
    width = SCAN_PAIRS * PAIR
    groups = n_pairs // SCAN_PAIRS
    blk = pl.BlockSpec((SCAN_PAIRS, cpb, PAIR, PAIR), lambda b, p, t: (p, b * nblk + t, 0, 0))
    col = lambda off: pl.BlockSpec((tb, width), lambda b, p, t: (b * nblk + t, off + p))
    par = pl.BlockSpec((1, width), lambda b, p, t: (0, p))
    vec = lambda z: z.reshape(1, -1)
    return pl.pallas_call(
        functools.partial(_wkv_scan_body, cpb=cpb),
        grid=(n // seq, groups, nblk),
        in_specs=[blk] * 4 + [col(0), col(groups), col(2 * groups), col(0), col(0)] + [par] * 4,
        out_specs=col(0),
        out_shape=jax.ShapeDtypeStruct((n, n_pairs * PAIR), BF16),
        scratch_shapes=[pltpu.VMEM((SCAN_PAIRS, PAIR, PAIR), F32), pltpu.VMEM((tb, width), F32)],
        compiler_params=_params("parallel", "parallel", "arbitrary"),
        name="wkv_chunk_scan",
    )(g_all, s0c_all, rp_all, y0_all, proj, proj, proj, a, gate,
      vec(k_a), vec(r_k), vec(ln_g), vec(ln_b))


def _memattn_body(q_ref, m_ref, o_ref):
    width = MEM_HEADS * MEM_HEAD_DIM
    scale = MEM_HEAD_DIM ** -0.5
    cols = [slice(h * MEM_HEAD_DIM, (h + 1) * MEM_HEAD_DIM) for h in range(MEM_HEADS)]
    scores = [_dot_nt(q_ref[:, sl].astype(BF16), m_ref[:, sl].astype(BF16)) * scale for sl in cols]
    exps = [jnp.exp(s - jnp.max(s, axis=-1, keepdims=True)) for s in scores]
    outs = [_dot(e.astype(BF16), m_ref[:, width + sl.start:width + sl.stop].astype(BF16))
            for sl, e in zip(cols, exps)]
    for sl, e, o in zip(cols, exps, outs):
        o_ref[:, sl] = (o / jnp.sum(e, axis=-1, keepdims=True)).astype(o_ref.dtype)


def memory_attention(proj, q_col_block, mkv, seq, n_mem, tm=512):
    n = proj.shape[0]
    width = MEM_HEADS * MEM_HEAD_DIM
    tm = min(tm, seq)
    nblk = seq // tm
    return pl.pallas_call(
        _memattn_body,
        grid=(n // seq, nblk),
        in_specs=[pl.BlockSpec((tm, width), lambda b, t: (b * nblk + t, q_col_block)),
                  pl.BlockSpec((n_mem, 2 * width), lambda b, t: (b, 0))],
        out_specs=pl.BlockSpec((tm, width), lambda b, t: (b * nblk + t, 0)),
        out_shape=jax.ShapeDtypeStruct((n, width), BF16),
        compiler_params=_params("parallel", "parallel"),
        name="memory_attention",
    )(proj, mkv)


def _outproj_body(s_ref, m_ref, w1_ref, w2_ref, x_ref, o_ref):
    o_ref[...] = x_ref[...] + _dot(s_ref[...], w1_ref[...]) + _dot(m_ref[...], w2_ref[...])


def out_projection(seq_out, mem_out, w_seq, w_mem, x, tm=512, tn=1024):
    n, d = x.shape
    tm, tn = min(tm, n), min(tn, d)
    ws, wm = seq_out.shape[1], mem_out.shape[1]
    return pl.pallas_call(
        _outproj_body,
        grid=(n // tm, d // tn),
        in_specs=[pl.BlockSpec((tm, ws), lambda i, j: (i, 0)),
                  pl.BlockSpec((tm, wm), lambda i, j: (i, 0)),
                  pl.BlockSpec((ws, tn), lambda i, j: (0, j)),
                  pl.BlockSpec((wm, tn), lambda i, j: (0, j)),
                  pl.BlockSpec((tm, tn), lambda i, j: (i, j))],
        out_specs=pl.BlockSpec((tm, tn), lambda i, j: (i, j)),
        out_shape=jax.ShapeDtypeStruct((n, d), F32),
        compiler_params=_params("parallel", "parallel"),
        name="out_projection",
    )(seq_out, mem_out, w_seq, w_mem, x)


def _bandattn_body(q_ref, kp_ref, kc_ref, vp_ref, vc_ref, bias_ref, o_ref, *, tq):
    qi = pl.program_id(2)
    scale = HEAD_DIM ** -0.5
    k_all = jnp.concatenate([kp_ref[...], kc_ref[...]], axis=0).astype(BF16)
    v_all = jnp.concatenate([vp_ref[...], vc_ref[...]], axis=0).astype(BF16)
    lane = lax.broadcasted_iota(jnp.int32, (CHUNK, PAIR), 1)
    head0 = lane < HEAD_DIM
    kcol = lax.broadcasted_iota(jnp.int32, (2 * CHUNK, BAND), 1)
    pad = LEFT_CHUNKS * CHUNK
    bias = bias_ref[...]
    chunks = range(tq // CHUNK)
    starts = [tq - pad + j * CHUNK for j in chunks]
    scores = []
    for j, start in zip(chunks, starts):
        q = q_ref[j * CHUNK:(j + 1) * CHUNK, :] * scale
        q2 = jnp.concatenate([jnp.where(head0, q, 0.0), jnp.where(head0, 0.0, q)], axis=0).astype(BF16)
        scores.append(_dot_nt(q2, k_all[start:start + BAND]))
    exps = []
    for start, s in zip(starts, scores):
        valid = jnp.logical_or(qi > 0, kcol + start >= tq)
        s = jnp.where(valid, s + bias, -jnp.inf)
        exps.append(jnp.exp(s - jnp.max(s, axis=-1, keepdims=True)))
    outs = [_dot(e.astype(BF16), v_all[start:start + BAND]) for start, e in zip(starts, exps)]
    for j, e, o in zip(chunks, exps, outs):
        o = o / jnp.sum(e, axis=-1, keepdims=True)
        o_ref[j * CHUNK:(j + 1) * CHUNK, :] = jnp.where(head0, o[:CHUNK], o[CHUNK:]).astype(o_ref.dtype)


def band_attention(proj, kv, bias, n_pairs, seq, tq=512):
    n = proj.shape[0]
    tq = min(tq, seq)
    assert tq >= LEFT_CHUNKS * CHUNK
    nblk = seq // tq
    cur = lambda off: pl.BlockSpec((tq, PAIR), lambda b, p, t: (b * nblk + t, off + p))
    prev = lambda off: pl.BlockSpec((tq, PAIR), lambda b, p, t: (b * nblk + jnp.maximum(t - 1, 0), off + p))
    return pl.pallas_call(
        functools.partial(_bandattn_body, tq=tq),
        grid=(n // seq, n_pairs, nblk),
        in_specs=[cur(0), prev(0), cur(0), prev(n_pairs), cur(n_pairs),
                  pl.BlockSpec((2 * CHUNK, BAND), lambda b, p, t: (p, 0))],
        out_specs=cur(0),
        out_shape=jax.ShapeDtypeStruct((n, n_pairs * PAIR), BF16),
        compiler_params=_params("parallel", "parallel", "parallel"),
        name="band_attention",
    )(proj, kv, kv, kv, kv, bias.reshape(-1, BAND))


def _top16(scores, payloads):
    nl = scores[0].shape[1]
    rids = [lax.broadcasted_iota(jnp.int32, s.shape, 0).astype(F32) for s in scores]
    slot = lax.broadcasted_iota(jnp.int32, (PEER_TOPK, nl), 0)
    tile = 8

    def winner(s, rid, payload):
        items = [(s[g:g + tile], rid[g:g + tile], None if payload is None else payload[g:g + tile])
                 for g in range(0, s.shape[0], tile)]
        while len(items) > 1:
            nxt = []
            for k in range(0, len(items) - 1, 2):
                (va, ia, pa), (vb, ib, pb) = items[k], items[k + 1]
                keep = va >= vb
                nxt.append((jnp.maximum(va, vb), jnp.where(keep, ia, ib),
                            None if pa is None else jnp.where(keep, pa, pb)))
            if len(items) % 2:
                nxt.append(items[-1])
            items = nxt
        v, idx, p = items[0]
        for shift in (4, 2, 1):
            pv, pi = pltpu.roll(v, shift, 0), pltpu.roll(idx, shift, 0)
            take = (pv > v) | ((pv == v) & (pi < idx))
            if p is not None:
                p = jnp.where(take, pltpu.roll(p, shift, 0), p)
            v, idx = jnp.where(take, pv, v), jnp.where(take, pi, idx)
        return v, idx, p

    def body(i, carry):
        sel = slot == i
        out = []
        for (s, vals, picks), rid, payload in zip(carry, rids, payloads):
            m, am, p = winner(s, rid, payload)
            hit = rid == jnp.concatenate([am] * (s.shape[0] // tile), axis=0)
            pick = am if payload is None else p
            out.append((jnp.where(hit, -jnp.inf, s), jnp.where(sel, m[0:1], vals), jnp.where(sel, pick[0:1], picks)))
        return tuple(out)

    zero = jnp.zeros((PEER_TOPK, nl), F32)
    res = lax.fori_loop(0, PEER_TOPK, body, tuple((s, zero, zero) for s in scores))
    return [(vals, picks) for _, vals, picks in res]


def _pair_candidates(a, b, combine):
    half = PEER_TOPK // 2
    rows = [combine(a[0:1], b)]
    rows += [combine(a[i:i + 1], b[:half]) for i in range(1, half)]
    rows.append(combine(a[half:], b[0:1]))
    return jnp.concatenate(rows, axis=0)


TOPK_HEADS = 2


def _peer_topk_body(q_ref, keys_ref, eidx_ref, gate_ref):
    q = q_ref[...].astype(BF16)
    scores = []
    for h in range(TOPK_HEADS):
        for half in range(2):
            col = (2 * h + half) * PEER_KEYS
            scores.append(_dot_nt(keys_ref[h, half].astype(BF16), q[:, col:col + PEER_KEYS]))
    cands, cidxs = [], []
    for h in range(TOPK_HEADS):
        (a, i1), (b, i2) = _top16(scores[2 * h:2 * h + 2], [None, None])
        cands.append(_pair_candidates(a, b, lambda x, y: x + y))
        cidxs.append(_pair_candidates(i1, i2, lambda x, y: x * PEER_KEYS + y))
    for h, (top, eidx) in enumerate(_top16(cands, cidxs)):
        rows = slice(h * PEER_TOPK, (h + 1) * PEER_TOPK)
        e = jnp.exp(top - top[0:1])
        gate_ref[rows, :] = e / jnp.sum(e, axis=0, keepdims=True)
        eidx_ref[rows, :] = eidx.astype(jnp.int32)


def peer_topk(q, keys, tl=128):
    n = q.shape[0]
    tl = min(tl, n)
    blk = pl.BlockSpec((TOPK_HEADS * PEER_TOPK, tl), lambda i, h: (h, i))
    return pl.pallas_call(
        _peer_topk_body,
        grid=(n // tl, PEER_HEADS // TOPK_HEADS),
        in_specs=[pl.BlockSpec((tl, TOPK_HEADS * 2 * PEER_KEYS), lambda i, h: (i, h)),
                  pl.BlockSpec((TOPK_HEADS, 2, PEER_KEYS, PEER_KEYS), lambda i, h: (h, 0, 0, 0))],
        out_specs=[blk, blk],
        out_shape=[jax.ShapeDtypeStruct((PEER_PICKS, n), jnp.int32),
                   jax.ShapeDtypeStruct((PEER_PICKS, n), F32)],
        compiler_params=_params("parallel", "parallel"),
        name="peer_topk",
    )(q, keys)


PEER_SLOTS = 8
LANES = 128
PITCH_PAD = 4


def _pack_body(u_ref, v_ref, o_ref):
    hi = lax.bitcast_convert_type(u_ref[0].astype(BF16).astype(F32), jnp.uint32)
    lo = lax.bitcast_convert_type(v_ref[0].astype(BF16).astype(F32), jnp.uint32)
    words = hi | (lo >> 16)
    for c in range(o_ref.shape[1]):
        o_ref[:, c, :] = words[:, c * LANES:(c + 1) * LANES]


def _pack_expert_table(u, v, layer, te=256):
    _, n_exp, d = u.shape
    chunks = d // LANES
    te = min(te, n_exp)
    packed = pl.pallas_call(
        _pack_body,
        grid=(n_exp // te,),
        in_specs=[pl.BlockSpec((1, te, d), lambda i: (layer, i, 0))] * 2,
        out_specs=pl.BlockSpec((te, chunks, LANES), lambda i: (i, 0, 0)),
        out_shape=jax.ShapeDtypeStruct((n_exp, chunks, LANES), jnp.uint32),
        compiler_params=_params("parallel"),
        name="pack_expert_table",
    )(u, v)
    return packed.reshape(n_exp * chunks, LANES)


def _peer_ffn_body(eidx_ref, gate_ref, hn_ref, x_ref, tab_ref, *rest, tb, d, out_norm):
    norm_ref, o_ref, scratch = (rest[0], rest[1], rest[2:]) if out_norm else (None, rest[0], rest[1:])
    rows_refs = scratch[:PEER_SLOTS]
    sem_ref, w_ref, stage_ref, hn_rows_ref, gate_rows_ref, ids_ref, ids_sem_ref = scratch[PEER_SLOTS:]
    chunks = d // LANES
    pitch = chunks + PITCH_PAD

    def issue(ids, slot):
        for e in range(PEER_PICKS):
            src = tab_ref.at[pl.ds(pl.multiple_of(ids(e) * chunks, chunks), chunks), :]
            dst = rows_refs[slot].at[pl.ds(e * pitch, chunks), :]
            pltpu.make_async_copy(src, dst, sem_ref.at[slot]).start(priority=e % 2)

    def block_ids(t):
        return lambda e: eidx_ref[t, e]

    def ids_window(g, half):
        rows = g * PEER_SLOTS if isinstance(g, int) else pl.multiple_of(g * PEER_SLOTS, PEER_SLOTS)
        return pltpu.make_async_copy(eidx_ref.at[pl.ds(rows, 2 * PEER_SLOTS), :], ids_ref.at[half],
                                     ids_sem_ref.at[half])

    def wait(slot):
        total = PEER_PICKS * chunks
        pltpu.make_async_copy(tab_ref.at[pl.ds(0, total), :],
                              rows_refs[slot].at[pl.ds(0, total), :], sem_ref.at[slot]).wait()

    pick_diag = (lax.broadcasted_iota(jnp.int32, (PEER_PICKS, PEER_PICKS), 0)
                 == lax.broadcasted_iota(jnp.int32, (PEER_PICKS, PEER_PICKS), 1))
    hi_mask = jnp.uint32(0xFFFF0000)

    def words(slot, c):
        return rows_refs[slot][pl.ds(c, PEER_PICKS, stride=pitch), :]

    def pick_weights(slot, hn_rows, gate_rows, row):
        acc = jnp.zeros((PEER_PICKS, LANES), F32)
        for c in range(chunks):
            u = lax.bitcast_convert_type(words(slot, c) & hi_mask, F32)
            acc = acc + u * hn_rows[row:row + 1, c * LANES:(c + 1) * LANES]
        act = jnp.sum(acc, axis=1, keepdims=True)
        act = 0.5 * act * (1.0 + lax.erf(act * (2.0 ** -0.5)))
        gate = jnp.sum(jnp.where(pick_diag, gate_rows[row:row + 1, :], 0.0), axis=1, keepdims=True)
        w_ref[slot] = jnp.broadcast_to(gate * act, (PEER_PICKS, LANES))
        return acc

    def combine(slot, after):
        last = lax.bitcast_convert_type(after[PEER_PICKS - 8:], jnp.int32) == -1
        never = jnp.concatenate([last] * (PEER_PICKS // 8), axis=0)
        w = jnp.where(never, 0.0, w_ref[slot])
        for c in range(chunks):
            v = lax.bitcast_convert_type(words(slot, c) << 16, F32)
            stage_ref[slot:slot + 1, c * LANES:(c + 1) * LANES] = jnp.sum(v * w, axis=0, keepdims=True)

    def store_group(g):
        first = g * PEER_SLOTS if isinstance(g, int) else pl.multiple_of(g * PEER_SLOTS, PEER_SLOTS)
        rows = pl.ds(first, PEER_SLOTS)
        y = x_ref[rows, :] + stage_ref[...]
        if norm_ref is not None:
            y = y * lax.rsqrt(jnp.mean(y * y, axis=-1, keepdims=True) + RMS_EPS) * norm_ref[...]
        o_ref[rows, :] = y

    ahead = PEER_SLOTS - 1
    groups = tb // PEER_SLOTS
    steady = groups - 1
    ids_window(0, 0).start()
    for t in range(ahead):
        issue(block_ids(t), t)
    wait(0)
    first_pass = pick_weights(0, hn_ref, gate_ref, 0)

    def group(g, half):
        first = g * PEER_SLOTS if isinstance(g, int) else pl.multiple_of(g * PEER_SLOTS, PEER_SLOTS)
        rows = pl.ds(first, 2 * PEER_SLOTS)
        hn_rows_ref[...] = hn_ref[rows, :]
        gate_rows_ref[...] = gate_ref[rows, :]
        ids_window(g, half).wait()
        if isinstance(g, int):
            if g + 1 < steady:
                ids_window(g + 1, 1 - half).start()
        else:
            @pl.when(g + 1 < steady)
            def _():
                ids_window(g + 1, 1 - half).start()
        for s in range(PEER_SLOTS):
            wait((s + 1) % PEER_SLOTS)
            combine(s, pick_weights((s + 1) % PEER_SLOTS, hn_rows_ref, gate_rows_ref, s + 1))
            issue(lambda e, s=s: ids_ref[half, s + ahead, e], (s + ahead) % PEER_SLOTS)
        store_group(g)

    def group_pair(k, carry):
        group(2 * k, 0)
        group(2 * k + 1, 1)
        return carry

    lax.fori_loop(0, steady // 2, group_pair, 0)
    if steady % 2:
        group(steady - 1, 0)

    for s in range(PEER_SLOTS):
        t = (groups - 1) * PEER_SLOTS + s
        if t + ahead < tb:
            issue(block_ids(t + ahead), (s + ahead) % PEER_SLOTS)
        if t + 1 < tb:
            wait((s + 1) % PEER_SLOTS)
            first_pass = pick_weights((s + 1) % PEER_SLOTS, hn_ref, gate_ref, t + 1)
        combine(s, first_pass)
    store_group(groups - 1)


def peer_expert_ffn(eidx, gate, hn, x, table, tb=128, out_norm_g=None):
    n, d = x.shape
    tb = min(tb, n)
    assert tb % PEER_SLOTS == 0 and table.shape[1] == LANES
    assert table.shape[0] >= PEER_PICKS * (d // LANES)
    slot_rows = PEER_PICKS * (d // LANES + PITCH_PAD)
    out_norm = out_norm_g is not None
    in_specs = [pl.BlockSpec((tb, PEER_PICKS), lambda i: (i, 0), memory_space=pltpu.SMEM),
                pl.BlockSpec((tb, PEER_PICKS), lambda i: (i, 0)),
                pl.BlockSpec((tb, d), lambda i: (i, 0)),
                pl.BlockSpec((tb, d), lambda i: (i, 0)),
                pl.BlockSpec(memory_space=pl.ANY)]
    operands = [eidx, gate, hn, x, table]
    if out_norm:
        in_specs.append(pl.BlockSpec((1, d), lambda i: (0, 0)))
        operands.append(out_norm_g.reshape(1, d))
    return pl.pallas_call(
        functools.partial(_peer_ffn_body, tb=tb, d=d, out_norm=out_norm),
        grid=(n // tb,),
        in_specs=in_specs,
        out_specs=pl.BlockSpec((tb, d), lambda i: (i, 0)),
        out_shape=jax.ShapeDtypeStruct((n, d), F32),
        scratch_shapes=[pltpu.VMEM((slot_rows, LANES), jnp.uint32)] * PEER_SLOTS
                       + [pltpu.SemaphoreType.DMA((PEER_SLOTS,)),
                          pltpu.VMEM((PEER_SLOTS, PEER_PICKS, LANES), F32),
                          pltpu.VMEM((PEER_SLOTS, d), F32),
                          pltpu.VMEM((2 * PEER_SLOTS, d), F32),
                          pltpu.VMEM((2 * PEER_SLOTS, PEER_PICKS), F32),
                          pltpu.SMEM((2, 2 * PEER_SLOTS, PEER_PICKS), jnp.int32),
                          pltpu.SemaphoreType.DMA((2,))],
        compiler_params=_params("arbitrary"),
        name="peer_expert_ffn",
    )(*operands)


def peer_layer(x, norm_g, wq, keys, u_all, v_all, layer, tb=256, out_norm_g=None):
    q, hn = norm_matmul(x, norm_g, wq.astype(BF16), emit_hn=True)
    eidx, gate = peer_topk(q, keys)
    return peer_expert_ffn(eidx.T, gate.T, hn, x, _pack_expert_table(u_all, v_all, layer), tb=tb,
                           out_norm_g=out_norm_g)


def _band_bias(rel_bias):
    n_rel = rel_bias.shape[1]
    far = BAND - n_rel + CHUNK - 1
    long_row = jnp.concatenate([jnp.broadcast_to(rel_bias[:, n_rel - 1:], (rel_bias.shape[0], far)),
                                rel_bias[:, ::-1]], axis=1)
    rows = [long_row[:, CHUNK - 1 - i:CHUNK - 1 - i + BAND] for i in range(CHUNK)]
    return jnp.stack(rows, axis=1).astype(F32)


def kernel(x, mem, norm_mix, norm_ffn, norm_mem, w_mem_kv, w_out, peer_wq, peer_keys, peer_u, peer_v, a_mix, a_w_in, a_w0, a_w1, a_w2, a_a0, a_a1, a_a2, a_g1, a_g2, a_k_k, a_k_a, a_r_k, a_ln_g, a_ln_b, kv_norm, w_kv_shared, b_w_in, b_rel_bias, final_norm):
    bsz, seq, d = x.shape
    n = bsz * seq
    n_mem = mem.shape[1]
    seq_width = a_w0.shape[1]
    n_pairs = seq_width // PAIR
    mem_width = MEM_HEADS * MEM_HEAD_DIM
    x = x.reshape(n, d)
    mem2 = mem.reshape(bsz * n_mem, d)

    def mixer_tail(x, seq_out, proj, q_col_block, layer, out_norm_g=None):
        mkv = norm_matmul(mem2, norm_mem[layer], w_mem_kv[layer].astype(BF16), tm=256)
        mem_out = memory_attention(proj, q_col_block, mkv, seq, n_mem)
        wo = w_out[layer].astype(BF16)
        x = out_projection(seq_out, mem_out, wo[:seq_width], wo[seq_width:], x)
        return peer_layer(x, norm_ffn[layer], peer_wq[layer], peer_keys[layer], peer_u, peer_v, layer,
                          out_norm_g=out_norm_g)

    h = rmsnorm(x, norm_mix[0])
    mix_tab = jnp.concatenate([a_mix[0], jnp.zeros((1, d), F32)], axis=0)
    tiles = seq_width // 512
    gid_main = jnp.asarray([0] * tiles + [2] * tiles + [3] * tiles + [6] * (mem_width // 512), jnp.int32)
    proj = mix_matmul(h, mix_tab, gid_main, a_w_in[0].astype(BF16), seq, tn=512)
    rank = a_w1.shape[2]
    padc = lambda w: jnp.pad(w, ((0, 0), (0, LORA_PAD - rank)))
    padr = lambda w: jnp.pad(w, ((0, LORA_PAD - rank), (0, 0)))
    w_l1 = jnp.concatenate([padc(a_w1[0]), padc(a_a1[0]), a_g1[0]], axis=1).astype(BF16)
    gid_l1 = jnp.asarray([1, 4] + [5] * (a_g1.shape[2] // LORA_PAD), jnp.int32)
    t1 = mix_matmul(h, mix_tab, gid_l1, w_l1, seq, tn=LORA_PAD)
    lw, a_iclr, gate = lora_stage2(t1, padr(a_w2[0]).astype(BF16), padr(a_a2[0]).astype(BF16),
                                   a_g2[0].astype(BF16), a_w0[0], a_a0[0])
    prep = wkv_chunk_prepare(proj, lw, a_iclr, a_k_k[0], a_k_a[0], n_pairs)
    seq_out = wkv_chunk_scan(prep, proj, a_iclr, gate, a_k_a[0], a_r_k[0].reshape(-1),
                             a_ln_g[0], a_ln_b[0], n_pairs, seq)
    x = mixer_tail(x, seq_out, proj, (3 * seq_width) // mem_width, 0)

    kv = norm_matmul(x, kv_norm, w_kv_shared.astype(BF16), tm=1024)

    proj = norm_matmul(x, norm_mix[1], b_w_in[0].astype(BF16), tm=1024)
    seq_out = band_attention(proj, kv, _band_bias(b_rel_bias[0]), n_pairs, seq)
    x = mixer_tail(x, seq_out, proj, seq_width // mem_width, 1, out_norm_g=final_norm)
    return x.reshape(bsz, seq, d)
```

```python
import functools

import numpy as np
import jax
import jax.numpy as jnp
from jax import lax
from jax.experimental import pallas as pl
from jax.experimental.pallas import tpu as pltpu

F32 = jnp.float32
BF16 = jnp.bfloat16

HEAD_DIM = 64
PAIR = 2 * HEAD_DIM
CHUNK = 64
SCAN_PAIRS = 2
LEFT_CHUNKS = 8
BAND = (LEFT_CHUNKS + 1) * CHUNK
REL_MAX = 128
MEM_HEADS = 4
MEM_HEAD_DIM = 128
PEER_KEYS = 128
PEER_HEADS = 8
PEER_TOPK = 16
PEER_PICKS = PEER_HEADS * PEER_TOPK
GN_EPS = 64e-5
RMS_EPS = 1e-6
LORA_PAD = 128
VMEM_LIMIT = 48 * 1024 * 1024


def _params(*sem):
    return pltpu.CompilerParams(dimension_semantics=sem, vmem_limit_bytes=VMEM_LIMIT)


def _dot(a, b):
    return jnp.dot(a, b, preferred_element_type=F32)


def _dot_nt(a, b, precision=None):
    return lax.dot_general(a, b, (((1,), (1,)), ((), ())), precision=precision,
                           preferred_element_type=F32)


def _dot_tn(a, b, precision=None):
    return lax.dot_general(a, b, (((0,), (0,)), ((), ())), precision=precision,
                           preferred_element_type=F32)


def _rmsnorm_body(x_ref, g_ref, o_ref):
    x = x_ref[...]
    ms = jnp.mean(x * x, axis=-1, keepdims=True)
    o_ref[...] = (x * lax.rsqrt(ms + RMS_EPS) * g_ref[...]).astype(o_ref.dtype)


def rmsnorm(x, g, tm=512):
    n, d = x.shape
    tm = min(tm, n)
    return pl.pallas_call(
        _rmsnorm_body,
        grid=(n // tm,),
        in_specs=[pl.BlockSpec((tm, d), lambda i: (i, 0)), pl.BlockSpec((1, d), lambda i: (0, 0))],
        out_specs=pl.BlockSpec((tm, d), lambda i: (i, 0)),
        out_shape=jax.ShapeDtypeStruct((n, d), F32),
        compiler_params=_params("parallel"),
        name="rmsnorm",
    )(x, g.reshape(1, d))


def _normmm_body(x_ref, g_ref, w_ref, o_ref, *rest, emit_hn):
    lhs_ref = rest[-1]

    @pl.when(pl.program_id(1) == 0)
    def _():
        x = x_ref[...]
        ms = jnp.mean(x * x, axis=-1, keepdims=True)
        hn = x * lax.rsqrt(ms + RMS_EPS) * g_ref[...]
        lhs_ref[...] = hn.astype(BF16)
        if emit_hn:
            rest[0][...] = hn

    o_ref[...] = _dot(lhs_ref[...], w_ref[...]).astype(o_ref.dtype)


def norm_matmul(x, g, w_bf16, emit_hn=False, tm=512, tn=1024, out_dtype=BF16):
    n, d = x.shape
    nc = w_bf16.shape[1]
    tm, tn = min(tm, n), min(tn, nc)
    out_shape = [jax.ShapeDtypeStruct((n, nc), out_dtype)]
    out_specs = [pl.BlockSpec((tm, tn), lambda i, j: (i, j))]
    if emit_hn:
        out_shape.append(jax.ShapeDtypeStruct((n, d), F32))
        out_specs.append(pl.BlockSpec((tm, d), lambda i, j: (i, 0)))
    res = pl.pallas_call(
        functools.partial(_normmm_body, emit_hn=emit_hn),
        grid=(n // tm, nc // tn),
        in_specs=[pl.BlockSpec((tm, d), lambda i, j: (i, 0)),
                  pl.BlockSpec((1, d), lambda i, j: (0, 0)),
                  pl.BlockSpec((d, tn), lambda i, j: (0, j))],
        out_specs=out_specs,
        out_shape=out_shape,
        scratch_shapes=[pltpu.VMEM((tm, d), BF16)],
        compiler_params=_params("parallel", "arbitrary"),
        name="norm_matmul",
    )(x, g.reshape(1, d), w_bf16)
    return res if emit_hn else res[0]


def _mixmm_body(gid_ref, h_ref, hp_ref, mix_ref, w_ref, o_ref, lhs_ref, diff_ref, *, tm, seq):
    i = pl.program_id(0)
    j = pl.program_id(1)

    @pl.when(j == 0)
    def _():
        h = h_ref[...]
        prev = jnp.where((i * tm) % seq == 0, 0.0, hp_ref[7:8, :])
        row = lax.broadcasted_iota(jnp.int32, h.shape, 0)
        diff_ref[...] = jnp.where(row == 0, prev, pltpu.roll(h, 1, 0)) - h

    new_group = jnp.logical_or(j == 0, gid_ref[j] != gid_ref[jnp.maximum(j - 1, 0)])

    @pl.when(new_group)
    def _():
        lhs_ref[...] = (h_ref[...] + diff_ref[...] * mix_ref[0]).astype(BF16)

    o_ref[...] = _dot(lhs_ref[...], w_ref[...])


def mix_matmul(h, mix_tab, gid, w_bf16, seq, tn, tm=1024):
    n, d = h.shape
    nc = w_bf16.shape[1]
    tm = min(tm, seq)
    sub = tm // 8
    grid_spec = pltpu.PrefetchScalarGridSpec(
        num_scalar_prefetch=1,
        grid=(n // tm, nc // tn),
        in_specs=[pl.BlockSpec((tm, d), lambda i, j, g: (i, 0)),
                  pl.BlockSpec((8, d), lambda i, j, g: (jnp.maximum(i * sub - 1, 0), 0)),
                  pl.BlockSpec((1, 1, d), lambda i, j, g: (g[j], 0, 0)),
                  pl.BlockSpec((d, tn), lambda i, j, g: (0, j))],
        out_specs=pl.BlockSpec((tm, tn), lambda i, j, g: (i, j)),
        scratch_shapes=[pltpu.VMEM((tm, d), BF16), pltpu.VMEM((tm, d), F32)],
    )
    return pl.pallas_call(
        functools.partial(_mixmm_body, tm=tm, seq=seq),
        grid_spec=grid_spec,
        out_shape=jax.ShapeDtypeStruct((n, nc), F32),
        compiler_params=_params("parallel", "arbitrary"),
        name="mix_matmul",
    )(gid, h, h, mix_tab.reshape(mix_tab.shape[0], 1, d), w_bf16)


def _sigmoid(x):
    return 1.0 / (1.0 + jnp.exp(-x))


def _lora2_body(t_ref, w2_ref, a2_ref, g2_ref, w0_ref, a0_ref, lw_ref, a_ref, g_ref):
    t = t_ref[...]
    tw = jnp.tanh(t[:, :LORA_PAD]).astype(BF16)
    ta = t[:, LORA_PAD:2 * LORA_PAD].astype(BF16)
    tg = _sigmoid(t[:, 2 * LORA_PAD:]).astype(BF16)
    u = w0_ref[...] + _dot(tw, w2_ref[...])
    softplus_neg_u = jnp.maximum(-u, 0.0) + jnp.log(1.0 + jnp.exp(-jnp.abs(u)))
    lw_ref[...] = -jnp.exp(-softplus_neg_u - 0.5)
    a_ref[...] = _sigmoid(a0_ref[...] + _dot(ta, a2_ref[...]))
    g_ref[...] = _dot(tg, g2_ref[...])


def lora_stage2(t1, w2p, a2p, g2, w0, a0, tm=256):
    n = t1.shape[0]
    width = w2p.shape[1]
    tm = min(tm, n)
    full = lambda a: pl.BlockSpec(a.shape, lambda i: (0, 0))
    row = pl.BlockSpec((tm, width), lambda i: (i, 0))
    w0 = w0.reshape(1, width)
    a0 = a0.reshape(1, width)
    return pl.pallas_call(
        _lora2_body,
        grid=(n // tm,),
        in_specs=[pl.BlockSpec((tm, t1.shape[1]), lambda i: (i, 0)), full(w2p), full(a2p), full(g2),
                  full(w0), full(a0)],
        out_specs=[row, row, row],
        out_shape=[jax.ShapeDtypeStruct((n, width), F32)] * 3,
        compiler_params=_params("parallel"),
        name="lora_stage2",
    )(t1, w2p, a2p, g2, w0, a0)


def _head_group_sum(x):
    width = x.shape[-1]
    r = lax.broadcasted_iota(jnp.int32, (width, width), 0) // HEAD_DIM
    c = lax.broadcasted_iota(jnp.int32, (width, width), 1) // HEAD_DIM
    ones = jnp.where(r == c, 1.0, 0.0).astype(BF16)
    hi = x.astype(BF16)
    rest = x - hi.astype(F32)
    mid = rest.astype(BF16)
    lo = (rest - mid.astype(F32)).astype(BF16)
    return _dot(hi, ones) + _dot(mid, ones) + _dot(lo, ones)


def _wkv_prepare_body(r_ref, k_ref, v_ref, lw_ref, a_ref, kk_ref, ka_ref,
                      g_ref, s0c_ref, rp_ref, y0_ref, *, tb):
    r = r_ref[...]
    k = k_ref[...]
    v = v_ref[...]
    lw = lw_ref[...]
    a = a_ref[...]
    kk = k * kk_ref[...]
    norm = jnp.sqrt(_head_group_sum(kk * kk))
    kk = kk / jnp.maximum(norm, 1e-12)
    kmod = k * (1.0 + (a - 1.0) * ka_ref[...])
    avec = -kk
    bvec = kk * a

    tr = lax.broadcasted_iota(jnp.int32, (CHUNK, CHUNK), 0)
    tc = lax.broadcasted_iota(jnp.int32, (CHUNK, CHUNK), 1)
    tri = jnp.where(tr >= tc, 1.0, 0.0).astype(BF16)
    lw_hi = lw.astype(BF16)
    lw_rest = lw - lw_hi.astype(F32)
    lw_mid = lw_rest.astype(BF16)
    lw_lo = (lw_rest - lw_mid.astype(F32)).astype(BF16)

    lane = lax.broadcasted_iota(jnp.int32, (CHUNK, PAIR), 1)
    head0 = lane < HEAD_DIM
    row = lax.broadcasted_iota(jnp.int32, (PAIR, PAIR), 0)
    col = lax.broadcasted_iota(jnp.int32, (PAIR, PAIR), 1)
    strict = row > col
    lower = row >= col
    eye = jnp.where(row == col, 1.0, 0.0).astype(F32)

    def stack(x):
        return jnp.concatenate([jnp.where(head0, x, 0.0), jnp.where(head0, 0.0, x)], axis=0)

    chunks = range(tb // CHUNK)
    pre = []
    for c in chunks:
        sl = slice(c * CHUNK, (c + 1) * CHUNK)
        cm = _dot(tri, lw_hi[sl]) + _dot(tri, lw_mid[sl]) + _dot(tri, lw_lo[sl])
        cend = cm[CHUNK - 1:CHUNK]
        e_in = jnp.exp(cm)
        e_out = jnp.exp(-cm)
        e_tail = jnp.exp(cend - cm)
        a_s = stack(avec[sl] * jnp.exp(cm - lw[sl]))
        r_s = stack(r[sl] * e_in)
        b_s = stack(bvec[sl] * e_out)
        k_s = stack(kmod[sl] * e_out)
        bt_s = stack(bvec[sl] * e_tail)
        kt_s = stack(kmod[sl] * e_tail)
        v_s = stack(v[sl])
        p = _dot_nt(jnp.concatenate([a_s, r_s], axis=0).astype(BF16),
                    jnp.concatenate([b_s, k_s], axis=0).astype(BF16))
        pre.append(dict(cend=cend, a_s=a_s, r_s=r_s, bt_s=bt_s, kt_s=kt_s, v_s=v_s,
                        l_ab=jnp.where(strict, p[:PAIR, :PAIR], 0.0),
                        l_ak=jnp.where(strict, p[:PAIR, PAIR:], 0.0),
                        a_rb=jnp.where(lower, p[PAIR:, :PAIR], 0.0),
                        a_rk=jnp.where(lower, p[PAIR:, PAIR:], 0.0)))

    ms = [q["l_ab"] for q in pre]
    ts = [eye + m for m in ms]
    ms = [_dot(m.astype(BF16), m.astype(BF16)) for m in ms]
    for _ in range(4):
        both = [_dot(m.astype(BF16), jnp.concatenate([t, m], axis=1).astype(BF16)) for t, m in zip(ts, ms)]
        ts = [t + b[:, :PAIR] for t, b in zip(ts, both)]
        ms = [b[:, PAIR:] for b in both]
    ts = [t + _dot(m.astype(BF16), t.astype(BF16)) for t, m in zip(ts, ms)]

    lvs = [_dot(q["l_ak"].astype(BF16), q["v_s"].astype(BF16)) for q in pre]
    aus = [_dot(t.astype(BF16), jnp.concatenate([q["a_s"], lv], axis=1).astype(BF16))
           for t, q, lv in zip(ts, pre, lvs)]
    zero = jnp.zeros((PAIR, PAIR), F32)
    xs = [_dot(jnp.concatenate([q["a_rb"], q["a_rk"]], axis=1).astype(BF16),
               jnp.concatenate([au, jnp.concatenate([zero, q["v_s"]], axis=1)], axis=0).astype(BF16))
          for q, au in zip(pre, aus)]
    for c, q, au, x in zip(chunks, pre, aus, xs):
        rp_ref[0, c] = (q["r_s"] + x[:, :PAIR]).astype(rp_ref.dtype)
        y0_ref[0, c] = x[:, PAIR:]
        ap = au[:, :PAIR]
        u0 = au[:, PAIR:]
        g_ref[0, c] = (eye * jnp.exp(q["cend"])
                       + _dot_tn(ap.astype(BF16), q["bt_s"].astype(BF16))).astype(g_ref.dtype)
        s0c_ref[0, c] = _dot_tn(jnp.concatenate([u0, q["v_s"]], axis=0).astype(BF16),
                                jnp.concatenate([q["bt_s"], q["kt_s"]], axis=0).astype(BF16))


def wkv_chunk_prepare(proj, lw, a, k_k, k_a, n_pairs, tb=1024):
    n = proj.shape[0]
    tb = min(tb, n)
    cpb = tb // CHUNK
    col = lambda off: pl.BlockSpec((tb, PAIR), lambda i, p: (i, off + p))
    par = pl.BlockSpec((1, PAIR), lambda i, p: (0, p))
    blk = pl.BlockSpec((1, cpb, PAIR, PAIR), lambda i, p: (p, i, 0, 0))
    shp = lambda dt: jax.ShapeDtypeStruct((n_pairs, n // CHUNK, PAIR, PAIR), dt)
    return pl.pallas_call(
        functools.partial(_wkv_prepare_body, tb=tb),
        grid=(n // tb, n_pairs),
        in_specs=[col(0), col(n_pairs), col(2 * n_pairs), col(0), col(0), par, par],
        out_specs=[blk] * 4,
        out_shape=[shp(BF16), shp(F32), shp(BF16), shp(F32)],
        compiler_params=_params("parallel", "parallel"),
        name="wkv_chunk_prepare",
    )(proj, proj, proj, lw, a, k_k.reshape(1, -1), k_a.reshape(1, -1))


def _wkv_scan_body(g_ref, s0c_ref, rp_ref, y0_ref, r_ref, k_ref, v_ref, a_ref, gate_ref,
                   ka_ref, rk_ref, lng_ref, lnb_ref, o_ref, s_ref, y_ref, *, cpb):
    @pl.when(pl.program_id(2) == 0)
    def _():
        s_ref[...] = jnp.zeros_like(s_ref)

    pairs = range(SCAN_PAIRS)
    states = [[s_ref[w].astype(BF16)] for w in pairs]
    for c in range(cpb):
        for w in pairs:
            s = _dot(states[w][c], g_ref[w, c].astype(BF16)) + s0c_ref[w, c]
            if c + 1 < cpb:
                states[w].append(s.astype(BF16))
            else:
                s_ref[w] = s
    for c in range(cpb):
        for w in pairs:
            y_st = _dot_nt(rp_ref[w, c].astype(BF16), states[w][c]) + y0_ref[w, c]
            y_ref[c * CHUNK:(c + 1) * CHUNK, w * PAIR:(w + 1) * PAIR] = y_st[:CHUNK] + y_st[CHUNK:]

    y = y_ref[...]
    inv = 1.0 / HEAD_DIM
    mu = _head_group_sum(y) * inv
    yc = y - mu
    var = _head_group_sum(yc * yc) * inv
    yn = yc * lax.rsqrt(var + GN_EPS) * lng_ref[...] + lnb_ref[...]
    r = r_ref[...]
    kmod = k_ref[...] * (1.0 + (a_ref[...] - 1.0) * ka_ref[...])
    bonus = _head_group_sum(r * kmod * rk_ref[...]) * v_ref[...]
    o_ref[...] = ((yn + bonus) * gate_ref[...]).astype(o_ref.dtype)


def wkv_chunk_scan(prep, proj, a, gate, k_a, r_k, ln_g, ln_b, n_pairs, seq, tb=1024):
    g_all, s0c_all, rp_all, y0_all = prep
    n = proj.shape[0]
    tb = min(tb, seq)
    cpb = tb // CHUNK
    nblk = seq // tb
    assert n_pairs % SCAN_PAIRS == 0
    width = SCAN_PAIRS * PAIR
    groups = n_pairs // SCAN_PAIRS
    blk = pl.BlockSpec((SCAN_PAIRS, cpb, PAIR, PAIR), lambda b, p, t: (p, b * nblk + t, 0, 0))
    col = lambda off: pl.BlockSpec((tb, width), lambda b, p, t: (b * nblk + t, off + p))
    par = pl.BlockSpec((1, width), lambda b, p, t: (0, p))
    vec = lambda z: z.reshape(1, -1)
    return pl.pallas_call(
        functools.partial(_wkv_scan_body, cpb=cpb),
        grid=(n // seq, groups, nblk),
        in_specs=[blk] * 4 + [col(0), col(groups), col(2 * groups), col(0), col(0)] + [par] * 4,
        out_specs=col(0),
        out_shape=jax.ShapeDtypeStruct((n, n_pairs * PAIR), BF16),
        scratch_shapes=[pltpu.VMEM((SCAN_PAIRS, PAIR, PAIR), F32), pltpu.VMEM((tb, width), F32)],
        compiler_params=_params("parallel", "parallel", "arbitrary"),
        name="wkv_chunk_scan",
    )(g_all, s0c_all, rp_all, y0_all, proj, proj, proj, a, gate,
      vec(k_a), vec(r_k), vec(ln_g), vec(ln_b))


def _memattn_body(q_ref, m_ref, o_ref):
    width = MEM_HEADS * MEM_HEAD_DIM
    scale = MEM_HEAD_DIM ** -0.5
    cols = [slice(h * MEM_HEAD_DIM, (h + 1) * MEM_HEAD_DIM) for h in range(MEM_HEADS)]
    scores = [_dot_nt(q_ref[:, sl].astype(BF16), m_ref[:, sl].astype(BF16)) * scale for sl in cols]
    exps = [jnp.exp(s - jnp.max(s, axis=-1, keepdims=True)) for s in scores]
    outs = [_dot(e.astype(BF16), m_ref[:, width + sl.start:width + sl.stop].astype(BF16))
            for sl, e in zip(cols, exps)]
    for sl, e, o in zip(cols, exps, outs):
        o_ref[:, sl] = (o / jnp.sum(e, axis=-1, keepdims=True)).astype(o_ref.dtype)


def memory_attention(proj, q_col_block, mkv, seq, n_mem, tm=512):
    n = proj.shape[0]
    width = MEM_HEADS * MEM_HEAD_DIM
    tm = min(tm, seq)
    nblk = seq // tm
    return pl.pallas_call(
        _memattn_body,
        grid=(n // seq, nblk),
        in_specs=[pl.BlockSpec((tm, width), lambda b, t: (b * nblk + t, q_col_block)),
                  pl.BlockSpec((n_mem, 2 * width), lambda b, t: (b, 0))],
        out_specs=pl.BlockSpec((tm, width), lambda b, t: (b * nblk + t, 0)),
        out_shape=jax.ShapeDtypeStruct((n, width), BF16),
        compiler_params=_params("parallel", "parallel"),
        name="memory_attention",
    )(proj, mkv)


def _outproj_body(s_ref, m_ref, w1_ref, w2_ref, x_ref, o_ref):
    o_ref[...] = x_ref[...] + _dot(s_ref[...], w1_ref[...]) + _dot(m_ref[...], w2_ref[...])


def out_projection(seq_out, mem_out, w_seq, w_mem, x, tm=512, tn=1024):
    n, d = x.shape
    tm, tn = min(tm, n), min(tn, d)
    ws, wm = seq_out.shape[1], mem_out.shape[1]
    return pl.pallas_call(
        _outproj_body,
        grid=(n // tm, d // tn),
        in_specs=[pl.BlockSpec((tm, ws), lambda i, j: (i, 0)),
                  pl.BlockSpec((tm, wm), lambda i, j: (i, 0)),
                  pl.BlockSpec((ws, tn), lambda i, j: (0, j)),
                  pl.BlockSpec((wm, tn), lambda i, j: (0, j)),
                  pl.BlockSpec((tm, tn), lambda i, j: (i, j))],
        out_specs=pl.BlockSpec((tm, tn), lambda i, j: (i, j)),
        out_shape=jax.ShapeDtypeStruct((n, d), F32),
        compiler_params=_params("parallel", "parallel"),
        name="out_projection",
    )(seq_out, mem_out, w_seq, w_mem, x)


def _bandattn_body(q_ref, kp_ref, kc_ref, vp_ref, vc_ref, bias_ref, o_ref, *, tq):
    qi = pl.program_id(2)
    scale = HEAD_DIM ** -0.5
    k_all = jnp.concatenate([kp_ref[...], kc_ref[...]], axis=0).astype(BF16)
    v_all = jnp.concatenate([vp_ref[...], vc_ref[...]], axis=0).astype(BF16)
    lane = lax.broadcasted_iota(jnp.int32, (CHUNK, PAIR), 1)
    head0 = lane < HEAD_DIM
    kcol = lax.broadcasted_iota(jnp.int32, (2 * CHUNK, BAND), 1)
    pad = LEFT_CHUNKS * CHUNK
    bias = bias_ref[...]
    chunks = range(tq // CHUNK)
    starts = [tq - pad + j * CHUNK for j in chunks]
    scores = []
    for j, start in zip(chunks, starts):
        q = q_ref[j * CHUNK:(j + 1) * CHUNK, :] * scale
        q2 = jnp.concatenate([jnp.where(head0, q, 0.0), jnp.where(head0, 0.0, q)], axis=0).astype(BF16)
        scores.append(_dot_nt(q2, k_all[start:start + BAND]))
    exps = []
    for start, s in zip(starts, scores):
        valid = jnp.logical_or(qi > 0, kcol + start >= tq)
        s = jnp.where(valid, s + bias, -jnp.inf)
        exps.append(jnp.exp(s - jnp.max(s, axis=-1, keepdims=True)))
    outs = [_dot(e.astype(BF16), v_all[start:start + BAND]) for start, e in zip(starts, exps)]
    for j, e, o in zip(chunks, exps, outs):
        o = o / jnp.sum(e, axis=-1, keepdims=True)
        o_ref[j * CHUNK:(j + 1) * CHUNK, :] = jnp.where(head0, o[:CHUNK], o[CHUNK:]).astype(o_ref.dtype)


def band_attention(proj, kv, bias, n_pairs, seq, tq=512):
    n = proj.shape[0]
    tq = min(tq, seq)
    assert tq >= LEFT_CHUNKS * CHUNK
    nblk = seq // tq
    cur = lambda off: pl.BlockSpec((tq, PAIR), lambda b, p, t: (b * nblk + t, off + p))
    prev = lambda off: pl.BlockSpec((tq, PAIR), lambda b, p, t: (b * nblk + jnp.maximum(t - 1, 0), off + p))
    return pl.pallas_call(
        functools.partial(_bandattn_body, tq=tq),
        grid=(n // seq, n_pairs, nblk),
        in_specs=[cur(0), prev(0), cur(0), prev(n_pairs), cur(n_pairs),
                  pl.BlockSpec((2 * CHUNK, BAND), lambda b, p, t: (p, 0))],
        out_specs=cur(0),
        out_shape=jax.ShapeDtypeStruct((n, n_pairs * PAIR), BF16),
        compiler_params=_params("parallel", "parallel", "parallel"),
        name="band_attention",
    )(proj, kv, kv, kv, kv, bias.reshape(-1, BAND))


def _top16(scores, payloads):
    nl = scores[0].shape[1]
    rids = [lax.broadcasted_iota(jnp.int32, s.shape, 0).astype(F32) for s in scores]
    slot = lax.broadcasted_iota(jnp.int32, (PEER_TOPK, nl), 0)
    tile = 8

    def winner(s, rid, payload):
        items = [(s[g:g + tile], rid[g:g + tile], None if payload is None else payload[g:g + tile])
                 for g in range(0, s.shape[0], tile)]
        while len(items) > 1:
            nxt = []
            for k in range(0, len(items) - 1, 2):
                (va, ia, pa), (vb, ib, pb) = items[k], items[k + 1]
                keep = va >= vb
                nxt.append((jnp.maximum(va, vb), jnp.where(keep, ia, ib),
                            None if pa is None else jnp.where(keep, pa, pb)))
            if len(items) % 2:
                nxt.append(items[-1])
            items = nxt
        v, idx, p = items[0]
        for shift in (4, 2, 1):
            pv, pi = pltpu.roll(v, shift, 0), pltpu.roll(idx, shift, 0)
            take = (pv > v) | ((pv == v) & (pi < idx))
            if p is not None:
                p = jnp.where(take, pltpu.roll(p, shift, 0), p)
            v, idx = jnp.where(take, pv, v), jnp.where(take, pi, idx)
        return v, idx, p

    def body(i, carry):
        sel = slot == i
        out = []
        for (s, vals, picks), rid, payload in zip(carry, rids, payloads):
            m, am, p = winner(s, rid, payload)
            hit = rid == jnp.concatenate([am] * (s.shape[0] // tile), axis=0)
            pick = am if payload is None else p
            out.append((jnp.where(hit, -jnp.inf, s), jnp.where(sel, m[0:1], vals), jnp.where(sel, pick[0:1], picks)))
        return tuple(out)

    zero = jnp.zeros((PEER_TOPK, nl), F32)
    res = lax.fori_loop(0, PEER_TOPK, body, tuple((s, zero, zero) for s in scores))
    return [(vals, picks) for _, vals, picks in res]


def _pair_candidates(a, b, combine):
    half = PEER_TOPK // 2
    rows = [combine(a[0:1], b)]
    rows += [combine(a[i:i + 1], b[:half]) for i in range(1, half)]
    rows.append(combine(a[half:], b[0:1]))
    return jnp.concatenate(rows, axis=0)


TOPK_HEADS = 2


def _peer_topk_body(q_ref, keys_ref, eidx_ref, gate_ref):
    q = q_ref[...].astype(BF16)
    scores = []
    for h in range(TOPK_HEADS):
        for half in range(2):
            col = (2 * h + half) * PEER_KEYS
            scores.append(_dot_nt(keys_ref[h, half].astype(BF16), q[:, col:col + PEER_KEYS]))
    cands, cidxs = [], []
    for h in range(TOPK_HEADS):
        (a, i1), (b, i2) = _top16(scores[2 * h:2 * h + 2], [None, None])
        cands.append(_pair_candidates(a, b, lambda x, y: x + y))
        cidxs.append(_pair_candidates(i1, i2, lambda x, y: x * PEER_KEYS + y))
    for h, (top, eidx) in enumerate(_top16(cands, cidxs)):
        rows = slice(h * PEER_TOPK, (h + 1) * PEER_TOPK)
        e = jnp.exp(top - top[0:1])
        gate_ref[rows, :] = e / jnp.sum(e, axis=0, keepdims=True)
        eidx_ref[rows, :] = eidx.astype(jnp.int32)


def peer_topk(q, keys, tl=128):
    n = q.shape[0]
    tl = min(tl, n)
    blk = pl.BlockSpec((TOPK_HEADS * PEER_TOPK, tl), lambda i, h: (h, i))
    return pl.pallas_call(
        _peer_topk_body,
        grid=(n // tl, PEER_HEADS // TOPK_HEADS),
        in_specs=[pl.BlockSpec((tl, TOPK_HEADS * 2 * PEER_KEYS), lambda i, h: (i, h)),
                  pl.BlockSpec((TOPK_HEADS, 2, PEER_KEYS, PEER_KEYS), lambda i, h: (h, 0, 0, 0))],
        out_specs=[blk, blk],
        out_shape=[jax.ShapeDtypeStruct((PEER_PICKS, n), jnp.int32),
                   jax.ShapeDtypeStruct((PEER_PICKS, n), F32)],
        compiler_params=_params("parallel", "parallel"),
        name="peer_topk",
    )(q, keys)


PEER_SLOTS = 8
LANES = 128
PITCH_PAD = 4


def _pack_body(u_ref, v_ref, o_ref):
    hi = lax.bitcast_convert_type(u_ref[0].astype(BF16).astype(F32), jnp.uint32)
    lo = lax.bitcast_convert_type(v_ref[0].astype(BF16).astype(F32), jnp.uint32)
    words = hi | (lo >> 16)
    for c in range(o_ref.shape[1]):
        o_ref[:, c, :] = words[:, c * LANES:(c + 1) * LANES]


def _pack_expert_table(u, v, layer, te=256):
    _, n_exp, d = u.shape
    chunks = d // LANES
    te = min(te, n_exp)
    packed = pl.pallas_call(
        _pack_body,
        grid=(n_exp // te,),
        in_specs=[pl.BlockSpec((1, te, d), lambda i: (layer, i, 0))] * 2,
        out_specs=pl.BlockSpec((te, chunks, LANES), lambda i: (i, 0, 0)),
        out_shape=jax.ShapeDtypeStruct((n_exp, chunks, LANES), jnp.uint32),
        compiler_params=_params("parallel"),
        name="pack_expert_table",
    )(u, v)
    return packed.reshape(n_exp * chunks, LANES)


def _peer_ffn_body(eidx_ref, gate_ref, hn_ref, x_ref, tab_ref, *rest, tb, d, out_norm):
    norm_ref, o_ref, scratch = (rest[0], rest[1], rest[2:]) if out_norm else (None, rest[0], rest[1:])
    rows_refs = scratch[:PEER_SLOTS]
    sem_ref, w_ref, stage_ref, hn_rows_ref, gate_rows_ref, ids_ref, ids_sem_ref = scratch[PEER_SLOTS:]
    chunks = d // LANES
    pitch = chunks + PITCH_PAD

    def issue(ids, slot):
        for e in range(PEER_PICKS):
            src = tab_ref.at[pl.ds(pl.multiple_of(ids(e) * chunks, chunks), chunks), :]
            dst = rows_refs[slot].at[pl.ds(e * pitch, chunks), :]
            pltpu.make_async_copy(src, dst, sem_ref.at[slot]).start(priority=e % 2)

    def block_ids(t):
        return lambda e: eidx_ref[t, e]

    def ids_window(g, half):
        rows = g * PEER_SLOTS if isinstance(g, int) else pl.multiple_of(g * PEER_SLOTS, PEER_SLOTS)
        return pltpu.make_async_copy(eidx_ref.at[pl.ds(rows, 2 * PEER_SLOTS), :], ids_ref.at[half],
                                     ids_sem_ref.at[half])

    def wait(slot):
        total = PEER_PICKS * chunks
        pltpu.make_async_copy(tab_ref.at[pl.ds(0, total), :],
                              rows_refs[slot].at[pl.ds(0, total), :], sem_ref.at[slot]).wait()

    pick_diag = (lax.broadcasted_iota(jnp.int32, (PEER_PICKS, PEER_PICKS), 0)
                 == lax.broadcasted_iota(jnp.int32, (PEER_PICKS, PEER_PICKS), 1))
    hi_mask = jnp.uint32(0xFFFF0000)

    def words(slot, c):
        return rows_refs[slot][pl.ds(c, PEER_PICKS, stride=pitch), :]

    def pick_weights(slot, hn_rows, gate_rows, row):
        acc = jnp.zeros((PEER_PICKS, LANES), F32)
        for c in range(chunks):
            u = lax.bitcast_convert_type(words(slot, c) & hi_mask, F32)
            acc = acc + u * hn_rows[row:row + 1, c * LANES:(c + 1) * LANES]
        act = jnp.sum(acc, axis=1, keepdims=True)
        act = 0.5 * act * (1.0 + lax.erf(act * (2.0 ** -0.5)))
        gate = jnp.sum(jnp.where(pick_diag, gate_rows[row:row + 1, :], 0.0), axis=1, keepdims=True)
        w_ref[slot] = jnp.broadcast_to(gate * act, (PEER_PICKS, LANES))
        return acc

    def combine(slot, after):
        last = lax.bitcast_convert_type(after[PEER_PICKS - 8:], jnp.int32) == -1
        never = jnp.concatenate([last] * (PEER_PICKS // 8), axis=0)
        w = jnp.where(never, 0.0, w_ref[slot])
        for c in range(chunks):
            v = lax.bitcast_convert_type(words(slot, c) << 16, F32)
            stage_ref[slot:slot + 1, c * LANES:(c + 1) * LANES] = jnp.sum(v * w, axis=0, keepdims=True)

    def store_group(g):
        first = g * PEER_SLOTS if isinstance(g, int) else pl.multiple_of(g * PEER_SLOTS, PEER_SLOTS)
        rows = pl.ds(first, PEER_SLOTS)
        y = x_ref[rows, :] + stage_ref[...]
        if norm_ref is not None:
            y = y * lax.rsqrt(jnp.mean(y * y, axis=-1, keepdims=True) + RMS_EPS) * norm_ref[...]
        o_ref[rows, :] = y

    ahead = PEER_SLOTS - 1
    groups = tb // PEER_SLOTS
    steady = groups - 1
    ids_window(0, 0).start()
    for t in range(ahead):
        issue(block_ids(t), t)
    wait(0)
    first_pass = pick_weights(0, hn_ref, gate_ref, 0)

    def group(g, half):
        first = g * PEER_SLOTS if isinstance(g, int) else pl.multiple_of(g * PEER_SLOTS, PEER_SLOTS)
        rows = pl.ds(first, 2 * PEER_SLOTS)
        hn_rows_ref[...] = hn_ref[rows, :]
        gate_rows_ref[...] = gate_ref[rows, :]
        ids_window(g, half).wait()
        if isinstance(g, int):
            if g + 1 < steady:
                ids_window(g + 1, 1 - half).start()
        else:
            @pl.when(g + 1 < steady)
            def _():
                ids_window(g + 1, 1 - half).start()
        for s in range(PEER_SLOTS):
            wait((s + 1) % PEER_SLOTS)
            combine(s, pick_weights((s + 1) % PEER_SLOTS, hn_rows_ref, gate_rows_ref, s + 1))
            issue(lambda e, s=s: ids_ref[half, s + ahead, e], (s + ahead) % PEER_SLOTS)
        store_group(g)

    def group_pair(k, carry):
        group(2 * k, 0)
        group(2 * k + 1, 1)
        return carry

    lax.fori_loop(0, steady // 2, group_pair, 0)
    if steady % 2:
        group(steady - 1, 0)

    for s in range(PEER_SLOTS):
        t = (groups - 1) * PEER_SLOTS + s
        if t + ahead < tb:
            issue(block_ids(t + ahead), (s + ahead) % PEER_SLOTS)
        if t + 1 < tb:
            wait((s + 1) % PEER_SLOTS)
            first_pass = pick_weights((s + 1) % PEER_SLOTS, hn_ref, gate_ref, t + 1)
        combine(s, first_pass)
    store_group(groups - 1)


def peer_expert_ffn(eidx, gate, hn, x, table, tb=128, out_norm_g=None):
    n, d = x.shape
    tb = min(tb, n)
    assert tb % PEER_SLOTS == 0 and table.shape[1] == LANES
    assert table.shape[0] >= PEER_PICKS * (d // LANES)
    slot_rows = PEER_PICKS * (d // LANES + PITCH_PAD)
    out_norm = out_norm_g is not None
    in_specs = [pl.BlockSpec((tb, PEER_PICKS), lambda i: (i, 0), memory_space=pltpu.SMEM),
                pl.BlockSpec((tb, PEER_PICKS), lambda i: (i, 0)),
                pl.BlockSpec((tb, d), lambda i: (i, 0)),
                pl.BlockSpec((tb, d), lambda i: (i, 0)),
                pl.BlockSpec(memory_space=pl.ANY)]
    operands = [eidx, gate, hn, x, table]
    if out_norm:
        in_specs.append(pl.BlockSpec((1, d), lambda i: (0, 0)))
        operands.append(out_norm_g.reshape(1, d))
    return pl.pallas_call(
        functools.partial(_peer_ffn_body, tb=tb, d=d, out_norm=out_norm),
        grid=(n // tb,),
        in_specs=in_specs,
        out_specs=pl.BlockSpec((tb, d), lambda i: (i, 0)),
        out_shape=jax.ShapeDtypeStruct((n, d), F32),
        scratch_shapes=[pltpu.VMEM((slot_rows, LANES), jnp.uint32)] * PEER_SLOTS
                       + [pltpu.SemaphoreType.DMA((PEER_SLOTS,)),
                          pltpu.VMEM((PEER_SLOTS, PEER_PICKS, LANES), F32),
                          pltpu.VMEM((PEER_SLOTS, d), F32),
                          pltpu.VMEM((2 * PEER_SLOTS, d), F32),
                          pltpu.VMEM((2 * PEER_SLOTS, PEER_PICKS), F32),
                          pltpu.SMEM((2, 2 * PEER_SLOTS, PEER_PICKS), jnp.int32),
                          pltpu.SemaphoreType.DMA((2,))],
        compiler_params=_params("arbitrary"),
        name="peer_expert_ffn",
    )(*operands)


def peer_layer(x, norm_g, wq, keys, u_all, v_all, layer, tb=512, out_norm_g=None):
    q, hn = norm_matmul(x, norm_g, wq.astype(BF16), emit_hn=True)
    eidx, gate = peer_topk(q, keys)
    return peer_expert_ffn(eidx.T, gate.T, hn, x, _pack_expert_table(u_all, v_all, layer), tb=tb,
                           out_norm_g=out_norm_g)


def _band_bias(rel_bias):
    n_rel = rel_bias.shape[1]
    far = BAND - n_rel + CHUNK - 1
    long_row = jnp.concatenate([jnp.broadcast_to(rel_bias[:, n_rel - 1:], (rel_bias.shape[0], far)),
                                rel_bias[:, ::-1]], axis=1)
    rows = [long_row[:, CHUNK - 1 - i:CHUNK - 1 - i + BAND] for i in range(CHUNK)]
    return jnp.stack(rows, axis=1).astype(F32)


def kernel(x, mem, norm_mix, norm_ffn, norm_mem, w_mem_kv, w_out, peer_wq, peer_keys, peer_u, peer_v, a_mix, a_w_in, a_w0, a_w1, a_w2, a_a0, a_a1, a_a2, a_g1, a_g2, a_k_k, a_k_a, a_r_k, a_ln_g, a_ln_b, kv_norm, w_kv_shared, b_w_in, b_rel_bias, final_norm):
    bsz, seq, d = x.shape
    n = bsz * seq
    n_mem = mem.shape[1]
    seq_width = a_w0.shape[1]
    n_pairs = seq_width // PAIR
    mem_width = MEM_HEADS * MEM_HEAD_DIM
    x = x.reshape(n, d)
    mem2 = mem.reshape(bsz * n_mem, d)

    def mixer_tail(x, seq_out, proj, q_col_block, layer, out_norm_g=None):
        mkv = norm_matmul(mem2, norm_mem[layer], w_mem_kv[layer].astype(BF16), tm=256)
        mem_out = memory_attention(proj, q_col_block, mkv, seq, n_mem)
        wo = w_out[layer].astype(BF16)
        x = out_projection(seq_out, mem_out, wo[:seq_width], wo[seq_width:], x)
        return peer_layer(x, norm_ffn[layer], peer_wq[layer], peer_keys[layer], peer_u, peer_v, layer,
                          out_norm_g=out_norm_g)

    h = rmsnorm(x, norm_mix[0])
    mix_tab = jnp.concatenate([a_mix[0], jnp.zeros((1, d), F32)], axis=0)
    tiles = seq_width // 512
    gid_main = jnp.asarray([0] * tiles + [2] * tiles + [3] * tiles + [6] * (mem_width // 512), jnp.int32)
    proj = mix_matmul(h, mix_tab, gid_main, a_w_in[0].astype(BF16), seq, tn=512)
    rank = a_w1.shape[2]
    padc = lambda w: jnp.pad(w, ((0, 0), (0, LORA_PAD - rank)))
    padr = lambda w: jnp.pad(w, ((0, LORA_PAD - rank), (0, 0)))
    w_l1 = jnp.concatenate([padc(a_w1[0]), padc(a_a1[0]), a_g1[0]], axis=1).astype(BF16)
    gid_l1 = jnp.asarray([1, 4] + [5] * (a_g1.shape[2] // LORA_PAD), jnp.int32)
    t1 = mix_matmul(h, mix_tab, gid_l1, w_l1, seq, tn=LORA_PAD)
    lw, a_iclr, gate = lora_stage2(t1, padr(a_w2[0]).astype(BF16), padr(a_a2[0]).astype(BF16),
                                   a_g2[0].astype(BF16), a_w0[0], a_a0[0])
    prep = wkv_chunk_prepare(proj, lw, a_iclr, a_k_k[0], a_k_a[0], n_pairs)
    seq_out = wkv_chunk_scan(prep, proj, a_iclr, gate, a_k_a[0], a_r_k[0].reshape(-1),
                             a_ln_g[0], a_ln_b[0], n_pairs, seq)
    x = mixer_tail(x, seq_out, proj, (3 * seq_width) // mem_width, 0)

    kv = norm_matmul(x, kv_norm, w_kv_shared.astype(BF16), tm=1024)

    proj = norm_matmul(x, norm_mix[1], b_w_in[0].astype(BF16), tm=1024)
    seq_out = band_attention(proj, kv, _band_bias(b_rel_bias[0]), n_pairs, seq)
    x = mixer_tail(x, seq_out, proj, seq_width // mem_width, 1, out_norm_g=final_norm)
    return x.reshape(bsz, seq, d)
```

```python
import functools

import numpy as np
import jax
import jax.numpy as jnp
from jax import lax
from jax.experimental import pallas as pl
from jax.experimental.pallas import tpu as pltpu

F32 = jnp.float32
BF16 = jnp.bfloat16

HEAD_DIM = 64
PAIR = 2 * HEAD_DIM
CHUNK = 64
SCAN_PAIRS = 2
LEFT_CHUNKS = 8
BAND = (LEFT_CHUNKS + 1) * CHUNK
REL_MAX = 128
MEM_HEADS = 4
MEM_HEAD_DIM = 128
PEER_KEYS = 128
PEER_HEADS = 8
PEER_TOPK = 16
PEER_PICKS = PEER_HEADS * PEER_TOPK
GN_EPS = 64e-5
RMS_EPS = 1e-6
LORA_PAD = 128
VMEM_LIMIT = 48 * 1024 * 1024


def _params(*sem):
    return pltpu.CompilerParams(dimension_semantics=sem, vmem_limit_bytes=VMEM_LIMIT)


def _dot(a, b):
    return jnp.dot(a, b, preferred_element_type=F32)


def _dot_nt(a, b, precision=None):
    return lax.dot_general(a, b, (((1,), (1,)), ((), ())), precision=precision,
                           preferred_element_type=F32)


def _dot_tn(a, b, precision=None):
    return lax.dot_general(a, b, (((0,), (0,)), ((), ())), precision=precision,
                           preferred_element_type=F32)


def _rmsnorm_body(x_ref, g_ref, o_ref):
    x = x_ref[...]
    ms = jnp.mean(x * x, axis=-1, keepdims=True)
    o_ref[...] = (x * lax.rsqrt(ms + RMS_EPS) * g_ref[...]).astype(o_ref.dtype)


def rmsnorm(x, g, tm=512):
    n, d = x.shape
    tm = min(tm, n)
    return pl.pallas_call(
        _rmsnorm_body,
        grid=(n // tm,),
        in_specs=[pl.BlockSpec((tm, d), lambda i: (i, 0)), pl.BlockSpec((1, d), lambda i: (0, 0))],
        out_specs=pl.BlockSpec((tm, d), lambda i: (i, 0)),
        out_shape=jax.ShapeDtypeStruct((n, d), F32),
        compiler_params=_params("parallel"),
        name="rmsnorm",
    )(x, g.reshape(1, d))


def _normmm_body(x_ref, g_ref, w_ref, o_ref, *rest, emit_hn):
    lhs_ref = rest[-1]

    @pl.when(pl.program_id(1) == 0)
    def _():
        x = x_ref[...]
        ms = jnp.mean(x * x, axis=-1, keepdims=True)
        hn = x * lax.rsqrt(ms + RMS_EPS) * g_ref[...]
        lhs_ref[...] = hn.astype(BF16)
        if emit_hn:
            rest[0][...] = hn

    o_ref[...] = _dot(lhs_ref[...], w_ref[...]).astype(o_ref.dtype)


def norm_matmul(x, g, w_bf16, emit_hn=False, tm=512, tn=1024, out_dtype=BF16):
    n, d = x.shape
    nc = w_bf16.shape[1]
    tm, tn = min(tm, n), min(tn, nc)
    out_shape = [jax.ShapeDtypeStruct((n, nc), out_dtype)]
    out_specs = [pl.BlockSpec((tm, tn), lambda i, j: (i, j))]
    if emit_hn:
        out_shape.append(jax.ShapeDtypeStruct((n, d), F32))
        out_specs.append(pl.BlockSpec((tm, d), lambda i, j: (i, 0)))
    res = pl.pallas_call(
        functools.partial(_normmm_body, emit_hn=emit_hn),
        grid=(n // tm, nc // tn),
        in_specs=[pl.BlockSpec((tm, d), lambda i, j: (i, 0)),
                  pl.BlockSpec((1, d), lambda i, j: (0, 0)),
                  pl.BlockSpec((d, tn), lambda i, j: (0, j))],
        out_specs=out_specs,
        out_shape=out_shape,
        scratch_shapes=[pltpu.VMEM((tm, d), BF16)],
        compiler_params=_params("parallel", "arbitrary"),
        name="norm_matmul",
    )(x, g.reshape(1, d), w_bf16)
    return res if emit_hn else res[0]


def _mixmm_body(gid_ref, h_ref, hp_ref, mix_ref, w_ref, o_ref, lhs_ref, diff_ref, *, tm, seq):
    i = pl.program_id(0)
    j = pl.program_id(1)

    @pl.when(j == 0)
    def _():
        h = h_ref[...]
        prev = jnp.where((i * tm) % seq == 0, 0.0, hp_ref[7:8, :])
        row = lax.broadcasted_iota(jnp.int32, h.shape, 0)
        diff_ref[...] = jnp.where(row == 0, prev, pltpu.roll(h, 1, 0)) - h

    new_group = jnp.logical_or(j == 0, gid_ref[j] != gid_ref[jnp.maximum(j - 1, 0)])

    @pl.when(new_group)
    def _():
        lhs_ref[...] = (h_ref[...] + diff_ref[...] * mix_ref[0]).astype(BF16)

    o_ref[...] = _dot(lhs_ref[...], w_ref[...])


def mix_matmul(h, mix_tab, gid, w_bf16, seq, tn, tm=1024):
    n, d = h.shape
    nc = w_bf16.shape[1]
    tm = min(tm, seq)
    sub = tm // 8
    grid_spec = pltpu.PrefetchScalarGridSpec(
        num_scalar_prefetch=1,
        grid=(n // tm, nc // tn),
        in_specs=[pl.BlockSpec((tm, d), lambda i, j, g: (i, 0)),
                  pl.BlockSpec((8, d), lambda i, j, g: (jnp.maximum(i * sub - 1, 0), 0)),
                  pl.BlockSpec((1, 1, d), lambda i, j, g: (g[j], 0, 0)),
                  pl.BlockSpec((d, tn), lambda i, j, g: (0, j))],
        out_specs=pl.BlockSpec((tm, tn), lambda i, j, g: (i, j)),
        scratch_shapes=[pltpu.VMEM((tm, d), BF16), pltpu.VMEM((tm, d), F32)],
    )
    return pl.pallas_call(
        functools.partial(_mixmm_body, tm=tm, seq=seq),
        grid_spec=grid_spec,
        out_shape=jax.ShapeDtypeStruct((n, nc), F32),
        compiler_params=_params("parallel", "arbitrary"),
        name="mix_matmul",
    )(gid, h, h, mix_tab.reshape(mix_tab.shape[0], 1, d), w_bf16)


def _sigmoid(x):
    return 1.0 / (1.0 + jnp.exp(-x))


def _lora2_body(t_ref, w2_ref, a2_ref, g2_ref, w0_ref, a0_ref, lw_ref, a_ref, g_ref):
    t = t_ref[...]
    tw = jnp.tanh(t[:, :LORA_PAD]).astype(BF16)
    ta = t[:, LORA_PAD:2 * LORA_PAD].astype(BF16)
    tg = _sigmoid(t[:, 2 * LORA_PAD:]).astype(BF16)
    u = w0_ref[...] + _dot(tw, w2_ref[...])
    softplus_neg_u = jnp.maximum(-u, 0.0) + jnp.log(1.0 + jnp.exp(-jnp.abs(u)))
    lw_ref[...] = -jnp.exp(-softplus_neg_u - 0.5)
    a_ref[...] = _sigmoid(a0_ref[...] + _dot(ta, a2_ref[...]))
    g_ref[...] = _dot(tg, g2_ref[...])


def lora_stage2(t1, w2p, a2p, g2, w0, a0, tm=256):
    n = t1.shape[0]
    width = w2p.shape[1]
    tm = min(tm, n)
    full = lambda a: pl.BlockSpec(a.shape, lambda i: (0, 0))
    row = pl.BlockSpec((tm, width), lambda i: (i, 0))
    w0 = w0.reshape(1, width)
    a0 = a0.reshape(1, width)
    return pl.pallas_call(
        _lora2_body,
        grid=(n // tm,),
        in_specs=[pl.BlockSpec((tm, t1.shape[1]), lambda i: (i, 0)), full(w2p), full(a2p), full(g2),
                  full(w0), full(a0)],
        out_specs=[row, row, row],
        out_shape=[jax.ShapeDtypeStruct((n, width), F32)] * 3,
        compiler_params=_params("parallel"),
        name="lora_stage2",
    )(t1, w2p, a2p, g2, w0, a0)


def _head_group_sum(x):
    width = x.shape[-1]
    r = lax.broadcasted_iota(jnp.int32, (width, width), 0) // HEAD_DIM
    c = lax.broadcasted_iota(jnp.int32, (width, width), 1) // HEAD_DIM
    ones = jnp.where(r == c, 1.0, 0.0).astype(BF16)
    hi = x.astype(BF16)
    rest = x - hi.astype(F32)
    mid = rest.astype(BF16)
    lo = (rest - mid.astype(F32)).astype(BF16)
    return _dot(hi, ones) + _dot(mid, ones) + _dot(lo, ones)


def _wkv_prepare_body(r_ref, k_ref, v_ref, lw_ref, a_ref, kk_ref, ka_ref,
                      g_ref, s0c_ref, rp_ref, y0_ref, *, tb):
    r = r_ref[...]
    k = k_ref[...]
    v = v_ref[...]
    lw = lw_ref[...]
    a = a_ref[...]
    kk = k * kk_ref[...]
    norm = jnp.sqrt(_head_group_sum(kk * kk))
    kk = kk / jnp.maximum(norm, 1e-12)
    kmod = k * (1.0 + (a - 1.0) * ka_ref[...])
    avec = -kk
    bvec = kk * a

    tr = lax.broadcasted_iota(jnp.int32, (CHUNK, CHUNK), 0)
    tc = lax.broadcasted_iota(jnp.int32, (CHUNK, CHUNK), 1)
    tri = jnp.where(tr >= tc, 1.0, 0.0).astype(BF16)
    lw_hi = lw.astype(BF16)
    lw_rest = lw - lw_hi.astype(F32)
    lw_mid = lw_rest.astype(BF16)
    lw_lo = (lw_rest - lw_mid.astype(F32)).astype(BF16)

    lane = lax.broadcasted_iota(jnp.int32, (CHUNK, PAIR), 1)
    head0 = lane < HEAD_DIM
    row = lax.broadcasted_iota(jnp.int32, (PAIR, PAIR), 0)
    col = lax.broadcasted_iota(jnp.int32, (PAIR, PAIR), 1)
    strict = row > col
    lower = row >= col
    eye = jnp.where(row == col, 1.0, 0.0).astype(F32)

    def stack(x):
        return jnp.concatenate([jnp.where(head0, x, 0.0), jnp.where(head0, 0.0, x)], axis=0)

    chunks = range(tb // CHUNK)
    pre = []
    for c in chunks:
        sl = slice(c * CHUNK, (c + 1) * CHUNK)
        cm = _dot(tri, lw_hi[sl]) + _dot(tri, lw_mid[sl]) + _dot(tri, lw_lo[sl])
        cend = cm[CHUNK - 1:CHUNK]
        e_in = jnp.exp(cm)
        e_out = jnp.exp(-cm)
        e_tail = jnp.exp(cend - cm)
        a_s = stack(avec[sl] * jnp.exp(cm - lw[sl]))
        r_s = stack(r[sl] * e_in)
        b_s = stack(bvec[sl] * e_out)
        k_s = stack(kmod[sl] * e_out)
        bt_s = stack(bvec[sl] * e_tail)
        kt_s = stack(kmod[sl] * e_tail)
        v_s = stack(v[sl])
        p = _dot_nt(jnp.concatenate([a_s, r_s], axis=0).astype(BF16),
                    jnp.concatenate([b_s, k_s], axis=0).astype(BF16))
        pre.append(dict(cend=cend, a_s=a_s, r_s=r_s, bt_s=bt_s, kt_s=kt_s, v_s=v_s,
                        l_ab=jnp.where(strict, p[:PAIR, :PAIR], 0.0),
                        l_ak=jnp.where(strict, p[:PAIR, PAIR:], 0.0),
                        a_rb=jnp.where(lower, p[PAIR:, :PAIR], 0.0),
                        a_rk=jnp.where(lower, p[PAIR:, PAIR:], 0.0)))

    ms = [q["l_ab"] for q in pre]
    ts = [eye + m for m in ms]
    ms = [_dot(m.astype(BF16), m.astype(BF16)) for m in ms]
    for _ in range(4):
        both = [_dot(m.astype(BF16), jnp.concatenate([t, m], axis=1).astype(BF16)) for t, m in zip(ts, ms)]
        ts = [t + b[:, :PAIR] for t, b in zip(ts, both)]
        ms = [b[:, PAIR:] for b in both]
    ts = [t + _dot(m.astype(BF16), t.astype(BF16)) for t, m in zip(ts, ms)]

    lvs = [_dot(q["l_ak"].astype(BF16), q["v_s"].astype(BF16)) for q in pre]
    aus = [_dot(t.astype(BF16), jnp.concatenate([q["a_s"], lv], axis=1).astype(BF16))
           for t, q, lv in zip(ts, pre, lvs)]
    zero = jnp.zeros((PAIR, PAIR), F32)
    xs = [_dot(jnp.concatenate([q["a_rb"], q["a_rk"]], axis=1).astype(BF16),
               jnp.concatenate([au, jnp.concatenate([zero, q["v_s"]], axis=1)], axis=0).astype(BF16))
          for q, au in zip(pre, aus)]
    for c, q, au, x in zip(chunks, pre, aus, xs):
        rp_ref[0, c] = (q["r_s"] + x[:, :PAIR]).astype(rp_ref.dtype)
        y0_ref[0, c] = x[:, PAIR:]
        ap = au[:, :PAIR]
        u0 = au[:, PAIR:]
        g_ref[0, c] = (eye * jnp.exp(q["cend"])
                       + _dot_tn(ap.astype(BF16), q["bt_s"].astype(BF16))).astype(g_ref.dtype)
        s0c_ref[0, c] = _dot_tn(jnp.concatenate([u0, q["v_s"]], axis=0).astype(BF16),
                                jnp.concatenate([q["bt_s"], q["kt_s"]], axis=0).astype(BF16))


def wkv_chunk_prepare(proj, lw, a, k_k, k_a, n_pairs, tb=1024):
    n = proj.shape[0]
    tb = min(tb, n)
    cpb = tb // CHUNK
    col = lambda off: pl.BlockSpec((tb, PAIR), lambda i, p: (i, off + p))
    par = pl.BlockSpec((1, PAIR), lambda i, p: (0, p))
    blk = pl.BlockSpec((1, cpb, PAIR, PAIR), lambda i, p: (p, i, 0, 0))
    shp = lambda dt: jax.ShapeDtypeStruct((n_pairs, n // CHUNK, PAIR, PAIR), dt)
    return pl.pallas_call(
        functools.partial(_wkv_prepare_body, tb=tb),
        grid=(n // tb, n_pairs),
        in_specs=[col(0), col(n_pairs), col(2 * n_pairs), col(0), col(0), par, par],
        out_specs=[blk] * 4,
        out_shape=[shp(BF16), shp(F32), shp(BF16), shp(F32)],
        compiler_params=_params("parallel", "parallel"),
        name="wkv_chunk_prepare",
    )(proj, proj, proj, lw, a, k_k.reshape(1, -1), k_a.reshape(1, -1))


def _wkv_scan_body(g_ref, s0c_ref, rp_ref, y0_ref, r_ref, k_ref, v_ref, a_ref, gate_ref,
                   ka_ref, rk_ref, lng_ref, lnb_ref, o_ref, s_ref, y_ref, *, cpb):
    @pl.when(pl.program_id(2) == 0)
    def _():
        s_ref[...] = jnp.zeros_like(s_ref)

    pairs = range(SCAN_PAIRS)
    states = [[s_ref[w].astype(BF16)] for w in pairs]
    for c in range(cpb):
        for w in pairs:
            s = _dot(states[w][c], g_ref[w, c].astype(BF16)) + s0c_ref[w, c]
            if c + 1 < cpb:
                states[w].append(s.astype(BF16))
            else:
                s_ref[w] = s
    for c in range(cpb):
        for w in pairs:
            y_st = _dot_nt(rp_ref[w, c].astype(BF16), states[w][c]) + y0_ref[w, c]
            y_ref[c * CHUNK:(c + 1) * CHUNK, w * PAIR:(w + 1) * PAIR] = y_st[:CHUNK] + y_st[CHUNK:]

    y = y_ref[...]
    inv = 1.0 / HEAD_DIM
    mu = _head_group_sum(y) * inv
    yc = y - mu
    var = _head_group_sum(yc * yc) * inv
    yn = yc * lax.rsqrt(var + GN_EPS) * lng_ref[...] + lnb_ref[...]
    r = r_ref[...]
    kmod = k_ref[...] * (1.0 + (a_ref[...] - 1.0) * ka_ref[...])
    bonus = _head_group_sum(r * kmod * rk_ref[...]) * v_ref[...]
    o_ref[...] = ((yn + bonus) * gate_ref[...]).astype(o_ref.dtype)


def wkv_chunk_scan(prep, proj, a, gate, k_a, r_k, ln_g, ln_b, n_pairs, seq, tb=1024):
    g_all, s0c_all, rp_all, y0_all = prep
    n = proj.shape[0]
    tb = min(tb, seq)
    cpb = tb // CHUNK
    nblk = seq // tb
    assert n_pairs % SCAN_PAIRS == 0
    width = SCAN_PAIRS * PAIR
    groups = n_pairs // SCAN_PAIRS
    blk = pl.BlockSpec((SCAN_PAIRS, cpb, PAIR, PAIR), lambda b, p, t: (p, b * nblk + t, 0, 0))
    col = lambda off: pl.BlockSpec((tb, width), lambda b, p, t: (b * nblk + t, off + p))
    par = pl.BlockSpec((1, width), lambda b, p, t: (0, p))
    vec = lambda z: z.reshape(1, -1)
    return pl.pallas_call(
        functools.partial(_wkv_scan_body, cpb=cpb),
        grid=(n // seq, groups, nblk),
        in_specs=[blk] * 4 + [col(0), col(groups), col(2 * groups), col(0), col(0)] + [par] * 4,
        out_specs=col(0),
        out_shape=jax.ShapeDtypeStruct((n, n_pairs * PAIR), BF16),
        scratch_shapes=[pltpu.VMEM((SCAN_PAIRS, PAIR, PAIR), F32), pltpu.VMEM((tb, width), F32)],
        compiler_params=_params("parallel", "parallel", "arbitrary"),
        name="wkv_chunk_scan",
    )(g_all, s0c_all, rp_all, y0_all, proj, proj, proj, a, gate,
      vec(k_a), vec(r_k), vec(ln_g), vec(ln_b))


def _memattn_body(q_ref, m_ref, o_ref):
    width = MEM_HEADS * MEM_HEAD_DIM
    scale = MEM_HEAD_DIM ** -0.5
    cols = [slice(h * MEM_HEAD_DIM, (h + 1) * MEM_HEAD_DIM) for h in range(MEM_HEADS)]
    scores = [_dot_nt(q_ref[:, sl].astype(BF16), m_ref[:, sl].astype(BF16)) * scale for sl in cols]
    exps = [jnp.exp(s - jnp.max(s, axis=-1, keepdims=True)) for s in scores]
    outs = [_dot(e.astype(BF16), m_ref[:, width + sl.start:width + sl.stop].astype(BF16))
            for sl, e in zip(cols, exps)]
    for sl, e, o in zip(cols, exps, outs):
        o_ref[:, sl] = (o / jnp.sum(e, axis=-1, keepdims=True)).astype(o_ref.dtype)


def memory_attention(proj, q_col_block, mkv, seq, n_mem, tm=512):
    n = proj.shape[0]
    width = MEM_HEADS * MEM_HEAD_DIM
    tm = min(tm, seq)
    nblk = seq // tm
    return pl.pallas_call(
        _memattn_body,
        grid=(n // seq, nblk),
        in_specs=[pl.BlockSpec((tm, width), lambda b, t: (b * nblk + t, q_col_block)),
                  pl.BlockSpec((n_mem, 2 * width), lambda b, t: (b, 0))],
        out_specs=pl.BlockSpec((tm, width), lambda b, t: (b * nblk + t, 0)),
        out_shape=jax.ShapeDtypeStruct((n, width), BF16),
        compiler_params=_params("parallel", "parallel"),
        name="memory_attention",
    )(proj, mkv)


def _outproj_body(s_ref, m_ref, w1_ref, w2_ref, x_ref, o_ref):
    o_ref[...] = x_ref[...] + _dot(s_ref[...], w1_ref[...]) + _dot(m_ref[...], w2_ref[...])


def out_projection(seq_out, mem_out, w_seq, w_mem, x, tm=512, tn=1024):
    n, d = x.shape
    tm, tn = min(tm, n), min(tn, d)
    ws, wm = seq_out.shape[1], mem_out.shape[1]
    return pl.pallas_call(
        _outproj_body,
        grid=(n // tm, d // tn),
        in_specs=[pl.BlockSpec((tm, ws), lambda i, j: (i, 0)),
                  pl.BlockSpec((tm, wm), lambda i, j: (i, 0)),
                  pl.BlockSpec((ws, tn), lambda i, j: (0, j)),
                  pl.BlockSpec((wm, tn), lambda i, j: (0, j)),
                  pl.BlockSpec((tm, tn), lambda i, j: (i, j))],
        out_specs=pl.BlockSpec((tm, tn), lambda i, j: (i, j)),
        out_shape=jax.ShapeDtypeStruct((n, d), F32),
        compiler_params=_params("parallel", "parallel"),
        name="out_projection",
    )(seq_out, mem_out, w_seq, w_mem, x)


def _bandattn_body(q_ref, kp_ref, kc_ref, vp_ref, vc_ref, bias_ref, o_ref, *, tq):
    qi = pl.program_id(2)
    scale = HEAD_DIM ** -0.5
    k_all = jnp.concatenate([kp_ref[...], kc_ref[...]], axis=0).astype(BF16)
    v_all = jnp.concatenate([vp_ref[...], vc_ref[...]], axis=0).astype(BF16)
    lane = lax.broadcasted_iota(jnp.int32, (CHUNK, PAIR), 1)
    head0 = lane < HEAD_DIM
    kcol = lax.broadcasted_iota(jnp.int32, (2 * CHUNK, BAND), 1)
    pad = LEFT_CHUNKS * CHUNK
    bias = bias_ref[...]
    chunks = range(tq // CHUNK)
    starts = [tq - pad + j * CHUNK for j in chunks]
    scores = []
    for j, start in zip(chunks, starts):
        q = q_ref[j * CHUNK:(j + 1) * CHUNK, :] * scale
        q2 = jnp.concatenate([jnp.where(head0, q, 0.0), jnp.where(head0, 0.0, q)], axis=0).astype(BF16)
        scores.append(_dot_nt(q2, k_all[start:start + BAND]))
    exps = []
    for start, s in zip(starts, scores):
        valid = jnp.logical_or(qi > 0, kcol + start >= tq)
        s = jnp.where(valid, s + bias, -jnp.inf)
        exps.append(jnp.exp(s - jnp.max(s, axis=-1, keepdims=True)))
    outs = [_dot(e.astype(BF16), v_all[start:start + BAND]) for start, e in zip(starts, exps)]
    for j, e, o in zip(chunks, exps, outs):
        o = o / jnp.sum(e, axis=-1, keepdims=True)
        o_ref[j * CHUNK:(j + 1) * CHUNK, :] = jnp.where(head0, o[:CHUNK], o[CHUNK:]).astype(o_ref.dtype)


def band_attention(proj, kv, bias, n_pairs, seq, tq=512):
    n = proj.shape[0]
    tq = min(tq, seq)
    assert tq >= LEFT_CHUNKS * CHUNK
    nblk = seq // tq
    cur = lambda off: pl.BlockSpec((tq, PAIR), lambda b, p, t: (b * nblk + t, off + p))
    prev = lambda off: pl.BlockSpec((tq, PAIR), lambda b, p, t: (b * nblk + jnp.maximum(t - 1, 0), off + p))
    return pl.pallas_call(
        functools.partial(_bandattn_body, tq=tq),
        grid=(n // seq, n_pairs, nblk),
        in_specs=[cur(0), prev(0), cur(0), prev(n_pairs), cur(n_pairs),
                  pl.BlockSpec((2 * CHUNK, BAND), lambda b, p, t: (p, 0))],
        out_specs=cur(0),
        out_shape=jax.ShapeDtypeStruct((n, n_pairs * PAIR), BF16),
        compiler_params=_params("parallel", "parallel", "parallel"),
        name="band_attention",
    )(proj, kv, kv, kv, kv, bias.reshape(-1, BAND))


def _top16(scores, payloads):
    nl = scores[0].shape[1]
    rids = [lax.broadcasted_iota(jnp.int32, s.shape, 0).astype(F32) for s in scores]
    slot = lax.broadcasted_iota(jnp.int32, (PEER_TOPK, nl), 0)
    tile = 8

    def winner(s, rid, payload):
        items = [(s[g:g + tile], rid[g:g + tile], None if payload is None else payload[g:g + tile])
                 for g in range(0, s.shape[0], tile)]
        while len(items) > 1:
            nxt = []
            for k in range(0, len(items) - 1, 2):
                (va, ia, pa), (vb, ib, pb) = items[k], items[k + 1]
                keep = va >= vb
                nxt.append((jnp.maximum(va, vb), jnp.where(keep, ia, ib),
                            None if pa is None else jnp.where(keep, pa, pb)))
            if len(items) % 2:
                nxt.append(items[-1])
            items = nxt
        v, idx, p = items[0]
        for shift in (4, 2, 1):
            pv, pi = pltpu.roll(v, shift, 0), pltpu.roll(idx, shift, 0)
            take = (pv > v) | ((pv == v) & (pi < idx))
            if p is not None:
                p = jnp.where(take, pltpu.roll(p, shift, 0), p)
            v, idx = jnp.where(take, pv, v), jnp.where(take, pi, idx)
        return v, idx, p

    def body(i, carry):
        sel = slot == i
        out = []
        for (s, vals, picks), rid, payload in zip(carry, rids, payloads):
            m, am, p = winner(s, rid, payload)
            hit = rid == jnp.concatenate([am] * (s.shape[0] // tile), axis=0)
            pick = am if payload is None else p
            out.append((jnp.where(hit, -jnp.inf, s), jnp.where(sel, m[0:1], vals), jnp.where(sel, pick[0:1], picks)))
        return tuple(out)

    zero = jnp.zeros((PEER_TOPK, nl), F32)
    res = lax.fori_loop(0, PEER_TOPK, body, tuple((s, zero, zero) for s in scores))
    return [(vals, picks) for _, vals, picks in res]


def _pair_candidates(a, b, combine):
    half = PEER_TOPK // 2
    rows = [combine(a[0:1], b)]
    rows += [combine(a[i:i + 1], b[:half]) for i in range(1, half)]
    rows.append(combine(a[half:], b[0:1]))
    return jnp.concatenate(rows, axis=0)


TOPK_HEADS = 2


def _peer_topk_body(q_ref, keys_ref, eidx_ref, gate_ref):
    q = q_ref[...].astype(BF16)
    scores = []
    for h in range(TOPK_HEADS):
        for half in range(2):
            col = (2 * h + half) * PEER_KEYS
            scores.append(_dot_nt(keys_ref[h, half].astype(BF16), q[:, col:col + PEER_KEYS]))
    cands, cidxs = [], []
    for h in range(TOPK_HEADS):
        (a, i1), (b, i2) = _top16(scores[2 * h:2 * h + 2], [None, None])
        cands.append(_pair_candidates(a, b, lambda x, y: x + y))
        cidxs.append(_pair_candidates(i1, i2, lambda x, y: x * PEER_KEYS + y))
    for h, (top, eidx) in enumerate(_top16(cands, cidxs)):
        rows = slice(h * PEER_TOPK, (h + 1) * PEER_TOPK)
        e = jnp.exp(top - top[0:1])
        gate_ref[rows, :] = e / jnp.sum(e, axis=0, keepdims=True)
        eidx_ref[rows, :] = eidx.astype(jnp.int32)


def peer_topk(q, keys, tl=128):
    n = q.shape[0]
    tl = min(tl, n)
    blk = pl.BlockSpec((TOPK_HEADS * PEER_TOPK, tl), lambda i, h: (h, i))
    return pl.pallas_call(
        _peer_topk_body,
        grid=(n // tl, PEER_HEADS // TOPK_HEADS),
        in_specs=[pl.BlockSpec((tl, TOPK_HEADS * 2 * PEER_KEYS), lambda i, h: (i, h)),
                  pl.BlockSpec((TOPK_HEADS, 2, PEER_KEYS, PEER_KEYS), lambda i, h: (h, 0, 0, 0))],
        out_specs=[blk, blk],
        out_shape=[jax.ShapeDtypeStruct((PEER_PICKS, n), jnp.int32),
                   jax.ShapeDtypeStruct((PEER_PICKS, n), F32)],
        compiler_params=_params("parallel", "parallel"),
        name="peer_topk",
    )(q, keys)


PEER_SLOTS = 8
LANES = 128
PITCH_PAD = 4


def _pack_body(u_ref, v_ref, o_ref):
    hi = lax.bitcast_convert_type(u_ref[0].astype(BF16).astype(F32), jnp.uint32)
    lo = lax.bitcast_convert_type(v_ref[0].astype(BF16).astype(F32), jnp.uint32)
    words = hi | (lo >> 16)
    for c in range(o_ref.shape[1]):
        o_ref[:, c, :] = words[:, c * LANES:(c + 1) * LANES]


def _pack_expert_table(u, v, layer, te=256):
    _, n_exp, d = u.shape
    chunks = d // LANES
    te = min(te, n_exp)
    packed = pl.pallas_call(
        _pack_body,
        grid=(n_exp // te,),
        in_specs=[pl.BlockSpec((1, te, d), lambda i: (layer, i, 0))] * 2,
        out_specs=pl.BlockSpec((te, chunks, LANES), lambda i: (i, 0, 0)),
        out_shape=jax.ShapeDtypeStruct((n_exp, chunks, LANES), jnp.uint32),
        compiler_params=_params("parallel"),
        name="pack_expert_table",
    )(u, v)
    return packed.reshape(n_exp * chunks, LANES)


def _peer_ffn_body(eidx_ref, gate_ref, hn_ref, x_ref, tab_ref, *rest, tb, d, out_norm):
    norm_ref, o_ref, scratch = (rest[0], rest[1], rest[2:]) if out_norm else (None, rest[0], rest[1:])
    rows_refs = scratch[:PEER_SLOTS]
    sem_ref, w_ref, stage_ref, hn_rows_ref, gate_rows_ref, ids_ref, ids_sem_ref = scratch[PEER_SLOTS:]
    chunks = d // LANES
    pitch = chunks + PITCH_PAD

    def issue(ids, slot):
        for e in range(PEER_PICKS):
            src = tab_ref.at[pl.ds(pl.multiple_of(ids(e) * chunks, chunks), chunks), :]
            dst = rows_refs[slot].at[pl.ds(e * pitch, chunks), :]
            pltpu.make_async_copy(src, dst, sem_ref.at[slot]).start(priority=e % 2)

    def block_ids(t):
        return lambda e: eidx_ref[t, e]

    def ids_window(g, half):
        rows = g * PEER_SLOTS if isinstance(g, int) else pl.multiple_of(g * PEER_SLOTS, PEER_SLOTS)
        return pltpu.make_async_copy(eidx_ref.at[pl.ds(rows, 2 * PEER_SLOTS), :], ids_ref.at[half],
                                     ids_sem_ref.at[half])

    def wait(slot):
        total = PEER_PICKS * chunks
        pltpu.make_async_copy(tab_ref.at[pl.ds(0, total), :],
                              rows_refs[slot].at[pl.ds(0, total), :], sem_ref.at[slot]).wait()

    pick_diag = (lax.broadcasted_iota(jnp.int32, (PEER_PICKS, PEER_PICKS), 0)
                 == lax.broadcasted_iota(jnp.int32, (PEER_PICKS, PEER_PICKS), 1))
    hi_mask = jnp.uint32(0xFFFF0000)

    def words(slot, c):
        return rows_refs[slot][pl.ds(c, PEER_PICKS, stride=pitch), :]

    def pick_weights(slot, hn_rows, gate_rows, row):
        acc = jnp.zeros((PEER_PICKS, LANES), F32)
        for c in range(chunks):
            u = lax.bitcast_convert_type(words(slot, c) & hi_mask, F32)
            acc = acc + u * hn_rows[row:row + 1, c * LANES:(c + 1) * LANES]
        act = jnp.sum(acc, axis=1, keepdims=True)
        act = 0.5 * act * (1.0 + lax.erf(act * (2.0 ** -0.5)))
        gate = jnp.sum(jnp.where(pick_diag, gate_rows[row:row + 1, :], 0.0), axis=1, keepdims=True)
        w_ref[slot] = jnp.broadcast_to(gate * act, (PEER_PICKS, LANES))

    def combine(slot):
        w = w_ref[slot]
        for c in range(chunks):
            v = lax.bitcast_convert_type(words(slot, c) << 16, F32)
            stage_ref[slot:slot + 1, c * LANES:(c + 1) * LANES] = jnp.sum(v * w, axis=0, keepdims=True)

    def store_group(g):
        first = g * PEER_SLOTS if isinstance(g, int) else pl.multiple_of(g * PEER_SLOTS, PEER_SLOTS)
        rows = pl.ds(first, PEER_SLOTS)
        y = x_ref[rows, :] + stage_ref[...]
        if norm_ref is not None:
            y = y * lax.rsqrt(jnp.mean(y * y, axis=-1, keepdims=True) + RMS_EPS) * norm_ref[...]
        o_ref[rows, :] = y

    ahead = PEER_SLOTS - 1
    groups = tb // PEER_SLOTS
    steady = groups - 1
    ids_window(0, 0).start()
    for t in range(ahead):
        issue(block_ids(t), t)
    wait(0)
    pick_weights(0, hn_ref, gate_ref, 0)

    def group(g, half):
        first = g * PEER_SLOTS if isinstance(g, int) else pl.multiple_of(g * PEER_SLOTS, PEER_SLOTS)
        rows = pl.ds(first, 2 * PEER_SLOTS)
        hn_rows_ref[...] = hn_ref[rows, :]
        gate_rows_ref[...] = gate_ref[rows, :]
        ids_window(g, half).wait()
        if isinstance(g, int):
            if g + 1 < steady:
                ids_window(g + 1, 1 - half).start()
        else:
            @pl.when(g + 1 < steady)
            def _():
                ids_window(g + 1, 1 - half).start()
        for s in range(PEER_SLOTS):
            wait((s + 1) % PEER_SLOTS)
            pick_weights((s + 1) % PEER_SLOTS, hn_rows_ref, gate_rows_ref, s + 1)
            combine(s)
            issue(lambda e, s=s: ids_ref[half, s + ahead, e], (s + ahead) % PEER_SLOTS)
        store_group(g)

    def group_pair(k, carry):
        group(2 * k, 0)
        group(2 * k + 1, 1)
        return carry

    lax.fori_loop(0, steady // 2, group_pair, 0)
    if steady % 2:
        group(steady - 1, 0)

    for s in range(PEER_SLOTS):
        t = (groups - 1) * PEER_SLOTS + s
        if t + ahead < tb:
            issue(block_ids(t + ahead), (s + ahead) % PEER_SLOTS)
        if t + 1 < tb:
            wait((s + 1) % PEER_SLOTS)
            pick_weights((s + 1) % PEER_SLOTS, hn_ref, gate_ref, t + 1)
        combine(s)
    store_group(groups - 1)


def peer_expert_ffn(eidx, gate, hn, x, table, tb=128, out_norm_g=None):
    n, d = x.shape
    tb = min(tb, n)
    assert tb % PEER_SLOTS == 0 and table.shape[1] == LANES
    assert table.shape[0] >= PEER_PICKS * (d // LANES)
    slot_rows = PEER_PICKS * (d // LANES + PITCH_PAD)
    out_norm = out_norm_g is not None
    in_specs = [pl.BlockSpec((tb, PEER_PICKS), lambda i: (i, 0), memory_space=pltpu.SMEM),
                pl.BlockSpec((tb, PEER_PICKS), lambda i: (i, 0)),
                pl.BlockSpec((tb, d), lambda i: (i, 0)),
                pl.BlockSpec((tb, d), lambda i: (i, 0)),
                pl.BlockSpec(memory_space=pl.ANY)]
    operands = [eidx, gate, hn, x, table]
    if out_norm:
        in_specs.append(pl.BlockSpec((1, d), lambda i: (0, 0)))
        operands.append(out_norm_g.reshape(1, d))
    return pl.pallas_call(
        functools.partial(_peer_ffn_body, tb=tb, d=d, out_norm=out_norm),
        grid=(n // tb,),
        in_specs=in_specs,
        out_specs=pl.BlockSpec((tb, d), lambda i: (i, 0)),
        out_shape=jax.ShapeDtypeStruct((n, d), F32),
        scratch_shapes=[pltpu.VMEM((slot_rows, LANES), jnp.uint32)] * PEER_SLOTS
                       + [pltpu.SemaphoreType.DMA((PEER_SLOTS,)),
                          pltpu.VMEM((PEER_SLOTS, PEER_PICKS, LANES), F32),
                          pltpu.VMEM((PEER_SLOTS, d), F32),
                          pltpu.VMEM((2 * PEER_SLOTS, d), F32),
                          pltpu.VMEM((2 * PEER_SLOTS, PEER_PICKS), F32),
                          pltpu.SMEM((2, 2 * PEER_SLOTS, PEER_PICKS), jnp.int32),
                          pltpu.SemaphoreType.DMA((2,))],
        compiler_params=_params("arbitrary"),
        name="peer_expert_ffn",
    )(*operands)


def peer_layer(x, norm_g, wq, keys, u_all, v_all, layer, tb=512, out_norm_g=None):
    q, hn = norm_matmul(x, norm_g, wq.astype(BF16), emit_hn=True)
    eidx, gate = peer_topk(q, keys)
    return peer_expert_ffn(eidx.T, gate.T, hn, x, _pack_expert_table(u_all, v_all, layer), tb=tb,
                           out_norm_g=out_norm_g)


def _band_bias(rel_bias):
    n_rel = rel_bias.shape[1]
    far = BAND - n_rel + CHUNK - 1
    long_row = jnp.concatenate([jnp.broadcast_to(rel_bias[:, n_rel - 1:], (rel_bias.shape[0], far)),
                                rel_bias[:, ::-1]], axis=1)
    rows = [long_row[:, CHUNK - 1 - i:CHUNK - 1 - i + BAND] for i in range(CHUNK)]
    return jnp.stack(rows, axis=1).astype(F32)


def kernel(x, mem, norm_mix, norm_ffn, norm_mem, w_mem_kv, w_out, peer_wq, peer_keys, peer_u, peer_v, a_mix, a_w_in, a_w0, a_w1, a_w2, a_a0, a_a1, a_a2, a_g1, a_g2, a_k_k, a_k_a, a_r_k, a_ln_g, a_ln_b, kv_norm, w_kv_shared, b_w_in, b_rel_bias, final_norm):
    bsz, seq, d = x.shape
    n = bsz * seq
    n_mem = mem.shape[1]
    seq_width = a_w0.shape[1]
    n_pairs = seq_width // PAIR
    mem_width = MEM_HEADS * MEM_HEAD_DIM
    x = x.reshape(n, d)
    mem2 = mem.reshape(bsz * n_mem, d)

    def mixer_tail(x, seq_out, proj, q_col_block, layer, out_norm_g=None):
        mkv = norm_matmul(mem2, norm_mem[layer], w_mem_kv[layer].astype(BF16), tm=256)
        mem_out = memory_attention(proj, q_col_block, mkv, seq, n_mem)
        wo = w_out[layer].astype(BF16)
        x = out_projection(seq_out, mem_out, wo[:seq_width], wo[seq_width:], x)
        return peer_layer(x, norm_ffn[layer], peer_wq[layer], peer_keys[layer], peer_u, peer_v, layer,
                          out_norm_g=out_norm_g)

    h = rmsnorm(x, norm_mix[0])
    mix_tab = jnp.concatenate([a_mix[0], jnp.zeros((1, d), F32)], axis=0)
    tiles = seq_width // 512
    gid_main = jnp.asarray([0] * tiles + [2] * tiles + [3] * tiles + [6] * (mem_width // 512), jnp.int32)
    proj = mix_matmul(h, mix_tab, gid_main, a_w_in[0].astype(BF16), seq, tn=512)
    rank = a_w1.shape[2]
    padc = lambda w: jnp.pad(w, ((0, 0), (0, LORA_PAD - rank)))
    padr = lambda w: jnp.pad(w, ((0, LORA_PAD - rank), (0, 0)))
    w_l1 = jnp.concatenate([padc(a_w1[0]), padc(a_a1[0]), a_g1[0]], axis=1).astype(BF16)
    gid_l1 = jnp.asarray([1, 4] + [5] * (a_g1.shape[2] // LORA_PAD), jnp.int32)
    t1 = mix_matmul(h, mix_tab, gid_l1, w_l1, seq, tn=LORA_PAD)
    lw, a_iclr, gate = lora_stage2(t1, padr(a_w2[0]).astype(BF16), padr(a_a2[0]).astype(BF16),
                                   a_g2[0].astype(BF16), a_w0[0], a_a0[0])
    prep = wkv_chunk_prepare(proj, lw, a_iclr, a_k_k[0], a_k_a[0], n_pairs)
    seq_out = wkv_chunk_scan(prep, proj, a_iclr, gate, a_k_a[0], a_r_k[0].reshape(-1),
                             a_ln_g[0], a_ln_b[0], n_pairs, seq)
    x = mixer_tail(x, seq_out, proj, (3 * seq_width) // mem_width, 0)

    kv = norm_matmul(x, kv_norm, w_kv_shared.astype(BF16), tm=1024)

    proj = norm_matmul(x, norm_mix[1], b_w_in[0].astype(BF16), tm=1024)
    seq_out = band_attention(proj, kv, _band_bias(b_rel_bias[0]), n_pairs, seq)
    x = mixer_tail(x, seq_out, proj, seq_width // mem_width, 1, out_norm_g=final_norm)
    return x.reshape(bsz, seq, d)
```

```python
import functools

import jax
import jax.numpy as jnp
from jax import lax
from jax.experimental import pallas as pl
from jax.experimental.pallas import tpu as pltpu

F32 = jnp.float32
BF16 = jnp.bfloat16

HEAD_DIM = 64
PAIR = 2 * HEAD_DIM
CHUNK = 64
SCAN_PAIRS = 2
LEFT_CHUNKS = 8
BAND = (LEFT_CHUNKS + 1) * CHUNK
MEM_HEADS = 4
MEM_HEAD_DIM = 128
PEER_KEYS = 128
PEER_HEADS = 8
PEER_TOPK = 16
PEER_PICKS = PEER_HEADS * PEER_TOPK
GN_EPS = 64e-5
RMS_EPS = 1e-6
LORA_PAD = 128
VMEM_LIMIT = 48 * 1024 * 1024


def _params(*sem):
    return pltpu.CompilerParams(dimension_semantics=sem, vmem_limit_bytes=VMEM_LIMIT)


def _dot(a, b):
    return jnp.dot(a, b, preferred_element_type=F32)


def _dot_nt(a, b, precision=None):
    return lax.dot_general(a, b, (((1,), (1,)), ((), ())), precision=precision,
                           preferred_element_type=F32)


def _dot_tn(a, b, precision=None):
    return lax.dot_general(a, b, (((0,), (0,)), ((), ())), precision=precision,
                           preferred_element_type=F32)


def _rmsnorm_body(x_ref, g_ref, o_ref):
    x = x_ref[...]
    ms = jnp.mean(x * x, axis=-1, keepdims=True)
    o_ref[...] = (x * lax.rsqrt(ms + RMS_EPS) * g_ref[...]).astype(o_ref.dtype)


def rmsnorm(x, g, tm=512):
    n, d = x.shape
    tm = min(tm, n)
    return pl.pallas_call(
        _rmsnorm_body,
        grid=(n // tm,),
        in_specs=[pl.BlockSpec((tm, d), lambda i: (i, 0)), pl.BlockSpec((1, d), lambda i: (0, 0))],
        out_specs=pl.BlockSpec((tm, d), lambda i: (i, 0)),
        out_shape=jax.ShapeDtypeStruct((n, d), F32),
        compiler_params=_params("parallel"),
        name="rmsnorm",
    )(x, g.reshape(1, d))


def _normmm_body(x_ref, g_ref, w_ref, o_ref, *rest, emit_hn):
    lhs_ref = rest[-1]

    @pl.when(pl.program_id(1) == 0)
    def _():
        x = x_ref[...]
        ms = jnp.mean(x * x, axis=-1, keepdims=True)
        hn = x * lax.rsqrt(ms + RMS_EPS) * g_ref[...]
        lhs_ref[...] = hn.astype(BF16)
        if emit_hn:
            rest[0][...] = hn

    o_ref[...] = _dot(lhs_ref[...], w_ref[...]).astype(o_ref.dtype)


def norm_matmul(x, g, w_bf16, emit_hn=False, tm=512, tn=1024, out_dtype=BF16):
    n, d = x.shape
    nc = w_bf16.shape[1]
    tm, tn = min(tm, n), min(tn, nc)
    out_shape = [jax.ShapeDtypeStruct((n, nc), out_dtype)]
    out_specs = [pl.BlockSpec((tm, tn), lambda i, j: (i, j))]
    if emit_hn:
        out_shape.append(jax.ShapeDtypeStruct((n, d), F32))
        out_specs.append(pl.BlockSpec((tm, d), lambda i, j: (i, 0)))
    res = pl.pallas_call(
        functools.partial(_normmm_body, emit_hn=emit_hn),
        grid=(n // tm, nc // tn),
        in_specs=[pl.BlockSpec((tm, d), lambda i, j: (i, 0)),
                  pl.BlockSpec((1, d), lambda i, j: (0, 0)),
                  pl.BlockSpec((d, tn), lambda i, j: (0, j))],
        out_specs=out_specs,
        out_shape=out_shape,
        scratch_shapes=[pltpu.VMEM((tm, d), BF16)],
        compiler_params=_params("parallel", "arbitrary"),
        name="norm_matmul",
    )(x, g.reshape(1, d), w_bf16)
    return res if emit_hn else res[0]


def _mixmm_body(gid_ref, h_ref, hp_ref, mix_ref, w_ref, o_ref, lhs_ref, diff_ref, *, tm, seq):
    i = pl.program_id(0)
    j = pl.program_id(1)

    @pl.when(j == 0)
    def _():
        h = h_ref[...]
        prev = jnp.where((i * tm) % seq == 0, 0.0, hp_ref[7:8, :])
        row = lax.broadcasted_iota(jnp.int32, h.shape, 0)
        diff_ref[...] = jnp.where(row == 0, prev, pltpu.roll(h, 1, 0)) - h

    new_group = jnp.logical_or(j == 0, gid_ref[j] != gid_ref[jnp.maximum(j - 1, 0)])

    @pl.when(new_group)
    def _():
        lhs_ref[...] = (h_ref[...] + diff_ref[...] * mix_ref[0]).astype(BF16)

    o_ref[...] = _dot(lhs_ref[...], w_ref[...])


def mix_matmul(h, mix_tab, gid, w_bf16, seq, tn, tm=1024):
    n, d = h.shape
    nc = w_bf16.shape[1]
    tm = min(tm, seq)
    sub = tm // 8
    grid_spec = pltpu.PrefetchScalarGridSpec(
        num_scalar_prefetch=1,
        grid=(n // tm, nc // tn),
        in_specs=[pl.BlockSpec((tm, d), lambda i, j, g: (i, 0)),
                  pl.BlockSpec((8, d), lambda i, j, g: (jnp.maximum(i * sub - 1, 0), 0)),
                  pl.BlockSpec((1, 1, d), lambda i, j, g: (g[j], 0, 0)),
                  pl.BlockSpec((d, tn), lambda i, j, g: (0, j))],
        out_specs=pl.BlockSpec((tm, tn), lambda i, j, g: (i, j)),
        scratch_shapes=[pltpu.VMEM((tm, d), BF16), pltpu.VMEM((tm, d), F32)],
    )
    return pl.pallas_call(
        functools.partial(_mixmm_body, tm=tm, seq=seq),
        grid_spec=grid_spec,
        out_shape=jax.ShapeDtypeStruct((n, nc), F32),
        compiler_params=_params("parallel", "arbitrary"),
        name="mix_matmul",
    )(gid, h, h, mix_tab.reshape(mix_tab.shape[0], 1, d), w_bf16)


def _sigmoid(x):
    return 1.0 / (1.0 + jnp.exp(-x))


def _lora2_body(t_ref, w2_ref, a2_ref, g2_ref, w0_ref, a0_ref, lw_ref, a_ref, g_ref):
    t = t_ref[...]
    tw = jnp.tanh(t[:, :LORA_PAD]).astype(BF16)
    ta = t[:, LORA_PAD:2 * LORA_PAD].astype(BF16)
    tg = _sigmoid(t[:, 2 * LORA_PAD:]).astype(BF16)
    u = w0_ref[...] + _dot(tw, w2_ref[...])
    softplus_neg_u = jnp.maximum(-u, 0.0) + jnp.log(1.0 + jnp.exp(-jnp.abs(u)))
    lw_ref[...] = -jnp.exp(-softplus_neg_u - 0.5)
    a_ref[...] = _sigmoid(a0_ref[...] + _dot(ta, a2_ref[...]))
    g_ref[...] = _dot(tg, g2_ref[...])


def lora_stage2(t1, w2p, a2p, g2, w0, a0, tm=256):
    n = t1.shape[0]
    width = w2p.shape[1]
    tm = min(tm, n)
    full = lambda a: pl.BlockSpec(a.shape, lambda i: (0, 0))
    row = pl.BlockSpec((tm, width), lambda i: (i, 0))
    w0 = w0.reshape(1, width)
    a0 = a0.reshape(1, width)
    return pl.pallas_call(
        _lora2_body,
        grid=(n // tm,),
        in_specs=[pl.BlockSpec((tm, t1.shape[1]), lambda i: (i, 0)), full(w2p), full(a2p), full(g2),
                  full(w0), full(a0)],
        out_specs=[row, row, row],
        out_shape=[jax.ShapeDtypeStruct((n, width), F32)] * 3,
        compiler_params=_params("parallel"),
        name="lora_stage2",
    )(t1, w2p, a2p, g2, w0, a0)


def _head_group_sum(x):
    width = x.shape[-1]
    r = lax.broadcasted_iota(jnp.int32, (width, width), 0) // HEAD_DIM
    c = lax.broadcasted_iota(jnp.int32, (width, width), 1) // HEAD_DIM
    ones = jnp.where(r == c, 1.0, 0.0).astype(BF16)
    hi = x.astype(BF16)
    rest = x - hi.astype(F32)
    mid = rest.astype(BF16)
    lo = (rest - mid.astype(F32)).astype(BF16)
    return _dot(hi, ones) + _dot(mid, ones) + _dot(lo, ones)


def _wkv_prepare_body(r_ref, k_ref, v_ref, lw_ref, a_ref, kk_ref, ka_ref,
                      g_ref, s0c_ref, rp_ref, y0_ref, *, tb):
    r = r_ref[...]
    k = k_ref[...]
    v = v_ref[...]
    lw = lw_ref[...]
    a = a_ref[...]
    kk = k * kk_ref[...]
    norm = jnp.sqrt(_head_group_sum(kk * kk))
    kk = kk / jnp.maximum(norm, 1e-12)
    kmod = k * (1.0 + (a - 1.0) * ka_ref[...])
    avec = -kk
    bvec = kk * a

    tr = lax.broadcasted_iota(jnp.int32, (CHUNK, CHUNK), 0)
    tc = lax.broadcasted_iota(jnp.int32, (CHUNK, CHUNK), 1)
    tri = jnp.where(tr >= tc, 1.0, 0.0).astype(BF16)
    lw_hi = lw.astype(BF16)
    lw_rest = lw - lw_hi.astype(F32)
    lw_mid = lw_rest.astype(BF16)
    lw_lo = (lw_rest - lw_mid.astype(F32)).astype(BF16)

    lane = lax.broadcasted_iota(jnp.int32, (CHUNK, PAIR), 1)
    head0 = lane < HEAD_DIM
    row = lax.broadcasted_iota(jnp.int32, (PAIR, PAIR), 0)
    col = lax.broadcasted_iota(jnp.int32, (PAIR, PAIR), 1)
    strict = row > col
    lower = row >= col
    eye = jnp.where(row == col, 1.0, 0.0).astype(F32)

    def stack(x):
        return jnp.concatenate([jnp.where(head0, x, 0.0), jnp.where(head0, 0.0, x)], axis=0)

    chunks = range(tb // CHUNK)
    pre = []
    for c in chunks:
        sl = slice(c * CHUNK, (c + 1) * CHUNK)
        cm = _dot(tri, lw_hi[sl]) + _dot(tri, lw_mid[sl]) + _dot(tri, lw_lo[sl])
        cend = cm[CHUNK - 1:CHUNK]
        e_in = jnp.exp(cm)
        e_out = jnp.exp(-cm)
        e_tail = jnp.exp(cend - cm)
        a_s = stack(avec[sl] * jnp.exp(cm - lw[sl]))
        r_s = stack(r[sl] * e_in)
        b_s = stack(bvec[sl] * e_out)
        k_s = stack(kmod[sl] * e_out)
        bt_s = stack(bvec[sl] * e_tail)
        kt_s = stack(kmod[sl] * e_tail)
        v_s = stack(v[sl])
        p = _dot_nt(jnp.concatenate([a_s, r_s], axis=0).astype(BF16),
                    jnp.concatenate([b_s, k_s], axis=0).astype(BF16))
        pre.append(dict(cend=cend, a_s=a_s, r_s=r_s, bt_s=bt_s, kt_s=kt_s, v_s=v_s,
                        l_ab=jnp.where(strict, p[:PAIR, :PAIR], 0.0),
                        l_ak=jnp.where(strict, p[:PAIR, PAIR:], 0.0),
                        a_rb=jnp.where(lower, p[PAIR:, :PAIR], 0.0),
                        a_rk=jnp.where(lower, p[PAIR:, PAIR:], 0.0)))

    ms = [q["l_ab"] for q in pre]
    ts = [eye + m for m in ms]
    ms = [_dot(m.astype(BF16), m.astype(BF16)) for m in ms]
    for _ in range(4):
        both = [_dot(m.astype(BF16), jnp.concatenate([t, m], axis=1).astype(BF16)) for t, m in zip(ts, ms)]
        ts = [t + b[:, :PAIR] for t, b in zip(ts, both)]
        ms = [b[:, PAIR:] for b in both]
    ts = [t + _dot(m.astype(BF16), t.astype(BF16)) for t, m in zip(ts, ms)]

    lvs = [_dot(q["l_ak"].astype(BF16), q["v_s"].astype(BF16)) for q in pre]
    aus = [_dot(t.astype(BF16), jnp.concatenate([q["a_s"], lv], axis=1).astype(BF16))
           for t, q, lv in zip(ts, pre, lvs)]
    zero = jnp.zeros((PAIR, PAIR), F32)
    xs = [_dot(jnp.concatenate([q["a_rb"], q["a_rk"]], axis=1).astype(BF16),
               jnp.concatenate([au, jnp.concatenate([zero, q["v_s"]], axis=1)], axis=0).astype(BF16))
          for q, au in zip(pre, aus)]
    for c, q, au, x in zip(chunks, pre, aus, xs):
        rp_ref[0, c] = (q["r_s"] + x[:, :PAIR]).astype(rp_ref.dtype)
        y0_ref[0, c] = x[:, PAIR:]
        ap = au[:, :PAIR]
        u0 = au[:, PAIR:]
        g_ref[0, c] = (eye * jnp.exp(q["cend"])
                       + _dot_tn(ap.astype(BF16), q["bt_s"].astype(BF16))).astype(g_ref.dtype)
        s0c_ref[0, c] = _dot_tn(jnp.concatenate([u0, q["v_s"]], axis=0).astype(BF16),
                                jnp.concatenate([q["bt_s"], q["kt_s"]], axis=0).astype(BF16))


def wkv_chunk_prepare(proj, lw, a, k_k, k_a, n_pairs, tb=1024):
    n = proj.shape[0]
    tb = min(tb, n)
    cpb = tb // CHUNK
    col = lambda off: pl.BlockSpec((tb, PAIR), lambda i, p: (i, off + p))
    par = pl.BlockSpec((1, PAIR), lambda i, p: (0, p))
    blk = pl.BlockSpec((1, cpb, PAIR, PAIR), lambda i, p: (p, i, 0, 0))
    shp = lambda dt: jax.ShapeDtypeStruct((n_pairs, n // CHUNK, PAIR, PAIR), dt)
    return pl.pallas_call(
        functools.partial(_wkv_prepare_body, tb=tb),
        grid=(n // tb, n_pairs),
        in_specs=[col(0), col(n_pairs), col(2 * n_pairs), col(0), col(0), par, par],
        out_specs=[blk] * 4,
        out_shape=[shp(BF16), shp(F32), shp(BF16), shp(F32)],
        compiler_params=_params("parallel", "parallel"),
        name="wkv_chunk_prepare",
    )(proj, proj, proj, lw, a, k_k.reshape(1, -1), k_a.reshape(1, -1))


def _wkv_scan_body(g_ref, s0c_ref, rp_ref, y0_ref, r_ref, k_ref, v_ref, a_ref, gate_ref,
                   ka_ref, rk_ref, lng_ref, lnb_ref, o_ref, s_ref, y_ref, *, cpb):
    @pl.when(pl.program_id(2) == 0)
    def _():
        s_ref[...] = jnp.zeros_like(s_ref)

    pairs = range(SCAN_PAIRS)
    states = [[s_ref[w].astype(BF16)] for w in pairs]
    for c in range(cpb):
        for w in pairs:
            s = _dot(states[w][c], g_ref[w, c].astype(BF16)) + s0c_ref[w, c]
            if c + 1 < cpb:
                states[w].append(s.astype(BF16))
            else:
                s_ref[w] = s
    for c in range(cpb):
        for w in pairs:
            y_st = _dot_nt(rp_ref[w, c].astype(BF16), states[w][c]) + y0_ref[w, c]
            y_ref[c * CHUNK:(c + 1) * CHUNK, w * PAIR:(w + 1) * PAIR] = y_st[:CHUNK] + y_st[CHUNK:]

    y = y_ref[...]
    inv = 1.0 / HEAD_DIM
    mu = _head_group_sum(y) * inv
    yc = y - mu
    var = _head_group_sum(yc * yc) * inv
    yn = yc * lax.rsqrt(var + GN_EPS) * lng_ref[...] + lnb_ref[...]
    r = r_ref[...]
    kmod = k_ref[...] * (1.0 + (a_ref[...] - 1.0) * ka_ref[...])
    bonus = _head_group_sum(r * kmod * rk_ref[...]) * v_ref[...]
    o_ref[...] = ((yn + bonus) * gate_ref[...]).astype(o_ref.dtype)


def wkv_chunk_scan(prep, proj, a, gate, k_a, r_k, ln_g, ln_b, n_pairs, seq, tb=1024):
    g_all, s0c_all, rp_all, y0_all = prep
    n = proj.shape[0]
    tb = min(tb, seq)
    cpb = tb // CHUNK
    nblk = seq // tb
    assert n_pairs % SCAN_PAIRS == 0
    width = SCAN_PAIRS * PAIR
    groups = n_pairs // SCAN_PAIRS
    blk = pl.BlockSpec((SCAN_PAIRS, cpb, PAIR, PAIR), lambda b, p, t: (p, b * nblk + t, 0, 0))
    col = lambda off: pl.BlockSpec((tb, width), lambda b, p, t: (b * nblk + t, off + p))
    par = pl.BlockSpec((1, width), lambda b, p, t: (0, p))
    vec = lambda z: z.reshape(1, -1)
    return pl.pallas_call(
        functools.partial(_wkv_scan_body, cpb=cpb),
        grid=(n // seq, groups, nblk),
        in_specs=[blk] * 4 + [col(0), col(groups), col(2 * groups), col(0), col(0)] + [par] * 4,
        out_specs=col(0),
        out_shape=jax.ShapeDtypeStruct((n, n_pairs * PAIR), BF16),
        scratch_shapes=[pltpu.VMEM((SCAN_PAIRS, PAIR, PAIR), F32), pltpu.VMEM((tb, width), F32)],
        compiler_params=_params("parallel", "parallel", "arbitrary"),
        name="wkv_chunk_scan",
    )(g_all, s0c_all, rp_all, y0_all, proj, proj, proj, a, gate,
      vec(k_a), vec(r_k), vec(ln_g), vec(ln_b))


def _memattn_body(q_ref, m_ref, o_ref):
    width = MEM_HEADS * MEM_HEAD_DIM
    scale = MEM_HEAD_DIM ** -0.5
    cols = [slice(h * MEM_HEAD_DIM, (h + 1) * MEM_HEAD_DIM) for h in range(MEM_HEADS)]
    scores = [_dot_nt(q_ref[:, sl].astype(BF16), m_ref[:, sl].astype(BF16)) * scale for sl in cols]
    exps = [jnp.exp(s - jnp.max(s, axis=-1, keepdims=True)) for s in scores]
    outs = [_dot(e.astype(BF16), m_ref[:, width + sl.start:width + sl.stop].astype(BF16))
            for sl, e in zip(cols, exps)]
    for sl, e, o in zip(cols, exps, outs):
        o_ref[:, sl] = (o / jnp.sum(e, axis=-1, keepdims=True)).astype(o_ref.dtype)


def memory_attention(proj, q_col_block, mkv, seq, n_mem, tm=512):
    n = proj.shape[0]
    width = MEM_HEADS * MEM_HEAD_DIM
    tm = min(tm, seq)
    nblk = seq // tm
    return pl.pallas_call(
        _memattn_body,
        grid=(n // seq, nblk),
        in_specs=[pl.BlockSpec((tm, width), lambda b, t: (b * nblk + t, q_col_block)),
                  pl.BlockSpec((n_mem, 2 * width), lambda b, t: (b, 0))],
        out_specs=pl.BlockSpec((tm, width), lambda b, t: (b * nblk + t, 0)),
        out_shape=jax.ShapeDtypeStruct((n, width), BF16),
        compiler_params=_params("parallel", "parallel"),
        name="memory_attention",
    )(proj, mkv)


def _outproj_body(s_ref, m_ref, w1_ref, w2_ref, x_ref, o_ref):
    o_ref[...] = x_ref[...] + _dot(s_ref[...], w1_ref[...]) + _dot(m_ref[...], w2_ref[...])


def out_projection(seq_out, mem_out, w_seq, w_mem, x, tm=512, tn=1024):
    n, d = x.shape
    tm, tn = min(tm, n), min(tn, d)
    ws, wm = seq_out.shape[1], mem_out.shape[1]
    return pl.pallas_call(
        _outproj_body,
        grid=(n // tm, d // tn),
        in_specs=[pl.BlockSpec((tm, ws), lambda i, j: (i, 0)),
                  pl.BlockSpec((tm, wm), lambda i, j: (i, 0)),
                  pl.BlockSpec((ws, tn), lambda i, j: (0, j)),
                  pl.BlockSpec((wm, tn), lambda i, j: (0, j)),
                  pl.BlockSpec((tm, tn), lambda i, j: (i, j))],
        out_specs=pl.BlockSpec((tm, tn), lambda i, j: (i, j)),
        out_shape=jax.ShapeDtypeStruct((n, d), F32),
        compiler_params=_params("parallel", "parallel"),
        name="out_projection",
    )(seq_out, mem_out, w_seq, w_mem, x)


def _bandattn_body(q_ref, kp_ref, kc_ref, vp_ref, vc_ref, bias_ref, o_ref, *, tq):
    qi = pl.program_id(2)
    scale = HEAD_DIM ** -0.5
    k_all = jnp.concatenate([kp_ref[...], kc_ref[...]], axis=0).astype(BF16)
    v_all = jnp.concatenate([vp_ref[...], vc_ref[...]], axis=0).astype(BF16)
    lane = lax.broadcasted_iota(jnp.int32, (CHUNK, PAIR), 1)
    head0 = lane < HEAD_DIM
    kcol = lax.broadcasted_iota(jnp.int32, (2 * CHUNK, BAND), 1)
    pad = LEFT_CHUNKS * CHUNK
    bias = bias_ref[...]
    chunks = range(tq // CHUNK)
    starts = [tq - pad + j * CHUNK for j in chunks]
    scores = []
    for j, start in zip(chunks, starts):
        q = q_ref[j * CHUNK:(j + 1) * CHUNK, :] * scale
        q2 = jnp.concatenate([jnp.where(head0, q, 0.0), jnp.where(head0, 0.0, q)], axis=0).astype(BF16)
        scores.append(_dot_nt(q2, k_all[start:start + BAND]))
    exps = []
    for start, s in zip(starts, scores):
        valid = jnp.logical_or(qi > 0, kcol + start >= tq)
        s = jnp.where(valid, s + bias, -jnp.inf)
        exps.append(jnp.exp(s - jnp.max(s, axis=-1, keepdims=True)))
    outs = [_dot(e.astype(BF16), v_all[start:start + BAND]) for start, e in zip(starts, exps)]
    for j, e, o in zip(chunks, exps, outs):
        o = o / jnp.sum(e, axis=-1, keepdims=True)
        o_ref[j * CHUNK:(j + 1) * CHUNK, :] = jnp.where(head0, o[:CHUNK], o[CHUNK:]).astype(o_ref.dtype)


def band_attention(proj, kv, bias, n_pairs, seq, tq=512):
    n = proj.shape[0]
    tq = min(tq, seq)
    assert tq >= LEFT_CHUNKS * CHUNK
    nblk = seq // tq
    cur = lambda off: pl.BlockSpec((tq, PAIR), lambda b, p, t: (b * nblk + t, off + p))
    prev = lambda off: pl.BlockSpec((tq, PAIR), lambda b, p, t: (b * nblk + jnp.maximum(t - 1, 0), off + p))
    return pl.pallas_call(
        functools.partial(_bandattn_body, tq=tq),
        grid=(n // seq, n_pairs, nblk),
        in_specs=[cur(0), prev(0), cur(0), prev(n_pairs), cur(n_pairs),
                  pl.BlockSpec((2 * CHUNK, BAND), lambda b, p, t: (p, 0))],
        out_specs=cur(0),
        out_shape=jax.ShapeDtypeStruct((n, n_pairs * PAIR), BF16),
        compiler_params=_params("parallel", "parallel", "parallel"),
        name="band_attention",
    )(proj, kv, kv, kv, kv, bias.reshape(-1, BAND))


def _top16(scores, payloads):
    nl = scores[0].shape[1]
    rids = [lax.broadcasted_iota(jnp.int32, s.shape, 0).astype(F32) for s in scores]
    slot = lax.broadcasted_iota(jnp.int32, (PEER_TOPK, nl), 0)
    tile = 8

    def winner(s, rid, payload):
        items = [(s[g:g + tile], rid[g:g + tile], None if payload is None else payload[g:g + tile])
                 for g in range(0, s.shape[0], tile)]
        while len(items) > 1:
            nxt = []
            for k in range(0, len(items) - 1, 2):
                (va, ia, pa), (vb, ib, pb) = items[k], items[k + 1]
                keep = va >= vb
                nxt.append((jnp.maximum(va, vb), jnp.where(keep, ia, ib),
                            None if pa is None else jnp.where(keep, pa, pb)))
            if len(items) % 2:
                nxt.append(items[-1])
            items = nxt
        v, idx, p = items[0]
        for shift in (4, 2, 1):
            pv, pi = pltpu.roll(v, shift, 0), pltpu.roll(idx, shift, 0)
            take = (pv > v) | ((pv == v) & (pi < idx))
            if p is not None:
                p = jnp.where(take, pltpu.roll(p, shift, 0), p)
            v, idx = jnp.where(take, pv, v), jnp.where(take, pi, idx)
        return v, idx, p

    def body(i, carry):
        sel = slot == i
        out = []
        for (s, vals, picks), rid, payload in zip(carry, rids, payloads):
            m, am, p = winner(s, rid, payload)
            hit = rid == jnp.concatenate([am] * (s.shape[0] // tile), axis=0)
            pick = am if payload is None else p
            out.append((jnp.where(hit, -jnp.inf, s), jnp.where(sel, m[0:1], vals), jnp.where(sel, pick[0:1], picks)))
        return tuple(out)

    zero = jnp.zeros((PEER_TOPK, nl), F32)
    res = lax.fori_loop(0, PEER_TOPK, body, tuple((s, zero, zero) for s in scores))
    return [(vals, picks) for _, vals, picks in res]


def _pair_candidates(a, b, combine):
    half = PEER_TOPK // 2
    rows = [combine(a[0:1], b)]
    rows += [combine(a[i:i + 1], b[:half]) for i in range(1, half)]
    rows.append(combine(a[half:], b[0:1]))
    return jnp.concatenate(rows, axis=0)


TOPK_HEADS = 2


def _peer_topk_body(q_ref, keys_ref, eidx_ref, gate_ref):
    q = q_ref[...].astype(BF16)
    scores = []
    for h in range(TOPK_HEADS):
        for half in range(2):
            col = (2 * h + half) * PEER_KEYS
            scores.append(_dot_nt(keys_ref[h, half].astype(BF16), q[:, col:col + PEER_KEYS]))
    cands, cidxs = [], []
    for h in range(TOPK_HEADS):
        (a, i1), (b, i2) = _top16(scores[2 * h:2 * h + 2], [None, None])
        cands.append(_pair_candidates(a, b, lambda x, y: x + y))
        cidxs.append(_pair_candidates(i1, i2, lambda x, y: x * PEER_KEYS + y))
    for h, (top, eidx) in enumerate(_top16(cands, cidxs)):
        rows = slice(h * PEER_TOPK, (h + 1) * PEER_TOPK)
        e = jnp.exp(top - top[0:1])
        gate_ref[rows, :] = e / jnp.sum(e, axis=0, keepdims=True)
        eidx_ref[rows, :] = eidx.astype(jnp.int32)


def peer_topk(q, keys, tl=128):
    n = q.shape[0]
    tl = min(tl, n)
    blk = pl.BlockSpec((TOPK_HEADS * PEER_TOPK, tl), lambda i, h: (h, i))
    return pl.pallas_call(
        _peer_topk_body,
        grid=(n // tl, PEER_HEADS // TOPK_HEADS),
        in_specs=[pl.BlockSpec((tl, TOPK_HEADS * 2 * PEER_KEYS), lambda i, h: (i, h)),
                  pl.BlockSpec((TOPK_HEADS, 2, PEER_KEYS, PEER_KEYS), lambda i, h: (h, 0, 0, 0))],
        out_specs=[blk, blk],
        out_shape=[jax.ShapeDtypeStruct((PEER_PICKS, n), jnp.int32),
                   jax.ShapeDtypeStruct((PEER_PICKS, n), F32)],
        compiler_params=_params("parallel", "parallel"),
        name="peer_topk",
    )(q, keys)


PEER_SLOTS = 8
LANES = 128
PITCH_PAD = 4


def _pack_body(u_ref, v_ref, o_ref):
    hi = lax.bitcast_convert_type(u_ref[0].astype(BF16).astype(F32), jnp.uint32)
    lo = lax.bitcast_convert_type(v_ref[0].astype(BF16).astype(F32), jnp.uint32)
    words = hi | (lo >> 16)
    for c in range(o_ref.shape[1]):
        o_ref[:, c, :] = words[:, c * LANES:(c + 1) * LANES]


def _pack_expert_table(u, v, layer, te=256):
    _, n_exp, d = u.shape
    chunks = d // LANES
    te = min(te, n_exp)
    packed = pl.pallas_call(
        _pack_body,
        grid=(n_exp // te,),
        in_specs=[pl.BlockSpec((1, te, d), lambda i: (layer, i, 0))] * 2,
        out_specs=pl.BlockSpec((te, chunks, LANES), lambda i: (i, 0, 0)),
        out_shape=jax.ShapeDtypeStruct((n_exp, chunks, LANES), jnp.uint32),
        compiler_params=_params("parallel"),
        name="pack_expert_table",
    )(u, v)
    return packed.reshape(n_exp * chunks, LANES)


def _peer_ffn_body(eidx_ref, gate_ref, hn_ref, x_ref, tab_ref, *rest, tb, d, out_norm):
    norm_ref, o_ref, scratch = (rest[0], rest[1], rest[2:]) if out_norm else (None, rest[0], rest[1:])
    rows_refs = scratch[:PEER_SLOTS]
    sem_ref, w_ref, stage_ref, hn_rows_ref, gate_rows_ref, ids_ref, ids_sem_ref = scratch[PEER_SLOTS:]
    chunks = d // LANES
    pitch = chunks + PITCH_PAD

    def issue(ids, slot):
        for e in range(PEER_PICKS):
            src = tab_ref.at[pl.ds(pl.multiple_of(ids(e) * chunks, chunks), chunks), :]
            dst = rows_refs[slot].at[pl.ds(e * pitch, chunks), :]
            pltpu.make_async_copy(src, dst, sem_ref.at[slot]).start(priority=e % 2)

    def block_ids(t):
        return lambda e: eidx_ref[t, e]

    def ids_window(g, half):
        rows = g * PEER_SLOTS if isinstance(g, int) else pl.multiple_of(g * PEER_SLOTS, PEER_SLOTS)
        return pltpu.make_async_copy(eidx_ref.at[pl.ds(rows, 2 * PEER_SLOTS), :], ids_ref.at[half],
                                     ids_sem_ref.at[half])

    def wait(slot):
        total = PEER_PICKS * chunks
        pltpu.make_async_copy(tab_ref.at[pl.ds(0, total), :],
                              rows_refs[slot].at[pl.ds(0, total), :], sem_ref.at[slot]).wait()

    pick_diag = (lax.broadcasted_iota(jnp.int32, (PEER_PICKS, PEER_PICKS), 0)
                 == lax.broadcasted_iota(jnp.int32, (PEER_PICKS, PEER_PICKS), 1))
    hi_mask = jnp.uint32(0xFFFF0000)

    def words(slot, c):
        return rows_refs[slot][pl.ds(c, PEER_PICKS, stride=pitch), :]

    def pick_weights(slot, hn_rows, gate_rows, row):
        acc = jnp.zeros((PEER_PICKS, LANES), F32)
        for c in range(chunks):
            u = lax.bitcast_convert_type(words(slot, c) & hi_mask, F32)
            acc = acc + u * hn_rows[row:row + 1, c * LANES:(c + 1) * LANES]
        act = jnp.sum(acc, axis=1, keepdims=True)
        act = 0.5 * act * (1.0 + lax.erf(act * (2.0 ** -0.5)))
        gate = jnp.sum(jnp.where(pick_diag, gate_rows[row:row + 1, :], 0.0), axis=1, keepdims=True)
        w_ref[slot] = jnp.broadcast_to(gate * act, (PEER_PICKS, LANES))

    def combine(slot):
        w = w_ref[slot]
        for c in range(chunks):
            v = lax.bitcast_convert_type(words(slot, c) << 16, F32)
            stage_ref[slot:slot + 1, c * LANES:(c + 1) * LANES] = jnp.sum(v * w, axis=0, keepdims=True)

    def store_group(g):
        first = g * PEER_SLOTS if isinstance(g, int) else pl.multiple_of(g * PEER_SLOTS, PEER_SLOTS)
        rows = pl.ds(first, PEER_SLOTS)
        y = x_ref[rows, :] + stage_ref[...]
        if norm_ref is not None:
            y = y * lax.rsqrt(jnp.mean(y * y, axis=-1, keepdims=True) + RMS_EPS) * norm_ref[...]
        o_ref[rows, :] = y

    ahead = PEER_SLOTS - 1
    groups = tb // PEER_SLOTS
    steady = groups - 1
    ids_window(0, 0).start()
    for t in range(ahead):
        issue(block_ids(t), t)
    wait(0)
    pick_weights(0, hn_ref, gate_ref, 0)

    def group(g, half):
        first = g * PEER_SLOTS if isinstance(g, int) else pl.multiple_of(g * PEER_SLOTS, PEER_SLOTS)
        rows = pl.ds(first, 2 * PEER_SLOTS)
        hn_rows_ref[...] = hn_ref[rows, :]
        gate_rows_ref[...] = gate_ref[rows, :]
        ids_window(g, half).wait()
        if isinstance(g, int):
            if g + 1 < steady:
                ids_window(g + 1, 1 - half).start()
        else:
            @pl.when(g + 1 < steady)
            def _():
                ids_window(g + 1, 1 - half).start()
        for s in range(PEER_SLOTS):
            wait((s + 1) % PEER_SLOTS)
            pick_weights((s + 1) % PEER_SLOTS, hn_rows_ref, gate_rows_ref, s + 1)
            combine(s)
            issue(lambda e, s=s: ids_ref[half, s + ahead, e], (s + ahead) % PEER_SLOTS)
        store_group(g)

    def group_pair(k, carry):
        group(2 * k, 0)
        group(2 * k + 1, 1)
        return carry

    lax.fori_loop(0, steady // 2, group_pair, 0)
    if steady % 2:
        group(steady - 1, 0)

    for s in range(PEER_SLOTS):
        t = (groups - 1) * PEER_SLOTS + s
        if t + ahead < tb:
            issue(block_ids(t + ahead), (s + ahead) % PEER_SLOTS)
        if t + 1 < tb:
            wait((s + 1) % PEER_SLOTS)
            pick_weights((s + 1) % PEER_SLOTS, hn_ref, gate_ref, t + 1)
        combine(s)
    store_group(groups - 1)


def peer_expert_ffn(eidx, gate, hn, x, table, tb=128, out_norm_g=None):
    n, d = x.shape
    tb = min(tb, n)
    assert tb % PEER_SLOTS == 0 and table.shape[1] == LANES
    assert table.shape[0] >= PEER_PICKS * (d // LANES)
    slot_rows = PEER_PICKS * (d // LANES + PITCH_PAD)
    out_norm = out_norm_g is not None
    in_specs = [pl.BlockSpec((tb, PEER_PICKS), lambda i: (i, 0), memory_space=pltpu.SMEM),
                pl.BlockSpec((tb, PEER_PICKS), lambda i: (i, 0)),
                pl.BlockSpec((tb, d), lambda i: (i, 0)),
                pl.BlockSpec((tb, d), lambda i: (i, 0)),
                pl.BlockSpec(memory_space=pl.ANY)]
    operands = [eidx, gate, hn, x, table]
    if out_norm:
        in_specs.append(pl.BlockSpec((1, d), lambda i: (0, 0)))
        operands.append(out_norm_g.reshape(1, d))
    return pl.pallas_call(
        functools.partial(_peer_ffn_body, tb=tb, d=d, out_norm=out_norm),
        grid=(n // tb,),
        in_specs=in_specs,
        out_specs=pl.BlockSpec((tb, d), lambda i: (i, 0)),
        out_shape=jax.ShapeDtypeStruct((n, d), F32),
        scratch_shapes=[pltpu.VMEM((slot_rows, LANES), jnp.uint32)] * PEER_SLOTS
                       + [pltpu.SemaphoreType.DMA((PEER_SLOTS,)),
                          pltpu.VMEM((PEER_SLOTS, PEER_PICKS, LANES), F32),
                          pltpu.VMEM((PEER_SLOTS, d), F32),
                          pltpu.VMEM((2 * PEER_SLOTS, d), F32),
                          pltpu.VMEM((2 * PEER_SLOTS, PEER_PICKS), F32),
                          pltpu.SMEM((2, 2 * PEER_SLOTS, PEER_PICKS), jnp.int32),
                          pltpu.SemaphoreType.DMA((2,))],
        compiler_params=_params("arbitrary"),
        name="peer_expert_ffn",
    )(*operands)


def peer_layer(x, norm_g, wq, keys, u_all, v_all, layer, tb=512, out_norm_g=None):
    q, hn = norm_matmul(x, norm_g, wq.astype(BF16), emit_hn=True)
    eidx, gate = peer_topk(q, keys)
    return peer_expert_ffn(eidx.T, gate.T, hn, x, _pack_expert_table(u_all, v_all, layer), tb=tb,
                           out_norm_g=out_norm_g)


def _band_bias(rel_bias):
    n_rel = rel_bias.shape[1]
    far = BAND - n_rel + CHUNK - 1
    long_row = jnp.concatenate([jnp.broadcast_to(rel_bias[:, n_rel - 1:], (rel_bias.shape[0], far)),
                                rel_bias[:, ::-1]], axis=1)
    rows = [long_row[:, CHUNK - 1 - i:CHUNK - 1 - i + BAND] for i in range(CHUNK)]
    return jnp.stack(rows, axis=1).astype(F32)


def kernel(x, mem, norm_mix, norm_ffn, norm_mem, w_mem_kv, w_out, peer_wq, peer_keys, peer_u, peer_v, a_mix, a_w_in, a_w0, a_w1, a_w2, a_a0, a_a1, a_a2, a_g1, a_g2, a_k_k, a_k_a, a_r_k, a_ln_g, a_ln_b, kv_norm, w_kv_shared, b_w_in, b_rel_bias, final_norm):
    bsz, seq, d = x.shape
    n = bsz * seq
    n_mem = mem.shape[1]
    seq_width = a_w0.shape[1]
    n_pairs = seq_width // PAIR
    mem_width = MEM_HEADS * MEM_HEAD_DIM
    x = x.reshape(n, d)
    mem2 = mem.reshape(bsz * n_mem, d)

    def mixer_tail(x, seq_out, proj, q_col_block, layer, out_norm_g=None):
        mkv = norm_matmul(mem2, norm_mem[layer], w_mem_kv[layer].astype(BF16), tm=256)
        mem_out = memory_attention(proj, q_col_block, mkv, seq, n_mem)
        wo = w_out[layer].astype(BF16)
        x = out_projection(seq_out, mem_out, wo[:seq_width], wo[seq_width:], x)
        return peer_layer(x, norm_ffn[layer], peer_wq[layer], peer_keys[layer], peer_u, peer_v, layer,
                          out_norm_g=out_norm_g)

    h = rmsnorm(x, norm_mix[0])
    mix_tab = jnp.concatenate([a_mix[0], jnp.zeros((1, d), F32)], axis=0)
    tiles = seq_width // 512
    gid_main = jnp.asarray([0] * tiles + [2] * tiles + [3] * tiles + [6] * (mem_width // 512), jnp.int32)
    proj = mix_matmul(h, mix_tab, gid_main, a_w_in[0].astype(BF16), seq, tn=512)
    rank = a_w1.shape[2]
    padc = lambda w: jnp.pad(w, ((0, 0), (0, LORA_PAD - rank)))
    padr = lambda w: jnp.pad(w, ((0, LORA_PAD - rank), (0, 0)))
    w_l1 = jnp.concatenate([padc(a_w1[0]), padc(a_a1[0]), a_g1[0]], axis=1).astype(BF16)
    gid_l1 = jnp.asarray([1, 4] + [5] * (a_g1.shape[2] // LORA_PAD), jnp.int32)
    t1 = mix_matmul(h, mix_tab, gid_l1, w_l1, seq, tn=LORA_PAD)
    lw, a_iclr, gate = lora_stage2(t1, padr(a_w2[0]).astype(BF16), padr(a_a2[0]).astype(BF16),
                                   a_g2[0].astype(BF16), a_w0[0], a_a0[0])
    prep = wkv_chunk_prepare(proj, lw, a_iclr, a_k_k[0], a_k_a[0], n_pairs)
    seq_out = wkv_chunk_scan(prep, proj, a_iclr, gate, a_k_a[0], a_r_k[0].reshape(-1),
                             a_ln_g[0], a_ln_b[0], n_pairs, seq)
    x = mixer_tail(x, seq_out, proj, (3 * seq_width) // mem_width, 0)

    kv = norm_matmul(x, kv_norm, w_kv_shared.astype(BF16), tm=1024)

    proj = norm_matmul(x, norm_mix[1], b_w_in[0].astype(BF16), tm=1024)
    seq_out = band_attention(proj, kv, _band_bias(b_rel_bias[0]), n_pairs, seq)
    x = mixer_tail(x, seq_out, proj, seq_width // mem_width, 1, out_norm_g=final_norm)
    return x.reshape(bsz, seq, d)
```

```python
import functools

import jax
import jax.numpy as jnp
from jax import lax
from jax.experimental import pallas as pl
from jax.experimental.pallas import tpu as pltpu

F32 = jnp.float32
BF16 = jnp.bfloat16

HEAD_DIM = 64
PAIR = 2 * HEAD_DIM
CHUNK = 64
SCAN_PAIRS = 2
LEFT_CHUNKS = 8
BAND = (LEFT_CHUNKS + 1) * CHUNK
MEM_HEADS = 4
MEM_HEAD_DIM = 128
PEER_KEYS = 128
PEER_HEADS = 8
PEER_TOPK = 16
PEER_PICKS = PEER_HEADS * PEER_TOPK
GN_EPS = 64e-5
RMS_EPS = 1e-6
LORA_PAD = 128
VMEM_LIMIT = 48 * 1024 * 1024


def _params(*sem):
    return pltpu.CompilerParams(dimension_semantics=sem, vmem_limit_bytes=VMEM_LIMIT)


def _dot(a, b):
    return jnp.dot(a, b, preferred_element_type=F32)


def _dot_nt(a, b, precision=None):
    return lax.dot_general(a, b, (((1,), (1,)), ((), ())), precision=precision,
                           preferred_element_type=F32)


def _dot_tn(a, b, precision=None):
    return lax.dot_general(a, b, (((0,), (0,)), ((), ())), precision=precision,
                           preferred_element_type=F32)


def _rmsnorm_body(x_ref, g_ref, o_ref):
    x = x_ref[...]
    ms = jnp.mean(x * x, axis=-1, keepdims=True)
    o_ref[...] = (x * lax.rsqrt(ms + RMS_EPS) * g_ref[...]).astype(o_ref.dtype)


def rmsnorm(x, g, tm=512):
    n, d = x.shape
    tm = min(tm, n)
    return pl.pallas_call(
        _rmsnorm_body,
        grid=(n // tm,),
        in_specs=[pl.BlockSpec((tm, d), lambda i: (i, 0)), pl.BlockSpec((1, d), lambda i: (0, 0))],
        out_specs=pl.BlockSpec((tm, d), lambda i: (i, 0)),
        out_shape=jax.ShapeDtypeStruct((n, d), F32),
        compiler_params=_params("parallel"),
        name="rmsnorm",
    )(x, g.reshape(1, d))


def _normmm_body(x_ref, g_ref, w_ref, o_ref, *rest, emit_hn):
    lhs_ref = rest[-1]

    @pl.when(pl.program_id(1) == 0)
    def _():
        x = x_ref[...]
        ms = jnp.mean(x * x, axis=-1, keepdims=True)
        hn = x * lax.rsqrt(ms + RMS_EPS) * g_ref[...]
        lhs_ref[...] = hn.astype(BF16)
        if emit_hn:
            rest[0][...] = hn

    o_ref[...] = _dot(lhs_ref[...], w_ref[...]).astype(o_ref.dtype)


def norm_matmul(x, g, w_bf16, emit_hn=False, tm=512, tn=1024, out_dtype=BF16):
    n, d = x.shape
    nc = w_bf16.shape[1]
    tm, tn = min(tm, n), min(tn, nc)
    out_shape = [jax.ShapeDtypeStruct((n, nc), out_dtype)]
    out_specs = [pl.BlockSpec((tm, tn), lambda i, j: (i, j))]
    if emit_hn:
        out_shape.append(jax.ShapeDtypeStruct((n, d), F32))
        out_specs.append(pl.BlockSpec((tm, d), lambda i, j: (i, 0)))
    res = pl.pallas_call(
        functools.partial(_normmm_body, emit_hn=emit_hn),
        grid=(n // tm, nc // tn),
        in_specs=[pl.BlockSpec((tm, d), lambda i, j: (i, 0)),
                  pl.BlockSpec((1, d), lambda i, j: (0, 0)),
                  pl.BlockSpec((d, tn), lambda i, j: (0, j))],
        out_specs=out_specs,
        out_shape=out_shape,
        scratch_shapes=[pltpu.VMEM((tm, d), BF16)],
        compiler_params=_params("parallel", "arbitrary"),
        name="norm_matmul",
    )(x, g.reshape(1, d), w_bf16)
    return res if emit_hn else res[0]


def _mixmm_body(gid_ref, h_ref, hp_ref, mix_ref, w_ref, o_ref, lhs_ref, diff_ref, *, tm, seq):
    i = pl.program_id(0)
    j = pl.program_id(1)

    @pl.when(j == 0)
    def _():
        h = h_ref[...]
        prev = jnp.where((i * tm) % seq == 0, 0.0, hp_ref[7:8, :])
        row = lax.broadcasted_iota(jnp.int32, h.shape, 0)
        diff_ref[...] = jnp.where(row == 0, prev, pltpu.roll(h, 1, 0)) - h

    new_group = jnp.logical_or(j == 0, gid_ref[j] != gid_ref[jnp.maximum(j - 1, 0)])

    @pl.when(new_group)
    def _():
        lhs_ref[...] = (h_ref[...] + diff_ref[...] * mix_ref[0]).astype(BF16)

    o_ref[...] = _dot(lhs_ref[...], w_ref[...])


def mix_matmul(h, mix_tab, gid, w_bf16, seq, tn, tm=1024):
    n, d = h.shape
    nc = w_bf16.shape[1]
    tm = min(tm, seq)
    sub = tm // 8
    grid_spec = pltpu.PrefetchScalarGridSpec(
        num_scalar_prefetch=1,
        grid=(n // tm, nc // tn),
        in_specs=[pl.BlockSpec((tm, d), lambda i, j, g: (i, 0)),
                  pl.BlockSpec((8, d), lambda i, j, g: (jnp.maximum(i * sub - 1, 0), 0)),
                  pl.BlockSpec((1, 1, d), lambda i, j, g: (g[j], 0, 0)),
                  pl.BlockSpec((d, tn), lambda i, j, g: (0, j))],
        out_specs=pl.BlockSpec((tm, tn), lambda i, j, g: (i, j)),
        scratch_shapes=[pltpu.VMEM((tm, d), BF16), pltpu.VMEM((tm, d), F32)],
    )
    return pl.pallas_call(
        functools.partial(_mixmm_body, tm=tm, seq=seq),
        grid_spec=grid_spec,
        out_shape=jax.ShapeDtypeStruct((n, nc), F32),
        compiler_params=_params("parallel", "arbitrary"),
        name="mix_matmul",
    )(gid, h, h, mix_tab.reshape(mix_tab.shape[0], 1, d), w_bf16)


def _sigmoid(x):
    return 1.0 / (1.0 + jnp.exp(-x))


def _lora2_body(t_ref, w2_ref, a2_ref, g2_ref, w0_ref, a0_ref, lw_ref, a_ref, g_ref):
    t = t_ref[...]
    tw = jnp.tanh(t[:, :LORA_PAD]).astype(BF16)
    ta = t[:, LORA_PAD:2 * LORA_PAD].astype(BF16)
    tg = _sigmoid(t[:, 2 * LORA_PAD:]).astype(BF16)
    u = w0_ref[...] + _dot(tw, w2_ref[...])
    softplus_neg_u = jnp.maximum(-u, 0.0) + jnp.log(1.0 + jnp.exp(-jnp.abs(u)))
    lw_ref[...] = -jnp.exp(-softplus_neg_u - 0.5)
    a_ref[...] = _sigmoid(a0_ref[...] + _dot(ta, a2_ref[...]))
    g_ref[...] = _dot(tg, g2_ref[...])


def lora_stage2(t1, w2p, a2p, g2, w0, a0, tm=256):
    n = t1.shape[0]
    width = w2p.shape[1]
    tm = min(tm, n)
    full = lambda a: pl.BlockSpec(a.shape, lambda i: (0, 0))
    row = pl.BlockSpec((tm, width), lambda i: (i, 0))
    w0 = w0.reshape(1, width)
    a0 = a0.reshape(1, width)
    return pl.pallas_call(
        _lora2_body,
        grid=(n // tm,),
        in_specs=[pl.BlockSpec((tm, t1.shape[1]), lambda i: (i, 0)), full(w2p), full(a2p), full(g2),
                  full(w0), full(a0)],
        out_specs=[row, row, row],
        out_shape=[jax.ShapeDtypeStruct((n, width), F32)] * 3,
        compiler_params=_params("parallel"),
        name="lora_stage2",
    )(t1, w2p, a2p, g2, w0, a0)


def _head_group_sum(x):
    width = x.shape[-1]
    r = lax.broadcasted_iota(jnp.int32, (width, width), 0) // HEAD_DIM
    c = lax.broadcasted_iota(jnp.int32, (width, width), 1) // HEAD_DIM
    ones = jnp.where(r == c, 1.0, 0.0).astype(BF16)
    hi = x.astype(BF16)
    rest = x - hi.astype(F32)
    mid = rest.astype(BF16)
    lo = (rest - mid.astype(F32)).astype(BF16)
    return _dot(hi, ones) + _dot(mid, ones) + _dot(lo, ones)


def _wkv_prepare_body(r_ref, k_ref, v_ref, lw_ref, a_ref, kk_ref, ka_ref,
                      g_ref, s0c_ref, rp_ref, y0_ref, *, tb):
    r = r_ref[...]
    k = k_ref[...]
    v = v_ref[...]
    lw = lw_ref[...]
    a = a_ref[...]
    kk = k * kk_ref[...]
    norm = jnp.sqrt(_head_group_sum(kk * kk))
    kk = kk / jnp.maximum(norm, 1e-12)
    kmod = k * (1.0 + (a - 1.0) * ka_ref[...])
    avec = -kk
    bvec = kk * a

    tr = lax.broadcasted_iota(jnp.int32, (CHUNK, CHUNK), 0)
    tc = lax.broadcasted_iota(jnp.int32, (CHUNK, CHUNK), 1)
    tri = jnp.where(tr >= tc, 1.0, 0.0).astype(BF16)
    lw_hi = lw.astype(BF16)
    lw_rest = lw - lw_hi.astype(F32)
    lw_mid = lw_rest.astype(BF16)
    lw_lo = (lw_rest - lw_mid.astype(F32)).astype(BF16)

    lane = lax.broadcasted_iota(jnp.int32, (CHUNK, PAIR), 1)
    head0 = lane < HEAD_DIM
    row = lax.broadcasted_iota(jnp.int32, (PAIR, PAIR), 0)
    col = lax.broadcasted_iota(jnp.int32, (PAIR, PAIR), 1)
    strict = row > col
    lower = row >= col
    eye = jnp.where(row == col, 1.0, 0.0).astype(F32)

    def stack(x):
        return jnp.concatenate([jnp.where(head0, x, 0.0), jnp.where(head0, 0.0, x)], axis=0)

    chunks = range(tb // CHUNK)
    pre = []
    for c in chunks:
        sl = slice(c * CHUNK, (c + 1) * CHUNK)
        cm = _dot(tri, lw_hi[sl]) + _dot(tri, lw_mid[sl]) + _dot(tri, lw_lo[sl])
        cend = cm[CHUNK - 1:CHUNK]
        e_in = jnp.exp(cm)
        e_out = jnp.exp(-cm)
        e_tail = jnp.exp(cend - cm)
        a_s = stack(avec[sl] * jnp.exp(cm - lw[sl]))
        r_s = stack(r[sl] * e_in)
        b_s = stack(bvec[sl] * e_out)
        k_s = stack(kmod[sl] * e_out)
        bt_s = stack(bvec[sl] * e_tail)
        kt_s = stack(kmod[sl] * e_tail)
        v_s = stack(v[sl])
        p = _dot_nt(jnp.concatenate([a_s, r_s], axis=0).astype(BF16),
                    jnp.concatenate([b_s, k_s], axis=0).astype(BF16))
        pre.append(dict(cend=cend, a_s=a_s, r_s=r_s, bt_s=bt_s, kt_s=kt_s, v_s=v_s,
                        l_ab=jnp.where(strict, p[:PAIR, :PAIR], 0.0),
                        l_ak=jnp.where(strict, p[:PAIR, PAIR:], 0.0),
                        a_rb=jnp.where(lower, p[PAIR:, :PAIR], 0.0),
                        a_rk=jnp.where(lower, p[PAIR:, PAIR:], 0.0)))

    ms = [q["l_ab"] for q in pre]
    ts = [eye + m for m in ms]
    ms = [_dot(m.astype(BF16), m.astype(BF16)) for m in ms]
    for _ in range(4):
        both = [_dot(m.astype(BF16), jnp.concatenate([t, m], axis=1).astype(BF16)) for t, m in zip(ts, ms)]
        ts = [t + b[:, :PAIR] for t, b in zip(ts, both)]
        ms = [b[:, PAIR:] for b in both]
    ts = [t + _dot(m.astype(BF16), t.astype(BF16)) for t, m in zip(ts, ms)]

    lvs = [_dot(q["l_ak"].astype(BF16), q["v_s"].astype(BF16)) for q in pre]
    aus = [_dot(t.astype(BF16), jnp.concatenate([q["a_s"], lv], axis=1).astype(BF16))
           for t, q, lv in zip(ts, pre, lvs)]
    zero = jnp.zeros((PAIR, PAIR), F32)
    xs = [_dot(jnp.concatenate([q["a_rb"], q["a_rk"]], axis=1).astype(BF16),
               jnp.concatenate([au, jnp.concatenate([zero, q["v_s"]], axis=1)], axis=0).astype(BF16))
          for q, au in zip(pre, aus)]
    for c, q, au, x in zip(chunks, pre, aus, xs):
        rp_ref[0, c] = (q["r_s"] + x[:, :PAIR]).astype(rp_ref.dtype)
        y0_ref[0, c] = x[:, PAIR:]
        ap = au[:, :PAIR]
        u0 = au[:, PAIR:]
        g_ref[0, c] = (eye * jnp.exp(q["cend"])
                       + _dot_tn(ap.astype(BF16), q["bt_s"].astype(BF16))).astype(g_ref.dtype)
        s0c_ref[0, c] = _dot_tn(jnp.concatenate([u0, q["v_s"]], axis=0).astype(BF16),
                                jnp.concatenate([q["bt_s"], q["kt_s"]], axis=0).astype(BF16))


def wkv_chunk_prepare(proj, lw, a, k_k, k_a, n_pairs, tb=1024):
    n = proj.shape[0]
    tb = min(tb, n)
    cpb = tb // CHUNK
    col = lambda off: pl.BlockSpec((tb, PAIR), lambda i, p: (i, off + p))
    par = pl.BlockSpec((1, PAIR), lambda i, p: (0, p))
    blk = pl.BlockSpec((1, cpb, PAIR, PAIR), lambda i, p: (p, i, 0, 0))
    shp = lambda dt: jax.ShapeDtypeStruct((n_pairs, n // CHUNK, PAIR, PAIR), dt)
    return pl.pallas_call(
        functools.partial(_wkv_prepare_body, tb=tb),
        grid=(n // tb, n_pairs),
        in_specs=[col(0), col(n_pairs), col(2 * n_pairs), col(0), col(0), par, par],
        out_specs=[blk] * 4,
        out_shape=[shp(BF16), shp(F32), shp(BF16), shp(F32)],
        compiler_params=_params("parallel", "parallel"),
        name="wkv_chunk_prepare",
    )(proj, proj, proj, lw, a, k_k.reshape(1, -1), k_a.reshape(1, -1))


def _wkv_scan_body(g_ref, s0c_ref, rp_ref, y0_ref, r_ref, k_ref, v_ref, a_ref, gate_ref,
                   ka_ref, rk_ref, lng_ref, lnb_ref, o_ref, s_ref, y_ref, *, cpb):
    @pl.when(pl.program_id(2) == 0)
    def _():
        s_ref[...] = jnp.zeros_like(s_ref)

    pairs = range(SCAN_PAIRS)
    states = [[s_ref[w].astype(BF16)] for w in pairs]
    for c in range(cpb):
        for w in pairs:
            s = _dot(states[w][c], g_ref[w, c].astype(BF16)) + s0c_ref[w, c]
            if c + 1 < cpb:
                states[w].append(s.astype(BF16))
            else:
                s_ref[w] = s
    for c in range(cpb):
        for w in pairs:
            y_st = _dot_nt(rp_ref[w, c].astype(BF16), states[w][c]) + y0_ref[w, c]
            y_ref[c * CHUNK:(c + 1) * CHUNK, w * PAIR:(w + 1) * PAIR] = y_st[:CHUNK] + y_st[CHUNK:]

    y = y_ref[...]
    inv = 1.0 / HEAD_DIM
    mu = _head_group_sum(y) * inv
    yc = y - mu
    var = _head_group_sum(yc * yc) * inv
    yn = yc * lax.rsqrt(var + GN_EPS) * lng_ref[...] + lnb_ref[...]
    r = r_ref[...]
    kmod = k_ref[...] * (1.0 + (a_ref[...] - 1.0) * ka_ref[...])
    bonus = _head_group_sum(r * kmod * rk_ref[...]) * v_ref[...]
    o_ref[...] = ((yn + bonus) * gate_ref[...]).astype(o_ref.dtype)


def wkv_chunk_scan(prep, proj, a, gate, k_a, r_k, ln_g, ln_b, n_pairs, seq, tb=1024):
    g_all, s0c_all, rp_all, y0_all = prep
    n = proj.shape[0]
    tb = min(tb, seq)
    cpb = tb // CHUNK
    nblk = seq // tb
    assert n_pairs % SCAN_PAIRS == 0
    width = SCAN_PAIRS * PAIR
    groups = n_pairs // SCAN_PAIRS
    blk = pl.BlockSpec((SCAN_PAIRS, cpb, PAIR, PAIR), lambda b, p, t: (p, b * nblk + t, 0, 0))
    col = lambda off: pl.BlockSpec((tb, width), lambda b, p, t: (b * nblk + t, off + p))
    par = pl.BlockSpec((1, width), lambda b, p, t: (0, p))
    vec = lambda z: z.reshape(1, -1)
    return pl.pallas_call(
        functools.partial(_wkv_scan_body, cpb=cpb),
        grid=(n // seq, groups, nblk),
        in_specs=[blk] * 4 + [col(0), col(groups), col(2 * groups), col(0), col(0)] + [par] * 4,
        out_specs=col(0),
        out_shape=jax.ShapeDtypeStruct((n, n_pairs * PAIR), BF16),
        scratch_shapes=[pltpu.VMEM((SCAN_PAIRS, PAIR, PAIR), F32), pltpu.VMEM((tb, width), F32)],
        compiler_params=_params("parallel", "parallel", "arbitrary"),
        name="wkv_chunk_scan",
    )(g_all, s0c_all, rp_all, y0_all, proj, proj, proj, a, gate,
      vec(k_a), vec(r_k), vec(ln_g), vec(ln_b))


def _memattn_body(q_ref, m_ref, o_ref):
    width = MEM_HEADS * MEM_HEAD_DIM
    scale = MEM_HEAD_DIM ** -0.5
    cols = [slice(h * MEM_HEAD_DIM, (h + 1) * MEM_HEAD_DIM) for h in range(MEM_HEADS)]
    scores = [_dot_nt(q_ref[:, sl].astype(BF16), m_ref[:, sl].astype(BF16)) * scale for sl in cols]
    exps = [jnp.exp(s - jnp.max(s, axis=-1, keepdims=True)) for s in scores]
    outs = [_dot(e.astype(BF16), m_ref[:, width + sl.start:width + sl.stop].astype(BF16))
            for sl, e in zip(cols, exps)]
    for sl, e, o in zip(cols, exps, outs):
        o_ref[:, sl] = (o / jnp.sum(e, axis=-1, keepdims=True)).astype(o_ref.dtype)


def memory_attention(proj, q_col_block, mkv, seq, n_mem, tm=512):
    n = proj.shape[0]
    width = MEM_HEADS * MEM_HEAD_DIM
    tm = min(tm, seq)
    nblk = seq // tm
    return pl.pallas_call(
        _memattn_body,
        grid=(n // seq, nblk),
        in_specs=[pl.BlockSpec((tm, width), lambda b, t: (b * nblk + t, q_col_block)),
                  pl.BlockSpec((n_mem, 2 * width), lambda b, t: (b, 0))],
        out_specs=pl.BlockSpec((tm, width), lambda b, t: (b * nblk + t, 0)),
        out_shape=jax.ShapeDtypeStruct((n, width), BF16),
        compiler_params=_params("parallel", "parallel"),
        name="memory_attention",
    )(proj, mkv)


def _outproj_body(s_ref, q_ref, mkv_ref, w1_ref, w2_ref, x_ref, o_ref, mem_ref):
    @pl.when(pl.program_id(1) == 0)
    def _():
        _memattn_body(q_ref, mkv_ref, mem_ref)

    o_ref[...] = x_ref[...] + _dot(s_ref[...], w1_ref[...]) + _dot(mem_ref[...], w2_ref[...])


def out_projection(seq_out, proj, q_col_block, mkv, w_seq, w_mem, x, seq, n_mem, tm=512, tn=1024):
    n, d = x.shape
    tm, tn = min(tm, seq), min(tn, d)
    ws, wm = seq_out.shape[1], w_mem.shape[0]
    per_seq = seq // tm
    return pl.pallas_call(
        _outproj_body,
        grid=(n // tm, d // tn),
        in_specs=[pl.BlockSpec((tm, ws), lambda i, j: (i, 0)),
                  pl.BlockSpec((tm, wm), lambda i, j: (i, q_col_block)),
                  pl.BlockSpec((n_mem, 2 * wm), lambda i, j: (i // per_seq, 0)),
                  pl.BlockSpec((ws, tn), lambda i, j: (0, j)),
                  pl.BlockSpec((wm, tn), lambda i, j: (0, j)),
                  pl.BlockSpec((tm, tn), lambda i, j: (i, j))],
        out_specs=pl.BlockSpec((tm, tn), lambda i, j: (i, j)),
        out_shape=jax.ShapeDtypeStruct((n, d), F32),
        scratch_shapes=[pltpu.VMEM((tm, wm), BF16)],
        compiler_params=_params("parallel", "arbitrary"),
        name="out_projection",
    )(seq_out, proj, mkv, w_seq, w_mem, x)


def _bandattn_body(q_ref, kp_ref, kc_ref, vp_ref, vc_ref, bias_ref, o_ref, *, tq):
    qi = pl.program_id(2)
    scale = HEAD_DIM ** -0.5
    k_all = jnp.concatenate([kp_ref[...], kc_ref[...]], axis=0).astype(BF16)
    v_all = jnp.concatenate([vp_ref[...], vc_ref[...]], axis=0).astype(BF16)
    lane = lax.broadcasted_iota(jnp.int32, (CHUNK, PAIR), 1)
    head0 = lane < HEAD_DIM
    kcol = lax.broadcasted_iota(jnp.int32, (2 * CHUNK, BAND), 1)
    pad = LEFT_CHUNKS * CHUNK
    bias = bias_ref[...]
    chunks = range(tq // CHUNK)
    starts = [tq - pad + j * CHUNK for j in chunks]
    scores = []
    for j, start in zip(chunks, starts):
        q = q_ref[j * CHUNK:(j + 1) * CHUNK, :] * scale
        q2 = jnp.concatenate([jnp.where(head0, q, 0.0), jnp.where(head0, 0.0, q)], axis=0).astype(BF16)
        scores.append(_dot_nt(q2, k_all[start:start + BAND]))
    exps = []
    for start, s in zip(starts, scores):
        valid = jnp.logical_or(qi > 0, kcol + start >= tq)
        s = jnp.where(valid, s + bias, -jnp.inf)
        exps.append(jnp.exp(s - jnp.max(s, axis=-1, keepdims=True)))
    outs = [_dot(e.astype(BF16), v_all[start:start + BAND]) for start, e in zip(starts, exps)]
    for j, e, o in zip(chunks, exps, outs):
        o = o / jnp.sum(e, axis=-1, keepdims=True)
        o_ref[j * CHUNK:(j + 1) * CHUNK, :] = jnp.where(head0, o[:CHUNK], o[CHUNK:]).astype(o_ref.dtype)


def band_attention(proj, kv, bias, n_pairs, seq, tq=512):
    n = proj.shape[0]
    tq = min(tq, seq)
    assert tq >= LEFT_CHUNKS * CHUNK
    nblk = seq // tq
    cur = lambda off: pl.BlockSpec((tq, PAIR), lambda b, p, t: (b * nblk + t, off + p))
    prev = lambda off: pl.BlockSpec((tq, PAIR), lambda b, p, t: (b * nblk + jnp.maximum(t - 1, 0), off + p))
    return pl.pallas_call(
        functools.partial(_bandattn_body, tq=tq),
        grid=(n // seq, n_pairs, nblk),
        in_specs=[cur(0), prev(0), cur(0), prev(n_pairs), cur(n_pairs),
                  pl.BlockSpec((2 * CHUNK, BAND), lambda b, p, t: (p, 0))],
        out_specs=cur(0),
        out_shape=jax.ShapeDtypeStruct((n, n_pairs * PAIR), BF16),
        compiler_params=_params("parallel", "parallel", "parallel"),
        name="band_attention",
    )(proj, kv, kv, kv, kv, bias.reshape(-1, BAND))


def _top16(scores, payloads):
    nl = scores[0].shape[1]
    rids = [lax.broadcasted_iota(jnp.int32, s.shape, 0).astype(F32) for s in scores]
    slot = lax.broadcasted_iota(jnp.int32, (PEER_TOPK, nl), 0)
    tile = 8

    def winner(s, rid, payload):
        items = [(s[g:g + tile], rid[g:g + tile], None if payload is None else payload[g:g + tile])
                 for g in range(0, s.shape[0], tile)]
        while len(items) > 1:
            nxt = []
            for k in range(0, len(items) - 1, 2):
                (va, ia, pa), (vb, ib, pb) = items[k], items[k + 1]
                keep = va >= vb
                nxt.append((jnp.maximum(va, vb), jnp.where(keep, ia, ib),
                            None if pa is None else jnp.where(keep, pa, pb)))
            if len(items) % 2:
                nxt.append(items[-1])
            items = nxt
        v, idx, p = items[0]
        for shift in (4, 2, 1):
            pv, pi = pltpu.roll(v, shift, 0), pltpu.roll(idx, shift, 0)
            take = (pv > v) | ((pv == v) & (pi < idx))
            if p is not None:
                p = jnp.where(take, pltpu.roll(p, shift, 0), p)
            v, idx = jnp.where(take, pv, v), jnp.where(take, pi, idx)
        return v, idx, p

    def body(i, carry):
        sel = slot == i
        out = []
        for (s, vals, picks), rid, payload in zip(carry, rids, payloads):
            m, am, p = winner(s, rid, payload)
            hit = rid == jnp.concatenate([am] * (s.shape[0] // tile), axis=0)
            pick = am if payload is None else p
            out.append((jnp.where(hit, -jnp.inf, s), jnp.where(sel, m[0:1], vals), jnp.where(sel, pick[0:1], picks)))
        return tuple(out)

    zero = jnp.zeros((PEER_TOPK, nl), F32)
    res = lax.fori_loop(0, PEER_TOPK, body, tuple((s, zero, zero) for s in scores))
    return [(vals, picks) for _, vals, picks in res]


def _pair_candidates(a, b, combine):
    half = PEER_TOPK // 2
    rows = [combine(a[0:1], b)]
    rows += [combine(a[i:i + 1], b[:half]) for i in range(1, half)]
    rows.append(combine(a[half:], b[0:1]))
    return jnp.concatenate(rows, axis=0)


TOPK_HEADS = 2


def _peer_topk_body(q_ref, keys_ref, eidx_ref, gate_ref):
    q = q_ref[...].astype(BF16)
    scores = []
    for h in range(TOPK_HEADS):
        for half in range(2):
            col = (2 * h + half) * PEER_KEYS
            scores.append(_dot_nt(keys_ref[h, half].astype(BF16), q[:, col:col + PEER_KEYS]))
    cands, cidxs = [], []
    for h in range(TOPK_HEADS):
        (a, i1), (b, i2) = _top16(scores[2 * h:2 * h + 2], [None, None])
        cands.append(_pair_candidates(a, b, lambda x, y: x + y))
        cidxs.append(_pair_candidates(i1, i2, lambda x, y: x * PEER_KEYS + y))
    for h, (top, eidx) in enumerate(_top16(cands, cidxs)):
        rows = slice(h * PEER_TOPK, (h + 1) * PEER_TOPK)
        e = jnp.exp(top - top[0:1])
        gate_ref[rows, :] = e / jnp.sum(e, axis=0, keepdims=True)
        eidx_ref[rows, :] = eidx.astype(jnp.int32)


def peer_topk(q, keys, tl=128):
    n = q.shape[0]
    tl = min(tl, n)
    blk = pl.BlockSpec((TOPK_HEADS * PEER_TOPK, tl), lambda i, h: (h, i))
    return pl.pallas_call(
        _peer_topk_body,
        grid=(n // tl, PEER_HEADS // TOPK_HEADS),
        in_specs=[pl.BlockSpec((tl, TOPK_HEADS * 2 * PEER_KEYS), lambda i, h: (i, h)),
                  pl.BlockSpec((TOPK_HEADS, 2, PEER_KEYS, PEER_KEYS), lambda i, h: (h, 0, 0, 0))],
        out_specs=[blk, blk],
        out_shape=[jax.ShapeDtypeStruct((PEER_PICKS, n), jnp.int32),
                   jax.ShapeDtypeStruct((PEER_PICKS, n), F32)],
        compiler_params=_params("parallel", "parallel"),
        name="peer_topk",
    )(q, keys)


PEER_SLOTS = 8
LANES = 128
PITCH_PAD = 4


def _pack_body(u_ref, v_ref, o_ref):
    hi = lax.bitcast_convert_type(u_ref[0].astype(BF16).astype(F32), jnp.uint32)
    lo = lax.bitcast_convert_type(v_ref[0].astype(BF16).astype(F32), jnp.uint32)
    words = hi | (lo >> 16)
    for c in range(o_ref.shape[1]):
        o_ref[:, c, :] = words[:, c * LANES:(c + 1) * LANES]


def _pack_expert_table(u, v, layer, te=256):
    _, n_exp, d = u.shape
    chunks = d // LANES
    te = min(te, n_exp)
    packed = pl.pallas_call(
        _pack_body,
        grid=(n_exp // te,),
        in_specs=[pl.BlockSpec((1, te, d), lambda i: (layer, i, 0))] * 2,
        out_specs=pl.BlockSpec((te, chunks, LANES), lambda i: (i, 0, 0)),
        out_shape=jax.ShapeDtypeStruct((n_exp, chunks, LANES), jnp.uint32),
        compiler_params=_params("parallel"),
        name="pack_expert_table",
    )(u, v)
    return packed.reshape(n_exp * chunks, LANES)


def _peer_ffn_body(eidx_ref, gate_ref, hn_ref, x_ref, tab_ref, *rest, tb, d, out_norm):
    norm_ref, o_ref, scratch = (rest[0], rest[1], rest[2:]) if out_norm else (None, rest[0], rest[1:])
    rows_refs = scratch[:PEER_SLOTS]
    sem_ref, w_ref, stage_ref, hn_rows_ref, gate_rows_ref, ids_ref, ids_sem_ref = scratch[PEER_SLOTS:]
    chunks = d // LANES
    pitch = chunks + PITCH_PAD

    def issue(ids, slot):
        for e in range(PEER_PICKS):
            src = tab_ref.at[pl.ds(pl.multiple_of(ids(e) * chunks, chunks), chunks), :]
            dst = rows_refs[slot].at[pl.ds(e * pitch, chunks), :]
            pltpu.make_async_copy(src, dst, sem_ref.at[slot]).start(priority=e % 2)

    def block_ids(t):
        return lambda e: eidx_ref[t, e]

    def ids_window(g, half):
        rows = g * PEER_SLOTS if isinstance(g, int) else pl.multiple_of(g * PEER_SLOTS, PEER_SLOTS)
        return pltpu.make_async_copy(eidx_ref.at[pl.ds(rows, 2 * PEER_SLOTS), :], ids_ref.at[half],
                                     ids_sem_ref.at[half])

    def wait(slot):
        total = PEER_PICKS * chunks
        pltpu.make_async_copy(tab_ref.at[pl.ds(0, total), :],
                              rows_refs[slot].at[pl.ds(0, total), :], sem_ref.at[slot]).wait()

    pick_diag = (lax.broadcasted_iota(jnp.int32, (PEER_PICKS, PEER_PICKS), 0)
                 == lax.broadcasted_iota(jnp.int32, (PEER_PICKS, PEER_PICKS), 1))
    hi_mask = jnp.uint32(0xFFFF0000)

    def words(slot, c):
        return rows_refs[slot][pl.ds(c, PEER_PICKS, stride=pitch), :]

    def pick_weights(slot, hn_rows, gate_rows, row):
        acc = jnp.zeros((PEER_PICKS, LANES), F32)
        for c in range(chunks):
            u = lax.bitcast_convert_type(words(slot, c) & hi_mask, F32)
            acc = acc + u * hn_rows[row:row + 1, c * LANES:(c + 1) * LANES]
        act = jnp.sum(acc, axis=1, keepdims=True)
        act = 0.5 * act * (1.0 + lax.erf(act * (2.0 ** -0.5)))
        gate = jnp.sum(jnp.where(pick_diag, gate_rows[row:row + 1, :], 0.0), axis=1, keepdims=True)
        w_ref[slot] = jnp.broadcast_to(gate * act, (PEER_PICKS, LANES))

    def combine(slot):
        w = w_ref[slot]
        for c in range(chunks):
            v = lax.bitcast_convert_type(words(slot, c) << 16, F32)
            stage_ref[slot:slot + 1, c * LANES:(c + 1) * LANES] = jnp.sum(v * w, axis=0, keepdims=True)

    def store_group(g):
        first = g * PEER_SLOTS if isinstance(g, int) else pl.multiple_of(g * PEER_SLOTS, PEER_SLOTS)
        rows = pl.ds(first, PEER_SLOTS)
        y = x_ref[rows, :] + stage_ref[...]
        if norm_ref is not None:
            y = y * lax.rsqrt(jnp.mean(y * y, axis=-1, keepdims=True) + RMS_EPS) * norm_ref[...]
        o_ref[rows, :] = y

    ahead = PEER_SLOTS - 1
    groups = tb // PEER_SLOTS
    steady = groups - 1
    ids_window(0, 0).start()
    for t in range(ahead):
        issue(block_ids(t), t)
    wait(0)
    pick_weights(0, hn_ref, gate_ref, 0)

    def group(g, half):
        first = g * PEER_SLOTS if isinstance(g, int) else pl.multiple_of(g * PEER_SLOTS, PEER_SLOTS)
        rows = pl.ds(first, 2 * PEER_SLOTS)
        hn_rows_ref[...] = hn_ref[rows, :]
        gate_rows_ref[...] = gate_ref[rows, :]
        ids_window(g, half).wait()
        if isinstance(g, int):
            if g + 1 < steady:
                ids_window(g + 1, 1 - half).start()
        else:
            @pl.when(g + 1 < steady)
            def _():
                ids_window(g + 1, 1 - half).start()
        for s in range(PEER_SLOTS):
            wait((s + 1) % PEER_SLOTS)
            pick_weights((s + 1) % PEER_SLOTS, hn_rows_ref, gate_rows_ref, s + 1)
            combine(s)
            issue(lambda e, s=s: ids_ref[half, s + ahead, e], (s + ahead) % PEER_SLOTS)
        store_group(g)

    def group_pair(k, carry):
        group(2 * k, 0)
        group(2 * k + 1, 1)
        return carry

    lax.fori_loop(0, steady // 2, group_pair, 0)
    if steady % 2:
        group(steady - 1, 0)

    for s in range(PEER_SLOTS):
        t = (groups - 1) * PEER_SLOTS + s
        if t + ahead < tb:
            issue(block_ids(t + ahead), (s + ahead) % PEER_SLOTS)
        if t + 1 < tb:
            wait((s + 1) % PEER_SLOTS)
            pick_weights((s + 1) % PEER_SLOTS, hn_ref, gate_ref, t + 1)
        combine(s)
    store_group(groups - 1)


def peer_expert_ffn(eidx, gate, hn, x, table, tb=128, out_norm_g=None):
    n, d = x.shape
    tb = min(tb, n)
    assert tb % PEER_SLOTS == 0 and table.shape[1] == LANES
    assert table.shape[0] >= PEER_PICKS * (d // LANES)
    slot_rows = PEER_PICKS * (d // LANES + PITCH_PAD)
    out_norm = out_norm_g is not None
    in_specs = [pl.BlockSpec((tb, PEER_PICKS), lambda i: (i, 0), memory_space=pltpu.SMEM),
                pl.BlockSpec((tb, PEER_PICKS), lambda i: (i, 0)),
                pl.BlockSpec((tb, d), lambda i: (i, 0)),
                pl.BlockSpec((tb, d), lambda i: (i, 0)),
                pl.BlockSpec(memory_space=pl.ANY)]
    operands = [eidx, gate, hn, x, table]
    if out_norm:
        in_specs.append(pl.BlockSpec((1, d), lambda i: (0, 0)))
        operands.append(out_norm_g.reshape(1, d))
    return pl.pallas_call(
        functools.partial(_peer_ffn_body, tb=tb, d=d, out_norm=out_norm),
        grid=(n // tb,),
        in_specs=in_specs,
        out_specs=pl.BlockSpec((tb, d), lambda i: (i, 0)),
        out_shape=jax.ShapeDtypeStruct((n, d), F32),
        scratch_shapes=[pltpu.VMEM((slot_rows, LANES), jnp.uint32)] * PEER_SLOTS
                       + [pltpu.SemaphoreType.DMA((PEER_SLOTS,)),
                          pltpu.VMEM((PEER_SLOTS, PEER_PICKS, LANES), F32),
                          pltpu.VMEM((PEER_SLOTS, d), F32),
                          pltpu.VMEM((2 * PEER_SLOTS, d), F32),
                          pltpu.VMEM((2 * PEER_SLOTS, PEER_PICKS), F32),
                          pltpu.SMEM((2, 2 * PEER_SLOTS, PEER_PICKS), jnp.int32),
                          pltpu.SemaphoreType.DMA((2,))],
        compiler_params=_params("arbitrary"),
        name="peer_expert_ffn",
    )(*operands)


def peer_layer(x, norm_g, wq, keys, u_all, v_all, layer, tb=512, out_norm_g=None):
    q, hn = norm_matmul(x, norm_g, wq.astype(BF16), emit_hn=True)
    eidx, gate = peer_topk(q, keys)
    return peer_expert_ffn(eidx.T, gate.T, hn, x, _pack_expert_table(u_all, v_all, layer), tb=tb,
                           out_norm_g=out_norm_g)


def _band_bias(rel_bias):
    n_rel = rel_bias.shape[1]
    far = BAND - n_rel + CHUNK - 1
    long_row = jnp.concatenate([jnp.broadcast_to(rel_bias[:, n_rel - 1:], (rel_bias.shape[0], far)),
                                rel_bias[:, ::-1]], axis=1)
    rows = [long_row[:, CHUNK - 1 - i:CHUNK - 1 - i + BAND] for i in range(CHUNK)]
    return jnp.stack(rows, axis=1).astype(F32)


def kernel(x, mem, norm_mix, norm_ffn, norm_mem, w_mem_kv, w_out, peer_wq, peer_keys, peer_u, peer_v, a_mix, a_w_in, a_w0, a_w1, a_w2, a_a0, a_a1, a_a2, a_g1, a_g2, a_k_k, a_k_a, a_r_k, a_ln_g, a_ln_b, kv_norm, w_kv_shared, b_w_in, b_rel_bias, final_norm):
    bsz, seq, d = x.shape
    n = bsz * seq
    n_mem = mem.shape[1]
    seq_width = a_w0.shape[1]
    n_pairs = seq_width // PAIR
    mem_width = MEM_HEADS * MEM_HEAD_DIM
    x = x.reshape(n, d)
    mem2 = mem.reshape(bsz * n_mem, d)

    def mixer_tail(x, seq_out, proj, q_col_block, layer, out_norm_g=None):
        mkv = norm_matmul(mem2, norm_mem[layer], w_mem_kv[layer].astype(BF16), tm=256)
        wo = w_out[layer].astype(BF16)
        x = out_projection(seq_out, proj, q_col_block, mkv, wo[:seq_width], wo[seq_width:], x, seq, n_mem)
        return peer_layer(x, norm_ffn[layer], peer_wq[layer], peer_keys[layer], peer_u, peer_v, layer,
                          out_norm_g=out_norm_g)

    h = rmsnorm(x, norm_mix[0])
    mix_tab = jnp.concatenate([a_mix[0], jnp.zeros((1, d), F32)], axis=0)
    tiles = seq_width // 512
    gid_main = jnp.asarray([0] * tiles + [2] * tiles + [3] * tiles + [6] * (mem_width // 512), jnp.int32)
    proj = mix_matmul(h, mix_tab, gid_main, a_w_in[0].astype(BF16), seq, tn=512)
    rank = a_w1.shape[2]
    padc = lambda w: jnp.pad(w, ((0, 0), (0, LORA_PAD - rank)))
    padr = lambda w: jnp.pad(w, ((0, LORA_PAD - rank), (0, 0)))
    w_l1 = jnp.concatenate([padc(a_w1[0]), padc(a_a1[0]), a_g1[0]], axis=1).astype(BF16)
    gid_l1 = jnp.asarray([1, 4] + [5] * (a_g1.shape[2] // LORA_PAD), jnp.int32)
    t1 = mix_matmul(h, mix_tab, gid_l1, w_l1, seq, tn=LORA_PAD)
    lw, a_iclr, gate = lora_stage2(t1, padr(a_w2[0]).astype(BF16), padr(a_a2[0]).astype(BF16),
                                   a_g2[0].astype(BF16), a_w0[0], a_a0[0])
    prep = wkv_chunk_prepare(proj, lw, a_iclr, a_k_k[0], a_k_a[0], n_pairs)
    seq_out = wkv_chunk_scan(prep, proj, a_iclr, gate, a_k_a[0], a_r_k[0].reshape(-1),
                             a_ln_g[0], a_ln_b[0], n_pairs, seq)
    x = mixer_tail(x, seq_out, proj, (3 * seq_width) // mem_width, 0)

    kv = norm_matmul(x, kv_norm, w_kv_shared.astype(BF16), tm=1024)

    proj = norm_matmul(x, norm_mix[1], b_w_in[0].astype(BF16), tm=1024)
    seq_out = band_attention(proj, kv, _band_bias(b_rel_bias[0]), n_pairs, seq)
    x = mixer_tail(x, seq_out, proj, seq_width // mem_width, 1, out_norm_g=final_norm)
    return x.reshape(bsz, seq, d)
```
